```python
import math
import jax
import jax.numpy as jnp
from jax import lax
import numpy as np

D_MODEL = 2048
BATCH = 4
SEQ = 8192
DEPTH = 2

GRID_W = 64
CTX_LEN = 256
N_DIR = 2

ML_HEADS = 4
ML_DH = 256
ML_W = ML_HEADS * ML_DH
ML_CHUNK = 64
ML_NORM_EPS = 1e-6

RW_HEADS = 16
RW_DH = 64
RW_W = RW_HEADS * RW_DH
RW_DECAY_RANK = 64
RW_A_RANK = 64
RW_G_RANK = 128
RW_DECAY_OFFSET = 0.5
RW_NORM_EPS = 64e-5

S5_W = 1024
S5_GROUP = 16
S5_GROUPS = S5_W // S5_GROUP
S5_STATE = 64

N_BRANCH = 3

N_GROUPS = 4
EXPERTS_PER_GROUP = 4
N_EXPERTS = N_GROUPS * EXPERTS_PER_GROUP
TOP_K = 2
D_EXPERT = 1024

DEEPNORM_ALPHA = (2.0 * DEPTH) ** 0.25
DEEPNORM_BETA = (8.0 * DEPTH) ** -0.25
LN_EPS = 1e-5
N_MOD = 6

IN_WIDTHS = (ML_W, ML_W, ML_W, ML_W, N_DIR * ML_HEADS, N_DIR * ML_HEADS,
             RW_W, RW_W, RW_W, N_DIR * RW_DECAY_RANK, N_DIR * RW_A_RANK, RW_G_RANK,
             S5_W, N_BRANCH * D_MODEL)
D_IN = sum(IN_WIDTHS)
RW_IN_WIDTHS = (RW_W, RW_W, RW_W, N_DIR * RW_DECAY_RANK, N_DIR * RW_A_RANK, RW_G_RANK)
RW_IN_W = sum(RW_IN_WIDTHS)

kernel_name = 'hybrid_mlstm_rwkv7_s5_moe_deepnorm'


def layer_norm(x, eps=LN_EPS):
    xf = x.astype(jnp.float32)
    mu = jnp.mean(xf, axis=-1, keepdims=True)
    var = jnp.mean(jnp.square(xf - mu), axis=-1, keepdims=True)
    return (xf - mu) * lax.rsqrt(var + eps)


def modulate(x, shift, scale):
    return (layer_norm(x) * (1.0 + scale) + shift).astype(x.dtype)


def post_norm(z, gain, bias):
    return (layer_norm(z) * gain + bias).astype(z.dtype)


def head_norm(h, gain, bias, eps):
    y = layer_norm(h, eps)
    return y.reshape(h.shape[:-2] + (-1,)) * gain + bias


def split_cols(z, widths):
    return jnp.split(z, np.cumsum(widths)[:-1].tolist(), axis=-1)


def depthwise_conv3x3(z, w, b):
    ch = z.shape[-1]
    y = lax.conv_general_dilated(z, w[:, :, None, :].astype(z.dtype), window_strides=(1, 1), padding='SAME',
                                 dimension_numbers=('NHWC', 'HWIO', 'NHWC'), feature_group_count=ch)
    return y + b.astype(z.dtype)


def centred_shift(z):
    zp = jnp.pad(z, ((0, 0), (1, 1), (0, 0)))
    return 0.5 * (zp[:, :-2] + zp[:, 2:])


def dir_seq(c, x):
    fwd = jnp.concatenate([c[0], x[0]], axis=1)
    bwd = jnp.concatenate([jnp.flip(c[-1], 1), jnp.flip(x[-1], 1)], axis=1)
    return jnp.stack([fwd, bwd])


def dir_unsum(y, n_ctx):
    yc = y[0, :, :n_ctx] + jnp.flip(y[1, :, :n_ctx], 1)
    yx = y[0, :, n_ctx:] + jnp.flip(y[1, :, n_ctx:], 1)
    return yc, yx


def mlstm_chunkwise(q, k, v, ig, fg):
    P, B, T, H, dk = q.shape
    dv = v.shape[-1]
    L = ML_CHUNK
    n_chunks = T // L

    def to_chunks(a):
        a = a.astype(jnp.float32).reshape((P, B, n_chunks, L) + a.shape[3:])
        return jnp.swapaxes(jnp.moveaxis(a, 2, 0), 3, 4)

    xs = (to_chunks(q), to_chunks(k * dk ** -0.5), to_chunks(v), to_chunks(ig),
          to_chunks(jax.nn.log_sigmoid(fg.astype(jnp.float32))))
    tri = jnp.tril(jnp.ones((L, L), dtype=bool))

    def step(carry, inp):
        cmat, nvec, m = carry
        qj, kj, vj, li, lf = inp
        bcum = jnp.cumsum(lf, axis=-1)
        log_d = jnp.where(tri, bcum[..., :, None] - bcum[..., None, :] + li[..., None, :], -jnp.inf)
        inter = bcum + m[..., None]
        m_j = jnp.maximum(jnp.max(log_d, axis=-1), inter)
        scores = jnp.einsum('pbhjd,pbhsd->pbhjs', qj, kj) * jnp.exp(log_d - m_j[..., None])
        s_inter = jnp.exp(inter - m_j)
        num = (jnp.einsum('pbhjs,pbhsv->pbhjv', scores, vj)
               + s_inter[..., None] * jnp.einsum('pbhvd,pbhjd->pbhjv', cmat, qj))
        den = jnp.sum(scores, axis=-1) + s_inter * jnp.einsum('pbhd,pbhjd->pbhj', nvec, qj)
        h = num / jnp.maximum(jnp.abs(den), jnp.exp(-m_j))[..., None]
        b_last = bcum[..., -1]
        log_w = b_last[..., None] - bcum + li
        m_new = jnp.maximum(b_last + m, jnp.max(log_w, axis=-1))
        wts = jnp.exp(log_w - m_new[..., None])
        carry_decay = jnp.exp(b_last + m - m_new)
        cmat = carry_decay[..., None, None] * cmat + jnp.einsum('pbhs,pbhsv,pbhsd->pbhvd', wts, vj, kj)
        nvec = carry_decay[..., None] * nvec + jnp.einsum('pbhs,pbhsd->pbhd', wts, kj)
        return (cmat, nvec, m_new), h

    init = (jnp.zeros((P, B, H, dv, dk), jnp.float32), jnp.zeros((P, B, H, dk), jnp.float32),
            jnp.zeros((P, B, H), jnp.float32))
    _, h = lax.scan(step, init, xs)
    h = jnp.moveaxis(jnp.swapaxes(h, 3, 4), 0, 2)
    return h.reshape(P, B, T, H, dv)


def rwkv7_scan(r, w, k, v, kk, a):
    P, B, T, H, n = r.shape

    def to_time(z):
        return jnp.moveaxis(z.astype(jnp.float32), 2, 0)

    def step(s, inp):
        rt, wt, kt, vt, kkt, at = inp
        s_kk = jnp.einsum('pbhvk,pbhk->pbhv', s, kkt)
        s = s * wt[..., None, :] - s_kk[..., :, None] * (kkt * at)[..., None, :] + vt[..., :, None] * kt[..., None, :]
        return s, jnp.einsum('pbhvk,pbhk->pbhv', s, rt)

    s0 = jnp.zeros((P, B, H, n, n), jnp.float32)
    _, y = lax.scan(step, s0, tuple(to_time(z) for z in (r, w, k, v, kk, a)))
    return jnp.moveaxis(y, 0, 2)


def _diag_linear_combine(e1, e2):
    a1, b1 = e1
    a2, b2 = e2
    return a1 * a2, a2 * b1 + b2


def mlstm_branch(seg_c, seg_x, hw_c, hw_x, conv_w, conv_b, ig_b, fg_b, norm_g, norm_b):
    def heads(z):
        return z.reshape(z.shape[:-1] + (ML_HEADS, ML_DH))[None]

    prepped = []
    for (q, k, v, o, ig, fg), (rows, cols) in ((seg_c, hw_c), (seg_x, hw_x)):
        bsz, length, _ = q.shape
        qk = jnp.concatenate([q, k], axis=-1).reshape(bsz, rows, cols, 2 * ML_W)
        qk = jax.nn.silu(depthwise_conv3x3(qk, conv_w, conv_b)).reshape(bsz, length, 2 * ML_W)
        q, k = jnp.split(qk, 2, axis=-1)
        ig = jnp.moveaxis(ig.reshape(bsz, length, N_DIR, ML_HEADS) + ig_b, 2, 0)
        fg = jnp.moveaxis(fg.reshape(bsz, length, N_DIR, ML_HEADS) + fg_b, 2, 0)
        prepped.append((heads(q), heads(k), heads(v), ig, fg))
    (qc, kc, vc, ic, fc), (qx, kx, vx, ix, fx) = prepped
    h = mlstm_chunkwise(dir_seq(qc, qx), dir_seq(kc, kx), dir_seq(vc, vx), dir_seq(ic, ix), dir_seq(fc, fx))
    hc, hx = dir_unsum(h, qc.shape[2])
    oc, ox = seg_c[3], seg_x[3]
    yc = (jax.nn.sigmoid(oc) * head_norm(hc, norm_g, norm_b, ML_NORM_EPS)).astype(oc.dtype)
    yx = (jax.nn.sigmoid(ox) * head_norm(hx, norm_g, norm_b, ML_NORM_EPS)).astype(ox.dtype)
    return yc, yx


def rwkv7_branch(seg_c, seg_x, mu, w0, w_up, a0, a_up, g_up, k_k, k_a, r_k, norm_g, norm_b):
    def hd(z):
        return z.reshape(z.shape[:-1] + (RW_HEADS, RW_DH))

    prepped = []
    for seg in (seg_c, seg_x):
        z = jnp.concatenate(seg, axis=-1)
        z = z + mu * (centred_shift(z) - z)
        r, k, v, wd, ad, gd = split_cols(z, RW_IN_WIDTHS)
        bsz, length, _ = r.shape
        wd = wd.reshape(bsz, length, N_DIR, RW_DECAY_RANK)
        ad = ad.reshape(bsz, length, N_DIR, RW_A_RANK)
        w_pre = (w0 + jnp.einsum('bldr,drc->bldc', jnp.tanh(wd), w_up)).astype(jnp.float32)
        decay = jnp.exp(-jnp.exp(-jax.nn.softplus(-w_pre) - RW_DECAY_OFFSET))
        a = jax.nn.sigmoid(a0 + jnp.einsum('bldr,drc->bldc', ad, a_up))
        g = jax.nn.sigmoid(gd) @ g_up
        kk = hd(k * k_k).astype(jnp.float32)
        kk = kk / jnp.maximum(jnp.sqrt(jnp.sum(jnp.square(kk), axis=-1, keepdims=True)), 1e-12)
        k_dir = k[:, :, None, :] * (1.0 + (a - 1.0) * k_a)
        bonus = jnp.sum(hd(r[:, :, None, :] * k_dir * r_k.reshape(-1)), axis=(2, 4))
        bonus = bonus[..., None] * hd(v)
        prepped.append((hd(r)[None], jnp.moveaxis(hd(decay), 2, 0), jnp.moveaxis(hd(k_dir), 2, 0),
                        hd(v)[None], kk[None], jnp.moveaxis(hd(a), 2, 0), bonus, g))
    (rc, dc, kc, vc, kkc, ac, bc, gc), (rx, dx, kx, vx, kkx, ax, bx, gx) = prepped
    y = rwkv7_scan(dir_seq(rc, rx), dir_seq(dc, dx), dir_seq(kc, kx), dir_seq(vc, vx),
                   dir_seq(kkc, kkx), dir_seq(ac, ax))
    yc, yx = dir_unsum(y, rc.shape[2])

    def finish(yy, bonus, g):
        out = (head_norm(yy, norm_g, norm_b, RW_NORM_EPS) + bonus.reshape(bonus.shape[:-2] + (-1,))) * g
        return out.astype(g.dtype)

    return finish(yc, bc, gc), finish(yx, bx, gx)


def s5_branch(uc, ux, lam_re, lam_im, log_dt, b_re, b_im, c_re, c_im, d_skip):
    n_ctx = uc.shape[1]
    useq = dir_seq(uc[None], ux[None]).astype(jnp.float32)
    _, bsz, t_len, _ = useq.shape
    yc = d_skip * uc.astype(jnp.float32)
    yx = d_skip * ux.astype(jnp.float32)
    for d in range(N_DIR):
        lam = lax.complex(lam_re[d].astype(jnp.float32), lam_im[d].astype(jnp.float32))
        lam_bar = jnp.exp(lam * jnp.exp(log_dt[d].astype(jnp.float32))[:, None])
        b_bar = ((lam_bar - 1.0) / lam)[..., None] * lax.complex(b_re[d].astype(jnp.float32),
                                                                  b_im[d].astype(jnp.float32))
        u = useq[d].reshape(bsz, t_len, S5_GROUPS, S5_GROUP).astype(jnp.complex64)
        bu = jnp.einsum('gnc,btgc->btgn', b_bar, u)
        a_el = jnp.broadcast_to(lam_bar, (1, t_len) + lam_bar.shape)
        _, state = lax.associative_scan(_diag_linear_combine, (a_el, bu), axis=1)
        c_mat = lax.complex(c_re[d].astype(jnp.float32), c_im[d].astype(jnp.float32))
        y = jnp.real(jnp.einsum('gcn,btgn->btgc', c_mat, state)).reshape(bsz, t_len, S5_W)
        if d == 0:
            yc = yc + y[:, :n_ctx]
            yx = yx + y[:, n_ctx:]
        else:
            yc = yc + jnp.flip(y[:, :n_ctx], 1)
            yx = yx + jnp.flip(y[:, n_ctx:], 1)
    return jax.nn.gelu(yc).astype(uc.dtype), jax.nn.gelu(yx).astype(ux.dtype)


def token_mixer(uc, ux, need_ctx, w_in,
                ml_conv_w, ml_conv_b, ml_ig_b, ml_fg_b, ml_norm_g, ml_norm_b, ml_proj,
                rw_mu, rw_w0, rw_w_up, rw_a0, rw_a_up, rw_g_up, rw_k_k, rw_k_a, rw_r_k,
                rw_norm_g, rw_norm_b, rw_proj,
                s5_lam_re, s5_lam_im, s5_log_dt, s5_b_re, s5_b_im, s5_c_re, s5_c_im, s5_d,
                s5_w_val, s5_w_gate, w_out):
    hw_c = (1, uc.shape[1])
    hw_x = (ux.shape[1] // GRID_W, GRID_W)
    pc = split_cols(uc @ w_in, IN_WIDTHS)
    px = split_cols(ux @ w_in, IN_WIDTHS)
    ml_c, ml_x = mlstm_branch(pc[0:6], px[0:6], hw_c, hw_x, ml_conv_w, ml_conv_b, ml_ig_b, ml_fg_b,
                              ml_norm_g, ml_norm_b)
    rw_c, rw_x = rwkv7_branch(pc[6:12], px[6:12], rw_mu, rw_w0, rw_w_up, rw_a0, rw_a_up, rw_g_up,
                              rw_k_k, rw_k_a, rw_r_k, rw_norm_g, rw_norm_b)
    s5_c, s5_x = s5_branch(pc[12], px[12], s5_lam_re, s5_lam_im, s5_log_dt, s5_b_re, s5_b_im,
                           s5_c_re, s5_c_im, s5_d)

    def merge(ml, rw, s5, gate_pre):
        g = jax.nn.sigmoid(gate_pre).reshape(gate_pre.shape[:-1] + (N_BRANCH, D_MODEL))
        z = (g[..., 0, :] * (ml @ ml_proj) + g[..., 1, :] * (rw @ rw_proj)
             + g[..., 2, :] * ((s5 @ s5_w_val) * jax.nn.sigmoid(s5 @ s5_w_gate)))
        return z @ w_out

    out_x = merge(ml_x, rw_x, s5_x, px[13])
    out_c = merge(ml_c, rw_c, s5_c, pc[13]) if need_ctx else None
    return out_c, out_x


def moe_ffn(u, router_w, router_b, w_gate, w_up, w_down):
    n_tok = u.shape[0]
    affinity = jax.nn.sigmoid((u @ router_w).astype(jnp.float32))
    grouped = (affinity + router_b.astype(jnp.float32)).reshape(n_tok, N_GROUPS, EXPERTS_PER_GROUP)
    group_score = jnp.sum(lax.top_k(grouped, TOP_K)[0], axis=-1)
    best = jnp.argmax(group_score, axis=-1)
    keep = (best[:, None] == jnp.arange(N_GROUPS))[..., None]
    masked = jnp.where(keep, grouped, -jnp.inf).reshape(n_tok, N_EXPERTS)
    _, idx = lax.top_k(masked, TOP_K)
    wsel = jnp.take_along_axis(affinity, idx, axis=-1)
    wsel = wsel / jnp.sum(wsel, axis=-1, keepdims=True)
    gates = jnp.sum((idx[..., None] == jnp.arange(N_EXPERTS)).astype(jnp.float32) * wsel[..., None], axis=1)
    gates = gates.astype(u.dtype)
    out = jnp.zeros_like(u)
    for e in range(N_EXPERTS):
        h = jax.nn.silu(u @ w_gate[e]) * (u @ w_up[e])
        out = out + gates[:, e, None] * (h @ w_down[e])
    return out


def setup_inputs(seed: int = 0) -> dict:
    key = jax.random.key(seed)
    ks = iter(jax.random.split(key, 64))

    def nrm(shape, scale):
        return scale * jax.random.normal(next(ks), shape, jnp.float32)

    def unif(shape, lo, hi):
        return jax.random.uniform(next(ks), shape, jnp.float32, minval=lo, maxval=hi)

    L = DEPTH
    D = D_MODEL
    return {
        'x': nrm((BATCH, SEQ, D), 1.0),
        'c': nrm((BATCH, D), 1.0),
        'ctx': nrm((BATCH, CTX_LEN, D), 1.0),
        'c_ctx': nrm((D,), 1.0),
        'ada_w': nrm((L, D, N_MOD * D), D ** -0.5),
        'ada_b': nrm((L, N_MOD * D), 0.02),
        'w_in': nrm((L, D, D_IN), D ** -0.5),
        'ml_conv_w': nrm((L, 3, 3, 2 * ML_W), 1.0 / 3.0),
        'ml_conv_b': nrm((L, 2 * ML_W), 0.02),
        'ml_ig_b': nrm((L, N_DIR, ML_HEADS), 0.5),
        'ml_fg_b': jnp.linspace(3.0, 6.0, ML_HEADS) + nrm((L, N_DIR, ML_HEADS), 0.1),
        'ml_norm_g': 1.0 + nrm((L, ML_W), 0.02),
        'ml_norm_b': nrm((L, ML_W), 0.02),
        'ml_proj': nrm((L, ML_W, D), DEEPNORM_BETA * ML_W ** -0.5),
        'rw_mu': unif((L, RW_IN_W), 0.0, 1.0),
        'rw_w0': jnp.linspace(-6.0, -1.0, RW_W) + nrm((L, N_DIR, RW_W), 0.1),
        'rw_w_up': nrm((L, N_DIR, RW_DECAY_RANK, RW_W), 0.5 * RW_DECAY_RANK ** -0.5),
        'rw_a0': nrm((L, N_DIR, RW_W), 0.1),
        'rw_a_up': nrm((L, N_DIR, RW_A_RANK, RW_W), 0.5 * RW_A_RANK ** -0.5),
        'rw_g_up': nrm((L, RW_G_RANK, RW_W), RW_G_RANK ** -0.5),
        'rw_k_k': 0.85 + nrm((L, RW_W), 0.02),
        'rw_k_a': 1.0 + nrm((L, RW_W), 0.02),
        'rw_r_k': nrm((L, RW_HEADS, RW_DH), 0.1),
        'rw_norm_g': 1.0 + nrm((L, RW_W), 0.02),
        'rw_norm_b': nrm((L, RW_W), 0.02),
        'rw_proj': nrm((L, RW_W, D), DEEPNORM_BETA * RW_W ** -0.5),
        's5_lam_re': -0.5 + nrm((L, N_DIR, S5_GROUPS, S5_STATE), 0.01),
        's5_lam_im': math.pi * jnp.arange(S5_STATE, dtype=jnp.float32) + nrm((L, N_DIR, S5_GROUPS, S5_STATE), 0.01),
        's5_log_dt': unif((L, N_DIR, S5_GROUPS), math.log(1e-3), math.log(1e-1)),
        's5_b_re': nrm((L, N_DIR, S5_GROUPS, S5_STATE, S5_GROUP), (2.0 * S5_GROUP) ** -0.5),
        's5_b_im': nrm((L, N_DIR, S5_GROUPS, S5_STATE, S5_GROUP), (2.0 * S5_GROUP) ** -0.5),
        's5_c_re': nrm((L, N_DIR, S5_GROUPS, S5_GROUP, S5_STATE), 1.0),
        's5_c_im': nrm((L, N_DIR, S5_GROUPS, S5_GROUP, S5_STATE), 1.0),
        's5_d': nrm((L, S5_W), 0.5),
        's5_w_val': nrm((L, S5_W, D), DEEPNORM_BETA * S5_W ** -0.5),
        's5_w_gate': nrm((L, S5_W, D), S5_W ** -0.5),
        'w_out': nrm((L, D, D), DEEPNORM_BETA * D ** -0.5),
        'ln1_g': 1.0 + nrm((L, D), 0.02),
        'ln1_b': nrm((L, D), 0.02),
        'ln2_g': 1.0 + nrm((L, D), 0.02),
        'ln2_b': nrm((L, D), 0.02),
        'router_w': nrm((D, N_EXPERTS), D ** -0.5),
        'router_b': nrm((N_EXPERTS,), 0.01),
        'exp_w_gate': nrm((L, N_EXPERTS, D, D_EXPERT), D ** -0.5),
        'exp_w_up': nrm((L, N_EXPERTS, D, D_EXPERT), D ** -0.5),
        'exp_w_down': nrm((L, N_EXPERTS, D_EXPERT, D), DEEPNORM_BETA * D_EXPERT ** -0.5),
    }


def reference(x, c, ctx, c_ctx, ada_w, ada_b, w_in, ml_conv_w, ml_conv_b, ml_ig_b, ml_fg_b, ml_norm_g,
              ml_norm_b, ml_proj, rw_mu, rw_w0, rw_w_up, rw_a0, rw_a_up, rw_g_up, rw_k_k, rw_k_a, rw_r_k,
              rw_norm_g, rw_norm_b, rw_proj, s5_lam_re, s5_lam_im, s5_log_dt, s5_b_re, s5_b_im, s5_c_re,
              s5_c_im, s5_d, s5_w_val, s5_w_gate, w_out, ln1_g, ln1_b, ln2_g, ln2_b, router_w, router_b,
              exp_w_gate, exp_w_up, exp_w_down):
    silu_c = jax.nn.silu(c)[:, None, :]
    silu_cc = jax.nn.silu(c_ctx)[None, None, :]
    xc = ctx
    for i in range(DEPTH):
        need_ctx = i < DEPTH - 1
        mx = jnp.split(silu_c @ ada_w[i] + ada_b[i], N_MOD, axis=-1)
        mc = jnp.split(silu_cc @ ada_w[i] + ada_b[i], N_MOD, axis=-1)
        mix_c, mix_x = token_mixer(
            modulate(xc, mc[0], mc[1]), modulate(x, mx[0], mx[1]), need_ctx, w_in[i],
            ml_conv_w[i], ml_conv_b[i], ml_ig_b[i], ml_fg_b[i], ml_norm_g[i], ml_norm_b[i], ml_proj[i],
            rw_mu[i], rw_w0[i], rw_w_up[i], rw_a0[i], rw_a_up[i], rw_g_up[i], rw_k_k[i], rw_k_a[i], rw_r_k[i],
            rw_norm_g[i], rw_norm_b[i], rw_proj[i],
            s5_lam_re[i], s5_lam_im[i], s5_log_dt[i], s5_b_re[i], s5_b_im[i], s5_c_re[i], s5_c_im[i], s5_d[i],
            s5_w_val[i], s5_w_gate[i], w_out[i])
        x = post_norm(DEEPNORM_ALPHA * x + mx[2] * mix_x, ln1_g[i], ln1_b[i])
        tokens = [modulate(x, mx[3], mx[4]).reshape(-1, D_MODEL)]
        if need_ctx:
            xc = post_norm(DEEPNORM_ALPHA * xc + mc[2] * mix_c, ln1_g[i], ln1_b[i])
            tokens.append(modulate(xc, mc[3], mc[4]).reshape(-1, D_MODEL))
        ffn = moe_ffn(jnp.concatenate(tokens, axis=0), router_w, router_b, exp_w_gate[i], exp_w_up[i],
                      exp_w_down[i])
        n_lat = x.shape[0] * x.shape[1]
        x = post_norm(DEEPNORM_ALPHA * x + mx[5] * ffn[:n_lat].reshape(x.shape), ln2_g[i], ln2_b[i])
        if need_ctx:
            xc = post_norm(DEEPNORM_ALPHA * xc + mc[5] * ffn[n_lat:].reshape(xc.shape), ln2_g[i], ln2_b[i])
    return x
```

```python
import functools
import math

import jax
import jax.numpy as jnp
import numpy as np
from jax import lax
from jax.experimental import pallas as pl
from jax.experimental.pallas import tpu as pltpu

D_MODEL = 2048
DEPTH = 2
GRID_W = 64
N_DIR = 2
ML_HEADS = 4
ML_DH = 256
ML_W = ML_HEADS * ML_DH
ML_CHUNK = 64
ML_NORM_EPS = 1e-6
RW_HEADS = 16
RW_DH = 64
RW_W = RW_HEADS * RW_DH
RW_DECAY_RANK = 64
RW_A_RANK = 64
RW_G_RANK = 128
RW_DECAY_OFFSET = 0.5
RW_NORM_EPS = 64e-5
S5_W = 1024
S5_GROUP = 16
S5_GROUPS = S5_W // S5_GROUP
S5_STATE = 64
N_BRANCH = 3
N_GROUPS = 4
EXPERTS_PER_GROUP = 4
N_EXPERTS = N_GROUPS * EXPERTS_PER_GROUP
TOP_K = 2
D_EXPERT = 1024
DEEPNORM_ALPHA = (2.0 * DEPTH) ** 0.25
LN_EPS = 1e-5
N_MOD = 6
IN_WIDTHS = (ML_W, ML_W, ML_W, ML_W, N_DIR * ML_HEADS, N_DIR * ML_HEADS,
             RW_W, RW_W, RW_W, N_DIR * RW_DECAY_RANK, N_DIR * RW_A_RANK, RW_G_RANK,
             S5_W, N_BRANCH * D_MODEL)
D_IN = sum(IN_WIDTHS)
RW_IN_WIDTHS = (RW_W, RW_W, RW_W, N_DIR * RW_DECAY_RANK, N_DIR * RW_A_RANK, RW_G_RANK)

VMEM_LIMIT_BYTES = 56 * 1024 * 1024


def _mm_kernel(a_ref, w_ref, o_ref):
    o_ref[...] = jnp.dot(a_ref[...].astype(jnp.bfloat16), w_ref[...],
                         preferred_element_type=jnp.float32)


def _mm(a, w, tm=512, tn=512):
    m, k = a.shape
    n = w.shape[1]
    tm = min(tm, m)
    tn = min(tn, n)
    assert m % tm == 0 and n % tn == 0, (m, n, tm, tn)
    return pl.pallas_call(
        _mm_kernel,
        grid=(m // tm, n // tn),
        in_specs=[pl.BlockSpec((tm, k), lambda i, j: (i, 0)),
                  pl.BlockSpec((k, tn), lambda i, j: (0, j))],
        out_specs=pl.BlockSpec((tm, tn), lambda i, j: (i, j)),
        out_shape=jax.ShapeDtypeStruct((m, n), jnp.float32),
        compiler_params=pltpu.CompilerParams(
            dimension_semantics=("parallel", "arbitrary"),
            vmem_limit_bytes=VMEM_LIMIT_BYTES),
        name="mm",
    )(a, w)


def _mm_f32_kernel(a_ref, w_ref, o_ref):
    o_ref[...] = jnp.dot(a_ref[...], w_ref[...], preferred_element_type=jnp.float32,
                         precision=lax.Precision.HIGHEST)


def _mm_f32(a, w, tm=512):
    m, k = a.shape
    n = w.shape[1]
    assert m % tm == 0
    return pl.pallas_call(
        _mm_f32_kernel,
        grid=(m // tm,),
        in_specs=[pl.BlockSpec((tm, k), lambda i: (i, 0)),
                  pl.BlockSpec((k, n), lambda i: (0, 0))],
        out_specs=pl.BlockSpec((tm, n), lambda i: (i, 0)),
        out_shape=jax.ShapeDtypeStruct((m, n), jnp.float32),
        compiler_params=pltpu.CompilerParams(
            dimension_semantics=("parallel",),
            vmem_limit_bytes=VMEM_LIMIT_BYTES),
        name="mm_f32",
    )(a, w)


def _mm_any(a, w, tm=512, tn=512):
    lead = a.shape[:-1]
    a2 = a.reshape(-1, a.shape[-1])
    m, n = a2.shape[0], w.shape[1]
    mp = -(-m // 8) * 8
    if mp > tm:
        mp = -(-m // tm) * tm
    np_ = -(-n // 128) * 128
    if np_ > tn:
        np_ = -(-n // tn) * tn
    if mp != m:
        a2 = jnp.pad(a2, ((0, mp - m), (0, 0)))
    wb = w.astype(jnp.bfloat16)
    if np_ != n:
        wb = jnp.pad(wb, ((0, 0), (0, np_ - n)))
    out = _mm(a2, wb, tm, tn)[:m, :n]
    return out.reshape(lead + (n,))


def _moe_kernel(u_ref, g_ref, wg_ref, wu_ref, wd_ref, o_ref):
    e = pl.program_id(1)

    @pl.when(e == 0)
    def _():
        o_ref[...] = jnp.zeros_like(o_ref)

    u = u_ref[...].astype(jnp.bfloat16)
    hg = jnp.dot(u, wg_ref[0], preferred_element_type=jnp.float32)
    hu = jnp.dot(u, wu_ref[0], preferred_element_type=jnp.float32)
    h = (hg * jax.nn.sigmoid(hg)) * hu
    y = jnp.dot(h.astype(jnp.bfloat16), wd_ref[0], preferred_element_type=jnp.float32)
    gates = g_ref[...]
    col = lax.broadcasted_iota(jnp.int32, gates.shape, 1)
    ge = jnp.sum(jnp.where(col == e, gates, 0.0), axis=1, keepdims=True)
    o_ref[...] += ge * y


def _moe_dense(u, gates, wg, wu, wd, tm=512):
    n_tok, d = u.shape
    n_e, _, d_e = wg.shape
    assert n_tok % tm == 0
    return pl.pallas_call(
        _moe_kernel,
        grid=(n_tok // tm, n_e),
        in_specs=[pl.BlockSpec((tm, d), lambda i, e: (i, 0)),
                  pl.BlockSpec((tm, n_e), lambda i, e: (i, 0)),
                  pl.BlockSpec((1, d, d_e), lambda i, e: (e, 0, 0)),
                  pl.BlockSpec((1, d, d_e), lambda i, e: (e, 0, 0)),
                  pl.BlockSpec((1, d_e, d), lambda i, e: (e, 0, 0))],
        out_specs=pl.BlockSpec((tm, d), lambda i, e: (i, 0)),
        out_shape=jax.ShapeDtypeStruct((n_tok, d), jnp.float32),
        compiler_params=pltpu.CompilerParams(
            dimension_semantics=("parallel", "arbitrary"),
            vmem_limit_bytes=VMEM_LIMIT_BYTES),
        name="moe_dense",
    )(u, gates, wg, wu, wd)


def _layer_norm(x, eps=LN_EPS):
    mu = jnp.mean(x, axis=-1, keepdims=True)
    var = jnp.mean(jnp.square(x - mu), axis=-1, keepdims=True)
    return (x - mu) * lax.rsqrt(var + eps)


def _modulate(x, shift, scale):
    return _layer_norm(x) * (1.0 + scale) + shift


def _post_norm(z, gain, bias):
    return _layer_norm(z) * gain + bias


def _head_norm(h, gain, bias, eps):
    y = _layer_norm(h, eps)
    return y.reshape(h.shape[:-2] + (-1,)) * gain + bias


def _split_cols(z, widths):
    return jnp.split(z, np.cumsum(widths)[:-1].tolist(), axis=-1)


def _depthwise_conv3x3(z, w, b):
    ch = z.shape[-1]
    y = lax.conv_general_dilated(z, w[:, :, None, :], window_strides=(1, 1), padding='SAME',
                                 dimension_numbers=('NHWC', 'HWIO', 'NHWC'), feature_group_count=ch)
    return y + b


def _centred_shift(z):
    zp = jnp.pad(z, ((0, 0), (1, 1), (0, 0)))
    return 0.5 * (zp[:, :-2] + zp[:, 2:])


def _dir_seq(c, x):
    fwd = jnp.concatenate([c[0], x[0]], axis=1)
    bwd = jnp.concatenate([jnp.flip(c[-1], 1), jnp.flip(x[-1], 1)], axis=1)
    return jnp.stack([fwd, bwd])


def _dir_unsum(y, n_ctx):
    yc = y[0, :, :n_ctx] + jnp.flip(y[1, :, :n_ctx], 1)
    yx = y[0, :, n_ctx:] + jnp.flip(y[1, :, n_ctx:], 1)
    return yc, yx


def _mlstm_chunkwise(q, k, v, ig, fg):
    P, B, T, H, dk = q.shape
    dv = v.shape[-1]
    L = ML_CHUNK
    n_chunks = T // L

    def to_chunks(a):
        a = a.reshape((P, B, n_chunks, L) + a.shape[3:])
        return jnp.swapaxes(jnp.moveaxis(a, 2, 0), 3, 4)

    xs = (to_chunks(q), to_chunks(k * dk ** -0.5), to_chunks(v), to_chunks(ig),
          to_chunks(jax.nn.log_sigmoid(fg)))
    tri = jnp.tril(jnp.ones((L, L), dtype=bool))

    def step(carry, inp):
        cmat, nvec, m = carry
        qj, kj, vj, li, lf = inp
        bcum = jnp.cumsum(lf, axis=-1)
        log_d = jnp.where(tri, bcum[..., :, None] - bcum[..., None, :] + li[..., None, :], -jnp.inf)
        inter = bcum + m[..., None]
        m_j = jnp.maximum(jnp.max(log_d, axis=-1), inter)
        scores = jnp.einsum('pbhjd,pbhsd->pbhjs', qj, kj) * jnp.exp(log_d - m_j[..., None])
        s_inter = jnp.exp(inter - m_j)
        num = (jnp.einsum('pbhjs,pbhsv->pbhjv', scores, vj)
               + s_inter[..., None] * jnp.einsum('pbhvd,pbhjd->pbhjv', cmat, qj))
        den = jnp.sum(scores, axis=-1) + s_inter * jnp.einsum('pbhd,pbhjd->pbhj', nvec, qj)
        h = num / jnp.maximum(jnp.abs(den), jnp.exp(-m_j))[..., None]
        b_last = bcum[..., -1]
        log_w = b_last[..., None] - bcum + li
        m_new = jnp.maximum(b_last + m, jnp.max(log_w, axis=-1))
        wts = jnp.exp(log_w - m_new[..., None])
        carry_decay = jnp.exp(b_last + m - m_new)
        cmat = carry_decay[..., None, None] * cmat + jnp.einsum('pbhs,pbhsv,pbhsd->pbhvd', wts, vj, kj)
        nvec = carry_decay[..., None] * nvec + jnp.einsum('pbhs,pbhsd->pbhd', wts, kj)
        return (cmat, nvec, m_new), h

    init = (jnp.zeros((P, B, H, dv, dk), jnp.float32), jnp.zeros((P, B, H, dk), jnp.float32),
            jnp.zeros((P, B, H), jnp.float32))
    _, h = lax.scan(step, init, xs)
    h = jnp.moveaxis(jnp.swapaxes(h, 3, 4), 0, 2)
    return h.reshape(P, B, T, H, dv)


def _rwkv7_scan(r, w, k, v, kk, a):
    P, B, T, H, n = r.shape

    def to_time(z):
        return jnp.moveaxis(z, 2, 0)

    def step(s, inp):
        rt, wt, kt, vt, kkt, at = inp
        s_kk = jnp.einsum('pbhvk,pbhk->pbhv', s, kkt)
        s = s * wt[..., None, :] - s_kk[..., :, None] * (kkt * at)[..., None, :] + vt[..., :, None] * kt[..., None, :]
        return s, jnp.einsum('pbhvk,pbhk->pbhv', s, rt)

    s0 = jnp.zeros((P, B, H, n, n), jnp.float32)
    _, y = lax.scan(step, s0, tuple(to_time(z) for z in (r, w, k, v, kk, a)))
    return jnp.moveaxis(y, 0, 2)


def _diag_linear_combine(e1, e2):
    a1, b1 = e1
    a2, b2 = e2
    return a1 * a2, a2 * b1 + b2


def _mlstm_branch(seg_c, seg_x, hw_c, hw_x, conv_w, conv_b, ig_b, fg_b, norm_g, norm_b):
    def heads(z):
        return z.reshape(z.shape[:-1] + (ML_HEADS, ML_DH))[None]

    prepped = []
    for (q, k, v, o, ig, fg), (rows, cols) in ((seg_c, hw_c), (seg_x, hw_x)):
        bsz, length, _ = q.shape
        qk = jnp.concatenate([q, k], axis=-1).reshape(bsz, rows, cols, 2 * ML_W)
        qk = jax.nn.silu(_depthwise_conv3x3(qk, conv_w, conv_b)).reshape(bsz, length, 2 * ML_W)
        q, k = jnp.split(qk, 2, axis=-1)
        ig = jnp.moveaxis(ig.reshape(bsz, length, N_DIR, ML_HEADS) + ig_b, 2, 0)
        fg = jnp.moveaxis(fg.reshape(bsz, length, N_DIR, ML_HEADS) + fg_b, 2, 0)
        prepped.append((heads(q), heads(k), heads(v), ig, fg))
    (qc, kc, vc, ic, fc), (qx, kx, vx, ix, fx) = prepped
    h = _mlstm_chunkwise(_dir_seq(qc, qx), _dir_seq(kc, kx), _dir_seq(vc, vx), _dir_seq(ic, ix),
                         _dir_seq(fc, fx))
    hc, hx = _dir_unsum(h, qc.shape[2])
    oc, ox = seg_c[3], seg_x[3]
    yc = jax.nn.sigmoid(oc) * _head_norm(hc, norm_g, norm_b, ML_NORM_EPS)
    yx = jax.nn.sigmoid(ox) * _head_norm(hx, norm_g, norm_b, ML_NORM_EPS)
    return yc, yx


def _rwkv7_branch(seg_c, seg_x, mu, w0, w_up, a0, a_up, g_up, k_k, k_a, r_k, norm_g, norm_b):
    def hd(z):
        return z.reshape(z.shape[:-1] + (RW_HEADS, RW_DH))

    prepped = []
    for seg in (seg_c, seg_x):
        z = jnp.concatenate(seg, axis=-1)
        z = z + mu * (_centred_shift(z) - z)
        r, k, v, wd, ad, gd = _split_cols(z, RW_IN_WIDTHS)
        bsz, length, _ = r.shape
        wd = wd.reshape(bsz, length, N_DIR, RW_DECAY_RANK)
        ad = ad.reshape(bsz, length, N_DIR, RW_A_RANK)
        w_pre = w0 + jnp.einsum('bldr,drc->bldc', jnp.tanh(wd), w_up)
        decay = jnp.exp(-jnp.exp(-jax.nn.softplus(-w_pre) - RW_DECAY_OFFSET))
        a = jax.nn.sigmoid(a0 + jnp.einsum('bldr,drc->bldc', ad, a_up))
        g = jax.nn.sigmoid(gd) @ g_up
        kk = hd(k * k_k)
        kk = kk / jnp.maximum(jnp.sqrt(jnp.sum(jnp.square(kk), axis=-1, keepdims=True)), 1e-12)
        k_dir = k[:, :, None, :] * (1.0 + (a - 1.0) * k_a)
        bonus = jnp.sum(hd(r[:, :, None, :] * k_dir * r_k.reshape(-1)), axis=(2, 4))
        bonus = bonus[..., None] * hd(v)
        prepped.append((hd(r)[None], jnp.moveaxis(hd(decay), 2, 0), jnp.moveaxis(hd(k_dir), 2, 0),
                        hd(v)[None], kk[None], jnp.moveaxis(hd(a), 2, 0), bonus, g))
    (rc, dc, kc, vc, kkc, ac, bc, gc), (rx, dx, kx, vx, kkx, ax, bx, gx) = prepped
    y = _rwkv7_scan(_dir_seq(rc, rx), _dir_seq(dc, dx), _dir_seq(kc, kx), _dir_seq(vc, vx),
                    _dir_seq(kkc, kkx), _dir_seq(ac, ax))
    yc, yx = _dir_unsum(y, rc.shape[2])

    def finish(yy, bonus, g):
        return (_head_norm(yy, norm_g, norm_b, RW_NORM_EPS) + bonus.reshape(bonus.shape[:-2] + (-1,))) * g

    return finish(yc, bc, gc), finish(yx, bx, gx)


def _s5_branch(uc, ux, lam_re, lam_im, log_dt, b_re, b_im, c_re, c_im, d_skip):
    n_ctx = uc.shape[1]
    useq = _dir_seq(uc[None], ux[None])
    _, bsz, t_len, _ = useq.shape
    yc = d_skip * uc
    yx = d_skip * ux
    for d in range(N_DIR):
        lam = lax.complex(lam_re[d], lam_im[d])
        lam_bar = jnp.exp(lam * jnp.exp(log_dt[d])[:, None])
        b_bar = ((lam_bar - 1.0) / lam)[..., None] * lax.complex(b_re[d], b_im[d])
        u = useq[d].reshape(bsz, t_len, S5_GROUPS, S5_GROUP).astype(jnp.complex64)
        bu = jnp.einsum('gnc,btgc->btgn', b_bar, u)
        a_el = jnp.broadcast_to(lam_bar, (1, t_len) + lam_bar.shape)
        _, state = lax.associative_scan(_diag_linear_combine, (a_el, bu), axis=1)
        c_mat = lax.complex(c_re[d], c_im[d])
        y = jnp.real(jnp.einsum('gcn,btgn->btgc', c_mat, state)).reshape(bsz, t_len, S5_W)
        if d == 0:
            yc = yc + y[:, :n_ctx]
            yx = yx + y[:, n_ctx:]
        else:
            yc = yc + jnp.flip(y[:, :n_ctx], 1)
            yx = yx + jnp.flip(y[:, n_ctx:], 1)
    return jax.nn.gelu(yc), jax.nn.gelu(yx)


def _token_mixer(uc, ux, need_ctx, w_in,
                 ml_conv_w, ml_conv_b, ml_ig_b, ml_fg_b, ml_norm_g, ml_norm_b, ml_proj,
                 rw_mu, rw_w0, rw_w_up, rw_a0, rw_a_up, rw_g_up, rw_k_k, rw_k_a, rw_r_k,
                 rw_norm_g, rw_norm_b, rw_proj,
                 s5_lam_re, s5_lam_im, s5_log_dt, s5_b_re, s5_b_im, s5_c_re, s5_c_im, s5_d,
                 s5_w_val, s5_w_gate, w_out):
    hw_c = (1, uc.shape[1])
    hw_x = (ux.shape[1] // GRID_W, GRID_W)
    pc = _split_cols(_mm_any(uc, w_in), IN_WIDTHS)
    px = _split_cols(_mm_any(ux, w_in), IN_WIDTHS)
    ml_c, ml_x = _mlstm_branch(pc[0:6], px[0:6], hw_c, hw_x, ml_conv_w, ml_conv_b, ml_ig_b, ml_fg_b,
                               ml_norm_g, ml_norm_b)
    rw_c, rw_x = _rwkv7_branch(pc[6:12], px[6:12], rw_mu, rw_w0, rw_w_up, rw_a0, rw_a_up, rw_g_up,
                               rw_k_k, rw_k_a, rw_r_k, rw_norm_g, rw_norm_b)
    s5_c, s5_x = _s5_branch(pc[12], px[12], s5_lam_re, s5_lam_im, s5_log_dt, s5_b_re, s5_b_im,
                            s5_c_re, s5_c_im, s5_d)

    def merge(ml, rw, s5, gate_pre):
        g = jax.nn.sigmoid(gate_pre).reshape(gate_pre.shape[:-1] + (N_BRANCH, D_MODEL))
        z = (g[..., 0, :] * _mm_any(ml, ml_proj) + g[..., 1, :] * _mm_any(rw, rw_proj)
             + g[..., 2, :] * (_mm_any(s5, s5_w_val) * jax.nn.sigmoid(_mm_any(s5, s5_w_gate))))
        return _mm_any(z, w_out)

    out_x = merge(ml_x, rw_x, s5_x, px[13])
    out_c = merge(ml_c, rw_c, s5_c, pc[13]) if need_ctx else None
    return out_c, out_x


def _moe_ffn(u, router_w, router_b, w_gate, w_up, w_down):
    n_tok = u.shape[0]
    affinity = jax.nn.sigmoid(_mm_f32(u, router_w))
    grouped = (affinity + router_b).reshape(n_tok, N_GROUPS, EXPERTS_PER_GROUP)
    group_score = jnp.sum(lax.top_k(grouped, TOP_K)[0], axis=-1)
    best = jnp.argmax(group_score, axis=-1)
    keep = (best[:, None] == jnp.arange(N_GROUPS))[..., None]
    masked = jnp.where(keep, grouped, -jnp.inf).reshape(n_tok, N_EXPERTS)
    _, idx = lax.top_k(masked, TOP_K)
    wsel = jnp.take_along_axis(affinity, idx, axis=-1)
    wsel = wsel / jnp.sum(wsel, axis=-1, keepdims=True)
    gates = jnp.sum((idx[..., None] == jnp.arange(N_EXPERTS)).astype(jnp.float32) * wsel[..., None], axis=1)
    return _moe_dense(u, gates, w_gate.astype(jnp.bfloat16), w_up.astype(jnp.bfloat16),
                      w_down.astype(jnp.bfloat16))


def kernel(x, c, ctx, c_ctx, ada_w, ada_b, w_in, ml_conv_w, ml_conv_b, ml_ig_b, ml_fg_b, ml_norm_g,
           ml_norm_b, ml_proj, rw_mu, rw_w0, rw_w_up, rw_a0, rw_a_up, rw_g_up, rw_k_k, rw_k_a, rw_r_k,
           rw_norm_g, rw_norm_b, rw_proj, s5_lam_re, s5_lam_im, s5_log_dt, s5_b_re, s5_b_im, s5_c_re,
           s5_c_im, s5_d, s5_w_val, s5_w_gate, w_out, ln1_g, ln1_b, ln2_g, ln2_b, router_w, router_b,
           exp_w_gate, exp_w_up, exp_w_down):
    silu_c = jax.nn.silu(c)[:, None, :]
    silu_cc = jax.nn.silu(c_ctx)[None, None, :]
    xc = ctx
    for i in range(DEPTH):
        need_ctx = i < DEPTH - 1
        mx = jnp.split(_mm_any(silu_c, ada_w[i]) + ada_b[i], N_MOD, axis=-1)
        mc = jnp.split(_mm_any(silu_cc, ada_w[i]) + ada_b[i], N_MOD, axis=-1)
        mix_c, mix_x = _token_mixer(
            _modulate(xc, mc[0], mc[1]), _modulate(x, mx[0], mx[1]), need_ctx, w_in[i],
            ml_conv_w[i], ml_conv_b[i], ml_ig_b[i], ml_fg_b[i], ml_norm_g[i], ml_norm_b[i], ml_proj[i],
            rw_mu[i], rw_w0[i], rw_w_up[i], rw_a0[i], rw_a_up[i], rw_g_up[i], rw_k_k[i], rw_k_a[i], rw_r_k[i],
            rw_norm_g[i], rw_norm_b[i], rw_proj[i],
            s5_lam_re[i], s5_lam_im[i], s5_log_dt[i], s5_b_re[i], s5_b_im[i], s5_c_re[i], s5_c_im[i], s5_d[i],
            s5_w_val[i], s5_w_gate[i], w_out[i])
        x = _post_norm(DEEPNORM_ALPHA * x + mx[2] * mix_x, ln1_g[i], ln1_b[i])
        tokens = [_modulate(x, mx[3], mx[4]).reshape(-1, D_MODEL)]
        if need_ctx:
            xc = _post_norm(DEEPNORM_ALPHA * xc + mc[2] * mix_c, ln1_g[i], ln1_b[i])
            tokens.append(_modulate(xc, mc[3], mc[4]).reshape(-1, D_MODEL))
        ffn = _moe_ffn(jnp.concatenate(tokens, axis=0), router_w, router_b, exp_w_gate[i], exp_w_up[i],
                       exp_w_down[i])
        n_lat = x.shape[0] * x.shape[1]
        x = _post_norm(DEEPNORM_ALPHA * x + mx[5] * ffn[:n_lat].reshape(x.shape), ln2_g[i], ln2_b[i])
        if need_ctx:
            xc = _post_norm(DEEPNORM_ALPHA * xc + mc[5] * ffn[n_lat:].reshape(xc.shape), ln2_g[i], ln2_b[i])
    return x
```

```python
import functools
import math

import jax
import jax.numpy as jnp
import numpy as np
from jax import lax
from jax.experimental import pallas as pl
from jax.experimental.pallas import tpu as pltpu

D_MODEL = 2048
DEPTH = 2
GRID_W = 64
N_DIR = 2
ML_HEADS = 4
ML_DH = 256
ML_W = ML_HEADS * ML_DH
ML_CHUNK = 64
ML_NORM_EPS = 1e-6
RW_HEADS = 16
RW_DH = 64
RW_W = RW_HEADS * RW_DH
RW_DECAY_RANK = 64
RW_A_RANK = 64
RW_G_RANK = 128
RW_DECAY_OFFSET = 0.5
RW_NORM_EPS = 64e-5
S5_W = 1024
S5_GROUP = 16
S5_GROUPS = S5_W // S5_GROUP
S5_STATE = 64
N_BRANCH = 3
N_GROUPS = 4
EXPERTS_PER_GROUP = 4
N_EXPERTS = N_GROUPS * EXPERTS_PER_GROUP
TOP_K = 2
D_EXPERT = 1024
DEEPNORM_ALPHA = (2.0 * DEPTH) ** 0.25
LN_EPS = 1e-5
N_MOD = 6
IN_WIDTHS = (ML_W, ML_W, ML_W, ML_W, N_DIR * ML_HEADS, N_DIR * ML_HEADS,
             RW_W, RW_W, RW_W, N_DIR * RW_DECAY_RANK, N_DIR * RW_A_RANK, RW_G_RANK,
             S5_W, N_BRANCH * D_MODEL)
D_IN = sum(IN_WIDTHS)
RW_IN_WIDTHS = (RW_W, RW_W, RW_W, N_DIR * RW_DECAY_RANK, N_DIR * RW_A_RANK, RW_G_RANK)

VMEM_LIMIT_BYTES = 56 * 1024 * 1024
LANES = 128

S5_CHUNK = 16
ML_TBLK = 256
RW_CHUNK = 16
RW_PAIRS = RW_HEADS // 2
RW_ROWS = RW_PAIRS * RW_CHUNK
RW_TBLK = 256

_BF = jnp.bfloat16
_HI = lax.Precision.HIGHEST
_NT = (((1,), (1,)), ((), ()))
_TN = (((0,), (0,)), ((), ()))


def _scan_block(d, i, n_blocks, ctx_blocks):
    bwd = jnp.where(i < ctx_blocks, ctx_blocks - 1 - i, n_blocks - 1 + ctx_blocks - i)
    return jnp.where(d == 0, i, bwd)


def _mm_kernel(a_ref, w_ref, o_ref, abf_ref):
    @pl.when(pl.program_id(1) == 0)
    def _():
        abf_ref[...] = a_ref[...].astype(_BF)

    o_ref[...] = jnp.dot(abf_ref[...], w_ref[...], preferred_element_type=jnp.float32)


def _mm(a, w, tm, tn):
    m, k = a.shape
    n = w.shape[1]
    assert m % tm == 0 and n % tn == 0, (m, n, tm, tn)
    return pl.pallas_call(
        _mm_kernel,
        grid=(m // tm, n // tn),
        in_specs=[pl.BlockSpec((tm, k), lambda i, j: (i, 0)),
                  pl.BlockSpec((k, tn), lambda i, j: (0, j))],
        out_specs=pl.BlockSpec((tm, tn), lambda i, j: (i, j)),
        out_shape=jax.ShapeDtypeStruct((m, n), jnp.float32),
        scratch_shapes=[pltpu.VMEM((tm, k), _BF)],
        compiler_params=pltpu.CompilerParams(
            dimension_semantics=("parallel", "arbitrary"),
            vmem_limit_bytes=VMEM_LIMIT_BYTES),
        name="mm",
    )(a, w)


def _mm_f32_kernel(a_ref, w_ref, o_ref):
    o_ref[...] = jnp.dot(a_ref[...], w_ref[...], preferred_element_type=jnp.float32, precision=_HI)


def _mm_f32(a, w, tm=512):
    m, k = a.shape
    n = w.shape[1]
    assert m % tm == 0
    return pl.pallas_call(
        _mm_f32_kernel,
        grid=(m // tm,),
        in_specs=[pl.BlockSpec((tm, k), lambda i: (i, 0)),
                  pl.BlockSpec((k, n), lambda i: (0, 0))],
        out_specs=pl.BlockSpec((tm, n), lambda i: (i, 0)),
        out_shape=jax.ShapeDtypeStruct((m, n), jnp.float32),
        compiler_params=pltpu.CompilerParams(
            dimension_semantics=("parallel",),
            vmem_limit_bytes=VMEM_LIMIT_BYTES),
        name="mm_f32",
    )(a, w)


def _mm_any(a, w, tm=1024, tn=512):
    lead = a.shape[:-1]
    a2 = a.reshape(-1, a.shape[-1])
    m, n = a2.shape[0], w.shape[1]
    mp = -(-m // 8) * 8
    if mp > tm:
        mp = -(-m // tm) * tm
    tm = min(tm, mp)
    np_ = -(-n // LANES) * LANES
    if np_ > tn:
        np_ = -(-n // tn) * tn
    tn = min(tn, np_)
    if mp != m:
        a2 = jnp.pad(a2, ((0, mp - m), (0, 0)))
    wb = w.astype(_BF)
    if np_ != n:
        wb = jnp.pad(wb, ((0, 0), (0, np_ - n)))
    out = _mm(a2, wb, tm, tn)
    if mp != m or np_ != n:
        out = out[:m, :n]
    return out.reshape(lead + (n,))


def _moe_kernel(u_ref, g_ref, wg_ref, wu_ref, wd_ref, o_ref):
    e = pl.program_id(1)

    @pl.when(e == 0)
    def _():
        o_ref[...] = jnp.zeros_like(o_ref)

    u = u_ref[...].astype(_BF)
    hg = jnp.dot(u, wg_ref[0], preferred_element_type=jnp.float32)
    hu = jnp.dot(u, wu_ref[0], preferred_element_type=jnp.float32)
    h = (hg * jax.nn.sigmoid(hg)) * hu
    y = jnp.dot(h.astype(_BF), wd_ref[0], preferred_element_type=jnp.float32)
    gates = g_ref[...]
    col = lax.broadcasted_iota(jnp.int32, gates.shape, 1)
    ge = jnp.sum(jnp.where(col == e, gates, 0.0), axis=1, keepdims=True)
    o_ref[...] += ge * y


def _moe_dense(u, gates, wg, wu, wd, tm=512):
    n_tok, d = u.shape
    n_e, _, d_e = wg.shape
    assert n_tok % tm == 0
    return pl.pallas_call(
        _moe_kernel,
        grid=(n_tok // tm, n_e),
        in_specs=[pl.BlockSpec((tm, d), lambda i, e: (i, 0)),
                  pl.BlockSpec((tm, n_e), lambda i, e: (i, 0)),
                  pl.BlockSpec((1, d, d_e), lambda i, e: (e, 0, 0)),
                  pl.BlockSpec((1, d, d_e), lambda i, e: (e, 0, 0)),
                  pl.BlockSpec((1, d_e, d), lambda i, e: (e, 0, 0))],
        out_specs=pl.BlockSpec((tm, d), lambda i, e: (i, 0)),
        out_shape=jax.ShapeDtypeStruct((n_tok, d), jnp.float32),
        compiler_params=pltpu.CompilerParams(
            dimension_semantics=("parallel", "arbitrary"),
            vmem_limit_bytes=VMEM_LIMIT_BYTES),
        name="moe_dense",
    )(u, gates, wg, wu, wd)


def _s5_mats(lam_re, lam_im, log_dt, b_re, b_im, c_re, c_im):
    L = S5_CHUNK
    lam = lax.complex(lam_re, lam_im)
    ldt = lam * jnp.exp(log_dt)[..., None]
    lam_bar = jnp.exp(ldt)
    b_bar = ((lam_bar - 1.0) / lam)[..., None] * lax.complex(b_re, b_im)
    c_mat = lax.complex(c_re, c_im)
    tau = jnp.arange(L + 1, dtype=jnp.float32)
    pw = jnp.exp(ldt[:, :, None, :] * tau[None, None, :, None])
    kern = jnp.real(jnp.einsum('dgon,dgtn,dgni->dgtoi', c_mat, pw[:, :, :L], b_bar))
    s_idx = jnp.arange(L)[:, None]
    t_idx = jnp.arange(L)[None, :]

    def toeplitz(k, lag, valid):
        m = k[:, jnp.clip(lag, 0, L - 1)] * valid[None, :, :, None, None]
        return jnp.transpose(m, (0, 1, 4, 2, 3)).reshape(-1, L * S5_GROUP, L * S5_GROUP)

    tsum = (toeplitz(kern[0], t_idx - s_idx, (t_idx >= s_idx).astype(jnp.float32))
            + toeplitz(kern[1], s_idx - t_idx, (s_idx >= t_idx).astype(jnp.float32)))
    pin_f = pw[0][:, L - 1 - jnp.arange(L)]
    pin_b = pw[1][:, jnp.arange(L)]
    in_f = jnp.einsum('gsn,gni->gsin', pin_f, b_bar[0]).reshape(-1, L * S5_GROUP, S5_STATE)
    in_b = jnp.einsum('gsn,gni->gsin', pin_b, b_bar[1]).reshape(-1, L * S5_GROUP, S5_STATE)
    icat = jnp.concatenate([jnp.real(in_f), jnp.imag(in_f), jnp.real(in_b), jnp.imag(in_b)], axis=-1)
    pout_f = pw[0][:, 1 + jnp.arange(L)]
    pout_b = pw[1][:, L - jnp.arange(L)]
    out_f = jnp.einsum('gon,gtn->gnto', c_mat[0], pout_f).reshape(-1, S5_STATE, L * S5_GROUP)
    out_b = jnp.einsum('gon,gtn->gnto', c_mat[1], pout_b).reshape(-1, S5_STATE, L * S5_GROUP)
    ocat = jnp.concatenate([jnp.real(out_f), -jnp.imag(out_f), jnp.real(out_b), -jnp.imag(out_b)], axis=1)
    lam_l = pw[:, :, L]
    lam_chunk = jnp.stack([jnp.real(lam_l[0]), jnp.imag(lam_l[0]), jnp.real(lam_l[1]), jnp.imag(lam_l[1])],
                          axis=1)
    return tsum, icat, ocat, lam_chunk


def _s5_kernel(u_ref, t_ref, i_ref, o_ref, lam_ref, d_ref, y_ref, v_ref, xfr_ref, xfi_ref, xbr_ref, xbi_ref,
               *, n_chunks, ctx_chunks, bsz):
    n = S5_STATE
    u = u_ref[0]
    ub = u.astype(_BF)
    v_ref[...] = jnp.dot(ub, i_ref[0], preferred_element_type=jnp.float32)
    lam = lam_ref[0]
    lfr = jnp.broadcast_to(lam[0:1], (bsz, n))
    lfi = jnp.broadcast_to(lam[1:2], (bsz, n))
    lbr = jnp.broadcast_to(lam[2:3], (bsz, n))
    lbi = jnp.broadcast_to(lam[3:4], (bsz, n))

    def cmul_add(lr, li, xr, xi, vr, vi):
        return lr * xr - li * xi + vr, lr * xi + li * xr + vi

    def step(j, carry):
        fr, fi, br, bi = carry
        rf = pl.multiple_of(j * (2 * bsz), 2 * bsz)
        pb = jnp.where(j < ctx_chunks // 2, ctx_chunks // 2 - 1 - j, (n_chunks + ctx_chunks) // 2 - 1 - j)
        rb = pl.multiple_of(pb * (2 * bsz), 2 * bsz)
        vf = v_ref[pl.ds(rf, 2 * bsz), :]
        vb = v_ref[pl.ds(rb, 2 * bsz), :]
        fr1, fi1 = cmul_add(lfr, lfi, fr, fi, vf[:bsz, 0:n], vf[:bsz, n:2 * n])
        fr2, fi2 = cmul_add(lfr, lfi, fr1, fi1, vf[bsz:, 0:n], vf[bsz:, n:2 * n])
        br1, bi1 = cmul_add(lbr, lbi, br, bi, vb[bsz:, 2 * n:3 * n], vb[bsz:, 3 * n:4 * n])
        br2, bi2 = cmul_add(lbr, lbi, br1, bi1, vb[:bsz, 2 * n:3 * n], vb[:bsz, 3 * n:4 * n])
        xfr_ref[pl.ds(rf, 2 * bsz), :] = jnp.concatenate([fr, fr1], axis=0)
        xfi_ref[pl.ds(rf, 2 * bsz), :] = jnp.concatenate([fi, fi1], axis=0)
        xbr_ref[pl.ds(rb, 2 * bsz), :] = jnp.concatenate([br1, br], axis=0)
        xbi_ref[pl.ds(rb, 2 * bsz), :] = jnp.concatenate([bi1, bi], axis=0)
        return fr2, fi2, br2, bi2

    z = jnp.zeros((bsz, n), jnp.float32)
    lax.fori_loop(0, n_chunks // 2, step, (z, z, z, z))
    o = o_ref[0]
    y = jnp.dot(ub, t_ref[0], preferred_element_type=jnp.float32)
    y += jnp.dot(xfr_ref[...].astype(_BF), o[0:n], preferred_element_type=jnp.float32)
    y += jnp.dot(xfi_ref[...].astype(_BF), o[n:2 * n], preferred_element_type=jnp.float32)
    y += jnp.dot(xbr_ref[...].astype(_BF), o[2 * n:3 * n], preferred_element_type=jnp.float32)
    y += jnp.dot(xbi_ref[...].astype(_BF), o[3 * n:4 * n], preferred_element_type=jnp.float32)
    y += d_ref[0] * u
    y_ref[0] = 0.5 * y * (1.0 + jnp.tanh(math.sqrt(2.0 / math.pi) * (y + 0.044715 * (y * y * y))))


def _s5_pallas(u, n_ctx, lam_re, lam_im, log_dt, b_re, b_im, c_re, c_im, d_skip):
    bsz, t_len, _ = u.shape
    L, G, C = S5_CHUNK, S5_GROUPS, S5_GROUP
    nc = t_len // L
    assert (2 * bsz) % 8 == 0 and nc % 2 == 0 and (n_ctx // L) % 2 == 0
    tsum, icat, ocat, lam_chunk = _s5_mats(lam_re, lam_im, log_dt, b_re, b_im, c_re, c_im)
    ug = jnp.transpose(u.reshape(bsz, nc, L, G, C), (3, 1, 0, 2, 4)).reshape(G, nc * bsz, L * C)
    dvec = jnp.tile(d_skip.reshape(G, 1, C), (1, L, 1)).reshape(G, 1, L * C)
    rows = nc * bsz
    wspec = pl.BlockSpec((1, L * C, L * C), lambda g: (g, 0, 0))
    yg = pl.pallas_call(
        functools.partial(_s5_kernel, n_chunks=nc, ctx_chunks=n_ctx // L, bsz=bsz),
        grid=(G,),
        in_specs=[pl.BlockSpec((1, rows, L * C), lambda g: (g, 0, 0)), wspec, wspec, wspec,
                  pl.BlockSpec((1, 4, S5_STATE), lambda g: (g, 0, 0)),
                  pl.BlockSpec((1, 1, L * C), lambda g: (g, 0, 0))],
        out_specs=pl.BlockSpec((1, rows, L * C), lambda g: (g, 0, 0)),
        out_shape=jax.ShapeDtypeStruct((G, rows, L * C), jnp.float32),
        scratch_shapes=[pltpu.VMEM((rows, 4 * S5_STATE), jnp.float32)]
                       + [pltpu.VMEM((rows, S5_STATE), jnp.float32)] * 4,
        compiler_params=pltpu.CompilerParams(dimension_semantics=("parallel",),
                                             vmem_limit_bytes=VMEM_LIMIT_BYTES),
        name="s5_scan",
    )(ug, tsum.astype(_BF), icat.astype(_BF), ocat.astype(_BF), lam_chunk, dvec)
    return jnp.transpose(yg.reshape(G, nc, bsz, L, C), (2, 1, 3, 0, 4)).reshape(bsz, t_len, S5_W)


def _log_sigmoid(x):
    return jnp.minimum(x, 0.0) - jnp.log(1.0 + jnp.exp(-jnp.abs(x)))


def _mlstm_kernel(igb_ref, fgb_ref, q_ref, k_ref, v_ref, gc_ref, gr_ref, h_ref, cmat_ref, nvec_ref, m_ref):
    d = pl.program_id(0)
    hd = pl.program_id(2)
    L = ML_CHUNK
    n_sub = ML_TBLK // L

    @pl.when(pl.program_id(3) == 0)
    def _():
        cmat_ref[...] = jnp.zeros_like(cmat_ref)
        nvec_ref[...] = jnp.zeros_like(nvec_ref)
        m_ref[...] = jnp.zeros_like(m_ref)

    igb = igb_ref[d, hd]
    fgb = fgb_ref[d, hd]
    row = lax.broadcasted_iota(jnp.int32, (L, L), 0)
    col = lax.broadcasted_iota(jnp.int32, (L, L), 1)
    sign = 1 - 2 * d
    seen = (row - col) * sign >= 0
    seen_f = seen.astype(jnp.float32)
    seen_t = ((col - row) * sign >= 0).astype(jnp.float32)

    def chunk(jj, carry):
        cj = jnp.where(d == 0, jj, n_sub - 1 - jj)
        r0 = pl.multiple_of(cj * L, L)
        q = q_ref[0, pl.ds(r0, L), :]
        k = k_ref[0, pl.ds(r0, L), :]
        v = v_ref[0, pl.ds(r0, L), :]
        gc = gc_ref[0, 0, 0, pl.ds(r0, L), :]
        gr = gr_ref[0, 0, 0, cj]
        li_col = gc[:, 0:1] + igb
        lf_col = _log_sigmoid(gc[:, 1:2] + fgb)
        li_row = gr[0:1, :] + igb
        lf_row = _log_sigmoid(gr[1:2, :] + fgb)
        m_prev = m_ref[...]
        bcum_col = jnp.dot(seen_f, jnp.broadcast_to(lf_col, (L, L)), precision=_HI,
                           preferred_element_type=jnp.float32)
        bcum_row = jnp.dot(jnp.broadcast_to(lf_row, (8, L)), seen_t, precision=_HI,
                           preferred_element_type=jnp.float32)[0:1]
        log_d = jnp.where(seen, bcum_col - bcum_row + li_row, -jnp.inf)
        inter = bcum_col[:, 0:1] + m_prev
        m_j = jnp.maximum(jnp.max(log_d, axis=1, keepdims=True), inter)
        dmat = jnp.exp(log_d - m_j)
        qb = q.astype(_BF)
        kb = (k * (ML_DH ** -0.5)).astype(_BF)
        vb = v.astype(_BF)
        scores = lax.dot_general(qb, kb, _NT, preferred_element_type=jnp.float32) * dmat
        s_inter = jnp.exp(inter - m_j)
        num = (jnp.dot(scores.astype(_BF), vb, preferred_element_type=jnp.float32)
               + s_inter * jnp.dot(qb, cmat_ref[...].astype(_BF), preferred_element_type=jnp.float32))
        den = (jnp.sum(scores, axis=1, keepdims=True)
               + s_inter * jnp.sum(q * nvec_ref[...], axis=1, keepdims=True))
        h_ref[0, 0, pl.ds(r0, L), :] = num / jnp.maximum(jnp.abs(den), jnp.exp(-m_j))
        b_last = jnp.sum(lf_col, axis=0, keepdims=True)
        log_w = b_last - bcum_col[:, 0:1] + li_col
        m_new = jnp.maximum(b_last + m_prev, jnp.max(log_w, axis=0, keepdims=True))
        wts = jnp.exp(log_w - m_new)
        decay = jnp.exp(b_last + m_prev - m_new)
        kw = k * (ML_DH ** -0.5) * wts
        cmat_ref[...] = decay * cmat_ref[...] + lax.dot_general(
            kw.astype(_BF), vb, _TN, preferred_element_type=jnp.float32)
        nvec_ref[...] = decay * nvec_ref[...] + jnp.sum(kw, axis=0, keepdims=True)
        m_ref[...] = m_new
        return carry

    lax.fori_loop(0, n_sub, chunk, 0)


def _mlstm_pallas(q, k, v, gates, ig_b, fg_b, n_ctx):
    bsz, t_len, _ = q.shape
    nb = t_len // ML_TBLK
    cb = n_ctx // ML_TBLK
    assert t_len % ML_TBLK == 0 and n_ctx % ML_TBLK == 0
    g = gates.reshape(bsz, t_len, 2, N_DIR, ML_HEADS)
    gcol = jnp.transpose(g, (3, 0, 4, 1, 2))
    grow = jnp.transpose(g.reshape(bsz, t_len // ML_CHUNK, ML_CHUNK, 2, N_DIR, ML_HEADS),
                         (4, 0, 5, 1, 3, 2))
    tmap = lambda d, b, h, i, *_: (b, _scan_block(d, i, nb, cb), h)
    grid_spec = pltpu.PrefetchScalarGridSpec(
        num_scalar_prefetch=2,
        grid=(N_DIR, bsz, ML_HEADS, nb),
        in_specs=[pl.BlockSpec((1, ML_TBLK, ML_DH), tmap)] * 3 + [
            pl.BlockSpec((1, 1, 1, ML_TBLK, 2), lambda d, b, h, i, *_: (d, b, h, _scan_block(d, i, nb, cb), 0)),
            pl.BlockSpec((1, 1, 1, ML_TBLK // ML_CHUNK, 2, ML_CHUNK),
                         lambda d, b, h, i, *_: (d, b, h, _scan_block(d, i, nb, cb), 0, 0))],
        out_specs=pl.BlockSpec((1, 1, ML_TBLK, ML_DH),
                               lambda d, b, h, i, *_: (d, b, _scan_block(d, i, nb, cb), h)),
        scratch_shapes=[pltpu.VMEM((ML_DH, ML_DH), jnp.float32), pltpu.VMEM((1, ML_DH), jnp.float32),
                        pltpu.VMEM((1, 1), jnp.float32)])
    return pl.pallas_call(
        _mlstm_kernel, grid_spec=grid_spec,
        out_shape=jax.ShapeDtypeStruct((N_DIR, bsz, t_len, ML_W), jnp.float32),
        compiler_params=pltpu.CompilerParams(
            dimension_semantics=("parallel", "parallel", "parallel", "arbitrary"),
            vmem_limit_bytes=VMEM_LIMIT_BYTES),
        name="mlstm_scan",
    )(ig_b, fg_b, q, k, v, gcol, grow)


def _to_pairs(x):
    return jnp.concatenate([x[:, p * LANES:(p + 1) * LANES] for p in range(RW_PAIRS)], axis=0)


def _dotf(a, b, dims=None):
    a = a.astype(_BF)
    b = b.astype(_BF)
    if dims is None:
        return jnp.dot(a, b, preferred_element_type=jnp.float32)
    return lax.dot_general(a, b, dims, preferred_element_type=jnp.float32)


def _rwkv_a_kernel(r_ref, k_ref, v_ref, kk_ref, a_ref, lw_ref,
                   at_ref, rt_ref, bw_ref, kw_ref, vv_ref, u0_ref, y0_ref, wc_ref, *, n_sub):
    d = pl.program_id(0)
    C, R = RW_CHUNK, RW_ROWS
    row = lax.broadcasted_iota(jnp.int32, (R, R), 0)
    col = lax.broadcasted_iota(jnp.int32, (R, R), 1)
    same = (row // C) == (col // C)
    sign = 1 - 2 * d
    before = same & ((row - col) * sign > 0)
    upto = same & ((row - col) * sign >= 0)
    eye = (row == col).astype(jnp.float32)
    first = col < RW_DH
    tpos = row % C

    def chunk(j, carry):
        r0 = pl.multiple_of(j * C, C)
        ld = lambda ref: _to_pairs(ref[0, pl.ds(r0, C), :])
        ldd = lambda ref: _to_pairs(ref[0, 0, pl.ds(r0, C), :])
        r, v, kk = ld(r_ref), ld(v_ref), ld(kk_ref)
        k, a, lw = ldd(k_ref), ldd(a_ref), ldd(lw_ref)
        pre = lw
        suf = lw
        for s in (1, 2, 4, 8):
            pre = pre + jnp.where(tpos >= s, pltpu.roll(pre, s, axis=0), 0.0)
            suf = suf + jnp.where(tpos < C - s, pltpu.roll(suf, R - s, axis=0), 0.0)
        fwd = d == 0
        cum = jnp.where(fwd, pre, suf)
        aft = jnp.where(fwd, suf, pre) - lw
        w_aft = jnp.exp(aft)
        inv_w = jnp.exp(-cum)
        a_hat = -kk * jnp.exp(cum - lw)
        b_hat = kk * a * inv_w
        k_hat = k * inv_w
        r_hat = r * jnp.exp(cum)
        lhs = jnp.concatenate([jnp.where(first, a_hat, 0.0), jnp.where(first, 0.0, a_hat),
                               jnp.where(first, r_hat, 0.0), jnp.where(first, 0.0, r_hat)], axis=0)
        rhs = jnp.concatenate([b_hat, k_hat], axis=0)
        m = _dotf(lhs, rhs, _NT)
        vb = v.astype(_BF)
        ts, rbk, akv = [], [], []
        for h2 in range(2):
            x = jnp.where(before, m[h2 * R:(h2 + 1) * R, 0:R], 0.0)
            ak = jnp.where(before, m[h2 * R:(h2 + 1) * R, R:2 * R], 0.0)
            rbk.append(jnp.where(jnp.concatenate([upto, upto], axis=1), m[(2 + h2) * R:(3 + h2) * R, :], 0.0))
            x2 = _dotf(x, x)
            x4 = _dotf(x2, x2)
            x8 = _dotf(x4, x4)
            t = eye + x
            t = t + _dotf(t, x2)
            t = t + _dotf(t, x4)
            t = t + _dotf(t, x8)
            ts.append(t)
            akv.append(_dotf(ak, vb))
        akv = jnp.where(first, akv[0], akv[1])
        rhs2 = jnp.concatenate([a_hat, akv], axis=1)
        ta = [_dotf(ts[h2], rhs2) for h2 in range(2)]
        at = jnp.where(first, ta[0][:, :R], ta[1][:, :R])
        u0 = jnp.where(first, ta[0][:, R:], ta[1][:, R:])
        rhs3 = jnp.concatenate([jnp.concatenate([at, u0], axis=1),
                                jnp.concatenate([jnp.zeros_like(v), v], axis=1)], axis=0)
        ry = [_dotf(rbk[h2], rhs3) for h2 in range(2)]
        rt = r_hat + jnp.where(first, ry[0][:, :R], ry[1][:, :R])
        y0 = jnp.where(first, ry[0][:, R:], ry[1][:, R:])
        at_ref[0, 0, j] = at.astype(_BF)
        rt_ref[0, 0, j] = rt.astype(_BF)
        bw_ref[0, 0, j] = (kk * a * w_aft).astype(_BF)
        kw_ref[0, 0, j] = (k * w_aft).astype(_BF)
        vv_ref[0, 0, j] = vb
        u0_ref[0, 0, j] = u0
        y0_ref[0, 0, j] = y0
        tot = cum + aft
        wc_ref[0, 0, j] = jnp.exp(jnp.concatenate([tot[p * C:p * C + 1] for p in range(RW_PAIRS)], axis=0))
        return carry

    lax.fori_loop(0, n_sub, chunk, 0)


def _rwkv_b_kernel(at_ref, rt_ref, bw_ref, kw_ref, vv_ref, u0_ref, y0_ref, wc_ref, y_ref, s_ref, *, n_sub):
    d = pl.program_id(0)
    C = RW_CHUNK

    @pl.when(pl.program_id(2) == 0)
    def _():
        s_ref[...] = jnp.zeros_like(s_ref)

    row = lax.broadcasted_iota(jnp.int32, (LANES, LANES), 0)
    col = lax.broadcasted_iota(jnp.int32, (LANES, LANES), 1)
    diag = (row // RW_DH) == (col // RW_DH)

    def chunk(jj, carry):
        cj = jnp.where(d == 0, jj, n_sub - 1 - jj)
        r0 = pl.multiple_of(cj * C, C)
        wc = wc_ref[0, 0, cj]
        for p in range(RW_PAIRS):
            rows = slice(p * C, (p + 1) * C)
            ar = jnp.concatenate([at_ref[0, 0, cj, rows, :], rt_ref[0, 0, cj, rows, :]], axis=0)
            sp = s_ref[p]
            z = lax.dot_general(ar, sp.astype(_BF), _NT, preferred_element_type=jnp.float32)
            u = z[:C] + u0_ref[0, 0, cj, rows, :]
            y_ref[0, 0, pl.ds(r0, C), p * LANES:(p + 1) * LANES] = z[C:] + y0_ref[0, 0, cj, rows, :]
            uv = jnp.concatenate([u.astype(_BF), vv_ref[0, 0, cj, rows, :]], axis=0)
            bk = jnp.concatenate([bw_ref[0, 0, cj, rows, :], kw_ref[0, 0, cj, rows, :]], axis=0)
            upd = lax.dot_general(uv, bk, _TN, preferred_element_type=jnp.float32)
            s_ref[p] = jnp.where(diag, wc[p:p + 1, :] * sp + upd, 0.0)
        return carry

    lax.fori_loop(0, n_sub, chunk, 0)


def _rwkv_pallas(r, v, kk, k_dir, a_dir, lw_dir, n_ctx):
    bsz, t_len, _ = r.shape
    nb, cb = t_len // RW_TBLK, n_ctx // RW_TBLK
    assert t_len % RW_TBLK == 0 and n_ctx % RW_TBLK == 0
    n_sub = RW_TBLK // RW_CHUNK
    nc = t_len // RW_CHUNK
    sh_spec = pl.BlockSpec((1, RW_TBLK, RW_W), lambda d, b, i: (b, _scan_block(d, i, nb, cb), 0))
    dr_spec = pl.BlockSpec((1, 1, RW_TBLK, RW_W), lambda d, b, i: (d, b, _scan_block(d, i, nb, cb), 0))
    ch_spec = pl.BlockSpec((1, 1, n_sub, RW_ROWS, LANES), lambda d, b, i: (d, b, _scan_block(d, i, nb, cb), 0, 0))
    wc_spec = pl.BlockSpec((1, 1, n_sub, RW_PAIRS, LANES), lambda d, b, i: (d, b, _scan_block(d, i, nb, cb), 0, 0))
    ch_shape = lambda dt: jax.ShapeDtypeStruct((N_DIR, bsz, nc, RW_ROWS, LANES), dt)
    params = pltpu.CompilerParams(dimension_semantics=("parallel", "parallel", "arbitrary"),
                                  vmem_limit_bytes=VMEM_LIMIT_BYTES)
    chunk_local = pl.pallas_call(
        functools.partial(_rwkv_a_kernel, n_sub=n_sub),
        grid=(N_DIR, bsz, nb),
        in_specs=[sh_spec, dr_spec, sh_spec, sh_spec, dr_spec, dr_spec],
        out_specs=[ch_spec] * 7 + [wc_spec],
        out_shape=[ch_shape(_BF)] * 5 + [ch_shape(jnp.float32)] * 2
                  + [jax.ShapeDtypeStruct((N_DIR, bsz, nc, RW_PAIRS, LANES), jnp.float32)],
        compiler_params=params, name="rwkv_chunk_local",
    )(r, k_dir, v, kk, a_dir, lw_dir)
    return pl.pallas_call(
        functools.partial(_rwkv_b_kernel, n_sub=n_sub),
        grid=(N_DIR, bsz, nb),
        in_specs=[ch_spec] * 7 + [wc_spec],
        out_specs=dr_spec,
        out_shape=jax.ShapeDtypeStruct((N_DIR, bsz, t_len, RW_W), jnp.float32),
        scratch_shapes=[pltpu.VMEM((RW_PAIRS, LANES, LANES), jnp.float32)],
        compiler_params=params, name="rwkv_state_scan",
    )(*chunk_local)


def _layer_norm(x, eps=LN_EPS):
    mu = jnp.mean(x, axis=-1, keepdims=True)
    var = jnp.mean(jnp.square(x - mu), axis=-1, keepdims=True)
    return (x - mu) * lax.rsqrt(var + eps)


def _modulate(x, shift, scale):
    return _layer_norm(x) * (1.0 + scale) + shift


def _post_norm(z, gain, bias):
    return _layer_norm(z) * gain + bias


def _head_norm(h, gain, bias, eps):
    y = _layer_norm(h, eps)
    return y.reshape(h.shape[:-2] + (-1,)) * gain + bias


def _split_cols(z, widths):
    return jnp.split(z, np.cumsum(widths)[:-1].tolist(), axis=-1)


def _depthwise_conv3x3(z, w, b):
    ch = z.shape[-1]
    y = lax.conv_general_dilated(z, w[:, :, None, :], window_strides=(1, 1), padding='SAME',
                                 dimension_numbers=('NHWC', 'HWIO', 'NHWC'), feature_group_count=ch)
    return y + b


def _centred_shift(z):
    zp = jnp.pad(z, ((0, 0), (1, 1), (0, 0)))
    return 0.5 * (zp[:, :-2] + zp[:, 2:])


def _ctx_lat(fn, z, n_ctx):
    return jnp.concatenate([fn(z[:, :n_ctx]), fn(z[:, n_ctx:])], axis=1)


def _mlstm_branch(seg, n_ctx, conv_w, conv_b, ig_b, fg_b, norm_g, norm_b):
    q, k, v, o, ig, fg = seg
    bsz, t_len, _ = q.shape

    def conv(z):
        length = z.shape[1]
        rows, cols = (1, length) if length == n_ctx else (length // GRID_W, GRID_W)
        y = _depthwise_conv3x3(z.reshape(bsz, rows, cols, 2 * ML_W), conv_w, conv_b)
        return jax.nn.silu(y).reshape(bsz, length, 2 * ML_W)

    qk = jnp.concatenate([conv(jnp.concatenate([q[:, :n_ctx], k[:, :n_ctx]], axis=-1)),
                          conv(jnp.concatenate([q[:, n_ctx:], k[:, n_ctx:]], axis=-1))], axis=1)
    h = _mlstm_pallas(qk[..., :ML_W], qk[..., ML_W:], v, jnp.concatenate([ig, fg], axis=-1), ig_b, fg_b, n_ctx)
    h = (h[0] + h[1]).reshape(bsz, t_len, ML_HEADS, ML_DH)
    return jax.nn.sigmoid(o) * _head_norm(h, norm_g, norm_b, ML_NORM_EPS)


def _rwkv7_branch(seg, n_ctx, mu, w0, w_up, a0, a_up, g_up, k_k, k_a, r_k, norm_g, norm_b):
    def hd(z):
        return z.reshape(z.shape[:-1] + (RW_HEADS, RW_DH))

    z = jnp.concatenate(seg, axis=-1)
    z = z + mu * (_ctx_lat(_centred_shift, z, n_ctx) - z)
    r, k, v, wd, ad, gd = _split_cols(z, RW_IN_WIDTHS)
    bsz, length, _ = r.shape
    wd = wd.reshape(bsz, length, N_DIR, RW_DECAY_RANK)
    ad = ad.reshape(bsz, length, N_DIR, RW_A_RANK)
    w_pre = w0 + jnp.einsum('bldr,drc->bldc', jnp.tanh(wd), w_up)
    log_decay = -jnp.exp(-jax.nn.softplus(-w_pre) - RW_DECAY_OFFSET)
    a = jax.nn.sigmoid(a0 + jnp.einsum('bldr,drc->bldc', ad, a_up))
    g = _mm_any(jax.nn.sigmoid(gd), g_up)
    kk = hd(k * k_k)
    kk = kk / jnp.maximum(jnp.sqrt(jnp.sum(jnp.square(kk), axis=-1, keepdims=True)), 1e-12)
    k_dir = k[:, :, None, :] * (1.0 + (a - 1.0) * k_a)
    bonus = jnp.sum(hd(r[:, :, None, :] * k_dir * r_k.reshape(-1)), axis=(2, 4))
    bonus = (bonus[..., None] * hd(v)).reshape(bsz, length, RW_W)
    y = _rwkv_pallas(r, v, kk.reshape(bsz, length, RW_W), jnp.moveaxis(k_dir, 2, 0), jnp.moveaxis(a, 2, 0),
                     jnp.moveaxis(log_decay, 2, 0), n_ctx)
    y = hd(y[0] + y[1])
    return (_head_norm(y, norm_g, norm_b, RW_NORM_EPS) + bonus) * g


def _token_mixer(u, n_ctx, w_in,
                 ml_conv_w, ml_conv_b, ml_ig_b, ml_fg_b, ml_norm_g, ml_norm_b, ml_proj,
                 rw_mu, rw_w0, rw_w_up, rw_a0, rw_a_up, rw_g_up, rw_k_k, rw_k_a, rw_r_k,
                 rw_norm_g, rw_norm_b, rw_proj,
                 s5_lam_re, s5_lam_im, s5_log_dt, s5_b_re, s5_b_im, s5_c_re, s5_c_im, s5_d,
                 s5_w_val, s5_w_gate, w_out):
    p = _split_cols(_mm_any(u, w_in), IN_WIDTHS)
    ml = _mlstm_branch(p[0:6], n_ctx, ml_conv_w, ml_conv_b, ml_ig_b, ml_fg_b, ml_norm_g, ml_norm_b)
    rw = _rwkv7_branch(p[6:12], n_ctx, rw_mu, rw_w0, rw_w_up, rw_a0, rw_a_up, rw_g_up,
                       rw_k_k, rw_k_a, rw_r_k, rw_norm_g, rw_norm_b)
    s5 = _s5_pallas(p[12], n_ctx, s5_lam_re, s5_lam_im, s5_log_dt, s5_b_re, s5_b_im, s5_c_re, s5_c_im, s5_d)
    gate_pre = p[13]
    g = jax.nn.sigmoid(gate_pre).reshape(gate_pre.shape[:-1] + (N_BRANCH, D_MODEL))
    z = (g[..., 0, :] * _mm_any(ml, ml_proj) + g[..., 1, :] * _mm_any(rw, rw_proj)
         + g[..., 2, :] * (_mm_any(s5, s5_w_val) * jax.nn.sigmoid(_mm_any(s5, s5_w_gate))))
    return _mm_any(z, w_out)


def _moe_ffn(u, router_w, router_b, w_gate, w_up, w_down):
    n_tok = u.shape[0]
    affinity = jax.nn.sigmoid(_mm_f32(u, router_w))
    grouped = (affinity + router_b).reshape(n_tok, N_GROUPS, EXPERTS_PER_GROUP)
    group_score = jnp.sum(lax.top_k(grouped, TOP_K)[0], axis=-1)
    best = jnp.argmax(group_score, axis=-1)
    keep = (best[:, None] == jnp.arange(N_GROUPS))[..., None]
    masked = jnp.where(keep, grouped, -jnp.inf).reshape(n_tok, N_EXPERTS)
    _, idx = lax.top_k(masked, TOP_K)
    wsel = jnp.take_along_axis(affinity, idx, axis=-1)
    wsel = wsel / jnp.sum(wsel, axis=-1, keepdims=True)
    gates = jnp.sum((idx[..., None] == jnp.arange(N_EXPERTS)).astype(jnp.float32) * wsel[..., None], axis=1)
    return _moe_dense(u, gates, w_gate.astype(_BF), w_up.astype(_BF), w_down.astype(_BF))


def kernel(x, c, ctx, c_ctx, ada_w, ada_b, w_in, ml_conv_w, ml_conv_b, ml_ig_b, ml_fg_b, ml_norm_g,
           ml_norm_b, ml_proj, rw_mu, rw_w0, rw_w_up, rw_a0, rw_a_up, rw_g_up, rw_k_k, rw_k_a, rw_r_k,
           rw_norm_g, rw_norm_b, rw_proj, s5_lam_re, s5_lam_im, s5_log_dt, s5_b_re, s5_b_im, s5_c_re,
           s5_c_im, s5_d, s5_w_val, s5_w_gate, w_out, ln1_g, ln1_b, ln2_g, ln2_b, router_w, router_b,
           exp_w_gate, exp_w_up, exp_w_down):
    n_ctx = ctx.shape[1]
    silu_c = jax.nn.silu(c)[:, None, :]
    silu_cc = jax.nn.silu(c_ctx)[None, None, :]
    xc = ctx
    for i in range(DEPTH):
        need_ctx = i < DEPTH - 1
        mx = jnp.split(_mm_any(silu_c, ada_w[i]) + ada_b[i], N_MOD, axis=-1)
        mc = jnp.split(_mm_any(silu_cc, ada_w[i]) + ada_b[i], N_MOD, axis=-1)
        u = jnp.concatenate([_modulate(xc, mc[0], mc[1]), _modulate(x, mx[0], mx[1])], axis=1)
        mix = _token_mixer(
            u, n_ctx, w_in[i],
            ml_conv_w[i], ml_conv_b[i], ml_ig_b[i], ml_fg_b[i], ml_norm_g[i], ml_norm_b[i], ml_proj[i],
            rw_mu[i], rw_w0[i], rw_w_up[i], rw_a0[i], rw_a_up[i], rw_g_up[i], rw_k_k[i], rw_k_a[i], rw_r_k[i],
            rw_norm_g[i], rw_norm_b[i], rw_proj[i],
            s5_lam_re[i], s5_lam_im[i], s5_log_dt[i], s5_b_re[i], s5_b_im[i], s5_c_re[i], s5_c_im[i], s5_d[i],
            s5_w_val[i], s5_w_gate[i], w_out[i])
        x = _post_norm(DEEPNORM_ALPHA * x + mx[2] * mix[:, n_ctx:], ln1_g[i], ln1_b[i])
        tokens = [_modulate(x, mx[3], mx[4]).reshape(-1, D_MODEL)]
        if need_ctx:
            xc = _post_norm(DEEPNORM_ALPHA * xc + mc[2] * mix[:, :n_ctx], ln1_g[i], ln1_b[i])
            tokens.append(_modulate(xc, mc[3], mc[4]).reshape(-1, D_MODEL))
        ffn = _moe_ffn(jnp.concatenate(tokens, axis=0), router_w, router_b, exp_w_gate[i], exp_w_up[i],
                       exp_w_down[i])
        n_lat = x.shape[0] * x.shape[1]
        x = _post_norm(DEEPNORM_ALPHA * x + mx[5] * ffn[:n_lat].reshape(x.shape), ln2_g[i], ln2_b[i])
        if need_ctx:
            xc = _post_norm(DEEPNORM_ALPHA * xc + mc[5] * ffn[n_lat:].reshape(xc.shape), ln2_g[i], ln2_b[i])
    return x
```

```python
import functools
import math

import jax
import jax.numpy as jnp
import numpy as np
from jax import lax
from jax.experimental import pallas as pl
from jax.experimental.pallas import tpu as pltpu

D_MODEL = 2048
DEPTH = 2
GRID_W = 64
N_DIR = 2
ML_HEADS = 4
ML_DH = 256
ML_W = ML_HEADS * ML_DH
ML_CHUNK = 64
ML_NORM_EPS = 1e-6
RW_HEADS = 16
RW_DH = 64
RW_W = RW_HEADS * RW_DH
RW_DECAY_RANK = 64
RW_A_RANK = 64
RW_G_RANK = 128
RW_DECAY_OFFSET = 0.5
RW_NORM_EPS = 64e-5
S5_W = 1024
S5_GROUP = 16
S5_GROUPS = S5_W // S5_GROUP
S5_STATE = 64
N_BRANCH = 3
N_GROUPS = 4
EXPERTS_PER_GROUP = 4
N_EXPERTS = N_GROUPS * EXPERTS_PER_GROUP
TOP_K = 2
D_EXPERT = 1024
DEEPNORM_ALPHA = (2.0 * DEPTH) ** 0.25
LN_EPS = 1e-5
N_MOD = 6
IN_WIDTHS = (ML_W, ML_W, ML_W, ML_W, N_DIR * ML_HEADS, N_DIR * ML_HEADS,
             RW_W, RW_W, RW_W, N_DIR * RW_DECAY_RANK, N_DIR * RW_A_RANK, RW_G_RANK,
             S5_W, N_BRANCH * D_MODEL)
D_IN = sum(IN_WIDTHS)
RW_IN_WIDTHS = (RW_W, RW_W, RW_W, N_DIR * RW_DECAY_RANK, N_DIR * RW_A_RANK, RW_G_RANK)

VMEM_LIMIT_BYTES = 56 * 1024 * 1024
LANES = 128

S5_CHUNK = 16
ML_TBLK = 256
RW_CHUNK = 16
RW_PAIRS = RW_HEADS // 2
RW_ROWS = RW_PAIRS * RW_CHUNK
RW_TBLK = 256
RW_GROUP = 8

_BF = jnp.bfloat16
_HI = lax.Precision.HIGHEST
_NT = (((1,), (1,)), ((), ()))
_TN = (((0,), (0,)), ((), ()))


def _scan_block(d, i, n_blocks, ctx_blocks):
    bwd = jnp.where(i < ctx_blocks, ctx_blocks - 1 - i, n_blocks - 1 + ctx_blocks - i)
    return jnp.where(d == 0, i, bwd)


def _mm_kernel(a_ref, w_ref, o_ref, abf_ref):
    @pl.when(pl.program_id(1) == 0)
    def _():
        abf_ref[...] = a_ref[...].astype(_BF)

    o_ref[...] = jnp.dot(abf_ref[...], w_ref[...], preferred_element_type=jnp.float32)


def _mm(a, w, tm, tn):
    m, k = a.shape
    n = w.shape[1]
    assert m % tm == 0 and n % tn == 0, (m, n, tm, tn)
    return pl.pallas_call(
        _mm_kernel,
        grid=(m // tm, n // tn),
        in_specs=[pl.BlockSpec((tm, k), lambda i, j: (i, 0)),
                  pl.BlockSpec((k, tn), lambda i, j: (0, j))],
        out_specs=pl.BlockSpec((tm, tn), lambda i, j: (i, j)),
        out_shape=jax.ShapeDtypeStruct((m, n), jnp.float32),
        scratch_shapes=[pltpu.VMEM((tm, k), _BF)],
        compiler_params=pltpu.CompilerParams(
            dimension_semantics=("parallel", "arbitrary"),
            vmem_limit_bytes=VMEM_LIMIT_BYTES),
        name="mm",
    )(a, w)


def _mm_any(a, w, tm=1024, tn=512):
    lead = a.shape[:-1]
    a2 = a.reshape(-1, a.shape[-1])
    m, n = a2.shape[0], w.shape[1]
    mp = -(-m // 8) * 8
    if mp > tm:
        mp = -(-m // tm) * tm
    tm = min(tm, mp)
    np_ = -(-n // LANES) * LANES
    if np_ > tn:
        np_ = -(-n // tn) * tn
    tn = min(tn, np_)
    if mp != m:
        a2 = jnp.pad(a2, ((0, mp - m), (0, 0)))
    wb = w.astype(_BF)
    if np_ != n:
        wb = jnp.pad(wb, ((0, 0), (0, np_ - n)))
    out = _mm(a2, wb, tm, tn)
    if mp != m or np_ != n:
        out = out[:m, :n]
    return out.reshape(lead + (n,))


def _moe_kernel(u_ref, g_ref, wg_ref, wu_ref, wd_ref, o_ref):
    e = pl.program_id(1)

    @pl.when(e == 0)
    def _():
        o_ref[...] = jnp.zeros_like(o_ref)

    u = u_ref[...].astype(_BF)
    hg = jnp.dot(u, wg_ref[0], preferred_element_type=jnp.float32)
    hu = jnp.dot(u, wu_ref[0], preferred_element_type=jnp.float32)
    h = (hg * jax.nn.sigmoid(hg)) * hu
    y = jnp.dot(h.astype(_BF), wd_ref[0], preferred_element_type=jnp.float32)
    gates = g_ref[...]
    col = lax.broadcasted_iota(jnp.int32, gates.shape, 1)
    ge = jnp.sum(jnp.where(col == e, gates, 0.0), axis=1, keepdims=True)
    o_ref[...] += ge * y


def _moe_dense(u, gates, wg, wu, wd, tm=512):
    n_tok, d = u.shape
    n_e, _, d_e = wg.shape
    assert n_tok % tm == 0
    return pl.pallas_call(
        _moe_kernel,
        grid=(n_tok // tm, n_e),
        in_specs=[pl.BlockSpec((tm, d), lambda i, e: (i, 0)),
                  pl.BlockSpec((tm, n_e), lambda i, e: (i, 0)),
                  pl.BlockSpec((1, d, d_e), lambda i, e: (e, 0, 0)),
                  pl.BlockSpec((1, d, d_e), lambda i, e: (e, 0, 0)),
                  pl.BlockSpec((1, d_e, d), lambda i, e: (e, 0, 0))],
        out_specs=pl.BlockSpec((tm, d), lambda i, e: (i, 0)),
        out_shape=jax.ShapeDtypeStruct((n_tok, d), jnp.float32),
        compiler_params=pltpu.CompilerParams(
            dimension_semantics=("parallel", "arbitrary"),
            vmem_limit_bytes=VMEM_LIMIT_BYTES),
        name="moe_dense",
    )(u, gates, wg, wu, wd)


def _router_kernel(u_ref, wt_ref, b_ref, g_ref):
    logits = lax.dot_general(wt_ref[...], u_ref[...], _NT, precision=_HI, preferred_element_type=jnp.float32)
    aff = jax.nn.sigmoid(logits)
    score = aff + b_ref[...]
    s = [score[e:e + 1] for e in range(N_EXPERTS)]
    a = [aff[e:e + 1] for e in range(N_EXPERTS)]
    gs = []
    for g in range(N_GROUPS):
        m = s[g * EXPERTS_PER_GROUP:(g + 1) * EXPERTS_PER_GROUP]
        best = None
        for i in range(EXPERTS_PER_GROUP):
            for j in range(i + 1, EXPERTS_PER_GROUP):
                best = m[i] + m[j] if best is None else jnp.maximum(best, m[i] + m[j])
        gs.append(best)
    best_val = gs[0]
    best_grp = jnp.zeros_like(gs[0], dtype=jnp.int32)
    for g in range(1, N_GROUPS):
        better = gs[g] > best_val
        best_grp = jnp.where(better, g, best_grp)
        best_val = jnp.where(better, gs[g], best_val)
    sel = []
    for e in range(N_EXPERTS):
        g = e // EXPERTS_PER_GROUP
        rank = jnp.zeros_like(best_grp)
        for j in range(g * EXPERTS_PER_GROUP, (g + 1) * EXPERTS_PER_GROUP):
            if j != e:
                ahead = (s[j] > s[e]) | ((s[j] == s[e]) & (j < e))
                rank = rank + ahead.astype(jnp.int32)
        sel.append((best_grp == g) & (rank < TOP_K))
    wsum = sum(jnp.where(sel[e], a[e], 0.0) for e in range(N_EXPERTS))
    g_ref[...] = jnp.concatenate([jnp.where(sel[e], a[e] / wsum, 0.0) for e in range(N_EXPERTS)], axis=0)


def _router_pallas(u, router_w, router_b, tm=512):
    n_tok, d = u.shape
    assert n_tok % tm == 0
    return pl.pallas_call(
        _router_kernel,
        grid=(n_tok // tm,),
        in_specs=[pl.BlockSpec((tm, d), lambda i: (i, 0)),
                  pl.BlockSpec((N_EXPERTS, d), lambda i: (0, 0)),
                  pl.BlockSpec((N_EXPERTS, 1), lambda i: (0, 0))],
        out_specs=pl.BlockSpec((N_EXPERTS, tm), lambda i: (0, i)),
        out_shape=jax.ShapeDtypeStruct((N_EXPERTS, n_tok), jnp.float32),
        compiler_params=pltpu.CompilerParams(dimension_semantics=("parallel",),
                                             vmem_limit_bytes=VMEM_LIMIT_BYTES),
        name="moe_router",
    )(u, router_w.T, router_b.reshape(N_EXPERTS, 1))


def _s5_mats(lam_re, lam_im, log_dt, b_re, b_im, c_re, c_im):
    L = S5_CHUNK
    lam = lax.complex(lam_re, lam_im)
    ldt = lam * jnp.exp(log_dt)[..., None]
    lam_bar = jnp.exp(ldt)
    b_bar = ((lam_bar - 1.0) / lam)[..., None] * lax.complex(b_re, b_im)
    c_mat = lax.complex(c_re, c_im)
    tau = jnp.arange(L + 1, dtype=jnp.float32)
    pw = jnp.exp(ldt[:, :, None, :] * tau[None, None, :, None])
    kern = jnp.real(jnp.einsum('dgon,dgtn,dgni->dgtoi', c_mat, pw[:, :, :L], b_bar))
    s_idx = jnp.arange(L)[:, None]
    t_idx = jnp.arange(L)[None, :]

    def toeplitz(k, lag, valid):
        m = k[:, jnp.clip(lag, 0, L - 1)] * valid[None, :, :, None, None]
        return jnp.transpose(m, (0, 1, 4, 2, 3)).reshape(-1, L * S5_GROUP, L * S5_GROUP)

    tsum = (toeplitz(kern[0], t_idx - s_idx, (t_idx >= s_idx).astype(jnp.float32))
            + toeplitz(kern[1], s_idx - t_idx, (s_idx >= t_idx).astype(jnp.float32)))
    pin_f = pw[0][:, L - 1 - jnp.arange(L)]
    pin_b = pw[1][:, jnp.arange(L)]
    in_f = jnp.einsum('gsn,gni->gsin', pin_f, b_bar[0]).reshape(-1, L * S5_GROUP, S5_STATE)
    in_b = jnp.einsum('gsn,gni->gsin', pin_b, b_bar[1]).reshape(-1, L * S5_GROUP, S5_STATE)
    icat = jnp.concatenate([jnp.real(in_f), jnp.imag(in_f), jnp.real(in_b), jnp.imag(in_b)], axis=-1)
    pout_f = pw[0][:, 1 + jnp.arange(L)]
    pout_b = pw[1][:, L - jnp.arange(L)]
    out_f = jnp.einsum('gon,gtn->gnto', c_mat[0], pout_f).reshape(-1, S5_STATE, L * S5_GROUP)
    out_b = jnp.einsum('gon,gtn->gnto', c_mat[1], pout_b).reshape(-1, S5_STATE, L * S5_GROUP)
    ocat = jnp.concatenate([jnp.real(out_f), -jnp.imag(out_f), jnp.real(out_b), -jnp.imag(out_b)], axis=1)
    lam_l = pw[:, :, L]
    lam_chunk = jnp.stack([jnp.real(lam_l[0]), jnp.imag(lam_l[0]), jnp.real(lam_l[1]), jnp.imag(lam_l[1])],
                          axis=1)
    return tsum, icat, ocat, lam_chunk


def _s5_kernel(u_ref, t_ref, i_ref, o_ref, lam_ref, d_ref, y_ref, v_ref, xfr_ref, xfi_ref, xbr_ref, xbi_ref,
               *, n_chunks, ctx_chunks, bsz):
    n = S5_STATE
    u = u_ref[0]
    ub = u.astype(_BF)
    v_ref[...] = jnp.dot(ub, i_ref[0], preferred_element_type=jnp.float32)
    lam = lam_ref[0]
    lfr = jnp.broadcast_to(lam[0:1], (bsz, n))
    lfi = jnp.broadcast_to(lam[1:2], (bsz, n))
    lbr = jnp.broadcast_to(lam[2:3], (bsz, n))
    lbi = jnp.broadcast_to(lam[3:4], (bsz, n))

    def cmul_add(lr, li, xr, xi, vr, vi):
        return lr * xr - li * xi + vr, lr * xi + li * xr + vi

    def step(j, carry):
        fr, fi, br, bi = carry
        rf = pl.multiple_of(j * (2 * bsz), 2 * bsz)
        pb = jnp.where(j < ctx_chunks // 2, ctx_chunks // 2 - 1 - j, (n_chunks + ctx_chunks) // 2 - 1 - j)
        rb = pl.multiple_of(pb * (2 * bsz), 2 * bsz)
        vf = v_ref[pl.ds(rf, 2 * bsz), :]
        vb = v_ref[pl.ds(rb, 2 * bsz), :]
        fr1, fi1 = cmul_add(lfr, lfi, fr, fi, vf[:bsz, 0:n], vf[:bsz, n:2 * n])
        fr2, fi2 = cmul_add(lfr, lfi, fr1, fi1, vf[bsz:, 0:n], vf[bsz:, n:2 * n])
        br1, bi1 = cmul_add(lbr, lbi, br, bi, vb[bsz:, 2 * n:3 * n], vb[bsz:, 3 * n:4 * n])
        br2, bi2 = cmul_add(lbr, lbi, br1, bi1, vb[:bsz, 2 * n:3 * n], vb[:bsz, 3 * n:4 * n])
        xfr_ref[pl.ds(rf, 2 * bsz), :] = jnp.concatenate([fr, fr1], axis=0)
        xfi_ref[pl.ds(rf, 2 * bsz), :] = jnp.concatenate([fi, fi1], axis=0)
        xbr_ref[pl.ds(rb, 2 * bsz), :] = jnp.concatenate([br1, br], axis=0)
        xbi_ref[pl.ds(rb, 2 * bsz), :] = jnp.concatenate([bi1, bi], axis=0)
        return fr2, fi2, br2, bi2

    z = jnp.zeros((bsz, n), jnp.float32)
    lax.fori_loop(0, n_chunks // 2, step, (z, z, z, z))
    o = o_ref[0]
    y = jnp.dot(ub, t_ref[0], preferred_element_type=jnp.float32)
    y += jnp.dot(xfr_ref[...].astype(_BF), o[0:n], preferred_element_type=jnp.float32)
    y += jnp.dot(xfi_ref[...].astype(_BF), o[n:2 * n], preferred_element_type=jnp.float32)
    y += jnp.dot(xbr_ref[...].astype(_BF), o[2 * n:3 * n], preferred_element_type=jnp.float32)
    y += jnp.dot(xbi_ref[...].astype(_BF), o[3 * n:4 * n], preferred_element_type=jnp.float32)
    y += d_ref[0] * u
    y_ref[0] = 0.5 * y * (1.0 + jnp.tanh(math.sqrt(2.0 / math.pi) * (y + 0.044715 * (y * y * y))))


def _s5_pallas(u, n_ctx, lam_re, lam_im, log_dt, b_re, b_im, c_re, c_im, d_skip):
    bsz, t_len, _ = u.shape
    L, G, C = S5_CHUNK, S5_GROUPS, S5_GROUP
    nc = t_len // L
    assert (2 * bsz) % 8 == 0 and nc % 2 == 0 and (n_ctx // L) % 2 == 0
    tsum, icat, ocat, lam_chunk = _s5_mats(lam_re, lam_im, log_dt, b_re, b_im, c_re, c_im)
    ug = jnp.transpose(u.reshape(bsz, nc, L, G, C), (3, 1, 0, 2, 4)).reshape(G, nc * bsz, L * C)
    dvec = jnp.tile(d_skip.reshape(G, 1, C), (1, L, 1)).reshape(G, 1, L * C)
    rows = nc * bsz
    wspec = pl.BlockSpec((1, L * C, L * C), lambda g: (g, 0, 0))
    yg = pl.pallas_call(
        functools.partial(_s5_kernel, n_chunks=nc, ctx_chunks=n_ctx // L, bsz=bsz),
        grid=(G,),
        in_specs=[pl.BlockSpec((1, rows, L * C), lambda g: (g, 0, 0)), wspec, wspec, wspec,
                  pl.BlockSpec((1, 4, S5_STATE), lambda g: (g, 0, 0)),
                  pl.BlockSpec((1, 1, L * C), lambda g: (g, 0, 0))],
        out_specs=pl.BlockSpec((1, rows, L * C), lambda g: (g, 0, 0)),
        out_shape=jax.ShapeDtypeStruct((G, rows, L * C), jnp.float32),
        scratch_shapes=[pltpu.VMEM((rows, 4 * S5_STATE), jnp.float32)]
                       + [pltpu.VMEM((rows, S5_STATE), jnp.float32)] * 4,
        compiler_params=pltpu.CompilerParams(dimension_semantics=("parallel",),
                                             vmem_limit_bytes=VMEM_LIMIT_BYTES),
        name="s5_scan",
    )(ug, tsum.astype(_BF), icat.astype(_BF), ocat.astype(_BF), lam_chunk, dvec)
    return jnp.transpose(yg.reshape(G, nc, bsz, L, C), (2, 1, 3, 0, 4)).reshape(bsz, t_len, S5_W)


def _log_sigmoid(x):
    return jnp.minimum(x, 0.0) - jnp.log(1.0 + jnp.exp(-jnp.abs(x)))


def _mlstm_kernel(igb_ref, fgb_ref, q_ref, k_ref, v_ref, gc_ref, gr_ref, h_ref, cmat_ref, nvec_ref, m_ref):
    d = pl.program_id(0)
    L = ML_CHUNK
    n_sub = ML_TBLK // L
    H = range(ML_HEADS)

    @pl.when(pl.program_id(2) == 0)
    def _():
        cmat_ref[...] = jnp.zeros_like(cmat_ref)
        nvec_ref[...] = jnp.zeros_like(nvec_ref)
        m_ref[...] = jnp.zeros_like(m_ref)

    igb = [igb_ref[d, h] for h in H]
    fgb = [fgb_ref[d, h] for h in H]
    row = lax.broadcasted_iota(jnp.int32, (L, L), 0)
    col = lax.broadcasted_iota(jnp.int32, (L, L), 1)
    sign = 1 - 2 * d
    seen = (row - col) * sign >= 0
    seen_f = seen.astype(jnp.float32)
    seen_t = ((col - row) * sign >= 0).astype(jnp.float32)
    scale = ML_DH ** -0.5

    def chunk(jj, carry):
        cj = jnp.where(d == 0, jj, n_sub - 1 - jj)
        r0 = pl.multiple_of(cj * L, L)
        hs = lambda h: slice(h * ML_DH, (h + 1) * ML_DH)
        q = [q_ref[0, pl.ds(r0, L), hs(h)] for h in H]
        k = [k_ref[0, pl.ds(r0, L), hs(h)] * scale for h in H]
        vb = [v_ref[0, pl.ds(r0, L), hs(h)].astype(_BF) for h in H]
        gc = [gc_ref[0, 0, h, pl.ds(r0, L), :] for h in H]
        gr = [gr_ref[0, 0, h, cj] for h in H]
        li_col = [gc[h][:, 0:1] + igb[h] for h in H]
        lf_col = [_log_sigmoid(gc[h][:, 1:2] + fgb[h]) for h in H]
        li_row = [gr[h][0:1, :] + igb[h] for h in H]
        lf_row = [_log_sigmoid(gr[h][1:2, :] + fgb[h]) for h in H]
        m_prev = [m_ref[h] for h in H]
        bcum_col = [jnp.dot(seen_f, jnp.broadcast_to(lf_col[h], (L, L)), precision=_HI,
                            preferred_element_type=jnp.float32) for h in H]
        bcum_row = [jnp.dot(jnp.broadcast_to(lf_row[h], (8, L)), seen_t, precision=_HI,
                            preferred_element_type=jnp.float32)[0:1] for h in H]
        qb = [q[h].astype(_BF) for h in H]
        qk = [lax.dot_general(qb[h], k[h].astype(_BF), _NT, preferred_element_type=jnp.float32) for h in H]
        qc = [jnp.dot(qb[h], cmat_ref[h].astype(_BF), preferred_element_type=jnp.float32) for h in H]
        log_d = [jnp.where(seen, bcum_col[h] - bcum_row[h] + li_row[h], -jnp.inf) for h in H]
        inter = [bcum_col[h][:, 0:1] + m_prev[h] for h in H]
        m_j = [jnp.maximum(jnp.max(log_d[h], axis=1, keepdims=True), inter[h]) for h in H]
        scores = [qk[h] * jnp.exp(log_d[h] - m_j[h]) for h in H]
        s_inter = [jnp.exp(inter[h] - m_j[h]) for h in H]
        sv = [jnp.dot(scores[h].astype(_BF), vb[h], preferred_element_type=jnp.float32) for h in H]
        b_last = [jnp.sum(lf_col[h], axis=0, keepdims=True) for h in H]
        log_w = [b_last[h] - bcum_col[h][:, 0:1] + li_col[h] for h in H]
        m_new = [jnp.maximum(b_last[h] + m_prev[h], jnp.max(log_w[h], axis=0, keepdims=True)) for h in H]
        kw = [k[h] * jnp.exp(log_w[h] - m_new[h]) for h in H]
        decay = [jnp.exp(b_last[h] + m_prev[h] - m_new[h]) for h in H]
        kv = [lax.dot_general(kw[h].astype(_BF), vb[h], _TN, preferred_element_type=jnp.float32) for h in H]
        for h in H:
            num = sv[h] + s_inter[h] * qc[h]
            den = (jnp.sum(scores[h], axis=1, keepdims=True)
                   + s_inter[h] * jnp.sum(q[h] * nvec_ref[h], axis=1, keepdims=True))
            h_ref[0, 0, pl.ds(r0, L), hs(h)] = num / jnp.maximum(jnp.abs(den), jnp.exp(-m_j[h]))
        for h in H:
            cmat_ref[h] = decay[h] * cmat_ref[h] + kv[h]
            nvec_ref[h] = decay[h] * nvec_ref[h] + jnp.sum(kw[h], axis=0, keepdims=True)
            m_ref[h] = m_new[h]
        return carry

    lax.fori_loop(0, n_sub, chunk, 0)


def _mlstm_pallas(q, k, v, gates, ig_b, fg_b, n_ctx):
    bsz, t_len, _ = q.shape
    nb = t_len // ML_TBLK
    cb = n_ctx // ML_TBLK
    assert t_len % ML_TBLK == 0 and n_ctx % ML_TBLK == 0
    g = gates.reshape(bsz, t_len, 2, N_DIR, ML_HEADS)
    gcol = jnp.transpose(g, (3, 0, 4, 1, 2))
    grow = jnp.transpose(g.reshape(bsz, t_len // ML_CHUNK, ML_CHUNK, 2, N_DIR, ML_HEADS),
                         (4, 0, 5, 1, 3, 2))
    blk = lambda d, i: _scan_block(d, i, nb, cb)
    grid_spec = pltpu.PrefetchScalarGridSpec(
        num_scalar_prefetch=2,
        grid=(N_DIR, bsz, nb),
        in_specs=[pl.BlockSpec((1, ML_TBLK, ML_W), lambda d, b, i, *_: (b, blk(d, i), 0))] * 3 + [
            pl.BlockSpec((1, 1, ML_HEADS, ML_TBLK, 2), lambda d, b, i, *_: (d, b, 0, blk(d, i), 0)),
            pl.BlockSpec((1, 1, ML_HEADS, ML_TBLK // ML_CHUNK, 2, ML_CHUNK),
                         lambda d, b, i, *_: (d, b, 0, blk(d, i), 0, 0))],
        out_specs=pl.BlockSpec((1, 1, ML_TBLK, ML_W), lambda d, b, i, *_: (d, b, blk(d, i), 0)),
        scratch_shapes=[pltpu.VMEM((ML_HEADS, ML_DH, ML_DH), jnp.float32),
                        pltpu.VMEM((ML_HEADS, 1, ML_DH), jnp.float32),
                        pltpu.VMEM((ML_HEADS, 1, 1), jnp.float32)])
    return pl.pallas_call(
        _mlstm_kernel, grid_spec=grid_spec,
        out_shape=jax.ShapeDtypeStruct((N_DIR, bsz, t_len, ML_W), jnp.float32),
        compiler_params=pltpu.CompilerParams(
            dimension_semantics=("parallel", "parallel", "arbitrary"),
            vmem_limit_bytes=VMEM_LIMIT_BYTES),
        name="mlstm_scan",
    )(ig_b, fg_b, q, k, v, gcol, grow)


def _to_pairs(x):
    return jnp.concatenate([x[:, p * LANES:(p + 1) * LANES] for p in range(RW_PAIRS)], axis=0)


def _dotf(a, b, dims=None):
    a = a.astype(_BF)
    b = b.astype(_BF)
    if dims is None:
        return jnp.dot(a, b, preferred_element_type=jnp.float32)
    return lax.dot_general(a, b, dims, preferred_element_type=jnp.float32)


def _rwkv_a_kernel(r_ref, k_ref, v_ref, kk_ref, a_ref, lw_ref,
                   att_ref, rt_ref, bw_ref, kw_ref, vt_ref, u0t_ref, y0_ref, wc_ref, *, n_sub):
    d = pl.program_id(0)
    C, R = RW_CHUNK, RW_ROWS
    row = lax.broadcasted_iota(jnp.int32, (R, R), 0)
    col = lax.broadcasted_iota(jnp.int32, (R, R), 1)
    same = (row // C) == (col // C)
    sign = 1 - 2 * d
    before = same & ((row - col) * sign > 0)
    upto = same & ((row - col) * sign >= 0)
    upto2 = jnp.concatenate([upto, upto], axis=1)
    eye = (row == col).astype(jnp.float32)
    first = col < RW_DH
    tpos = row % C

    def chunk_group(jg, carry):
        js = [jg * RW_GROUP + i for i in range(RW_GROUP)]
        G2 = [(i, h2) for i in range(RW_GROUP) for h2 in range(2)]
        pick = lambda lst, off: [jnp.where(first, lst[2 * i][:, off:off + R], lst[2 * i + 1][:, off:off + R])
                                 for i in range(RW_GROUP)]
        r0s = [pl.multiple_of(j * C, C) for j in js]
        ld = lambda ref: [_to_pairs(ref[0, pl.ds(r0, C), :]) for r0 in r0s]
        ldd = lambda ref: [_to_pairs(ref[0, 0, pl.ds(r0, C), :]) for r0 in r0s]
        r, v, kk = ld(r_ref), ld(v_ref), ld(kk_ref)
        k, a, lw = ldd(k_ref), ldd(a_ref), ldd(lw_ref)
        fwd = d == 0
        cum, aft = [], []
        for i in range(RW_GROUP):
            pre = lw[i]
            suf = lw[i]
            for s in (1, 2, 4, 8):
                pre = pre + jnp.where(tpos >= s, pltpu.roll(pre, s, axis=0), 0.0)
                suf = suf + jnp.where(tpos < C - s, pltpu.roll(suf, R - s, axis=0), 0.0)
            cum.append(jnp.where(fwd, pre, suf))
            aft.append(jnp.where(fwd, suf, pre) - lw[i])
        a_hat = [-kk[i] * jnp.exp(cum[i] - lw[i]) for i in range(RW_GROUP)]
        r_hat = [r[i] * jnp.exp(cum[i]) for i in range(RW_GROUP)]
        vb = [v[i].astype(_BF) for i in range(RW_GROUP)]
        m = []
        for i in range(RW_GROUP):
            inv_w = jnp.exp(-cum[i])
            lhs = jnp.concatenate([jnp.where(first, a_hat[i], 0.0), jnp.where(first, 0.0, a_hat[i]),
                                   jnp.where(first, r_hat[i], 0.0), jnp.where(first, 0.0, r_hat[i])], axis=0)
            rhs = jnp.concatenate([kk[i] * a[i] * inv_w, k[i] * inv_w], axis=0)
            m.append(_dotf(lhs, rhs, _NT))
        x = [jnp.where(before, m[i][h2 * R:(h2 + 1) * R, 0:R], 0.0).astype(_BF) for i, h2 in G2]
        ak = [jnp.where(before, m[i][h2 * R:(h2 + 1) * R, R:2 * R], 0.0) for i, h2 in G2]
        rbk = [jnp.where(upto2, m[i][(2 + h2) * R:(3 + h2) * R, :], 0.0).astype(_BF) for i, h2 in G2]
        akv = [_dotf(ak[g], vb[g // 2]) for g in range(len(G2))]
        x2 = [_dotf(xx, xx).astype(_BF) for xx in x]
        x4 = [_dotf(xx, xx).astype(_BF) for xx in x2]
        x8 = [_dotf(xx, xx).astype(_BF) for xx in x4]
        t = [eye + xx.astype(jnp.float32) for xx in x]
        t = [t[g] + _dotf(t[g], x2[g]) for g in range(len(G2))]
        t = [t[g] + _dotf(t[g], x4[g]) for g in range(len(G2))]
        t = [t[g] + _dotf(t[g], x8[g]) for g in range(len(G2))]
        akv = pick(akv, 0)
        rhs2 = [jnp.concatenate([a_hat[i], akv[i]], axis=1).astype(_BF) for i in range(RW_GROUP)]
        ta = [_dotf(t[g], rhs2[g // 2]) for g in range(len(G2))]
        at = pick(ta, 0)
        u0 = pick(ta, R)
        rhs3 = [jnp.concatenate([jnp.concatenate([at[i], u0[i]], axis=1).astype(_BF),
                                 jnp.concatenate([jnp.zeros_like(vb[i]), vb[i]], axis=1)], axis=0)
                for i in range(RW_GROUP)]
        ry = [_dotf(rbk[g], rhs3[g // 2]) for g in range(len(G2))]
        rt = pick(ry, 0)
        y0 = pick(ry, R)
        for i, j in enumerate(js):
            w_aft = jnp.exp(aft[i])
            att_ref[0, 0, j] = at[i].T.astype(_BF)
            u0t_ref[0, 0, j] = u0[i].T
            vt_ref[0, 0, j] = v[i].T.astype(_BF)
            rt_ref[0, 0, j] = (r_hat[i] + rt[i]).astype(_BF)
            bw_ref[0, 0, j] = (kk[i] * a[i] * w_aft).astype(_BF)
            kw_ref[0, 0, j] = (k[i] * w_aft).astype(_BF)
            y0_ref[0, 0, j] = y0[i]
            tot = cum[i] + aft[i]
            wc_ref[0, 0, j] = jnp.exp(jnp.concatenate([tot[p * C:p * C + 1] for p in range(RW_PAIRS)], axis=0))
        return carry

    lax.fori_loop(0, n_sub // RW_GROUP, chunk_group, 0)


def _rwkv_b_kernel(att_ref, rt_ref, bw_ref, kw_ref, vt_ref, u0t_ref, y0_ref, wc_ref, y_ref, s_ref, *, n_sub):
    d = pl.program_id(0)
    C = RW_CHUNK

    @pl.when(pl.program_id(2) == 0)
    def _():
        s_ref[...] = jnp.zeros_like(s_ref)

    row = lax.broadcasted_iota(jnp.int32, (LANES, LANES), 0)
    col = lax.broadcasted_iota(jnp.int32, (LANES, LANES), 1)
    diag = (row // RW_DH) == (col // RW_DH)
    pair_of_row = lax.broadcasted_iota(jnp.int32, (2 * RW_ROWS, LANES), 0) % RW_ROWS // C

    def chunk(jj, carry):
        cj = jnp.where(d == 0, jj, n_sub - 1 - jj)
        r0 = pl.multiple_of(cj * C, C)
        wc = wc_ref[0, 0, cj]
        att = att_ref[0, 0, cj]
        u0t = u0t_ref[0, 0, cj]
        vt = vt_ref[0, 0, cj]
        bk = jnp.concatenate([bw_ref[0, 0, cj], kw_ref[0, 0, cj]], axis=0)
        sps = [s_ref[p] for p in range(RW_PAIRS)]
        spb = [sp.astype(_BF) for sp in sps]
        uts = [jnp.dot(spb[p], att, preferred_element_type=jnp.float32) + u0t for p in range(RW_PAIRS)]
        for p in range(RW_PAIRS):
            rows = slice(p * C, (p + 1) * C)
            y = lax.dot_general(rt_ref[0, 0, cj, rows, :], spb[p], _NT, preferred_element_type=jnp.float32)
            y_ref[0, 0, pl.ds(r0, C), p * LANES:(p + 1) * LANES] = y + y0_ref[0, 0, cj, rows, :]
        for p in range(RW_PAIRS):
            lhs = jnp.concatenate([uts[p].astype(_BF), vt], axis=1)
            rhs = jnp.where(pair_of_row == p, bk, jnp.zeros_like(bk))
            upd = jnp.dot(lhs, rhs, preferred_element_type=jnp.float32)
            s_ref[p] = jnp.where(diag, wc[p:p + 1, :] * sps[p] + upd, 0.0)
        return carry

    lax.fori_loop(0, n_sub, chunk, 0)


def _rwkv_pallas(r, v, kk, k_dir, a_dir, lw_dir, n_ctx):
    bsz, t_len, _ = r.shape
    nb, cb = t_len // RW_TBLK, n_ctx // RW_TBLK
    assert t_len % RW_TBLK == 0 and n_ctx % RW_TBLK == 0
    n_sub = RW_TBLK // RW_CHUNK
    nc = t_len // RW_CHUNK
    sh_spec = pl.BlockSpec((1, RW_TBLK, RW_W), lambda d, b, i: (b, _scan_block(d, i, nb, cb), 0))
    dr_spec = pl.BlockSpec((1, 1, RW_TBLK, RW_W), lambda d, b, i: (d, b, _scan_block(d, i, nb, cb), 0))
    ch_spec = pl.BlockSpec((1, 1, n_sub, RW_ROWS, LANES), lambda d, b, i: (d, b, _scan_block(d, i, nb, cb), 0, 0))
    wc_spec = pl.BlockSpec((1, 1, n_sub, RW_PAIRS, LANES), lambda d, b, i: (d, b, _scan_block(d, i, nb, cb), 0, 0))
    ch_shape = lambda dt: jax.ShapeDtypeStruct((N_DIR, bsz, nc, RW_ROWS, LANES), dt)
    params = pltpu.CompilerParams(dimension_semantics=("parallel", "parallel", "arbitrary"),
                                  vmem_limit_bytes=VMEM_LIMIT_BYTES)
    chunk_local = pl.pallas_call(
        functools.partial(_rwkv_a_kernel, n_sub=n_sub),
        grid=(N_DIR, bsz, nb),
        in_specs=[sh_spec, dr_spec, sh_spec, sh_spec, dr_spec, dr_spec],
        out_specs=[ch_spec] * 7 + [wc_spec],
        out_shape=[ch_shape(_BF)] * 5 + [ch_shape(jnp.float32)] * 2
                  + [jax.ShapeDtypeStruct((N_DIR, bsz, nc, RW_PAIRS, LANES), jnp.float32)],
        compiler_params=params, name="rwkv_chunk_local",
    )(r, k_dir, v, kk, a_dir, lw_dir)
    return pl.pallas_call(
        functools.partial(_rwkv_b_kernel, n_sub=n_sub),
        grid=(N_DIR, bsz, nb),
        in_specs=[ch_spec] * 7 + [wc_spec],
        out_specs=dr_spec,
        out_shape=jax.ShapeDtypeStruct((N_DIR, bsz, t_len, RW_W), jnp.float32),
        scratch_shapes=[pltpu.VMEM((RW_PAIRS, LANES, LANES), jnp.float32)],
        compiler_params=params, name="rwkv_state_scan",
    )(*chunk_local)


def _layer_norm(x, eps=LN_EPS):
    mu = jnp.mean(x, axis=-1, keepdims=True)
    var = jnp.mean(jnp.square(x - mu), axis=-1, keepdims=True)
    return (x - mu) * lax.rsqrt(var + eps)


def _modulate(x, shift, scale):
    return _layer_norm(x) * (1.0 + scale) + shift


def _post_norm(z, gain, bias):
    return _layer_norm(z) * gain + bias


def _head_norm(h, gain, bias, eps):
    y = _layer_norm(h, eps)
    return y.reshape(h.shape[:-2] + (-1,)) * gain + bias


def _split_cols(z, widths):
    return jnp.split(z, np.cumsum(widths)[:-1].tolist(), axis=-1)


def _depthwise_conv3x3(z, w, b):
    ch = z.shape[-1]
    y = lax.conv_general_dilated(z, w[:, :, None, :], window_strides=(1, 1), padding='SAME',
                                 dimension_numbers=('NHWC', 'HWIO', 'NHWC'), feature_group_count=ch)
    return y + b


def _centred_shift(z):
    zp = jnp.pad(z, ((0, 0), (1, 1), (0, 0)))
    return 0.5 * (zp[:, :-2] + zp[:, 2:])


def _ctx_lat(fn, z, n_ctx):
    return jnp.concatenate([fn(z[:, :n_ctx]), fn(z[:, n_ctx:])], axis=1)


def _mlstm_branch(seg, n_ctx, conv_w, conv_b, ig_b, fg_b, norm_g, norm_b):
    q, k, v, o, ig, fg = seg
    bsz, t_len, _ = q.shape

    def conv(z):
        length = z.shape[1]
        rows, cols = (1, length) if length == n_ctx else (length // GRID_W, GRID_W)
        y = _depthwise_conv3x3(z.reshape(bsz, rows, cols, 2 * ML_W), conv_w, conv_b)
        return jax.nn.silu(y).reshape(bsz, length, 2 * ML_W)

    qk = jnp.concatenate([conv(jnp.concatenate([q[:, :n_ctx], k[:, :n_ctx]], axis=-1)),
                          conv(jnp.concatenate([q[:, n_ctx:], k[:, n_ctx:]], axis=-1))], axis=1)
    h = _mlstm_pallas(qk[..., :ML_W], qk[..., ML_W:], v, jnp.concatenate([ig, fg], axis=-1), ig_b, fg_b, n_ctx)
    h = (h[0] + h[1]).reshape(bsz, t_len, ML_HEADS, ML_DH)
    return jax.nn.sigmoid(o) * _head_norm(h, norm_g, norm_b, ML_NORM_EPS)


def _rwkv7_branch(seg, n_ctx, mu, w0, w_up, a0, a_up, g_up, k_k, k_a, r_k, norm_g, norm_b):
    def hd(z):
        return z.reshape(z.shape[:-1] + (RW_HEADS, RW_DH))

    z = jnp.concatenate(seg, axis=-1)
    z = z + mu * (_ctx_lat(_centred_shift, z, n_ctx) - z)
    r, k, v, wd, ad, gd = _split_cols(z, RW_IN_WIDTHS)
    bsz, length, _ = r.shape
    wd = wd.reshape(bsz, length, N_DIR, RW_DECAY_RANK)
    ad = ad.reshape(bsz, length, N_DIR, RW_A_RANK)
    w_pre = w0 + jnp.einsum('bldr,drc->bldc', jnp.tanh(wd), w_up)
    log_decay = -jnp.exp(-jax.nn.softplus(-w_pre) - RW_DECAY_OFFSET)
    a = jax.nn.sigmoid(a0 + jnp.einsum('bldr,drc->bldc', ad, a_up))
    g = _mm_any(jax.nn.sigmoid(gd), g_up)
    kk = hd(k * k_k)
    kk = kk / jnp.maximum(jnp.sqrt(jnp.sum(jnp.square(kk), axis=-1, keepdims=True)), 1e-12)
    k_dir = k[:, :, None, :] * (1.0 + (a - 1.0) * k_a)
    bonus = jnp.sum(hd(r[:, :, None, :] * k_dir * r_k.reshape(-1)), axis=(2, 4))
    bonus = (bonus[..., None] * hd(v)).reshape(bsz, length, RW_W)
    y = _rwkv_pallas(r, v, kk.reshape(bsz, length, RW_W), jnp.moveaxis(k_dir, 2, 0), jnp.moveaxis(a, 2, 0),
                     jnp.moveaxis(log_decay, 2, 0), n_ctx)
    y = hd(y[0] + y[1])
    return (_head_norm(y, norm_g, norm_b, RW_NORM_EPS) + bonus) * g


def _token_mixer(u, n_ctx, w_in,
                 ml_conv_w, ml_conv_b, ml_ig_b, ml_fg_b, ml_norm_g, ml_norm_b, ml_proj,
                 rw_mu, rw_w0, rw_w_up, rw_a0, rw_a_up, rw_g_up, rw_k_k, rw_k_a, rw_r_k,
                 rw_norm_g, rw_norm_b, rw_proj,
                 s5_lam_re, s5_lam_im, s5_log_dt, s5_b_re, s5_b_im, s5_c_re, s5_c_im, s5_d,
                 s5_w_val, s5_w_gate, w_out):
    p = _split_cols(_mm_any(u, w_in), IN_WIDTHS)
    ml = _mlstm_branch(p[0:6], n_ctx, ml_conv_w, ml_conv_b, ml_ig_b, ml_fg_b, ml_norm_g, ml_norm_b)
    rw = _rwkv7_branch(p[6:12], n_ctx, rw_mu, rw_w0, rw_w_up, rw_a0, rw_a_up, rw_g_up,
                       rw_k_k, rw_k_a, rw_r_k, rw_norm_g, rw_norm_b)
    s5 = _s5_pallas(p[12], n_ctx, s5_lam_re, s5_lam_im, s5_log_dt, s5_b_re, s5_b_im, s5_c_re, s5_c_im, s5_d)
    gate_pre = p[13]
    g = jax.nn.sigmoid(gate_pre).reshape(gate_pre.shape[:-1] + (N_BRANCH, D_MODEL))
    z = (g[..., 0, :] * _mm_any(ml, ml_proj) + g[..., 1, :] * _mm_any(rw, rw_proj)
         + g[..., 2, :] * (_mm_any(s5, s5_w_val) * jax.nn.sigmoid(_mm_any(s5, s5_w_gate))))
    return _mm_any(z, w_out)


def _moe_ffn(u, router_w, router_b, w_gate, w_up, w_down):
    gates = _router_pallas(u, router_w, router_b).T
    return _moe_dense(u, gates, w_gate.astype(_BF), w_up.astype(_BF), w_down.astype(_BF))


def kernel(x, c, ctx, c_ctx, ada_w, ada_b, w_in, ml_conv_w, ml_conv_b, ml_ig_b, ml_fg_b, ml_norm_g,
           ml_norm_b, ml_proj, rw_mu, rw_w0, rw_w_up, rw_a0, rw_a_up, rw_g_up, rw_k_k, rw_k_a, rw_r_k,
           rw_norm_g, rw_norm_b, rw_proj, s5_lam_re, s5_lam_im, s5_log_dt, s5_b_re, s5_b_im, s5_c_re,
           s5_c_im, s5_d, s5_w_val, s5_w_gate, w_out, ln1_g, ln1_b, ln2_g, ln2_b, router_w, router_b,
           exp_w_gate, exp_w_up, exp_w_down):
    n_ctx = ctx.shape[1]
    silu_c = jax.nn.silu(c)[:, None, :]
    silu_cc = jax.nn.silu(c_ctx)[None, None, :]
    xc = ctx
    for i in range(DEPTH):
        need_ctx = i < DEPTH - 1
        mx = jnp.split(_mm_any(silu_c, ada_w[i]) + ada_b[i], N_MOD, axis=-1)
        mc = jnp.split(_mm_any(silu_cc, ada_w[i]) + ada_b[i], N_MOD, axis=-1)
        u = jnp.concatenate([_modulate(xc, mc[0], mc[1]), _modulate(x, mx[0], mx[1])], axis=1)
        mix = _token_mixer(
            u, n_ctx, w_in[i],
            ml_conv_w[i], ml_conv_b[i], ml_ig_b[i], ml_fg_b[i], ml_norm_g[i], ml_norm_b[i], ml_proj[i],
            rw_mu[i], rw_w0[i], rw_w_up[i], rw_a0[i], rw_a_up[i], rw_g_up[i], rw_k_k[i], rw_k_a[i], rw_r_k[i],
            rw_norm_g[i], rw_norm_b[i], rw_proj[i],
            s5_lam_re[i], s5_lam_im[i], s5_log_dt[i], s5_b_re[i], s5_b_im[i], s5_c_re[i], s5_c_im[i], s5_d[i],
            s5_w_val[i], s5_w_gate[i], w_out[i])
        x = _post_norm(DEEPNORM_ALPHA * x + mx[2] * mix[:, n_ctx:], ln1_g[i], ln1_b[i])
        tokens = [_modulate(x, mx[3], mx[4]).reshape(-1, D_MODEL)]
        if need_ctx:
            xc = _post_norm(DEEPNORM_ALPHA * xc + mc[2] * mix[:, :n_ctx], ln1_g[i], ln1_b[i])
            tokens.append(_modulate(xc, mc[3], mc[4]).reshape(-1, D_MODEL))
        ffn = _moe_ffn(jnp.concatenate(tokens, axis=0), router_w, router_b, exp_w_gate[i], exp_w_up[i],
                       exp_w_down[i])
        n_lat = x.shape[0] * x.shape[1]
        x = _post_norm(DEEPNORM_ALPHA * x + mx[5] * ffn[:n_lat].reshape(x.shape), ln2_g[i], ln2_b[i])
        if need_ctx:
            xc = _post_norm(DEEPNORM_ALPHA * xc + mc[5] * ffn[n_lat:].reshape(xc.shape), ln2_g[i], ln2_b[i])
    return x
```

```python
import functools
import math

import jax
import jax.numpy as jnp
import numpy as np
from jax import lax
from jax.experimental import pallas as pl
from jax.experimental.pallas import tpu as pltpu

D_MODEL = 2048
DEPTH = 2
GRID_W = 64
N_DIR = 2
ML_HEADS = 4
ML_DH = 256
ML_W = ML_HEADS * ML_DH
ML_CHUNK = 64
ML_NORM_EPS = 1e-6
RW_HEADS = 16
RW_DH = 64
RW_W = RW_HEADS * RW_DH
RW_DECAY_RANK = 64
RW_A_RANK = 64
RW_G_RANK = 128
RW_DECAY_OFFSET = 0.5
RW_NORM_EPS = 64e-5
S5_W = 1024
S5_GROUP = 16
S5_GROUPS = S5_W // S5_GROUP
S5_STATE = 64
N_BRANCH = 3
N_GROUPS = 4
EXPERTS_PER_GROUP = 4
N_EXPERTS = N_GROUPS * EXPERTS_PER_GROUP
TOP_K = 2
D_EXPERT = 1024
DEEPNORM_ALPHA = (2.0 * DEPTH) ** 0.25
LN_EPS = 1e-5
N_MOD = 6
IN_WIDTHS = (ML_W, ML_W, ML_W, ML_W, N_DIR * ML_HEADS, N_DIR * ML_HEADS,
             RW_W, RW_W, RW_W, N_DIR * RW_DECAY_RANK, N_DIR * RW_A_RANK, RW_G_RANK,
             S5_W, N_BRANCH * D_MODEL)
D_IN = sum(IN_WIDTHS)
RW_IN_WIDTHS = (RW_W, RW_W, RW_W, N_DIR * RW_DECAY_RANK, N_DIR * RW_A_RANK, RW_G_RANK)

VMEM_LIMIT_BYTES = 56 * 1024 * 1024
LANES = 128

S5_CHUNK = 16
ML_TBLK = 256
RW_CHUNK = 16
RW_PAIRS = RW_HEADS // 2
RW_ROWS = RW_PAIRS * RW_CHUNK
RW_TBLK = 256
RW_GROUP = 8
MOE_TM = 256

_BF = jnp.bfloat16
_HI = lax.Precision.HIGHEST
_NT = (((1,), (1,)), ((), ()))
_TN = (((0,), (0,)), ((), ()))


def _scan_block(d, i, n_blocks, ctx_blocks):
    bwd = jnp.where(i < ctx_blocks, ctx_blocks - 1 - i, n_blocks - 1 + ctx_blocks - i)
    return jnp.where(d == 0, i, bwd)


def _mm_kernel(a_ref, w_ref, o_ref, abf_ref):
    @pl.when(pl.program_id(1) == 0)
    def _():
        abf_ref[...] = a_ref[...].astype(_BF)

    o_ref[...] = jnp.dot(abf_ref[...], w_ref[...], preferred_element_type=jnp.float32)


def _mm(a, w, tm, tn):
    m, k = a.shape
    n = w.shape[1]
    assert m % tm == 0 and n % tn == 0, (m, n, tm, tn)
    return pl.pallas_call(
        _mm_kernel,
        grid=(m // tm, n // tn),
        in_specs=[pl.BlockSpec((tm, k), lambda i, j: (i, 0)),
                  pl.BlockSpec((k, tn), lambda i, j: (0, j))],
        out_specs=pl.BlockSpec((tm, tn), lambda i, j: (i, j)),
        out_shape=jax.ShapeDtypeStruct((m, n), jnp.float32),
        scratch_shapes=[pltpu.VMEM((tm, k), _BF)],
        compiler_params=pltpu.CompilerParams(
            dimension_semantics=("parallel", "arbitrary"),
            vmem_limit_bytes=VMEM_LIMIT_BYTES),
        name="mm",
    )(a, w)


def _mm_any(a, w, tm=1024, tn=512):
    lead = a.shape[:-1]
    a2 = a.reshape(-1, a.shape[-1])
    m, n = a2.shape[0], w.shape[1]
    mp = -(-m // 8) * 8
    if mp > tm:
        mp = -(-m // tm) * tm
    tm = min(tm, mp)
    np_ = -(-n // LANES) * LANES
    if np_ > tn:
        np_ = -(-n // tn) * tn
    tn = min(tn, np_)
    if mp != m:
        a2 = jnp.pad(a2, ((0, mp - m), (0, 0)))
    wb = w.astype(_BF)
    if np_ != n:
        wb = jnp.pad(wb, ((0, 0), (0, np_ - n)))
    out = _mm(a2, wb, tm, tn)
    if mp != m or np_ != n:
        out = out[:m, :n]
    return out.reshape(lead + (n,))


def _moe_ffn_kernel(te_ref, tv_ref, x_ref, wg_ref, wu_ref, wd_ref, y_ref):
    t = pl.program_id(0)

    @pl.when(tv_ref[t] == 1)
    def _():
        x = x_ref[...].astype(_BF)
        hg = jnp.dot(x, wg_ref[0], preferred_element_type=jnp.float32)
        hu = jnp.dot(x, wu_ref[0], preferred_element_type=jnp.float32)
        h = (hg * jax.nn.sigmoid(hg)) * hu
        y_ref[...] = jnp.dot(h.astype(_BF), wd_ref[0], preferred_element_type=jnp.float32)

    @pl.when(tv_ref[t] == 0)
    def _():
        y_ref[...] = jnp.zeros_like(y_ref)


def _moe_routed(u, gates_t, wg, wu, wd):
    n_tok, d = u.shape
    n_e, _, d_e = wg.shape
    tm = MOE_TM
    n_tiles = (TOP_K * n_tok) // tm + n_e
    n_slots = n_tiles * tm
    sel = gates_t > 0.0
    seli = sel.astype(jnp.int32)
    rank = jnp.cumsum(seli, axis=1) - 1
    cnt = jnp.sum(seli, axis=1)
    tiles_e = (cnt + tm - 1) // tm
    tile_end = jnp.cumsum(tiles_e)
    off = (tile_end - tiles_e) * tm
    slot = off[:, None] + rank
    order = jnp.cumsum(seli, axis=0)
    slots, gsel = [], []
    for j in range(TOP_K):
        pick = sel & (order == j + 1)
        slots.append(jnp.sum(jnp.where(pick, slot, 0), axis=0))
        gsel.append(jnp.sum(jnp.where(pick, gates_t, 0.0), axis=0))
    tok = jnp.arange(n_tok, dtype=jnp.int32)
    tok_of_slot = jnp.zeros((n_slots,), jnp.int32).at[jnp.concatenate(slots)].set(jnp.tile(tok, TOP_K))
    tile_ids = jnp.arange(n_tiles, dtype=jnp.int32)
    tile_valid = (tile_ids < tile_end[-1]).astype(jnp.int32)
    tile_expert = jnp.minimum(jnp.searchsorted(tile_end, tile_ids, side='right'), n_e - 1).astype(jnp.int32)
    last_e = jnp.max(jnp.where(cnt > 0, jnp.arange(n_e), 0)).astype(jnp.int32)
    tile_expert = jnp.where(tile_valid == 1, tile_expert, last_e)
    xs = jnp.take(u, tok_of_slot, axis=0)
    grid_spec = pltpu.PrefetchScalarGridSpec(
        num_scalar_prefetch=2,
        grid=(n_tiles,),
        in_specs=[pl.BlockSpec((tm, d), lambda t, te, tv: (t, 0)),
                  pl.BlockSpec((1, d, d_e), lambda t, te, tv: (te[t], 0, 0)),
                  pl.BlockSpec((1, d, d_e), lambda t, te, tv: (te[t], 0, 0)),
                  pl.BlockSpec((1, d_e, d), lambda t, te, tv: (te[t], 0, 0))],
        out_specs=pl.BlockSpec((tm, d), lambda t, te, tv: (t, 0)))
    ys = pl.pallas_call(
        _moe_ffn_kernel, grid_spec=grid_spec,
        out_shape=jax.ShapeDtypeStruct((n_slots, d), jnp.float32),
        compiler_params=pltpu.CompilerParams(dimension_semantics=("arbitrary",),
                                             vmem_limit_bytes=VMEM_LIMIT_BYTES),
        name="moe_routed_ffn",
    )(tile_expert, tile_valid, xs, wg, wu, wd)
    return sum(gsel[j][:, None] * jnp.take(ys, slots[j], axis=0) for j in range(TOP_K))


def _router_kernel(u_ref, wt_ref, b_ref, g_ref):
    logits = lax.dot_general(wt_ref[...], u_ref[...], _NT, precision=_HI, preferred_element_type=jnp.float32)
    aff = jax.nn.sigmoid(logits)
    score = aff + b_ref[...]
    s = [score[e:e + 1] for e in range(N_EXPERTS)]
    a = [aff[e:e + 1] for e in range(N_EXPERTS)]
    gs = []
    for g in range(N_GROUPS):
        m = s[g * EXPERTS_PER_GROUP:(g + 1) * EXPERTS_PER_GROUP]
        best = None
        for i in range(EXPERTS_PER_GROUP):
            for j in range(i + 1, EXPERTS_PER_GROUP):
                best = m[i] + m[j] if best is None else jnp.maximum(best, m[i] + m[j])
        gs.append(best)
    best_val = gs[0]
    best_grp = jnp.zeros_like(gs[0], dtype=jnp.int32)
    for g in range(1, N_GROUPS):
        better = gs[g] > best_val
        best_grp = jnp.where(better, g, best_grp)
        best_val = jnp.where(better, gs[g], best_val)
    sel = []
    for e in range(N_EXPERTS):
        g = e // EXPERTS_PER_GROUP
        rank = jnp.zeros_like(best_grp)
        for j in range(g * EXPERTS_PER_GROUP, (g + 1) * EXPERTS_PER_GROUP):
            if j != e:
                ahead = (s[j] > s[e]) | ((s[j] == s[e]) & (j < e))
                rank = rank + ahead.astype(jnp.int32)
        sel.append((best_grp == g) & (rank < TOP_K))
    wsum = sum(jnp.where(sel[e], a[e], 0.0) for e in range(N_EXPERTS))
    g_ref[...] = jnp.concatenate([jnp.where(sel[e], a[e] / wsum, 0.0) for e in range(N_EXPERTS)], axis=0)


def _router_pallas(u, router_w, router_b, tm=512):
    n_tok, d = u.shape
    assert n_tok % tm == 0
    return pl.pallas_call(
        _router_kernel,
        grid=(n_tok // tm,),
        in_specs=[pl.BlockSpec((tm, d), lambda i: (i, 0)),
                  pl.BlockSpec((N_EXPERTS, d), lambda i: (0, 0)),
                  pl.BlockSpec((N_EXPERTS, 1), lambda i: (0, 0))],
        out_specs=pl.BlockSpec((N_EXPERTS, tm), lambda i: (0, i)),
        out_shape=jax.ShapeDtypeStruct((N_EXPERTS, n_tok), jnp.float32),
        compiler_params=pltpu.CompilerParams(dimension_semantics=("parallel",),
                                             vmem_limit_bytes=VMEM_LIMIT_BYTES),
        name="moe_router",
    )(u, router_w.T, router_b.reshape(N_EXPERTS, 1))


def _s5_mats(lam_re, lam_im, log_dt, b_re, b_im, c_re, c_im):
    L = S5_CHUNK
    lam = lax.complex(lam_re, lam_im)
    ldt = lam * jnp.exp(log_dt)[..., None]
    lam_bar = jnp.exp(ldt)
    b_bar = ((lam_bar - 1.0) / lam)[..., None] * lax.complex(b_re, b_im)
    c_mat = lax.complex(c_re, c_im)
    tau = jnp.arange(L + 1, dtype=jnp.float32)
    pw = jnp.exp(ldt[:, :, None, :] * tau[None, None, :, None])
    kern = jnp.real(jnp.einsum('dgon,dgtn,dgni->dgtoi', c_mat, pw[:, :, :L], b_bar))
    s_idx = jnp.arange(L)[:, None]
    t_idx = jnp.arange(L)[None, :]

    def toeplitz(k, lag, valid):
        m = k[:, jnp.clip(lag, 0, L - 1)] * valid[None, :, :, None, None]
        return jnp.transpose(m, (0, 1, 4, 2, 3)).reshape(-1, L * S5_GROUP, L * S5_GROUP)

    tsum = (toeplitz(kern[0], t_idx - s_idx, (t_idx >= s_idx).astype(jnp.float32))
            + toeplitz(kern[1], s_idx - t_idx, (s_idx >= t_idx).astype(jnp.float32)))
    pin_f = pw[0][:, L - 1 - jnp.arange(L)]
    pin_b = pw[1][:, jnp.arange(L)]
    in_f = jnp.einsum('gsn,gni->gsin', pin_f, b_bar[0]).reshape(-1, L * S5_GROUP, S5_STATE)
    in_b = jnp.einsum('gsn,gni->gsin', pin_b, b_bar[1]).reshape(-1, L * S5_GROUP, S5_STATE)
    icat = jnp.concatenate([jnp.real(in_f), jnp.imag(in_f), jnp.real(in_b), jnp.imag(in_b)], axis=-1)
    pout_f = pw[0][:, 1 + jnp.arange(L)]
    pout_b = pw[1][:, L - jnp.arange(L)]
    out_f = jnp.einsum('gon,gtn->gnto', c_mat[0], pout_f).reshape(-1, S5_STATE, L * S5_GROUP)
    out_b = jnp.einsum('gon,gtn->gnto', c_mat[1], pout_b).reshape(-1, S5_STATE, L * S5_GROUP)
    ocat = jnp.concatenate([jnp.real(out_f), -jnp.imag(out_f), jnp.real(out_b), -jnp.imag(out_b)], axis=1)
    lam_l = pw[:, :, L]
    lam_chunk = jnp.stack([jnp.real(lam_l[0]), jnp.imag(lam_l[0]), jnp.real(lam_l[1]), jnp.imag(lam_l[1])],
                          axis=1)
    return tsum, icat, ocat, lam_chunk


def _s5_kernel(u_ref, t_ref, i_ref, o_ref, lam_ref, d_ref, y_ref, v_ref, xfr_ref, xfi_ref, xbr_ref, xbi_ref,
               *, n_chunks, ctx_chunks, bsz):
    n = S5_STATE
    u = u_ref[0]
    ub = u.astype(_BF)
    v_ref[...] = jnp.dot(ub, i_ref[0], preferred_element_type=jnp.float32)
    lam = lam_ref[0]
    lfr = jnp.broadcast_to(lam[0:1], (bsz, n))
    lfi = jnp.broadcast_to(lam[1:2], (bsz, n))
    lbr = jnp.broadcast_to(lam[2:3], (bsz, n))
    lbi = jnp.broadcast_to(lam[3:4], (bsz, n))

    def cmul_add(lr, li, xr, xi, vr, vi):
        return lr * xr - li * xi + vr, lr * xi + li * xr + vi

    def step(j, carry):
        fr, fi, br, bi = carry
        rf = pl.multiple_of(j * (2 * bsz), 2 * bsz)
        pb = jnp.where(j < ctx_chunks // 2, ctx_chunks // 2 - 1 - j, (n_chunks + ctx_chunks) // 2 - 1 - j)
        rb = pl.multiple_of(pb * (2 * bsz), 2 * bsz)
        vf = v_ref[pl.ds(rf, 2 * bsz), :]
        vb = v_ref[pl.ds(rb, 2 * bsz), :]
        fr1, fi1 = cmul_add(lfr, lfi, fr, fi, vf[:bsz, 0:n], vf[:bsz, n:2 * n])
        fr2, fi2 = cmul_add(lfr, lfi, fr1, fi1, vf[bsz:, 0:n], vf[bsz:, n:2 * n])
        br1, bi1 = cmul_add(lbr, lbi, br, bi, vb[bsz:, 2 * n:3 * n], vb[bsz:, 3 * n:4 * n])
        br2, bi2 = cmul_add(lbr, lbi, br1, bi1, vb[:bsz, 2 * n:3 * n], vb[:bsz, 3 * n:4 * n])
        xfr_ref[pl.ds(rf, 2 * bsz), :] = jnp.concatenate([fr, fr1], axis=0)
        xfi_ref[pl.ds(rf, 2 * bsz), :] = jnp.concatenate([fi, fi1], axis=0)
        xbr_ref[pl.ds(rb, 2 * bsz), :] = jnp.concatenate([br1, br], axis=0)
        xbi_ref[pl.ds(rb, 2 * bsz), :] = jnp.concatenate([bi1, bi], axis=0)
        return fr2, fi2, br2, bi2

    z = jnp.zeros((bsz, n), jnp.float32)
    lax.fori_loop(0, n_chunks // 2, step, (z, z, z, z))
    o = o_ref[0]
    y = jnp.dot(ub, t_ref[0], preferred_element_type=jnp.float32)
    y += jnp.dot(xfr_ref[...].astype(_BF), o[0:n], preferred_element_type=jnp.float32)
    y += jnp.dot(xfi_ref[...].astype(_BF), o[n:2 * n], preferred_element_type=jnp.float32)
    y += jnp.dot(xbr_ref[...].astype(_BF), o[2 * n:3 * n], preferred_element_type=jnp.float32)
    y += jnp.dot(xbi_ref[...].astype(_BF), o[3 * n:4 * n], preferred_element_type=jnp.float32)
    y += d_ref[0] * u
    y_ref[0] = 0.5 * y * (1.0 + jnp.tanh(math.sqrt(2.0 / math.pi) * (y + 0.044715 * (y * y * y))))


def _s5_pallas(u, n_ctx, lam_re, lam_im, log_dt, b_re, b_im, c_re, c_im, d_skip):
    bsz, t_len, _ = u.shape
    L, G, C = S5_CHUNK, S5_GROUPS, S5_GROUP
    nc = t_len // L
    assert (2 * bsz) % 8 == 0 and nc % 2 == 0 and (n_ctx // L) % 2 == 0
    tsum, icat, ocat, lam_chunk = _s5_mats(lam_re, lam_im, log_dt, b_re, b_im, c_re, c_im)
    ug = jnp.transpose(u.reshape(bsz, nc, L, G, C), (3, 1, 0, 2, 4)).reshape(G, nc * bsz, L * C)
    dvec = jnp.tile(d_skip.reshape(G, 1, C), (1, L, 1)).reshape(G, 1, L * C)
    rows = nc * bsz
    wspec = pl.BlockSpec((1, L * C, L * C), lambda g: (g, 0, 0))
    yg = pl.pallas_call(
        functools.partial(_s5_kernel, n_chunks=nc, ctx_chunks=n_ctx // L, bsz=bsz),
        grid=(G,),
        in_specs=[pl.BlockSpec((1, rows, L * C), lambda g: (g, 0, 0)), wspec, wspec, wspec,
                  pl.BlockSpec((1, 4, S5_STATE), lambda g: (g, 0, 0)),
                  pl.BlockSpec((1, 1, L * C), lambda g: (g, 0, 0))],
        out_specs=pl.BlockSpec((1, rows, L * C), lambda g: (g, 0, 0)),
        out_shape=jax.ShapeDtypeStruct((G, rows, L * C), jnp.float32),
        scratch_shapes=[pltpu.VMEM((rows, 4 * S5_STATE), jnp.float32)]
                       + [pltpu.VMEM((rows, S5_STATE), jnp.float32)] * 4,
        compiler_params=pltpu.CompilerParams(dimension_semantics=("parallel",),
                                             vmem_limit_bytes=VMEM_LIMIT_BYTES),
        name="s5_scan",
    )(ug, tsum.astype(_BF), icat.astype(_BF), ocat.astype(_BF), lam_chunk, dvec)
    return jnp.transpose(yg.reshape(G, nc, bsz, L, C), (2, 1, 3, 0, 4)).reshape(bsz, t_len, S5_W)


def _log_sigmoid(x):
    return jnp.minimum(x, 0.0) - jnp.log(1.0 + jnp.exp(-jnp.abs(x)))


def _mlstm_kernel(igb_ref, fgb_ref, q_ref, k_ref, v_ref, gc_ref, gr_ref, h_ref, cmat_ref, nvec_ref, m_ref):
    d = pl.program_id(0)
    L = ML_CHUNK
    n_sub = ML_TBLK // L
    H = range(ML_HEADS)

    @pl.when(pl.program_id(2) == 0)
    def _():
        cmat_ref[...] = jnp.zeros_like(cmat_ref)
        nvec_ref[...] = jnp.zeros_like(nvec_ref)
        m_ref[...] = jnp.zeros_like(m_ref)

    igb = [igb_ref[d, h] for h in H]
    fgb = [fgb_ref[d, h] for h in H]
    row = lax.broadcasted_iota(jnp.int32, (L, L), 0)
    col = lax.broadcasted_iota(jnp.int32, (L, L), 1)
    sign = 1 - 2 * d
    seen = (row - col) * sign >= 0
    seen_f = seen.astype(jnp.float32)
    seen_t = ((col - row) * sign >= 0).astype(jnp.float32)
    scale = ML_DH ** -0.5

    def chunk(jj, carry):
        cj = jnp.where(d == 0, jj, n_sub - 1 - jj)
        r0 = pl.multiple_of(cj * L, L)
        hs = lambda h: slice(h * ML_DH, (h + 1) * ML_DH)
        q = [q_ref[0, pl.ds(r0, L), hs(h)] for h in H]
        k = [k_ref[0, pl.ds(r0, L), hs(h)] * scale for h in H]
        vb = [v_ref[0, pl.ds(r0, L), hs(h)].astype(_BF) for h in H]
        gc = [gc_ref[0, 0, h, pl.ds(r0, L), :] for h in H]
        gr = [gr_ref[0, 0, h, cj] for h in H]
        li_col = [gc[h][:, 0:1] + igb[h] for h in H]
        lf_col = [_log_sigmoid(gc[h][:, 1:2] + fgb[h]) for h in H]
        li_row = [gr[h][0:1, :] + igb[h] for h in H]
        lf_row = [_log_sigmoid(gr[h][1:2, :] + fgb[h]) for h in H]
        m_prev = [m_ref[h] for h in H]
        bcum_col = [jnp.dot(seen_f, jnp.broadcast_to(lf_col[h], (L, L)), precision=_HI,
                            preferred_element_type=jnp.float32) for h in H]
        bcum_row = [jnp.dot(jnp.broadcast_to(lf_row[h], (8, L)), seen_t, precision=_HI,
                            preferred_element_type=jnp.float32)[0:1] for h in H]
        qb = [q[h].astype(_BF) for h in H]
        qk = [lax.dot_general(qb[h], k[h].astype(_BF), _NT, preferred_element_type=jnp.float32) for h in H]
        qc = [jnp.dot(qb[h], cmat_ref[h].astype(_BF), preferred_element_type=jnp.float32) for h in H]
        log_d = [jnp.where(seen, bcum_col[h] - bcum_row[h] + li_row[h], -jnp.inf) for h in H]
        inter = [bcum_col[h][:, 0:1] + m_prev[h] for h in H]
        m_j = [jnp.maximum(jnp.max(log_d[h], axis=1, keepdims=True), inter[h]) for h in H]
        scores = [qk[h] * jnp.exp(log_d[h] - m_j[h]) for h in H]
        s_inter = [jnp.exp(inter[h] - m_j[h]) for h in H]
        sv = [jnp.dot(scores[h].astype(_BF), vb[h], preferred_element_type=jnp.float32) for h in H]
        b_last = [jnp.sum(lf_col[h], axis=0, keepdims=True) for h in H]
        log_w = [b_last[h] - bcum_col[h][:, 0:1] + li_col[h] for h in H]
        m_new = [jnp.maximum(b_last[h] + m_prev[h], jnp.max(log_w[h], axis=0, keepdims=True)) for h in H]
        kw = [k[h] * jnp.exp(log_w[h] - m_new[h]) for h in H]
        decay = [jnp.exp(b_last[h] + m_prev[h] - m_new[h]) for h in H]
        kv = [lax.dot_general(kw[h].astype(_BF), vb[h], _TN, preferred_element_type=jnp.float32) for h in H]
        for h in H:
            num = sv[h] + s_inter[h] * qc[h]
            den = (jnp.sum(scores[h], axis=1, keepdims=True)
                   + s_inter[h] * jnp.sum(q[h] * nvec_ref[h], axis=1, keepdims=True))
            h_ref[0, 0, pl.ds(r0, L), hs(h)] = num / jnp.maximum(jnp.abs(den), jnp.exp(-m_j[h]))
        for h in H:
            cmat_ref[h] = decay[h] * cmat_ref[h] + kv[h]
            nvec_ref[h] = decay[h] * nvec_ref[h] + jnp.sum(kw[h], axis=0, keepdims=True)
            m_ref[h] = m_new[h]
        return carry

    lax.fori_loop(0, n_sub, chunk, 0)


def _mlstm_pallas(q, k, v, gates, ig_b, fg_b, n_ctx):
    bsz, t_len, _ = q.shape
    nb = t_len // ML_TBLK
    cb = n_ctx // ML_TBLK
    assert t_len % ML_TBLK == 0 and n_ctx % ML_TBLK == 0
    g = gates.reshape(bsz, t_len, 2, N_DIR, ML_HEADS)
    gcol = jnp.transpose(g, (3, 0, 4, 1, 2))
    grow = jnp.transpose(g.reshape(bsz, t_len // ML_CHUNK, ML_CHUNK, 2, N_DIR, ML_HEADS),
                         (4, 0, 5, 1, 3, 2))
    blk = lambda d, i: _scan_block(d, i, nb, cb)
    grid_spec = pltpu.PrefetchScalarGridSpec(
        num_scalar_prefetch=2,
        grid=(N_DIR, bsz, nb),
        in_specs=[pl.BlockSpec((1, ML_TBLK, ML_W), lambda d, b, i, *_: (b, blk(d, i), 0))] * 3 + [
            pl.BlockSpec((1, 1, ML_HEADS, ML_TBLK, 2), lambda d, b, i, *_: (d, b, 0, blk(d, i), 0)),
            pl.BlockSpec((1, 1, ML_HEADS, ML_TBLK // ML_CHUNK, 2, ML_CHUNK),
                         lambda d, b, i, *_: (d, b, 0, blk(d, i), 0, 0))],
        out_specs=pl.BlockSpec((1, 1, ML_TBLK, ML_W), lambda d, b, i, *_: (d, b, blk(d, i), 0)),
        scratch_shapes=[pltpu.VMEM((ML_HEADS, ML_DH, ML_DH), jnp.float32),
                        pltpu.VMEM((ML_HEADS, 1, ML_DH), jnp.float32),
                        pltpu.VMEM((ML_HEADS, 1, 1), jnp.float32)])
    return pl.pallas_call(
        _mlstm_kernel, grid_spec=grid_spec,
        out_shape=jax.ShapeDtypeStruct((N_DIR, bsz, t_len, ML_W), jnp.float32),
        compiler_params=pltpu.CompilerParams(
            dimension_semantics=("parallel", "parallel", "arbitrary"),
            vmem_limit_bytes=VMEM_LIMIT_BYTES),
        name="mlstm_scan",
    )(ig_b, fg_b, q, k, v, gcol, grow)


def _to_pairs(x):
    return jnp.concatenate([x[:, p * LANES:(p + 1) * LANES] for p in range(RW_PAIRS)], axis=0)


def _dotf(a, b, dims=None):
    a = a.astype(_BF)
    b = b.astype(_BF)
    if dims is None:
        return jnp.dot(a, b, preferred_element_type=jnp.float32)
    return lax.dot_general(a, b, dims, preferred_element_type=jnp.float32)


def _rwkv_a_kernel(r_ref, k_ref, v_ref, kk_ref, a_ref, lw_ref,
                   att_ref, rt_ref, bw_ref, kw_ref, vt_ref, u0t_ref, y0_ref, wc_ref, *, n_sub):
    d = pl.program_id(0)
    C, R = RW_CHUNK, RW_ROWS
    row = lax.broadcasted_iota(jnp.int32, (R, R), 0)
    col = lax.broadcasted_iota(jnp.int32, (R, R), 1)
    same = (row // C) == (col // C)
    sign = 1 - 2 * d
    before = same & ((row - col) * sign > 0)
    upto = same & ((row - col) * sign >= 0)
    upto2 = jnp.concatenate([upto, upto], axis=1)
    eye = (row == col).astype(jnp.float32)
    first = col < RW_DH
    tpos = row % C

    def chunk_group(jg, carry):
        js = [jg * RW_GROUP + i for i in range(RW_GROUP)]
        G2 = [(i, h2) for i in range(RW_GROUP) for h2 in range(2)]
        pick = lambda lst, off: [jnp.where(first, lst[2 * i][:, off:off + R], lst[2 * i + 1][:, off:off + R])
                                 for i in range(RW_GROUP)]
        r0s = [pl.multiple_of(j * C, C) for j in js]
        ld = lambda ref: [_to_pairs(ref[0, pl.ds(r0, C), :]) for r0 in r0s]
        ldd = lambda ref: [_to_pairs(ref[0, 0, pl.ds(r0, C), :]) for r0 in r0s]
        r, v, kk = ld(r_ref), ld(v_ref), ld(kk_ref)
        k, a, lw = ldd(k_ref), ldd(a_ref), ldd(lw_ref)
        fwd = d == 0
        cum, aft = [], []
        for i in range(RW_GROUP):
            pre = lw[i]
            suf = lw[i]
            for s in (1, 2, 4, 8):
                pre = pre + jnp.where(tpos >= s, pltpu.roll(pre, s, axis=0), 0.0)
                suf = suf + jnp.where(tpos < C - s, pltpu.roll(suf, R - s, axis=0), 0.0)
            cum.append(jnp.where(fwd, pre, suf))
            aft.append(jnp.where(fwd, suf, pre) - lw[i])
        a_hat = [-kk[i] * jnp.exp(cum[i] - lw[i]) for i in range(RW_GROUP)]
        r_hat = [r[i] * jnp.exp(cum[i]) for i in range(RW_GROUP)]
        vb = [v[i].astype(_BF) for i in range(RW_GROUP)]
        m = []
        for i in range(RW_GROUP):
            inv_w = jnp.exp(-cum[i])
            lhs = jnp.concatenate([jnp.where(first, a_hat[i], 0.0), jnp.where(first, 0.0, a_hat[i]),
                                   jnp.where(first, r_hat[i], 0.0), jnp.where(first, 0.0, r_hat[i])], axis=0)
            rhs = jnp.concatenate([kk[i] * a[i] * inv_w, k[i] * inv_w], axis=0)
            m.append(_dotf(lhs, rhs, _NT))
        x = [jnp.where(before, m[i][h2 * R:(h2 + 1) * R, 0:R], 0.0).astype(_BF) for i, h2 in G2]
        ak = [jnp.where(before, m[i][h2 * R:(h2 + 1) * R, R:2 * R], 0.0) for i, h2 in G2]
        rbk = [jnp.where(upto2, m[i][(2 + h2) * R:(3 + h2) * R, :], 0.0).astype(_BF) for i, h2 in G2]
        akv = [_dotf(ak[g], vb[g // 2]) for g in range(len(G2))]
        x2 = [_dotf(xx, xx).astype(_BF) for xx in x]
        x4 = [_dotf(xx, xx).astype(_BF) for xx in x2]
        x8 = [_dotf(xx, xx).astype(_BF) for xx in x4]
        t = [eye + xx.astype(jnp.float32) for xx in x]
        t = [t[g] + _dotf(t[g], x2[g]) for g in range(len(G2))]
        t = [t[g] + _dotf(t[g], x4[g]) for g in range(len(G2))]
        t = [t[g] + _dotf(t[g], x8[g]) for g in range(len(G2))]
        akv = pick(akv, 0)
        rhs2 = [jnp.concatenate([a_hat[i], akv[i]], axis=1).astype(_BF) for i in range(RW_GROUP)]
        ta = [_dotf(t[g], rhs2[g // 2]) for g in range(len(G2))]
        at = pick(ta, 0)
        u0 = pick(ta, R)
        rhs3 = [jnp.concatenate([jnp.concatenate([at[i], u0[i]], axis=1).astype(_BF),
                                 jnp.concatenate([jnp.zeros_like(vb[i]), vb[i]], axis=1)], axis=0)
                for i in range(RW_GROUP)]
        ry = [_dotf(rbk[g], rhs3[g // 2]) for g in range(len(G2))]
        rt = pick(ry, 0)
        y0 = pick(ry, R)
        for i, j in enumerate(js):
            w_aft = jnp.exp(aft[i])
            att_ref[0, 0, j] = at[i].T.astype(_BF)
            u0t_ref[0, 0, j] = u0[i].T
            vt_ref[0, 0, j] = v[i].T.astype(_BF)
            rt_ref[0, 0, j] = (r_hat[i] + rt[i]).astype(_BF)
            bw_ref[0, 0, j] = (kk[i] * a[i] * w_aft).astype(_BF)
            kw_ref[0, 0, j] = (k[i] * w_aft).astype(_BF)
            y0_ref[0, 0, j] = y0[i]
            tot = cum[i] + aft[i]
            wc_ref[0, 0, j] = jnp.exp(jnp.concatenate([tot[p * C:p * C + 1] for p in range(RW_PAIRS)], axis=0))
        return carry

    lax.fori_loop(0, n_sub // RW_GROUP, chunk_group, 0)


def _rwkv_b_kernel(att_ref, rt_ref, bw_ref, kw_ref, vt_ref, u0t_ref, y0_ref, wc_ref, y_ref, s_ref, *, n_sub):
    d = pl.program_id(0)
    C = RW_CHUNK

    @pl.when(pl.program_id(2) == 0)
    def _():
        s_ref[...] = jnp.zeros_like(s_ref)

    row = lax.broadcasted_iota(jnp.int32, (LANES, LANES), 0)
    col = lax.broadcasted_iota(jnp.int32, (LANES, LANES), 1)
    diag = (row // RW_DH) == (col // RW_DH)
    pair_of_row = lax.broadcasted_iota(jnp.int32, (2 * RW_ROWS, LANES), 0) % RW_ROWS // C

    def chunk(jj, carry):
        cj = jnp.where(d == 0, jj, n_sub - 1 - jj)
        r0 = pl.multiple_of(cj * C, C)
        wc = wc_ref[0, 0, cj]
        att = att_ref[0, 0, cj]
        u0t = u0t_ref[0, 0, cj]
        vt = vt_ref[0, 0, cj]
        bk = jnp.concatenate([bw_ref[0, 0, cj], kw_ref[0, 0, cj]], axis=0)
        sps = [s_ref[p] for p in range(RW_PAIRS)]
        spb = [sp.astype(_BF) for sp in sps]
        uts = [jnp.dot(spb[p], att, preferred_element_type=jnp.float32) + u0t for p in range(RW_PAIRS)]
        for p in range(RW_PAIRS):
            rows = slice(p * C, (p + 1) * C)
            y = lax.dot_general(rt_ref[0, 0, cj, rows, :], spb[p], _NT, preferred_element_type=jnp.float32)
            y_ref[0, 0, pl.ds(r0, C), p * LANES:(p + 1) * LANES] = y + y0_ref[0, 0, cj, rows, :]
        for p in range(RW_PAIRS):
            lhs = jnp.concatenate([uts[p].astype(_BF), vt], axis=1)
            rhs = jnp.where(pair_of_row == p, bk, jnp.zeros_like(bk))
            upd = jnp.dot(lhs, rhs, preferred_element_type=jnp.float32)
            s_ref[p] = jnp.where(diag, wc[p:p + 1, :] * sps[p] + upd, 0.0)
        return carry

    lax.fori_loop(0, n_sub, chunk, 0)


def _rwkv_pallas(r, v, kk, k_dir, a_dir, lw_dir, n_ctx):
    bsz, t_len, _ = r.shape
    nb, cb = t_len // RW_TBLK, n_ctx // RW_TBLK
    assert t_len % RW_TBLK == 0 and n_ctx % RW_TBLK == 0
    n_sub = RW_TBLK // RW_CHUNK
    nc = t_len // RW_CHUNK
    sh_spec = pl.BlockSpec((1, RW_TBLK, RW_W), lambda d, b, i: (b, _scan_block(d, i, nb, cb), 0))
    dr_spec = pl.BlockSpec((1, 1, RW_TBLK, RW_W), lambda d, b, i: (d, b, _scan_block(d, i, nb, cb), 0))
    ch_spec = pl.BlockSpec((1, 1, n_sub, RW_ROWS, LANES), lambda d, b, i: (d, b, _scan_block(d, i, nb, cb), 0, 0))
    wc_spec = pl.BlockSpec((1, 1, n_sub, RW_PAIRS, LANES), lambda d, b, i: (d, b, _scan_block(d, i, nb, cb), 0, 0))
    ch_shape = lambda dt: jax.ShapeDtypeStruct((N_DIR, bsz, nc, RW_ROWS, LANES), dt)
    params = pltpu.CompilerParams(dimension_semantics=("parallel", "parallel", "arbitrary"),
                                  vmem_limit_bytes=VMEM_LIMIT_BYTES)
    chunk_local = pl.pallas_call(
        functools.partial(_rwkv_a_kernel, n_sub=n_sub),
        grid=(N_DIR, bsz, nb),
        in_specs=[sh_spec, dr_spec, sh_spec, sh_spec, dr_spec, dr_spec],
        out_specs=[ch_spec] * 7 + [wc_spec],
        out_shape=[ch_shape(_BF)] * 5 + [ch_shape(jnp.float32)] * 2
                  + [jax.ShapeDtypeStruct((N_DIR, bsz, nc, RW_PAIRS, LANES), jnp.float32)],
        compiler_params=params, name="rwkv_chunk_local",
    )(r, k_dir, v, kk, a_dir, lw_dir)
    return pl.pallas_call(
        functools.partial(_rwkv_b_kernel, n_sub=n_sub),
        grid=(N_DIR, bsz, nb),
        in_specs=[ch_spec] * 7 + [wc_spec],
        out_specs=dr_spec,
        out_shape=jax.ShapeDtypeStruct((N_DIR, bsz, t_len, RW_W), jnp.float32),
        scratch_shapes=[pltpu.VMEM((RW_PAIRS, LANES, LANES), jnp.float32)],
        compiler_params=params, name="rwkv_state_scan",
    )(*chunk_local)


def _layer_norm(x, eps=LN_EPS):
    mu = jnp.mean(x, axis=-1, keepdims=True)
    var = jnp.mean(jnp.square(x - mu), axis=-1, keepdims=True)
    return (x - mu) * lax.rsqrt(var + eps)


def _modulate(x, shift, scale):
    return _layer_norm(x) * (1.0 + scale) + shift


def _post_norm(z, gain, bias):
    return _layer_norm(z) * gain + bias


def _head_norm(h, gain, bias, eps):
    y = _layer_norm(h, eps)
    return y.reshape(h.shape[:-2] + (-1,)) * gain + bias


def _split_cols(z, widths):
    return jnp.split(z, np.cumsum(widths)[:-1].tolist(), axis=-1)


def _depthwise_conv3x3(z, w, b):
    ch = z.shape[-1]
    y = lax.conv_general_dilated(z, w[:, :, None, :], window_strides=(1, 1), padding='SAME',
                                 dimension_numbers=('NHWC', 'HWIO', 'NHWC'), feature_group_count=ch)
    return y + b


def _centred_shift(z):
    zp = jnp.pad(z, ((0, 0), (1, 1), (0, 0)))
    return 0.5 * (zp[:, :-2] + zp[:, 2:])


def _ctx_lat(fn, z, n_ctx):
    return jnp.concatenate([fn(z[:, :n_ctx]), fn(z[:, n_ctx:])], axis=1)


def _mlstm_branch(seg, n_ctx, conv_w, conv_b, ig_b, fg_b, norm_g, norm_b):
    q, k, v, o, ig, fg = seg
    bsz, t_len, _ = q.shape

    def conv(z):
        length = z.shape[1]
        rows, cols = (1, length) if length == n_ctx else (length // GRID_W, GRID_W)
        y = _depthwise_conv3x3(z.reshape(bsz, rows, cols, 2 * ML_W), conv_w, conv_b)
        return jax.nn.silu(y).reshape(bsz, length, 2 * ML_W)

    qk = jnp.concatenate([conv(jnp.concatenate([q[:, :n_ctx], k[:, :n_ctx]], axis=-1)),
                          conv(jnp.concatenate([q[:, n_ctx:], k[:, n_ctx:]], axis=-1))], axis=1)
    h = _mlstm_pallas(qk[..., :ML_W], qk[..., ML_W:], v, jnp.concatenate([ig, fg], axis=-1), ig_b, fg_b, n_ctx)
    h = (h[0] + h[1]).reshape(bsz, t_len, ML_HEADS, ML_DH)
    return jax.nn.sigmoid(o) * _head_norm(h, norm_g, norm_b, ML_NORM_EPS)


def _rwkv7_branch(seg, n_ctx, mu, w0, w_up, a0, a_up, g_up, k_k, k_a, r_k, norm_g, norm_b):
    def hd(z):
        return z.reshape(z.shape[:-1] + (RW_HEADS, RW_DH))

    z = jnp.concatenate(seg, axis=-1)
    z = z + mu * (_ctx_lat(_centred_shift, z, n_ctx) - z)
    r, k, v, wd, ad, gd = _split_cols(z, RW_IN_WIDTHS)
    bsz, length, _ = r.shape
    wd = wd.reshape(bsz, length, N_DIR, RW_DECAY_RANK)
    ad = ad.reshape(bsz, length, N_DIR, RW_A_RANK)
    w_pre = w0[:, None, None, :] + jnp.einsum('bldr,drc->dblc', jnp.tanh(wd), w_up)
    log_decay = -jnp.exp(-jax.nn.softplus(-w_pre) - RW_DECAY_OFFSET)
    a = jax.nn.sigmoid(a0[:, None, None, :] + jnp.einsum('bldr,drc->dblc', ad, a_up))
    g = _mm_any(jax.nn.sigmoid(gd), g_up)
    kk = hd(k * k_k)
    kk = kk / jnp.maximum(jnp.sqrt(jnp.sum(jnp.square(kk), axis=-1, keepdims=True)), 1e-12)
    k_dir = k[None] * (1.0 + (a - 1.0) * k_a)
    bonus = jnp.sum(hd(r[None] * k_dir * r_k.reshape(-1)), axis=(0, 4))
    bonus = (bonus[..., None] * hd(v)).reshape(bsz, length, RW_W)
    y = _rwkv_pallas(r, v, kk.reshape(bsz, length, RW_W), k_dir, a, log_decay, n_ctx)
    y = hd(y[0] + y[1])
    return (_head_norm(y, norm_g, norm_b, RW_NORM_EPS) + bonus) * g


def _token_mixer(u, n_ctx, w_in,
                 ml_conv_w, ml_conv_b, ml_ig_b, ml_fg_b, ml_norm_g, ml_norm_b, ml_proj,
                 rw_mu, rw_w0, rw_w_up, rw_a0, rw_a_up, rw_g_up, rw_k_k, rw_k_a, rw_r_k,
                 rw_norm_g, rw_norm_b, rw_proj,
                 s5_lam_re, s5_lam_im, s5_log_dt, s5_b_re, s5_b_im, s5_c_re, s5_c_im, s5_d,
                 s5_w_val, s5_w_gate, w_out):
    p = _split_cols(_mm_any(u, w_in), IN_WIDTHS)
    ml = _mlstm_branch(p[0:6], n_ctx, ml_conv_w, ml_conv_b, ml_ig_b, ml_fg_b, ml_norm_g, ml_norm_b)
    rw = _rwkv7_branch(p[6:12], n_ctx, rw_mu, rw_w0, rw_w_up, rw_a0, rw_a_up, rw_g_up,
                       rw_k_k, rw_k_a, rw_r_k, rw_norm_g, rw_norm_b)
    s5 = _s5_pallas(p[12], n_ctx, s5_lam_re, s5_lam_im, s5_log_dt, s5_b_re, s5_b_im, s5_c_re, s5_c_im, s5_d)
    gate_pre = p[13]
    g = jax.nn.sigmoid(gate_pre).reshape(gate_pre.shape[:-1] + (N_BRANCH, D_MODEL))
    z = (g[..., 0, :] * _mm_any(ml, ml_proj) + g[..., 1, :] * _mm_any(rw, rw_proj)
         + g[..., 2, :] * (_mm_any(s5, s5_w_val) * jax.nn.sigmoid(_mm_any(s5, s5_w_gate))))
    return _mm_any(z, w_out)


def _moe_ffn(u, router_w, router_b, w_gate, w_up, w_down):
    gates_t = _router_pallas(u, router_w, router_b)
    return _moe_routed(u, gates_t, w_gate.astype(_BF), w_up.astype(_BF), w_down.astype(_BF))


def kernel(x, c, ctx, c_ctx, ada_w, ada_b, w_in, ml_conv_w, ml_conv_b, ml_ig_b, ml_fg_b, ml_norm_g,
           ml_norm_b, ml_proj, rw_mu, rw_w0, rw_w_up, rw_a0, rw_a_up, rw_g_up, rw_k_k, rw_k_a, rw_r_k,
           rw_norm_g, rw_norm_b, rw_proj, s5_lam_re, s5_lam_im, s5_log_dt, s5_b_re, s5_b_im, s5_c_re,
           s5_c_im, s5_d, s5_w_val, s5_w_gate, w_out, ln1_g, ln1_b, ln2_g, ln2_b, router_w, router_b,
           exp_w_gate, exp_w_up, exp_w_down):
    n_ctx = ctx.shape[1]
    silu_c = jax.nn.silu(c)[:, None, :]
    silu_cc = jax.nn.silu(c_ctx)[None, None, :]
    xc = ctx
    for i in range(DEPTH):
        need_ctx = i < DEPTH - 1
        mx = jnp.split(_mm_any(silu_c, ada_w[i]) + ada_b[i], N_MOD, axis=-1)
        mc = jnp.split(_mm_any(silu_cc, ada_w[i]) + ada_b[i], N_MOD, axis=-1)
        u = jnp.concatenate([_modulate(xc, mc[0], mc[1]), _modulate(x, mx[0], mx[1])], axis=1)
        mix = _token_mixer(
            u, n_ctx, w_in[i],
            ml_conv_w[i], ml_conv_b[i], ml_ig_b[i], ml_fg_b[i], ml_norm_g[i], ml_norm_b[i], ml_proj[i],
            rw_mu[i], rw_w0[i], rw_w_up[i], rw_a0[i], rw_a_up[i], rw_g_up[i], rw_k_k[i], rw_k_a[i], rw_r_k[i],
            rw_norm_g[i], rw_norm_b[i], rw_proj[i],
            s5_lam_re[i], s5_lam_im[i], s5_log_dt[i], s5_b_re[i], s5_b_im[i], s5_c_re[i], s5_c_im[i], s5_d[i],
            s5_w_val[i], s5_w_gate[i], w_out[i])
        x = _post_norm(DEEPNORM_ALPHA * x + mx[2] * mix[:, n_ctx:], ln1_g[i], ln1_b[i])
        tokens = [_modulate(x, mx[3], mx[4]).reshape(-1, D_MODEL)]
        if need_ctx:
            xc = _post_norm(DEEPNORM_ALPHA * xc + mc[2] * mix[:, :n_ctx], ln1_g[i], ln1_b[i])
            tokens.append(_modulate(xc, mc[3], mc[4]).reshape(-1, D_MODEL))
        ffn = _moe_ffn(jnp.concatenate(tokens, axis=0), router_w, router_b, exp_w_gate[i], exp_w_up[i],
                       exp_w_down[i])
        n_lat = x.shape[0] * x.shape[1]
        x = _post_norm(DEEPNORM_ALPHA * x + mx[5] * ffn[:n_lat].reshape(x.shape), ln2_g[i], ln2_b[i])
        if need_ctx:
            xc = _post_norm(DEEPNORM_ALPHA * xc + mc[5] * ffn[n_lat:].reshape(xc.shape), ln2_g[i], ln2_b[i])
    return x
```

```python
import functools
import math

import jax
import jax.numpy as jnp
import numpy as np
from jax import lax
from jax.experimental import pallas as pl
from jax.experimental.pallas import tpu as pltpu

D_MODEL = 2048
DEPTH = 2
GRID_W = 64
N_DIR = 2
ML_HEADS = 4
ML_DH = 256
ML_W = ML_HEADS * ML_DH
ML_CHUNK = 64
ML_NORM_EPS = 1e-6
RW_HEADS = 16
RW_DH = 64
RW_W = RW_HEADS * RW_DH
RW_DECAY_RANK = 64
RW_A_RANK = 64
RW_G_RANK = 128
RW_DECAY_OFFSET = 0.5
RW_NORM_EPS = 64e-5
S5_W = 1024
S5_GROUP = 16
S5_GROUPS = S5_W // S5_GROUP
S5_STATE = 64
N_BRANCH = 3
N_GROUPS = 4
EXPERTS_PER_GROUP = 4
N_EXPERTS = N_GROUPS * EXPERTS_PER_GROUP
TOP_K = 2
D_EXPERT = 1024
DEEPNORM_ALPHA = (2.0 * DEPTH) ** 0.25
LN_EPS = 1e-5
N_MOD = 6
IN_WIDTHS = (ML_W, ML_W, ML_W, ML_W, N_DIR * ML_HEADS, N_DIR * ML_HEADS,
             RW_W, RW_W, RW_W, N_DIR * RW_DECAY_RANK, N_DIR * RW_A_RANK, RW_G_RANK,
             S5_W, N_BRANCH * D_MODEL)
D_IN = sum(IN_WIDTHS)
RW_IN_WIDTHS = (RW_W, RW_W, RW_W, N_DIR * RW_DECAY_RANK, N_DIR * RW_A_RANK, RW_G_RANK)

VMEM_LIMIT_BYTES = 56 * 1024 * 1024
LANES = 128

S5_CHUNK = 16
ML_TBLK = 256
RW_CHUNK = 16
RW_PAIRS = RW_HEADS // 2
RW_ROWS = RW_PAIRS * RW_CHUNK
RW_TBLK = 256
RW_GROUP = 8
MOE_TM = 256
ROW_TM = 256

P_QK, P_V, P_O = 0, 2 * ML_W, 3 * ML_W
P_RW = 4 * ML_W
P_S5 = P_RW + 3 * RW_W
P_GATE = P_S5 + S5_W
P_RWLR = P_GATE + N_BRANCH * D_MODEL
P_MLG = P_RWLR + 2 * N_DIR * RW_DECAY_RANK + RW_G_RANK
assert P_GATE % D_MODEL == 0 and P_MLG + 2 * N_DIR * ML_HEADS == D_IN and RW_DECAY_RANK == RW_A_RANK

_BF = jnp.bfloat16
_HI = lax.Precision.HIGHEST
_NT = (((1,), (1,)), ((), ()))
_TN = (((0,), (0,)), ((), ()))


def _scan_block(d, i, n_blocks, ctx_blocks):
    bwd = jnp.where(i < ctx_blocks, ctx_blocks - 1 - i, n_blocks - 1 + ctx_blocks - i)
    return jnp.where(d == 0, i, bwd)


def _mm_kernel(a_ref, w_ref, o_ref, abf_ref):
    @pl.when(pl.program_id(1) == 0)
    def _():
        abf_ref[...] = a_ref[...].astype(_BF)

    o_ref[...] = jnp.dot(abf_ref[...], w_ref[...], preferred_element_type=jnp.float32)


def _mm(a, w, tm, tn):
    m, k = a.shape
    n = w.shape[1]
    assert m % tm == 0 and n % tn == 0, (m, n, tm, tn)
    return pl.pallas_call(
        _mm_kernel,
        grid=(m // tm, n // tn),
        in_specs=[pl.BlockSpec((tm, k), lambda i, j: (i, 0)),
                  pl.BlockSpec((k, tn), lambda i, j: (0, j))],
        out_specs=pl.BlockSpec((tm, tn), lambda i, j: (i, j)),
        out_shape=jax.ShapeDtypeStruct((m, n), jnp.float32),
        scratch_shapes=[pltpu.VMEM((tm, k), _BF)],
        compiler_params=pltpu.CompilerParams(
            dimension_semantics=("parallel", "arbitrary"),
            vmem_limit_bytes=VMEM_LIMIT_BYTES),
        name="mm",
    )(a, w)


def _mm_any(a, w, tm=1024, tn=512, keep_col_pad=False):
    lead = a.shape[:-1]
    a2 = a.reshape(-1, a.shape[-1])
    m, n = a2.shape[0], w.shape[1]
    mp = -(-m // 8) * 8
    if mp > tm:
        mp = -(-m // tm) * tm
    tm = min(tm, mp)
    np_ = -(-n // LANES) * LANES
    if np_ > tn:
        np_ = -(-n // tn) * tn
    tn = min(tn, np_)
    if mp != m:
        a2 = jnp.pad(a2, ((0, mp - m), (0, 0)))
    wb = w.astype(_BF)
    if np_ != n:
        wb = jnp.pad(wb, ((0, 0), (0, np_ - n)))
    out = _mm(a2, wb, tm, tn)
    if keep_col_pad:
        n = np_
    if mp != m or np_ != n:
        out = out[:m, :n]
    return out.reshape(lead + (n,))


def _const_spec(shape):
    nd = len(shape)
    return pl.BlockSpec(shape, lambda i: (0,) * nd, pipeline_mode=pl.Buffered(1))


def _merge_kernel(ml_ref, rw_ref, s5_ref, g0_ref, g1_ref, g2_ref, wml_ref, wrw_ref, wval_ref, wgate_ref, z_ref):
    dot = lambda a_ref, w_ref: jnp.dot(a_ref[...].astype(_BF), w_ref[...], preferred_element_type=jnp.float32)
    sval = dot(s5_ref, wval_ref) * jax.nn.sigmoid(dot(s5_ref, wgate_ref))
    z = (jax.nn.sigmoid(g0_ref[...]) * dot(ml_ref, wml_ref) + jax.nn.sigmoid(g1_ref[...]) * dot(rw_ref, wrw_ref)
         + jax.nn.sigmoid(g2_ref[...]) * sval)
    z_ref[...] = z.astype(_BF)


def _merge_pallas(p2, ml, rw, s5, wml, wrw, wval, wgate):
    n, w = ml.shape
    tm = ROW_TM
    gb = P_GATE // D_MODEL
    row = lambda i: (i, 0)
    return pl.pallas_call(
        _merge_kernel,
        grid=(n // tm,),
        in_specs=[pl.BlockSpec((tm, w), row)] * 3
                 + [pl.BlockSpec((tm, D_MODEL), functools.partial(lambda j, i: (i, gb + j), j))
                    for j in range(N_BRANCH)]
                 + [_const_spec((w, D_MODEL))] * 4,
        out_specs=pl.BlockSpec((tm, D_MODEL), row),
        out_shape=jax.ShapeDtypeStruct((n, D_MODEL), _BF),
        compiler_params=pltpu.CompilerParams(dimension_semantics=("parallel",), vmem_limit_bytes=VMEM_LIMIT_BYTES),
        name="merge_gate_proj",
    )(ml, rw, s5, p2, p2, p2, wml, wrw, wval, wgate)


def _ln_rows(x, eps=LN_EPS):
    mu = jnp.mean(x, axis=-1, keepdims=True)
    xc = x - mu
    var = jnp.mean(xc * xc, axis=-1, keepdims=True)
    return xc * lax.rsqrt(var + eps)


def _resid_kernel(*refs, with_w):
    if with_w:
        d_ref, w_ref, x_ref, gate_ref, g_ref, b_ref, sh_ref, sc_ref, xo_ref, uo_ref = refs
        delta = jnp.dot(d_ref[...], w_ref[...], preferred_element_type=jnp.float32)
    else:
        d_ref, x_ref, gate_ref, g_ref, b_ref, sh_ref, sc_ref, xo_ref, uo_ref = refs
        delta = d_ref[...]
    xn = _ln_rows(DEEPNORM_ALPHA * x_ref[...] + gate_ref[0, 0, 0] * delta) * g_ref[...] + b_ref[...]
    xo_ref[...] = xn
    uo_ref[...] = _ln_rows(xn) * (1.0 + sc_ref[0, 0, 0]) + sh_ref[0, 0, 0]


def _resid_norm_mod(delta, w, x, mod_a, ia, ln_g, ln_b, mod_b, ish, isc, blocks_per_seq, ctx_blocks):
    n, d = x.shape
    tm = ROW_TM
    row = lambda i: (i, 0)

    def mod_spec(m):
        return pl.BlockSpec((1, 1, 1, 1, d), lambda i: (i // blocks_per_seq,
                                                       (i % blocks_per_seq >= ctx_blocks).astype(jnp.int32), m, 0, 0))

    vec = pl.BlockSpec((1, d), lambda i: (0, 0))
    if w is not None:
        in_specs = [pl.BlockSpec((tm, delta.shape[1]), row), _const_spec(w.shape), pl.BlockSpec((tm, d), row)]
        args = (delta, w, x)
    else:
        in_specs = [pl.BlockSpec((tm, d), row), pl.BlockSpec((tm, d), row)]
        args = (delta, x)
    return pl.pallas_call(
        functools.partial(_resid_kernel, with_w=w is not None),
        grid=(n // tm,),
        in_specs=in_specs + [mod_spec(ia), vec, vec, mod_spec(ish), mod_spec(isc)],
        out_specs=[pl.BlockSpec((tm, d), row)] * 2,
        out_shape=[jax.ShapeDtypeStruct((n, d), jnp.float32)] * 2,
        compiler_params=pltpu.CompilerParams(dimension_semantics=("parallel",), vmem_limit_bytes=VMEM_LIMIT_BYTES),
        name="resid_norm_mod",
    )(*args, mod_a, ln_g.reshape(1, d), ln_b.reshape(1, d), mod_b, mod_b)


def _moe_ffn_kernel(te_ref, tv_ref, x_ref, wg_ref, wu_ref, wd_ref, y_ref):
    t = pl.program_id(0)

    @pl.when(tv_ref[t] == 1)
    def _():
        x = x_ref[...].astype(_BF)
        hg = jnp.dot(x, wg_ref[0], preferred_element_type=jnp.float32)
        hu = jnp.dot(x, wu_ref[0], preferred_element_type=jnp.float32)
        h = (hg * jax.nn.sigmoid(hg)) * hu
        y_ref[...] = jnp.dot(h.astype(_BF), wd_ref[0], preferred_element_type=jnp.float32)

    @pl.when(tv_ref[t] == 0)
    def _():
        y_ref[...] = jnp.zeros_like(y_ref)


def _moe_routed(u, gates_t, wg, wu, wd):
    n_tok, d = u.shape
    n_e, _, d_e = wg.shape
    tm = MOE_TM
    n_tiles = (TOP_K * n_tok) // tm + n_e
    n_slots = n_tiles * tm
    sel = gates_t > 0.0
    seli = sel.astype(jnp.int32)
    rank = jnp.cumsum(seli, axis=1) - 1
    cnt = jnp.sum(seli, axis=1)
    tiles_e = (cnt + tm - 1) // tm
    tile_end = jnp.cumsum(tiles_e)
    off = (tile_end - tiles_e) * tm
    slot = off[:, None] + rank
    order = jnp.cumsum(seli, axis=0)
    slots, gsel = [], []
    for j in range(TOP_K):
        pick = sel & (order == j + 1)
        slots.append(jnp.sum(jnp.where(pick, slot, 0), axis=0))
        gsel.append(jnp.sum(jnp.where(pick, gates_t, 0.0), axis=0))
    tok = jnp.arange(n_tok, dtype=jnp.int32)
    tok_of_slot = jnp.zeros((n_slots,), jnp.int32).at[jnp.concatenate(slots)].set(jnp.tile(tok, TOP_K))
    tile_ids = jnp.arange(n_tiles, dtype=jnp.int32)
    tile_valid = (tile_ids < tile_end[-1]).astype(jnp.int32)
    tile_expert = jnp.minimum(jnp.searchsorted(tile_end, tile_ids, side='right'), n_e - 1).astype(jnp.int32)
    last_e = jnp.max(jnp.where(cnt > 0, jnp.arange(n_e), 0)).astype(jnp.int32)
    tile_expert = jnp.where(tile_valid == 1, tile_expert, last_e)
    xs = jnp.take(u, tok_of_slot, axis=0)
    grid_spec = pltpu.PrefetchScalarGridSpec(
        num_scalar_prefetch=2,
        grid=(n_tiles,),
        in_specs=[pl.BlockSpec((tm, d), lambda t, te, tv: (t, 0)),
                  pl.BlockSpec((1, d, d_e), lambda t, te, tv: (te[t], 0, 0)),
                  pl.BlockSpec((1, d, d_e), lambda t, te, tv: (te[t], 0, 0)),
                  pl.BlockSpec((1, d_e, d), lambda t, te, tv: (te[t], 0, 0))],
        out_specs=pl.BlockSpec((tm, d), lambda t, te, tv: (t, 0)))
    ys = pl.pallas_call(
        _moe_ffn_kernel, grid_spec=grid_spec,
        out_shape=jax.ShapeDtypeStruct((n_slots, d), jnp.float32),
        compiler_params=pltpu.CompilerParams(dimension_semantics=("arbitrary",),
                                             vmem_limit_bytes=VMEM_LIMIT_BYTES),
        name="moe_routed_ffn",
    )(tile_expert, tile_valid, xs, wg, wu, wd)
    return sum(gsel[j][:, None] * jnp.take(ys, slots[j], axis=0) for j in range(TOP_K))


def _router_kernel(u_ref, wt_ref, b_ref, g_ref):
    logits = lax.dot_general(wt_ref[...], u_ref[...], _NT, precision=_HI, preferred_element_type=jnp.float32)
    aff = jax.nn.sigmoid(logits)
    score = aff + b_ref[...]
    s = [score[e:e + 1] for e in range(N_EXPERTS)]
    a = [aff[e:e + 1] for e in range(N_EXPERTS)]
    gs = []
    for g in range(N_GROUPS):
        m = s[g * EXPERTS_PER_GROUP:(g + 1) * EXPERTS_PER_GROUP]
        best = None
        for i in range(EXPERTS_PER_GROUP):
            for j in range(i + 1, EXPERTS_PER_GROUP):
                best = m[i] + m[j] if best is None else jnp.maximum(best, m[i] + m[j])
        gs.append(best)
    best_val = gs[0]
    best_grp = jnp.zeros_like(gs[0], dtype=jnp.int32)
    for g in range(1, N_GROUPS):
        better = gs[g] > best_val
        best_grp = jnp.where(better, g, best_grp)
        best_val = jnp.where(better, gs[g], best_val)
    sel = []
    for e in range(N_EXPERTS):
        g = e // EXPERTS_PER_GROUP
        rank = jnp.zeros_like(best_grp)
        for j in range(g * EXPERTS_PER_GROUP, (g + 1) * EXPERTS_PER_GROUP):
            if j != e:
                ahead = (s[j] > s[e]) | ((s[j] == s[e]) & (j < e))
                rank = rank + ahead.astype(jnp.int32)
        sel.append((best_grp == g) & (rank < TOP_K))
    wsum = sum(jnp.where(sel[e], a[e], 0.0) for e in range(N_EXPERTS))
    g_ref[...] = jnp.concatenate([jnp.where(sel[e], a[e] / wsum, 0.0) for e in range(N_EXPERTS)], axis=0)


def _router_pallas(u, router_w, router_b, tm=512):
    n_tok, d = u.shape
    assert n_tok % tm == 0
    return pl.pallas_call(
        _router_kernel,
        grid=(n_tok // tm,),
        in_specs=[pl.BlockSpec((tm, d), lambda i: (i, 0)),
                  pl.BlockSpec((N_EXPERTS, d), lambda i: (0, 0)),
                  pl.BlockSpec((N_EXPERTS, 1), lambda i: (0, 0))],
        out_specs=pl.BlockSpec((N_EXPERTS, tm), lambda i: (0, i)),
        out_shape=jax.ShapeDtypeStruct((N_EXPERTS, n_tok), jnp.float32),
        compiler_params=pltpu.CompilerParams(dimension_semantics=("parallel",),
                                             vmem_limit_bytes=VMEM_LIMIT_BYTES),
        name="moe_router",
    )(u, router_w.T, router_b.reshape(N_EXPERTS, 1))


def _s5_mats(lam_re, lam_im, log_dt, b_re, b_im, c_re, c_im):
    L = S5_CHUNK
    lam = lax.complex(lam_re, lam_im)
    ldt = lam * jnp.exp(log_dt)[..., None]
    lam_bar = jnp.exp(ldt)
    b_bar = ((lam_bar - 1.0) / lam)[..., None] * lax.complex(b_re, b_im)
    c_mat = lax.complex(c_re, c_im)
    tau = jnp.arange(L + 1, dtype=jnp.float32)
    pw = jnp.exp(ldt[:, :, None, :] * tau[None, None, :, None])
    kern = jnp.real(jnp.einsum('dgon,dgtn,dgni->dgtoi', c_mat, pw[:, :, :L], b_bar))
    s_idx = jnp.arange(L)[:, None]
    t_idx = jnp.arange(L)[None, :]

    def toeplitz(k, lag, valid):
        m = k[:, jnp.clip(lag, 0, L - 1)] * valid[None, :, :, None, None]
        return jnp.transpose(m, (0, 1, 4, 2, 3)).reshape(-1, L * S5_GROUP, L * S5_GROUP)

    tsum = (toeplitz(kern[0], t_idx - s_idx, (t_idx >= s_idx).astype(jnp.float32))
            + toeplitz(kern[1], s_idx - t_idx, (s_idx >= t_idx).astype(jnp.float32)))
    pin_f = pw[0][:, L - 1 - jnp.arange(L)]
    pin_b = pw[1][:, jnp.arange(L)]
    in_f = jnp.einsum('gsn,gni->gsin', pin_f, b_bar[0]).reshape(-1, L * S5_GROUP, S5_STATE)
    in_b = jnp.einsum('gsn,gni->gsin', pin_b, b_bar[1]).reshape(-1, L * S5_GROUP, S5_STATE)
    icat = jnp.concatenate([jnp.real(in_f), jnp.imag(in_f), jnp.real(in_b), jnp.imag(in_b)], axis=-1)
    pout_f = pw[0][:, 1 + jnp.arange(L)]
    pout_b = pw[1][:, L - jnp.arange(L)]
    out_f = jnp.einsum('gon,gtn->gnto', c_mat[0], pout_f).reshape(-1, S5_STATE, L * S5_GROUP)
    out_b = jnp.einsum('gon,gtn->gnto', c_mat[1], pout_b).reshape(-1, S5_STATE, L * S5_GROUP)
    ocat = jnp.concatenate([jnp.real(out_f), -jnp.imag(out_f), jnp.real(out_b), -jnp.imag(out_b)], axis=1)
    lam_l = pw[:, :, L]
    lam_chunk = jnp.stack([jnp.real(lam_l[0]), jnp.imag(lam_l[0]), jnp.real(lam_l[1]), jnp.imag(lam_l[1])],
                          axis=1)
    return tsum, icat, ocat, lam_chunk


def _s5_kernel(u_ref, t_ref, i_ref, o_ref, lam_ref, d_ref, y_ref, v_ref, xfr_ref, xfi_ref, xbr_ref, xbi_ref,
               *, n_chunks, ctx_chunks, bsz):
    n = S5_STATE
    u = u_ref[0]
    ub = u.astype(_BF)
    v_ref[...] = jnp.dot(ub, i_ref[0], preferred_element_type=jnp.float32)
    lam = lam_ref[0]
    lfr = jnp.broadcast_to(lam[0:1], (bsz, n))
    lfi = jnp.broadcast_to(lam[1:2], (bsz, n))
    lbr = jnp.broadcast_to(lam[2:3], (bsz, n))
    lbi = jnp.broadcast_to(lam[3:4], (bsz, n))

    def cmul_add(lr, li, xr, xi, vr, vi):
        return lr * xr - li * xi + vr, lr * xi + li * xr + vi

    def step(j, carry):
        fr, fi, br, bi = carry
        rf = pl.multiple_of(j * (2 * bsz), 2 * bsz)
        pb = jnp.where(j < ctx_chunks // 2, ctx_chunks // 2 - 1 - j, (n_chunks + ctx_chunks) // 2 - 1 - j)
        rb = pl.multiple_of(pb * (2 * bsz), 2 * bsz)
        vf = v_ref[pl.ds(rf, 2 * bsz), :]
        vb = v_ref[pl.ds(rb, 2 * bsz), :]
        fr1, fi1 = cmul_add(lfr, lfi, fr, fi, vf[:bsz, 0:n], vf[:bsz, n:2 * n])
        fr2, fi2 = cmul_add(lfr, lfi, fr1, fi1, vf[bsz:, 0:n], vf[bsz:, n:2 * n])
        br1, bi1 = cmul_add(lbr, lbi, br, bi, vb[bsz:, 2 * n:3 * n], vb[bsz:, 3 * n:4 * n])
        br2, bi2 = cmul_add(lbr, lbi, br1, bi1, vb[:bsz, 2 * n:3 * n], vb[:bsz, 3 * n:4 * n])
        xfr_ref[pl.ds(rf, 2 * bsz), :] = jnp.concatenate([fr, fr1], axis=0)
        xfi_ref[pl.ds(rf, 2 * bsz), :] = jnp.concatenate([fi, fi1], axis=0)
        xbr_ref[pl.ds(rb, 2 * bsz), :] = jnp.concatenate([br1, br], axis=0)
        xbi_ref[pl.ds(rb, 2 * bsz), :] = jnp.concatenate([bi1, bi], axis=0)
        return fr2, fi2, br2, bi2

    z = jnp.zeros((bsz, n), jnp.float32)
    lax.fori_loop(0, n_chunks // 2, step, (z, z, z, z))
    o = o_ref[0]
    y = jnp.dot(ub, t_ref[0], preferred_element_type=jnp.float32)
    y += jnp.dot(xfr_ref[...].astype(_BF), o[0:n], preferred_element_type=jnp.float32)
    y += jnp.dot(xfi_ref[...].astype(_BF), o[n:2 * n], preferred_element_type=jnp.float32)
    y += jnp.dot(xbr_ref[...].astype(_BF), o[2 * n:3 * n], preferred_element_type=jnp.float32)
    y += jnp.dot(xbi_ref[...].astype(_BF), o[3 * n:4 * n], preferred_element_type=jnp.float32)
    y += d_ref[0] * u
    y_ref[0] = 0.5 * y * (1.0 + jnp.tanh(math.sqrt(2.0 / math.pi) * (y + 0.044715 * (y * y * y))))


def _s5_pallas(u, n_ctx, lam_re, lam_im, log_dt, b_re, b_im, c_re, c_im, d_skip):
    bsz, t_len, _ = u.shape
    L, G, C = S5_CHUNK, S5_GROUPS, S5_GROUP
    nc = t_len // L
    assert (2 * bsz) % 8 == 0 and nc % 2 == 0 and (n_ctx // L) % 2 == 0
    tsum, icat, ocat, lam_chunk = _s5_mats(lam_re, lam_im, log_dt, b_re, b_im, c_re, c_im)
    ug = jnp.transpose(u.reshape(bsz, nc, L, G, C), (3, 1, 0, 2, 4)).reshape(G, nc * bsz, L * C)
    dvec = jnp.tile(d_skip.reshape(G, 1, C), (1, L, 1)).reshape(G, 1, L * C)
    rows = nc * bsz
    wspec = pl.BlockSpec((1, L * C, L * C), lambda g: (g, 0, 0))
    yg = pl.pallas_call(
        functools.partial(_s5_kernel, n_chunks=nc, ctx_chunks=n_ctx // L, bsz=bsz),
        grid=(G,),
        in_specs=[pl.BlockSpec((1, rows, L * C), lambda g: (g, 0, 0)), wspec, wspec, wspec,
                  pl.BlockSpec((1, 4, S5_STATE), lambda g: (g, 0, 0)),
                  pl.BlockSpec((1, 1, L * C), lambda g: (g, 0, 0))],
        out_specs=pl.BlockSpec((1, rows, L * C), lambda g: (g, 0, 0)),
        out_shape=jax.ShapeDtypeStruct((G, rows, L * C), jnp.float32),
        scratch_shapes=[pltpu.VMEM((rows, 4 * S5_STATE), jnp.float32)]
                       + [pltpu.VMEM((rows, S5_STATE), jnp.float32)] * 4,
        compiler_params=pltpu.CompilerParams(dimension_semantics=("parallel",),
                                             vmem_limit_bytes=VMEM_LIMIT_BYTES),
        name="s5_scan",
    )(ug, tsum.astype(_BF), icat.astype(_BF), ocat.astype(_BF), lam_chunk, dvec)
    return jnp.transpose(yg.reshape(G, nc, bsz, L, C), (2, 1, 3, 0, 4)).reshape(bsz, t_len, S5_W)


def _log_sigmoid(x):
    return jnp.minimum(x, 0.0) - jnp.log(1.0 + jnp.exp(-jnp.abs(x)))


def _mlstm_kernel(igb_ref, fgb_ref, q_ref, k_ref, v_ref, gc_ref, gr_ref, h_ref, cmat_ref, nvec_ref, m_ref):
    d = pl.program_id(0)
    L = ML_CHUNK
    n_sub = ML_TBLK // L
    H = range(ML_HEADS)

    @pl.when(pl.program_id(2) == 0)
    def _():
        cmat_ref[...] = jnp.zeros_like(cmat_ref)
        nvec_ref[...] = jnp.zeros_like(nvec_ref)
        m_ref[...] = jnp.zeros_like(m_ref)

    igb = [igb_ref[d, h] for h in H]
    fgb = [fgb_ref[d, h] for h in H]
    row = lax.broadcasted_iota(jnp.int32, (L, L), 0)
    col = lax.broadcasted_iota(jnp.int32, (L, L), 1)
    sign = 1 - 2 * d
    seen = (row - col) * sign >= 0
    seen_f = seen.astype(jnp.float32)
    seen_t = ((col - row) * sign >= 0).astype(jnp.float32)
    scale = ML_DH ** -0.5

    def chunk(jj, carry):
        cj = jnp.where(d == 0, jj, n_sub - 1 - jj)
        r0 = pl.multiple_of(cj * L, L)
        hs = lambda h: slice(h * ML_DH, (h + 1) * ML_DH)
        q = [q_ref[0, pl.ds(r0, L), hs(h)] for h in H]
        k = [k_ref[0, pl.ds(r0, L), hs(h)] * scale for h in H]
        vb = [v_ref[0, pl.ds(r0, L), hs(h)].astype(_BF) for h in H]
        gc = [gc_ref[0, 0, h, pl.ds(r0, L), :] for h in H]
        gr = [gr_ref[0, 0, h, cj] for h in H]
        li_col = [gc[h][:, 0:1] + igb[h] for h in H]
        lf_col = [_log_sigmoid(gc[h][:, 1:2] + fgb[h]) for h in H]
        li_row = [gr[h][0:1, :] + igb[h] for h in H]
        lf_row = [_log_sigmoid(gr[h][1:2, :] + fgb[h]) for h in H]
        m_prev = [m_ref[h] for h in H]
        bcum_col = [jnp.dot(seen_f, jnp.broadcast_to(lf_col[h], (L, L)), precision=_HI,
                            preferred_element_type=jnp.float32) for h in H]
        bcum_row = [jnp.dot(jnp.broadcast_to(lf_row[h], (8, L)), seen_t, precision=_HI,
                            preferred_element_type=jnp.float32)[0:1] for h in H]
        qb = [q[h].astype(_BF) for h in H]
        qk = [lax.dot_general(qb[h], k[h].astype(_BF), _NT, preferred_element_type=jnp.float32) for h in H]
        qc = [jnp.dot(qb[h], cmat_ref[h].astype(_BF), preferred_element_type=jnp.float32) for h in H]
        log_d = [jnp.where(seen, bcum_col[h] - bcum_row[h] + li_row[h], -jnp.inf) for h in H]
        inter = [bcum_col[h][:, 0:1] + m_prev[h] for h in H]
        m_j = [jnp.maximum(jnp.max(log_d[h], axis=1, keepdims=True), inter[h]) for h in H]
        scores = [qk[h] * jnp.exp(log_d[h] - m_j[h]) for h in H]
        s_inter = [jnp.exp(inter[h] - m_j[h]) for h in H]
        sv = [jnp.dot(scores[h].astype(_BF), vb[h], preferred_element_type=jnp.float32) for h in H]
        b_last = [jnp.sum(lf_col[h], axis=0, keepdims=True) for h in H]
        log_w = [b_last[h] - bcum_col[h][:, 0:1] + li_col[h] for h in H]
        m_new = [jnp.maximum(b_last[h] + m_prev[h], jnp.max(log_w[h], axis=0, keepdims=True)) for h in H]
        kw = [k[h] * jnp.exp(log_w[h] - m_new[h]) for h in H]
        decay = [jnp.exp(b_last[h] + m_prev[h] - m_new[h]) for h in H]
        kv = [lax.dot_general(kw[h].astype(_BF), vb[h], _TN, preferred_element_type=jnp.float32) for h in H]
        for h in H:
            num = sv[h] + s_inter[h] * qc[h]
            den = (jnp.sum(scores[h], axis=1, keepdims=True)
                   + s_inter[h] * jnp.sum(q[h] * nvec_ref[h], axis=1, keepdims=True))
            h_ref[0, 0, pl.ds(r0, L), hs(h)] = num / jnp.maximum(jnp.abs(den), jnp.exp(-m_j[h]))
        for h in H:
            cmat_ref[h] = decay[h] * cmat_ref[h] + kv[h]
            nvec_ref[h] = decay[h] * nvec_ref[h] + jnp.sum(kw[h], axis=0, keepdims=True)
            m_ref[h] = m_new[h]
        return carry

    lax.fori_loop(0, n_sub, chunk, 0)


def _mlstm_pallas(q, k, v, gates, ig_b, fg_b, n_ctx):
    bsz, t_len, _ = q.shape
    nb = t_len // ML_TBLK
    cb = n_ctx // ML_TBLK
    assert t_len % ML_TBLK == 0 and n_ctx % ML_TBLK == 0
    g = gates.reshape(bsz, t_len, 2, N_DIR, ML_HEADS)
    gcol = jnp.transpose(g, (3, 0, 4, 1, 2))
    grow = jnp.transpose(g.reshape(bsz, t_len // ML_CHUNK, ML_CHUNK, 2, N_DIR, ML_HEADS),
                         (4, 0, 5, 1, 3, 2))
    blk = lambda d, i: _scan_block(d, i, nb, cb)
    grid_spec = pltpu.PrefetchScalarGridSpec(
        num_scalar_prefetch=2,
        grid=(N_DIR, bsz, nb),
        in_specs=[pl.BlockSpec((1, ML_TBLK, ML_W), lambda d, b, i, *_: (b, blk(d, i), 0))] * 3 + [
            pl.BlockSpec((1, 1, ML_HEADS, ML_TBLK, 2), lambda d, b, i, *_: (d, b, 0, blk(d, i), 0)),
            pl.BlockSpec((1, 1, ML_HEADS, ML_TBLK // ML_CHUNK, 2, ML_CHUNK),
                         lambda d, b, i, *_: (d, b, 0, blk(d, i), 0, 0))],
        out_specs=pl.BlockSpec((1, 1, ML_TBLK, ML_W), lambda d, b, i, *_: (d, b, blk(d, i), 0)),
        scratch_shapes=[pltpu.VMEM((ML_HEADS, ML_DH, ML_DH), jnp.float32),
                        pltpu.VMEM((ML_HEADS, 1, ML_DH), jnp.float32),
                        pltpu.VMEM((ML_HEADS, 1, 1), jnp.float32)])
    return pl.pallas_call(
        _mlstm_kernel, grid_spec=grid_spec,
        out_shape=jax.ShapeDtypeStruct((N_DIR, bsz, t_len, ML_W), jnp.float32),
        compiler_params=pltpu.CompilerParams(
            dimension_semantics=("parallel", "parallel", "arbitrary"),
            vmem_limit_bytes=VMEM_LIMIT_BYTES),
        name="mlstm_scan",
    )(ig_b, fg_b, q, k, v, gcol, grow)


def _to_pairs(x):
    return jnp.concatenate([x[:, p * LANES:(p + 1) * LANES] for p in range(RW_PAIRS)], axis=0)


def _dotf(a, b, dims=None):
    a = a.astype(_BF)
    b = b.astype(_BF)
    if dims is None:
        return jnp.dot(a, b, preferred_element_type=jnp.float32)
    return lax.dot_general(a, b, dims, preferred_element_type=jnp.float32)


def _rwkv_a_kernel(r_ref, k_ref, v_ref, kk_ref, a_ref, lw_ref,
                   att_ref, rt_ref, bw_ref, kw_ref, vt_ref, u0t_ref, y0_ref, wc_ref, *, n_sub):
    d = pl.program_id(0)
    C, R = RW_CHUNK, RW_ROWS
    row = lax.broadcasted_iota(jnp.int32, (R, R), 0)
    col = lax.broadcasted_iota(jnp.int32, (R, R), 1)
    same = (row // C) == (col // C)
    sign = 1 - 2 * d
    before = same & ((row - col) * sign > 0)
    upto = same & ((row - col) * sign >= 0)
    upto2 = jnp.concatenate([upto, upto], axis=1)
    eye = (row == col).astype(jnp.float32)
    first = col < RW_DH
    tpos = row % C

    def chunk_group(jg, carry):
        js = [jg * RW_GROUP + i for i in range(RW_GROUP)]
        G2 = [(i, h2) for i in range(RW_GROUP) for h2 in range(2)]
        pick = lambda lst, off: [jnp.where(first, lst[2 * i][:, off:off + R], lst[2 * i + 1][:, off:off + R])
                                 for i in range(RW_GROUP)]
        r0s = [pl.multiple_of(j * C, C) for j in js]
        ld = lambda ref: [_to_pairs(ref[0, pl.ds(r0, C), :]) for r0 in r0s]
        ldd = lambda ref: [_to_pairs(ref[0, 0, pl.ds(r0, C), :]) for r0 in r0s]
        r, v, kk = ld(r_ref), ld(v_ref), ld(kk_ref)
        k, a, lw = ldd(k_ref), ldd(a_ref), ldd(lw_ref)
        fwd = d == 0
        cum, aft = [], []
        for i in range(RW_GROUP):
            pre = lw[i]
            suf = lw[i]
            for s in (1, 2, 4, 8):
                pre = pre + jnp.where(tpos >= s, pltpu.roll(pre, s, axis=0), 0.0)
                suf = suf + jnp.where(tpos < C - s, pltpu.roll(suf, R - s, axis=0), 0.0)
            cum.append(jnp.where(fwd, pre, suf))
            aft.append(jnp.where(fwd, suf, pre) - lw[i])
        a_hat = [-kk[i] * jnp.exp(cum[i] - lw[i]) for i in range(RW_GROUP)]
        r_hat = [r[i] * jnp.exp(cum[i]) for i in range(RW_GROUP)]
        vb = [v[i].astype(_BF) for i in range(RW_GROUP)]
        m = []
        for i in range(RW_GROUP):
            inv_w = jnp.exp(-cum[i])
            lhs = jnp.concatenate([jnp.where(first, a_hat[i], 0.0), jnp.where(first, 0.0, a_hat[i]),
                                   jnp.where(first, r_hat[i], 0.0), jnp.where(first, 0.0, r_hat[i])], axis=0)
            rhs = jnp.concatenate([kk[i] * a[i] * inv_w, k[i] * inv_w], axis=0)
            m.append(_dotf(lhs, rhs, _NT))
        x = [jnp.where(before, m[i][h2 * R:(h2 + 1) * R, 0:R], 0.0).astype(_BF) for i, h2 in G2]
        ak = [jnp.where(before, m[i][h2 * R:(h2 + 1) * R, R:2 * R], 0.0) for i, h2 in G2]
        rbk = [jnp.where(upto2, m[i][(2 + h2) * R:(3 + h2) * R, :], 0.0).astype(_BF) for i, h2 in G2]
        akv = [_dotf(ak[g], vb[g // 2]) for g in range(len(G2))]
        x2 = [_dotf(xx, xx).astype(_BF) for xx in x]
        x4 = [_dotf(xx, xx).astype(_BF) for xx in x2]
        x8 = [_dotf(xx, xx).astype(_BF) for xx in x4]
        t = [eye + xx.astype(jnp.float32) for xx in x]
        t = [t[g] + _dotf(t[g], x2[g]) for g in range(len(G2))]
        t = [t[g] + _dotf(t[g], x4[g]) for g in range(len(G2))]
        t = [t[g] + _dotf(t[g], x8[g]) for g in range(len(G2))]
        akv = pick(akv, 0)
        rhs2 = [jnp.concatenate([a_hat[i], akv[i]], axis=1).astype(_BF) for i in range(RW_GROUP)]
        ta = [_dotf(t[g], rhs2[g // 2]) for g in range(len(G2))]
        at = pick(ta, 0)
        u0 = pick(ta, R)
        rhs3 = [jnp.concatenate([jnp.concatenate([at[i], u0[i]], axis=1).astype(_BF),
                                 jnp.concatenate([jnp.zeros_like(vb[i]), vb[i]], axis=1)], axis=0)
                for i in range(RW_GROUP)]
        ry = [_dotf(rbk[g], rhs3[g // 2]) for g in range(len(G2))]
        rt = pick(ry, 0)
        y0 = pick(ry, R)
        for i, j in enumerate(js):
            w_aft = jnp.exp(aft[i])
            att_ref[0, 0, j] = at[i].T.astype(_BF)
            u0t_ref[0, 0, j] = u0[i].T
            vt_ref[0, 0, j] = v[i].T.astype(_BF)
            rt_ref[0, 0, j] = (r_hat[i] + rt[i]).astype(_BF)
            bw_ref[0, 0, j] = (kk[i] * a[i] * w_aft).astype(_BF)
            kw_ref[0, 0, j] = (k[i] * w_aft).astype(_BF)
            y0_ref[0, 0, j] = y0[i]
            tot = cum[i] + aft[i]
            wc_ref[0, 0, j] = jnp.exp(jnp.concatenate([tot[p * C:p * C + 1] for p in range(RW_PAIRS)], axis=0))
        return carry

    lax.fori_loop(0, n_sub // RW_GROUP, chunk_group, 0)


def _rwkv_b_kernel(att_ref, rt_ref, bw_ref, kw_ref, vt_ref, u0t_ref, y0_ref, wc_ref, y_ref, s_ref, *, n_sub):
    d = pl.program_id(0)
    C = RW_CHUNK

    @pl.when(pl.program_id(2) == 0)
    def _():
        s_ref[...] = jnp.zeros_like(s_ref)

    row = lax.broadcasted_iota(jnp.int32, (LANES, LANES), 0)
    col = lax.broadcasted_iota(jnp.int32, (LANES, LANES), 1)
    diag = (row // RW_DH) == (col // RW_DH)
    pair_of_row = lax.broadcasted_iota(jnp.int32, (2 * RW_ROWS, LANES), 0) % RW_ROWS // C

    def chunk(jj, carry):
        cj = jnp.where(d == 0, jj, n_sub - 1 - jj)
        r0 = pl.multiple_of(cj * C, C)
        wc = wc_ref[0, 0, cj]
        att = att_ref[0, 0, cj]
        u0t = u0t_ref[0, 0, cj]
        vt = vt_ref[0, 0, cj]
        bk = jnp.concatenate([bw_ref[0, 0, cj], kw_ref[0, 0, cj]], axis=0)
        sps = [s_ref[p] for p in range(RW_PAIRS)]
        spb = [sp.astype(_BF) for sp in sps]
        uts = [jnp.dot(spb[p], att, preferred_element_type=jnp.float32) + u0t for p in range(RW_PAIRS)]
        for p in range(RW_PAIRS):
            rows = slice(p * C, (p + 1) * C)
            y = lax.dot_general(rt_ref[0, 0, cj, rows, :], spb[p], _NT, preferred_element_type=jnp.float32)
            y_ref[0, 0, pl.ds(r0, C), p * LANES:(p + 1) * LANES] = y + y0_ref[0, 0, cj, rows, :]
        for p in range(RW_PAIRS):
            lhs = jnp.concatenate([uts[p].astype(_BF), vt], axis=1)
            rhs = jnp.where(pair_of_row == p, bk, jnp.zeros_like(bk))
            upd = jnp.dot(lhs, rhs, preferred_element_type=jnp.float32)
            s_ref[p] = jnp.where(diag, wc[p:p + 1, :] * sps[p] + upd, 0.0)
        return carry

    lax.fori_loop(0, n_sub, chunk, 0)


def _rwkv_pallas(r, v, kk, k_dir, a_dir, lw_dir, n_ctx):
    bsz, t_len, _ = r.shape
    nb, cb = t_len // RW_TBLK, n_ctx // RW_TBLK
    assert t_len % RW_TBLK == 0 and n_ctx % RW_TBLK == 0
    n_sub = RW_TBLK // RW_CHUNK
    nc = t_len // RW_CHUNK
    sh_spec = pl.BlockSpec((1, RW_TBLK, RW_W), lambda d, b, i: (b, _scan_block(d, i, nb, cb), 0))
    dr_spec = pl.BlockSpec((1, 1, RW_TBLK, RW_W), lambda d, b, i: (d, b, _scan_block(d, i, nb, cb), 0))
    ch_spec = pl.BlockSpec((1, 1, n_sub, RW_ROWS, LANES), lambda d, b, i: (d, b, _scan_block(d, i, nb, cb), 0, 0))
    wc_spec = pl.BlockSpec((1, 1, n_sub, RW_PAIRS, LANES), lambda d, b, i: (d, b, _scan_block(d, i, nb, cb), 0, 0))
    ch_shape = lambda dt: jax.ShapeDtypeStruct((N_DIR, bsz, nc, RW_ROWS, LANES), dt)
    params = pltpu.CompilerParams(dimension_semantics=("parallel", "parallel", "arbitrary"),
                                  vmem_limit_bytes=VMEM_LIMIT_BYTES)
    chunk_local = pl.pallas_call(
        functools.partial(_rwkv_a_kernel, n_sub=n_sub),
        grid=(N_DIR, bsz, nb),
        in_specs=[sh_spec, dr_spec, sh_spec, sh_spec, dr_spec, dr_spec],
        out_specs=[ch_spec] * 7 + [wc_spec],
        out_shape=[ch_shape(_BF)] * 5 + [ch_shape(jnp.float32)] * 2
                  + [jax.ShapeDtypeStruct((N_DIR, bsz, nc, RW_PAIRS, LANES), jnp.float32)],
        compiler_params=params, name="rwkv_chunk_local",
    )(r, k_dir, v, kk, a_dir, lw_dir)
    return pl.pallas_call(
        functools.partial(_rwkv_b_kernel, n_sub=n_sub),
        grid=(N_DIR, bsz, nb),
        in_specs=[ch_spec] * 7 + [wc_spec],
        out_specs=dr_spec,
        out_shape=jax.ShapeDtypeStruct((N_DIR, bsz, t_len, RW_W), jnp.float32),
        scratch_shapes=[pltpu.VMEM((RW_PAIRS, LANES, LANES), jnp.float32)],
        compiler_params=params, name="rwkv_state_scan",
    )(*chunk_local)


def _layer_norm(x, eps=LN_EPS):
    mu = jnp.mean(x, axis=-1, keepdims=True)
    var = jnp.mean(jnp.square(x - mu), axis=-1, keepdims=True)
    return (x - mu) * lax.rsqrt(var + eps)


def _modulate(x, shift, scale):
    return _layer_norm(x) * (1.0 + scale) + shift


def _head_norm(h, gain, bias, eps):
    y = _layer_norm(h, eps)
    return y.reshape(h.shape[:-2] + (-1,)) * gain + bias


def _split_cols(z, widths):
    return jnp.split(z, np.cumsum(widths)[:-1].tolist(), axis=-1)


def _depthwise_conv3x3(z, w, b):
    ch = z.shape[-1]
    y = lax.conv_general_dilated(z, w[:, :, None, :], window_strides=(1, 1), padding='SAME',
                                 dimension_numbers=('NHWC', 'HWIO', 'NHWC'), feature_group_count=ch)
    return y + b


def _centred_shift(z):
    zp = jnp.pad(z, ((0, 0), (1, 1), (0, 0)))
    return 0.5 * (zp[:, :-2] + zp[:, 2:])


def _ctx_lat(fn, z, n_ctx):
    return jnp.concatenate([fn(z[:, :n_ctx]), fn(z[:, n_ctx:])], axis=1)


def _mlstm_branch(seg, n_ctx, conv_w, conv_b, ig_b, fg_b, norm_g, norm_b):
    q, k, v, o, ig, fg = seg
    bsz, t_len, _ = q.shape

    def conv(z):
        length = z.shape[1]
        rows, cols = (1, length) if length == n_ctx else (length // GRID_W, GRID_W)
        y = _depthwise_conv3x3(z.reshape(bsz, rows, cols, 2 * ML_W), conv_w, conv_b)
        return jax.nn.silu(y).reshape(bsz, length, 2 * ML_W)

    qk = jnp.concatenate([conv(jnp.concatenate([q[:, :n_ctx], k[:, :n_ctx]], axis=-1)),
                          conv(jnp.concatenate([q[:, n_ctx:], k[:, n_ctx:]], axis=-1))], axis=1)
    h = _mlstm_pallas(qk[..., :ML_W], qk[..., ML_W:], v, jnp.concatenate([ig, fg], axis=-1), ig_b, fg_b, n_ctx)
    h = (h[0] + h[1]).reshape(bsz, t_len, ML_HEADS, ML_DH)
    return jax.nn.sigmoid(o) * _head_norm(h, norm_g, norm_b, ML_NORM_EPS)


def _rwkv7_branch(seg, n_ctx, mu, w0, w_up, a0, a_up, g_up, k_k, k_a, r_k, norm_g, norm_b):
    def hd(z):
        return z.reshape(z.shape[:-1] + (RW_HEADS, RW_DH))

    z = jnp.concatenate(seg, axis=-1)
    z = z + mu * (_ctx_lat(_centred_shift, z, n_ctx) - z)
    r, k, v, wd, ad, gd = _split_cols(z, RW_IN_WIDTHS)
    bsz, length, _ = r.shape
    wd = wd.reshape(bsz, length, N_DIR, RW_DECAY_RANK)
    ad = ad.reshape(bsz, length, N_DIR, RW_A_RANK)
    w_pre = w0[:, None, None, :] + jnp.einsum('bldr,drc->dblc', jnp.tanh(wd), w_up)
    log_decay = -jnp.exp(-jax.nn.softplus(-w_pre) - RW_DECAY_OFFSET)
    a = jax.nn.sigmoid(a0[:, None, None, :] + jnp.einsum('bldr,drc->dblc', ad, a_up))
    g = _mm_any(jax.nn.sigmoid(gd), g_up)
    kk = hd(k * k_k)
    kk = kk / jnp.maximum(jnp.sqrt(jnp.sum(jnp.square(kk), axis=-1, keepdims=True)), 1e-12)
    k_dir = k[None] * (1.0 + (a - 1.0) * k_a)
    bonus = jnp.sum(hd(r[None] * k_dir * r_k.reshape(-1)), axis=(0, 4))
    bonus = (bonus[..., None] * hd(v)).reshape(bsz, length, RW_W)
    y = _rwkv_pallas(r, v, kk.reshape(bsz, length, RW_W), k_dir, a, log_decay, n_ctx)
    y = hd(y[0] + y[1])
    return (_head_norm(y, norm_g, norm_b, RW_NORM_EPS) + bonus) * g


def _token_mixer(u, n_ctx, w_in,
                 ml_conv_w, ml_conv_b, ml_ig_b, ml_fg_b, ml_norm_g, ml_norm_b, ml_proj,
                 rw_mu, rw_w0, rw_w_up, rw_a0, rw_a_up, rw_g_up, rw_k_k, rw_k_a, rw_r_k,
                 rw_norm_g, rw_norm_b, rw_proj,
                 s5_lam_re, s5_lam_im, s5_log_dt, s5_b_re, s5_b_im, s5_c_re, s5_c_im, s5_d,
                 s5_w_val, s5_w_gate):
    bsz, t_len, _ = u.shape
    n = bsz * t_len
    p = _mm_any(u, _permute_w_in(w_in), keep_col_pad=True)

    def col(start, width):
        return p[..., start:start + width]

    n_gate = N_DIR * ML_HEADS
    ml = _mlstm_branch((col(P_QK, ML_W), col(P_QK + ML_W, ML_W), col(P_V, ML_W), col(P_O, ML_W),
                        col(P_MLG, n_gate), col(P_MLG + n_gate, n_gate)),
                       n_ctx, ml_conv_w, ml_conv_b, ml_ig_b, ml_fg_b, ml_norm_g, ml_norm_b)
    rw = _rwkv7_branch((col(P_RW, 3 * RW_W), col(P_RWLR, P_MLG - P_RWLR)), n_ctx, rw_mu, rw_w0, rw_w_up, rw_a0,
                       rw_a_up, rw_g_up, rw_k_k, rw_k_a, rw_r_k, rw_norm_g, rw_norm_b)
    s5 = _s5_pallas(col(P_S5, S5_W), n_ctx, s5_lam_re, s5_lam_im, s5_log_dt, s5_b_re, s5_b_im, s5_c_re, s5_c_im,
                    s5_d)
    return _merge_pallas(p.reshape(n, -1), ml.reshape(n, -1), rw.reshape(n, -1), s5.reshape(n, -1),
                         ml_proj.astype(_BF), rw_proj.astype(_BF), s5_w_val.astype(_BF), s5_w_gate.astype(_BF))


def _permute_w_in(w_in):
    offs = np.cumsum((0,) + IN_WIDTHS)
    seg = lambda a, b: w_in[:, offs[a]:offs[b]]
    return jnp.concatenate([seg(0, 4), seg(6, 9), seg(12, 13), seg(13, 14), seg(9, 12), seg(4, 6)], axis=1)


def _moe_ffn(u, router_w, router_b, w_gate, w_up, w_down):
    gates_t = _router_pallas(u, router_w, router_b)
    return _moe_routed(u, gates_t, w_gate.astype(_BF), w_up.astype(_BF), w_down.astype(_BF))


def kernel(x, c, ctx, c_ctx, ada_w, ada_b, w_in, ml_conv_w, ml_conv_b, ml_ig_b, ml_fg_b, ml_norm_g,
           ml_norm_b, ml_proj, rw_mu, rw_w0, rw_w_up, rw_a0, rw_a_up, rw_g_up, rw_k_k, rw_k_a, rw_r_k,
           rw_norm_g, rw_norm_b, rw_proj, s5_lam_re, s5_lam_im, s5_log_dt, s5_b_re, s5_b_im, s5_c_re,
           s5_c_im, s5_d, s5_w_val, s5_w_gate, w_out, ln1_g, ln1_b, ln2_g, ln2_b, router_w, router_b,
           exp_w_gate, exp_w_up, exp_w_down):
    bsz, n_ctx = ctx.shape[0], ctx.shape[1]
    t_len = n_ctx + x.shape[1]
    assert n_ctx % ROW_TM == 0 and t_len % ROW_TM == 0
    blocks_per_seq, ctx_blocks = t_len // ROW_TM, n_ctx // ROW_TM
    silu_c = jax.nn.silu(c)
    silu_cc = jax.nn.silu(c_ctx)[None, :]
    mods = []
    for i in range(DEPTH):
        mx = _mm_any(silu_c, ada_w[i]) + ada_b[i]
        mc = jnp.broadcast_to(_mm_any(silu_cc, ada_w[i]) + ada_b[i], mx.shape)
        mods.append(jnp.stack([mc, mx], axis=1).reshape(bsz, 2, N_MOD, 1, D_MODEL))
    xa = jnp.concatenate([ctx, x], axis=1).reshape(bsz * t_len, D_MODEL)
    m0 = mods[0]
    u = jnp.concatenate([_modulate(ctx, m0[:, 0, 0], m0[:, 0, 1]), _modulate(x, m0[:, 1, 0], m0[:, 1, 1])], axis=1)
    for i in range(DEPTH):
        z = _token_mixer(
            u, n_ctx, w_in[i],
            ml_conv_w[i], ml_conv_b[i], ml_ig_b[i], ml_fg_b[i], ml_norm_g[i], ml_norm_b[i], ml_proj[i],
            rw_mu[i], rw_w0[i], rw_w_up[i], rw_a0[i], rw_a_up[i], rw_g_up[i], rw_k_k[i], rw_k_a[i], rw_r_k[i],
            rw_norm_g[i], rw_norm_b[i], rw_proj[i],
            s5_lam_re[i], s5_lam_im[i], s5_log_dt[i], s5_b_re[i], s5_b_im[i], s5_c_re[i], s5_c_im[i], s5_d[i],
            s5_w_val[i], s5_w_gate[i])
        xa, u_ffn = _resid_norm_mod(z, w_out[i].astype(_BF), xa, mods[i], 2, ln1_g[i], ln1_b[i], mods[i], 3, 4,
                                    blocks_per_seq, ctx_blocks)
        ffn = _moe_ffn(u_ffn, router_w, router_b, exp_w_gate[i], exp_w_up[i], exp_w_down[i])
        xa, u = _resid_norm_mod(ffn, None, xa, mods[i], 5, ln2_g[i], ln2_b[i], mods[min(i + 1, DEPTH - 1)], 0, 1,
                                blocks_per_seq, ctx_blocks)
        u = u.reshape(bsz, t_len, D_MODEL)
    return xa.reshape(bsz, t_len, D_MODEL)[:, n_ctx:]
```

```python
import functools
import math

import jax
import jax.numpy as jnp
import numpy as np
from jax import lax
from jax.experimental import pallas as pl
from jax.experimental.pallas import tpu as pltpu

D_MODEL = 2048
DEPTH = 2
GRID_W = 64
N_DIR = 2
ML_HEADS = 4
ML_DH = 256
ML_W = ML_HEADS * ML_DH
ML_CHUNK = 64
ML_NORM_EPS = 1e-6
RW_HEADS = 16
RW_DH = 64
RW_W = RW_HEADS * RW_DH
RW_DECAY_RANK = 64
RW_A_RANK = 64
RW_G_RANK = 128
RW_DECAY_OFFSET = 0.5
RW_NORM_EPS = 64e-5
S5_W = 1024
S5_GROUP = 16
S5_GROUPS = S5_W // S5_GROUP
S5_STATE = 64
N_BRANCH = 3
N_GROUPS = 4
EXPERTS_PER_GROUP = 4
N_EXPERTS = N_GROUPS * EXPERTS_PER_GROUP
TOP_K = 2
D_EXPERT = 1024
DEEPNORM_ALPHA = (2.0 * DEPTH) ** 0.25
LN_EPS = 1e-5
N_MOD = 6
IN_WIDTHS = (ML_W, ML_W, ML_W, ML_W, N_DIR * ML_HEADS, N_DIR * ML_HEADS,
             RW_W, RW_W, RW_W, N_DIR * RW_DECAY_RANK, N_DIR * RW_A_RANK, RW_G_RANK,
             S5_W, N_BRANCH * D_MODEL)
D_IN = sum(IN_WIDTHS)
RW_IN_WIDTHS = (RW_W, RW_W, RW_W, N_DIR * RW_DECAY_RANK, N_DIR * RW_A_RANK, RW_G_RANK)

VMEM_LIMIT_BYTES = 56 * 1024 * 1024
LANES = 128

S5_CHUNK = 16
ML_TBLK = 256
RW_CHUNK = 16
RW_PAIRS = RW_HEADS // 2
RW_ROWS = RW_PAIRS * RW_CHUNK
RW_TBLK = 256
RW_GROUP = 8
MOE_TM = 256
ROW_TM = 256

P_QK, P_V, P_O = 0, 2 * ML_W, 3 * ML_W
P_RW = 4 * ML_W
P_S5 = P_RW + 3 * RW_W
P_GATE = P_S5 + S5_W
P_RWLR = P_GATE + N_BRANCH * D_MODEL
P_MLG = P_RWLR + 2 * N_DIR * RW_DECAY_RANK + RW_G_RANK
assert P_GATE % D_MODEL == 0 and P_MLG + 2 * N_DIR * ML_HEADS == D_IN and RW_DECAY_RANK == RW_A_RANK

_BF = jnp.bfloat16
_HI = lax.Precision.HIGHEST
_NT = (((1,), (1,)), ((), ()))
_TN = (((0,), (0,)), ((), ()))


def _scan_block(d, i, n_blocks, ctx_blocks):
    bwd = jnp.where(i < ctx_blocks, ctx_blocks - 1 - i, n_blocks - 1 + ctx_blocks - i)
    return jnp.where(d == 0, i, bwd)


def _mm_kernel(a_ref, w_ref, o_ref, abf_ref):
    @pl.when(pl.program_id(1) == 0)
    def _():
        abf_ref[...] = a_ref[...].astype(_BF)

    o_ref[...] = jnp.dot(abf_ref[...], w_ref[...], preferred_element_type=jnp.float32)


def _mm(a, w, tm, tn):
    m, k = a.shape
    n = w.shape[1]
    assert m % tm == 0 and n % tn == 0, (m, n, tm, tn)
    return pl.pallas_call(
        _mm_kernel,
        grid=(m // tm, n // tn),
        in_specs=[pl.BlockSpec((tm, k), lambda i, j: (i, 0)),
                  pl.BlockSpec((k, tn), lambda i, j: (0, j))],
        out_specs=pl.BlockSpec((tm, tn), lambda i, j: (i, j)),
        out_shape=jax.ShapeDtypeStruct((m, n), jnp.float32),
        scratch_shapes=[pltpu.VMEM((tm, k), _BF)],
        compiler_params=pltpu.CompilerParams(
            dimension_semantics=("parallel", "arbitrary"),
            vmem_limit_bytes=VMEM_LIMIT_BYTES),
        name="mm",
    )(a, w)


def _mm_any(a, w, tm=1024, tn=512, keep_col_pad=False):
    lead = a.shape[:-1]
    a2 = a.reshape(-1, a.shape[-1])
    m, n = a2.shape[0], w.shape[1]
    mp = -(-m // 8) * 8
    if mp > tm:
        mp = -(-m // tm) * tm
    tm = min(tm, mp)
    np_ = -(-n // LANES) * LANES
    if np_ > tn:
        np_ = -(-n // tn) * tn
    tn = min(tn, np_)
    if mp != m:
        a2 = jnp.pad(a2, ((0, mp - m), (0, 0)))
    wb = w.astype(_BF)
    if np_ != n:
        wb = jnp.pad(wb, ((0, 0), (0, np_ - n)))
    out = _mm(a2, wb, tm, tn)
    if keep_col_pad:
        n = np_
    if mp != m or np_ != n:
        out = out[:m, :n]
    return out.reshape(lead + (n,))


def _const_spec(shape):
    nd = len(shape)
    return pl.BlockSpec(shape, lambda i: (0,) * nd, pipeline_mode=pl.Buffered(1))


def _merge_kernel(ml_ref, rw_ref, s5_ref, g0_ref, g1_ref, g2_ref, wml_ref, wrw_ref, wval_ref, wgate_ref, z_ref):
    dot = lambda a_ref, w_ref: jnp.dot(a_ref[...].astype(_BF), w_ref[...], preferred_element_type=jnp.float32)
    sval = dot(s5_ref, wval_ref) * jax.nn.sigmoid(dot(s5_ref, wgate_ref))
    z = (jax.nn.sigmoid(g0_ref[...]) * dot(ml_ref, wml_ref) + jax.nn.sigmoid(g1_ref[...]) * dot(rw_ref, wrw_ref)
         + jax.nn.sigmoid(g2_ref[...]) * sval)
    z_ref[...] = z.astype(_BF)


def _merge_pallas(p2, ml, rw, s5, wml, wrw, wval, wgate):
    n, w = ml.shape
    tm = ROW_TM
    gb = P_GATE // D_MODEL
    row = lambda i: (i, 0)
    return pl.pallas_call(
        _merge_kernel,
        grid=(n // tm,),
        in_specs=[pl.BlockSpec((tm, w), row)] * 3
                 + [pl.BlockSpec((tm, D_MODEL), functools.partial(lambda j, i: (i, gb + j), j))
                    for j in range(N_BRANCH)]
                 + [_const_spec((w, D_MODEL))] * 4,
        out_specs=pl.BlockSpec((tm, D_MODEL), row),
        out_shape=jax.ShapeDtypeStruct((n, D_MODEL), _BF),
        compiler_params=pltpu.CompilerParams(dimension_semantics=("parallel",), vmem_limit_bytes=VMEM_LIMIT_BYTES),
        name="merge_gate_proj",
    )(ml, rw, s5, p2, p2, p2, wml, wrw, wval, wgate)


def _post_scan_kernel(hf_ref, hb_ref, o_ref, yf_ref, yb_ref, bonus_ref, g_ref, mlg_ref, mlb_ref, rwg_ref, rwb_ref,
                      ones_ref, ml_ref, rw_ref):
    h = hf_ref[0] + hb_ref[0]
    parts = []
    for hd in range(ML_HEADS):
        x = h[:, hd * ML_DH:(hd + 1) * ML_DH]
        xc = x - jnp.mean(x, axis=-1, keepdims=True)
        parts.append(xc * lax.rsqrt(jnp.mean(xc * xc, axis=-1, keepdims=True) + ML_NORM_EPS))
    hn = jnp.concatenate(parts, axis=1) * mlg_ref[...] + mlb_ref[...]
    ml_ref[...] = (jax.nn.sigmoid(o_ref[...]) * hn).astype(_BF)

    def head_mean(x):
        hi = x.astype(_BF)
        lo = (x - hi.astype(jnp.float32)).astype(_BF)
        s = (jnp.dot(hi, ones_ref[...], preferred_element_type=jnp.float32)
             + jnp.dot(lo, ones_ref[...], preferred_element_type=jnp.float32))
        return s * (1.0 / RW_DH)

    y = yf_ref[...] + yb_ref[...]
    yc = y - head_mean(y)
    yn = yc * lax.rsqrt(head_mean(yc * yc) + RW_NORM_EPS) * rwg_ref[...] + rwb_ref[...]
    rw_ref[...] = ((yn + bonus_ref[...]) * g_ref[...]).astype(_BF)


def _post_scan(p2, h_dir, y_f, y_b, bonus, g, ml_norm_g, ml_norm_b, rw_norm_g, rw_norm_b):
    n = y_f.shape[0]
    tm = ROW_TM
    head = jnp.arange(RW_W) // RW_DH
    ones_bd = (head[:, None] == head[None, :]).astype(_BF)
    vec = pl.BlockSpec((1, RW_W), lambda i: (0, 0))
    blk = pl.BlockSpec((tm, RW_W), lambda i: (i, 0))
    return pl.pallas_call(
        _post_scan_kernel,
        grid=(n // tm,),
        in_specs=[pl.BlockSpec((1, tm, ML_W), lambda i: (0, i, 0)), pl.BlockSpec((1, tm, ML_W), lambda i: (1, i, 0)),
                  pl.BlockSpec((tm, ML_W), lambda i: (i, P_O // ML_W)), blk, blk, blk, blk, vec, vec, vec, vec,
                  _const_spec((RW_W, RW_W))],
        out_specs=[blk, blk],
        out_shape=[jax.ShapeDtypeStruct((n, ML_W), _BF), jax.ShapeDtypeStruct((n, RW_W), _BF)],
        compiler_params=pltpu.CompilerParams(dimension_semantics=("parallel",), vmem_limit_bytes=VMEM_LIMIT_BYTES),
        name="post_scan_norm_gate",
    )(h_dir, h_dir, p2, y_f, y_b, bonus, g, ml_norm_g.reshape(1, -1), ml_norm_b.reshape(1, -1),
      rw_norm_g.reshape(1, -1), rw_norm_b.reshape(1, -1), ones_bd)


def _ln_rows(x, eps=LN_EPS):
    mu = jnp.mean(x, axis=-1, keepdims=True)
    xc = x - mu
    var = jnp.mean(xc * xc, axis=-1, keepdims=True)
    return xc * lax.rsqrt(var + eps)


def _resid_kernel(*refs, with_w):
    if with_w:
        d_ref, w_ref, x_ref, gate_ref, g_ref, b_ref, sh_ref, sc_ref, xo_ref, uo_ref = refs
        delta = jnp.dot(d_ref[...], w_ref[...], preferred_element_type=jnp.float32)
    else:
        d_ref, x_ref, gate_ref, g_ref, b_ref, sh_ref, sc_ref, xo_ref, uo_ref = refs
        delta = d_ref[...]
    xn = _ln_rows(DEEPNORM_ALPHA * x_ref[...] + gate_ref[0, 0, 0] * delta) * g_ref[...] + b_ref[...]
    xo_ref[...] = xn
    uo_ref[...] = _ln_rows(xn) * (1.0 + sc_ref[0, 0, 0]) + sh_ref[0, 0, 0]


def _resid_norm_mod(delta, w, x, mod_a, ia, ln_g, ln_b, mod_b, ish, isc, blocks_per_seq, ctx_blocks):
    n, d = x.shape
    tm = ROW_TM
    row = lambda i: (i, 0)

    def mod_spec(m):
        return pl.BlockSpec((1, 1, 1, 1, d), lambda i: (i // blocks_per_seq,
                                                       (i % blocks_per_seq >= ctx_blocks).astype(jnp.int32), m, 0, 0))

    vec = pl.BlockSpec((1, d), lambda i: (0, 0))
    if w is not None:
        in_specs = [pl.BlockSpec((tm, delta.shape[1]), row), _const_spec(w.shape), pl.BlockSpec((tm, d), row)]
        args = (delta, w, x)
    else:
        in_specs = [pl.BlockSpec((tm, d), row), pl.BlockSpec((tm, d), row)]
        args = (delta, x)
    return pl.pallas_call(
        functools.partial(_resid_kernel, with_w=w is not None),
        grid=(n // tm,),
        in_specs=in_specs + [mod_spec(ia), vec, vec, mod_spec(ish), mod_spec(isc)],
        out_specs=[pl.BlockSpec((tm, d), row)] * 2,
        out_shape=[jax.ShapeDtypeStruct((n, d), jnp.float32)] * 2,
        compiler_params=pltpu.CompilerParams(dimension_semantics=("parallel",), vmem_limit_bytes=VMEM_LIMIT_BYTES),
        name="resid_norm_mod",
    )(*args, mod_a, ln_g.reshape(1, d), ln_b.reshape(1, d), mod_b, mod_b)


def _moe_ffn_kernel(te_ref, tv_ref, x_ref, wg_ref, wu_ref, wd_ref, y_ref, wgb_ref, wub_ref, wdb_ref):
    t = pl.program_id(0)

    @pl.when((t == 0) | (te_ref[t] != te_ref[jnp.maximum(t - 1, 0)]))
    def _():
        wgb_ref[...] = wg_ref[0].astype(_BF)
        wub_ref[...] = wu_ref[0].astype(_BF)
        wdb_ref[...] = wd_ref[0].astype(_BF)

    @pl.when(tv_ref[t] == 1)
    def _():
        x = x_ref[...].astype(_BF)
        hg = jnp.dot(x, wgb_ref[...], preferred_element_type=jnp.float32)
        hu = jnp.dot(x, wub_ref[...], preferred_element_type=jnp.float32)
        h = (hg * jax.nn.sigmoid(hg)) * hu
        y_ref[...] = jnp.dot(h.astype(_BF), wdb_ref[...], preferred_element_type=jnp.float32)

    @pl.when(tv_ref[t] == 0)
    def _():
        y_ref[...] = jnp.zeros_like(y_ref)


def _moe_routed(u, gates_t, wg, wu, wd):
    n_tok, d = u.shape
    n_e, _, d_e = wg.shape
    tm = MOE_TM
    n_tiles = (TOP_K * n_tok) // tm + n_e
    n_slots = n_tiles * tm
    sel = gates_t > 0.0
    seli = sel.astype(jnp.int32)
    rank = jnp.cumsum(seli, axis=1) - 1
    cnt = jnp.sum(seli, axis=1)
    tiles_e = (cnt + tm - 1) // tm
    tile_end = jnp.cumsum(tiles_e)
    off = (tile_end - tiles_e) * tm
    slot = off[:, None] + rank
    order = jnp.cumsum(seli, axis=0)
    slots, gsel = [], []
    for j in range(TOP_K):
        pick = sel & (order == j + 1)
        slots.append(jnp.sum(jnp.where(pick, slot, 0), axis=0))
        gsel.append(jnp.sum(jnp.where(pick, gates_t, 0.0), axis=0))
    tok = jnp.arange(n_tok, dtype=jnp.int32)
    tok_of_slot = jnp.zeros((n_slots,), jnp.int32).at[jnp.concatenate(slots)].set(jnp.tile(tok, TOP_K))
    tile_ids = jnp.arange(n_tiles, dtype=jnp.int32)
    tile_valid = (tile_ids < tile_end[-1]).astype(jnp.int32)
    tile_expert = jnp.minimum(jnp.searchsorted(tile_end, tile_ids, side='right'), n_e - 1).astype(jnp.int32)
    last_e = jnp.max(jnp.where(cnt > 0, jnp.arange(n_e), 0)).astype(jnp.int32)
    tile_expert = jnp.where(tile_valid == 1, tile_expert, last_e)
    xs = jnp.take(u, tok_of_slot, axis=0)
    grid_spec = pltpu.PrefetchScalarGridSpec(
        num_scalar_prefetch=2,
        grid=(n_tiles,),
        in_specs=[pl.BlockSpec((tm, d), lambda t, te, tv: (t, 0)),
                  pl.BlockSpec((1, d, d_e), lambda t, te, tv: (te[t], 0, 0), pipeline_mode=pl.Buffered(1)),
                  pl.BlockSpec((1, d, d_e), lambda t, te, tv: (te[t], 0, 0), pipeline_mode=pl.Buffered(1)),
                  pl.BlockSpec((1, d_e, d), lambda t, te, tv: (te[t], 0, 0), pipeline_mode=pl.Buffered(1))],
        out_specs=pl.BlockSpec((tm, d), lambda t, te, tv: (t, 0)),
        scratch_shapes=[pltpu.VMEM((d, d_e), _BF), pltpu.VMEM((d, d_e), _BF), pltpu.VMEM((d_e, d), _BF)])
    ys = pl.pallas_call(
        _moe_ffn_kernel, grid_spec=grid_spec,
        out_shape=jax.ShapeDtypeStruct((n_slots, d), jnp.float32),
        compiler_params=pltpu.CompilerParams(dimension_semantics=("arbitrary",),
                                             vmem_limit_bytes=VMEM_LIMIT_BYTES),
        name="moe_routed_ffn",
    )(tile_expert, tile_valid, xs, wg, wu, wd)
    return sum(gsel[j][:, None] * jnp.take(ys, slots[j], axis=0) for j in range(TOP_K))


def _router_kernel(u_ref, wt_ref, b_ref, g_ref):
    logits = lax.dot_general(wt_ref[...], u_ref[...], _NT, precision=_HI, preferred_element_type=jnp.float32)
    aff = jax.nn.sigmoid(logits)
    score = aff + b_ref[...]
    s = [score[e:e + 1] for e in range(N_EXPERTS)]
    a = [aff[e:e + 1] for e in range(N_EXPERTS)]
    gs = []
    for g in range(N_GROUPS):
        m = s[g * EXPERTS_PER_GROUP:(g + 1) * EXPERTS_PER_GROUP]
        best = None
        for i in range(EXPERTS_PER_GROUP):
            for j in range(i + 1, EXPERTS_PER_GROUP):
                best = m[i] + m[j] if best is None else jnp.maximum(best, m[i] + m[j])
        gs.append(best)
    best_val = gs[0]
    best_grp = jnp.zeros_like(gs[0], dtype=jnp.int32)
    for g in range(1, N_GROUPS):
        better = gs[g] > best_val
        best_grp = jnp.where(better, g, best_grp)
        best_val = jnp.where(better, gs[g], best_val)
    sel = []
    for e in range(N_EXPERTS):
        g = e // EXPERTS_PER_GROUP
        rank = jnp.zeros_like(best_grp)
        for j in range(g * EXPERTS_PER_GROUP, (g + 1) * EXPERTS_PER_GROUP):
            if j != e:
                ahead = (s[j] > s[e]) | ((s[j] == s[e]) & (j < e))
                rank = rank + ahead.astype(jnp.int32)
        sel.append((best_grp == g) & (rank < TOP_K))
    wsum = sum(jnp.where(sel[e], a[e], 0.0) for e in range(N_EXPERTS))
    g_ref[...] = jnp.concatenate([jnp.where(sel[e], a[e] / wsum, 0.0) for e in range(N_EXPERTS)], axis=0)


def _router_pallas(u, router_w, router_b, tm=512):
    n_tok, d = u.shape
    assert n_tok % tm == 0
    return pl.pallas_call(
        _router_kernel,
        grid=(n_tok // tm,),
        in_specs=[pl.BlockSpec((tm, d), lambda i: (i, 0)),
                  pl.BlockSpec((N_EXPERTS, d), lambda i: (0, 0)),
                  pl.BlockSpec((N_EXPERTS, 1), lambda i: (0, 0))],
        out_specs=pl.BlockSpec((N_EXPERTS, tm), lambda i: (0, i)),
        out_shape=jax.ShapeDtypeStruct((N_EXPERTS, n_tok), jnp.float32),
        compiler_params=pltpu.CompilerParams(dimension_semantics=("parallel",),
                                             vmem_limit_bytes=VMEM_LIMIT_BYTES),
        name="moe_router",
    )(u, router_w.T, router_b.reshape(N_EXPERTS, 1))


def _s5_mats(lam_re, lam_im, log_dt, b_re, b_im, c_re, c_im):
    L = S5_CHUNK
    lam = lax.complex(lam_re, lam_im)
    ldt = lam * jnp.exp(log_dt)[..., None]
    lam_bar = jnp.exp(ldt)
    b_bar = ((lam_bar - 1.0) / lam)[..., None] * lax.complex(b_re, b_im)
    c_mat = lax.complex(c_re, c_im)
    tau = jnp.arange(L + 1, dtype=jnp.float32)
    pw = jnp.exp(ldt[:, :, None, :] * tau[None, None, :, None])
    kern = jnp.real(jnp.einsum('dgon,dgtn,dgni->dgtoi', c_mat, pw[:, :, :L], b_bar))
    s_idx = jnp.arange(L)[:, None]
    t_idx = jnp.arange(L)[None, :]

    def toeplitz(k, lag, valid):
        m = k[:, jnp.clip(lag, 0, L - 1)] * valid[None, :, :, None, None]
        return jnp.transpose(m, (0, 1, 4, 2, 3)).reshape(-1, L * S5_GROUP, L * S5_GROUP)

    tsum = (toeplitz(kern[0], t_idx - s_idx, (t_idx >= s_idx).astype(jnp.float32))
            + toeplitz(kern[1], s_idx - t_idx, (s_idx >= t_idx).astype(jnp.float32)))
    pin_f = pw[0][:, L - 1 - jnp.arange(L)]
    pin_b = pw[1][:, jnp.arange(L)]
    in_f = jnp.einsum('gsn,gni->gsin', pin_f, b_bar[0]).reshape(-1, L * S5_GROUP, S5_STATE)
    in_b = jnp.einsum('gsn,gni->gsin', pin_b, b_bar[1]).reshape(-1, L * S5_GROUP, S5_STATE)
    icat = jnp.concatenate([jnp.real(in_f), jnp.imag(in_f), jnp.real(in_b), jnp.imag(in_b)], axis=-1)
    pout_f = pw[0][:, 1 + jnp.arange(L)]
    pout_b = pw[1][:, L - jnp.arange(L)]
    out_f = jnp.einsum('gon,gtn->gnto', c_mat[0], pout_f).reshape(-1, S5_STATE, L * S5_GROUP)
    out_b = jnp.einsum('gon,gtn->gnto', c_mat[1], pout_b).reshape(-1, S5_STATE, L * S5_GROUP)
    ocat = jnp.concatenate([jnp.real(out_f), -jnp.imag(out_f), jnp.real(out_b), -jnp.imag(out_b)], axis=1)
    lam_l = pw[:, :, L]
    lam_chunk = jnp.stack([jnp.real(lam_l[0]), jnp.imag(lam_l[0]), jnp.real(lam_l[1]), jnp.imag(lam_l[1])],
                          axis=1)
    return tsum, icat, ocat, lam_chunk


def _s5_kernel(u_ref, t_ref, i_ref, o_ref, lam_ref, d_ref, y_ref, v_ref, xfr_ref, xfi_ref, xbr_ref, xbi_ref,
               *, n_chunks, ctx_chunks, bsz):
    n = S5_STATE
    u = u_ref[0]
    ub = u.astype(_BF)
    v_ref[...] = jnp.dot(ub, i_ref[0], preferred_element_type=jnp.float32)
    lam = lam_ref[0]
    lfr = jnp.broadcast_to(lam[0:1], (bsz, n))
    lfi = jnp.broadcast_to(lam[1:2], (bsz, n))
    lbr = jnp.broadcast_to(lam[2:3], (bsz, n))
    lbi = jnp.broadcast_to(lam[3:4], (bsz, n))

    def cmul_add(lr, li, xr, xi, vr, vi):
        return lr * xr - li * xi + vr, lr * xi + li * xr + vi

    def step(j, carry):
        fr, fi, br, bi = carry
        rf = pl.multiple_of(j * (2 * bsz), 2 * bsz)
        pb = jnp.where(j < ctx_chunks // 2, ctx_chunks // 2 - 1 - j, (n_chunks + ctx_chunks) // 2 - 1 - j)
        rb = pl.multiple_of(pb * (2 * bsz), 2 * bsz)
        vf = v_ref[pl.ds(rf, 2 * bsz), :]
        vb = v_ref[pl.ds(rb, 2 * bsz), :]
        fr1, fi1 = cmul_add(lfr, lfi, fr, fi, vf[:bsz, 0:n], vf[:bsz, n:2 * n])
        fr2, fi2 = cmul_add(lfr, lfi, fr1, fi1, vf[bsz:, 0:n], vf[bsz:, n:2 * n])
        br1, bi1 = cmul_add(lbr, lbi, br, bi, vb[bsz:, 2 * n:3 * n], vb[bsz:, 3 * n:4 * n])
        br2, bi2 = cmul_add(lbr, lbi, br1, bi1, vb[:bsz, 2 * n:3 * n], vb[:bsz, 3 * n:4 * n])
        xfr_ref[pl.ds(rf, 2 * bsz), :] = jnp.concatenate([fr, fr1], axis=0)
        xfi_ref[pl.ds(rf, 2 * bsz), :] = jnp.concatenate([fi, fi1], axis=0)
        xbr_ref[pl.ds(rb, 2 * bsz), :] = jnp.concatenate([br1, br], axis=0)
        xbi_ref[pl.ds(rb, 2 * bsz), :] = jnp.concatenate([bi1, bi], axis=0)
        return fr2, fi2, br2, bi2

    z = jnp.zeros((bsz, n), jnp.float32)
    lax.fori_loop(0, n_chunks // 2, step, (z, z, z, z))
    o = o_ref[0]
    y = jnp.dot(ub, t_ref[0], preferred_element_type=jnp.float32)
    y += jnp.dot(xfr_ref[...].astype(_BF), o[0:n], preferred_element_type=jnp.float32)
    y += jnp.dot(xfi_ref[...].astype(_BF), o[n:2 * n], preferred_element_type=jnp.float32)
    y += jnp.dot(xbr_ref[...].astype(_BF), o[2 * n:3 * n], preferred_element_type=jnp.float32)
    y += jnp.dot(xbi_ref[...].astype(_BF), o[3 * n:4 * n], preferred_element_type=jnp.float32)
    y += d_ref[0] * u
    y_ref[0] = 0.5 * y * (1.0 + jnp.tanh(math.sqrt(2.0 / math.pi) * (y + 0.044715 * (y * y * y))))


def _s5_pallas(u, n_ctx, lam_re, lam_im, log_dt, b_re, b_im, c_re, c_im, d_skip):
    bsz, t_len, _ = u.shape
    L, G, C = S5_CHUNK, S5_GROUPS, S5_GROUP
    nc = t_len // L
    assert (2 * bsz) % 8 == 0 and nc % 2 == 0 and (n_ctx // L) % 2 == 0
    tsum, icat, ocat, lam_chunk = _s5_mats(lam_re, lam_im, log_dt, b_re, b_im, c_re, c_im)
    ug = jnp.transpose(u.reshape(bsz, nc, L, G, C), (3, 1, 0, 2, 4)).reshape(G, nc * bsz, L * C)
    dvec = jnp.tile(d_skip.reshape(G, 1, C), (1, L, 1)).reshape(G, 1, L * C)
    rows = nc * bsz
    wspec = pl.BlockSpec((1, L * C, L * C), lambda g: (g, 0, 0))
    yg = pl.pallas_call(
        functools.partial(_s5_kernel, n_chunks=nc, ctx_chunks=n_ctx // L, bsz=bsz),
        grid=(G,),
        in_specs=[pl.BlockSpec((1, rows, L * C), lambda g: (g, 0, 0)), wspec, wspec, wspec,
                  pl.BlockSpec((1, 4, S5_STATE), lambda g: (g, 0, 0)),
                  pl.BlockSpec((1, 1, L * C), lambda g: (g, 0, 0))],
        out_specs=pl.BlockSpec((1, rows, L * C), lambda g: (g, 0, 0)),
        out_shape=jax.ShapeDtypeStruct((G, rows, L * C), jnp.float32),
        scratch_shapes=[pltpu.VMEM((rows, 4 * S5_STATE), jnp.float32)]
                       + [pltpu.VMEM((rows, S5_STATE), jnp.float32)] * 4,
        compiler_params=pltpu.CompilerParams(dimension_semantics=("parallel",),
                                             vmem_limit_bytes=VMEM_LIMIT_BYTES),
        name="s5_scan",
    )(ug, tsum.astype(_BF), icat.astype(_BF), ocat.astype(_BF), lam_chunk, dvec)
    return jnp.transpose(yg.reshape(G, nc, bsz, L, C), (2, 1, 3, 0, 4)).reshape(bsz, t_len, S5_W)


def _log_sigmoid(x):
    return jnp.minimum(x, 0.0) - jnp.log(1.0 + jnp.exp(-jnp.abs(x)))


def _mlstm_kernel(igb_ref, fgb_ref, q_ref, k_ref, v_ref, gc_ref, gr_ref, h_ref, cmat_ref, nvec_ref, m_ref):
    d = pl.program_id(0)
    L = ML_CHUNK
    n_sub = ML_TBLK // L
    H = range(ML_HEADS)

    @pl.when(pl.program_id(2) == 0)
    def _():
        cmat_ref[...] = jnp.zeros_like(cmat_ref)
        nvec_ref[...] = jnp.zeros_like(nvec_ref)
        m_ref[...] = jnp.zeros_like(m_ref)

    igb = [igb_ref[d, h] for h in H]
    fgb = [fgb_ref[d, h] for h in H]
    row = lax.broadcasted_iota(jnp.int32, (L, L), 0)
    col = lax.broadcasted_iota(jnp.int32, (L, L), 1)
    sign = 1 - 2 * d
    seen = (row - col) * sign >= 0
    seen_f = seen.astype(jnp.float32)
    seen_t = ((col - row) * sign >= 0).astype(jnp.float32)
    scale = ML_DH ** -0.5

    def chunk(jj, carry):
        cj = jnp.where(d == 0, jj, n_sub - 1 - jj)
        r0 = pl.multiple_of(cj * L, L)
        hs = lambda h: slice(h * ML_DH, (h + 1) * ML_DH)
        q = [q_ref[0, pl.ds(r0, L), hs(h)] for h in H]
        k = [k_ref[0, pl.ds(r0, L), hs(h)] * scale for h in H]
        vb = [v_ref[0, pl.ds(r0, L), hs(h)].astype(_BF) for h in H]
        gc = [gc_ref[0, 0, h, pl.ds(r0, L), :] for h in H]
        gr = [gr_ref[0, 0, h, cj] for h in H]
        li_col = [gc[h][:, 0:1] + igb[h] for h in H]
        lf_col = [_log_sigmoid(gc[h][:, 1:2] + fgb[h]) for h in H]
        li_row = [gr[h][0:1, :] + igb[h] for h in H]
        lf_row = [_log_sigmoid(gr[h][1:2, :] + fgb[h]) for h in H]
        m_prev = [m_ref[h] for h in H]
        bcum_col = [jnp.dot(seen_f, jnp.broadcast_to(lf_col[h], (L, L)), precision=_HI,
                            preferred_element_type=jnp.float32) for h in H]
        bcum_row = [jnp.dot(jnp.broadcast_to(lf_row[h], (8, L)), seen_t, precision=_HI,
                            preferred_element_type=jnp.float32)[0:1] for h in H]
        qb = [q[h].astype(_BF) for h in H]
        qk = [lax.dot_general(qb[h], k[h].astype(_BF), _NT, preferred_element_type=jnp.float32) for h in H]
        qc = [jnp.dot(qb[h], cmat_ref[h].astype(_BF), preferred_element_type=jnp.float32) for h in H]
        log_d = [jnp.where(seen, bcum_col[h] - bcum_row[h] + li_row[h], -jnp.inf) for h in H]
        inter = [bcum_col[h][:, 0:1] + m_prev[h] for h in H]
        m_j = [jnp.maximum(jnp.max(log_d[h], axis=1, keepdims=True), inter[h]) for h in H]
        scores = [qk[h] * jnp.exp(log_d[h] - m_j[h]) for h in H]
        s_inter = [jnp.exp(inter[h] - m_j[h]) for h in H]
        sv = [jnp.dot(scores[h].astype(_BF), vb[h], preferred_element_type=jnp.float32) for h in H]
        b_last = [jnp.sum(lf_col[h], axis=0, keepdims=True) for h in H]
        log_w = [b_last[h] - bcum_col[h][:, 0:1] + li_col[h] for h in H]
        m_new = [jnp.maximum(b_last[h] + m_prev[h], jnp.max(log_w[h], axis=0, keepdims=True)) for h in H]
        kw = [k[h] * jnp.exp(log_w[h] - m_new[h]) for h in H]
        decay = [jnp.exp(b_last[h] + m_prev[h] - m_new[h]) for h in H]
        kv = [lax.dot_general(kw[h].astype(_BF), vb[h], _TN, preferred_element_type=jnp.float32) for h in H]
        for h in H:
            num = sv[h] + s_inter[h] * qc[h]
            den = (jnp.sum(scores[h], axis=1, keepdims=True)
                   + s_inter[h] * jnp.sum(q[h] * nvec_ref[h], axis=1, keepdims=True))
            h_ref[0, 0, pl.ds(r0, L), hs(h)] = num / jnp.maximum(jnp.abs(den), jnp.exp(-m_j[h]))
        for h in H:
            cmat_ref[h] = decay[h] * cmat_ref[h] + kv[h]
            nvec_ref[h] = decay[h] * nvec_ref[h] + jnp.sum(kw[h], axis=0, keepdims=True)
            m_ref[h] = m_new[h]
        return carry

    lax.fori_loop(0, n_sub, chunk, 0)


def _mlstm_pallas(q, k, v, gates, ig_b, fg_b, n_ctx):
    bsz, t_len, _ = q.shape
    nb = t_len // ML_TBLK
    cb = n_ctx // ML_TBLK
    assert t_len % ML_TBLK == 0 and n_ctx % ML_TBLK == 0
    g = gates.reshape(bsz, t_len, 2, N_DIR, ML_HEADS)
    gcol = jnp.transpose(g, (3, 0, 4, 1, 2))
    grow = jnp.transpose(g.reshape(bsz, t_len // ML_CHUNK, ML_CHUNK, 2, N_DIR, ML_HEADS),
                         (4, 0, 5, 1, 3, 2))
    blk = lambda d, i: _scan_block(d, i, nb, cb)
    grid_spec = pltpu.PrefetchScalarGridSpec(
        num_scalar_prefetch=2,
        grid=(N_DIR, bsz, nb),
        in_specs=[pl.BlockSpec((1, ML_TBLK, ML_W), lambda d, b, i, *_: (b, blk(d, i), 0))] * 3 + [
            pl.BlockSpec((1, 1, ML_HEADS, ML_TBLK, 2), lambda d, b, i, *_: (d, b, 0, blk(d, i), 0)),
            pl.BlockSpec((1, 1, ML_HEADS, ML_TBLK // ML_CHUNK, 2, ML_CHUNK),
                         lambda d, b, i, *_: (d, b, 0, blk(d, i), 0, 0))],
        out_specs=pl.BlockSpec((1, 1, ML_TBLK, ML_W), lambda d, b, i, *_: (d, b, blk(d, i), 0)),
        scratch_shapes=[pltpu.VMEM((ML_HEADS, ML_DH, ML_DH), jnp.float32),
                        pltpu.VMEM((ML_HEADS, 1, ML_DH), jnp.float32),
                        pltpu.VMEM((ML_HEADS, 1, 1), jnp.float32)])
    return pl.pallas_call(
        _mlstm_kernel, grid_spec=grid_spec,
        out_shape=jax.ShapeDtypeStruct((N_DIR, bsz, t_len, ML_W), jnp.float32),
        compiler_params=pltpu.CompilerParams(
            dimension_semantics=("parallel", "parallel", "arbitrary"),
            vmem_limit_bytes=VMEM_LIMIT_BYTES),
        name="mlstm_scan",
    )(ig_b, fg_b, q, k, v, gcol, grow)


def _to_pairs(x):
    return jnp.concatenate([x[:, p * LANES:(p + 1) * LANES] for p in range(RW_PAIRS)], axis=0)


def _dotf(a, b, dims=None):
    a = a.astype(_BF)
    b = b.astype(_BF)
    if dims is None:
        return jnp.dot(a, b, preferred_element_type=jnp.float32)
    return lax.dot_general(a, b, dims, preferred_element_type=jnp.float32)


def _rwkv_a_kernel(r_ref, k_ref, v_ref, kk_ref, a_ref, lw_ref,
                   att_ref, rt_ref, bw_ref, kw_ref, vt_ref, u0t_ref, y0_ref, wc_ref, *, n_sub):
    d = pl.program_id(0)
    C, R = RW_CHUNK, RW_ROWS
    row = lax.broadcasted_iota(jnp.int32, (R, R), 0)
    col = lax.broadcasted_iota(jnp.int32, (R, R), 1)
    same = (row // C) == (col // C)
    sign = 1 - 2 * d
    before = same & ((row - col) * sign > 0)
    upto = same & ((row - col) * sign >= 0)
    upto2 = jnp.concatenate([upto, upto], axis=1)
    eye = (row == col).astype(jnp.float32)
    first = col < RW_DH
    tpos = row % C

    def chunk_group(jg, carry):
        js = [jg * RW_GROUP + i for i in range(RW_GROUP)]
        G2 = [(i, h2) for i in range(RW_GROUP) for h2 in range(2)]
        pick = lambda lst, off: [jnp.where(first, lst[2 * i][:, off:off + R], lst[2 * i + 1][:, off:off + R])
                                 for i in range(RW_GROUP)]
        r0s = [pl.multiple_of(j * C, C) for j in js]
        ld = lambda ref: [_to_pairs(ref[0, pl.ds(r0, C), :]) for r0 in r0s]
        ldd = lambda ref: [_to_pairs(ref[0, 0, pl.ds(r0, C), :]) for r0 in r0s]
        r, v, kk = ld(r_ref), ld(v_ref), ld(kk_ref)
        k, a, lw = ldd(k_ref), ldd(a_ref), ldd(lw_ref)
        fwd = d == 0
        cum, aft = [], []
        for i in range(RW_GROUP):
            pre = lw[i]
            suf = lw[i]
            for s in (1, 2, 4, 8):
                pre = pre + jnp.where(tpos >= s, pltpu.roll(pre, s, axis=0), 0.0)
                suf = suf + jnp.where(tpos < C - s, pltpu.roll(suf, R - s, axis=0), 0.0)
            cum.append(jnp.where(fwd, pre, suf))
            aft.append(jnp.where(fwd, suf, pre) - lw[i])
        a_hat = [-kk[i] * jnp.exp(cum[i] - lw[i]) for i in range(RW_GROUP)]
        r_hat = [r[i] * jnp.exp(cum[i]) for i in range(RW_GROUP)]
        vb = [v[i].astype(_BF) for i in range(RW_GROUP)]
        m = []
        for i in range(RW_GROUP):
            inv_w = jnp.exp(-cum[i])
            lhs = jnp.concatenate([jnp.where(first, a_hat[i], 0.0), jnp.where(first, 0.0, a_hat[i]),
                                   jnp.where(first, r_hat[i], 0.0), jnp.where(first, 0.0, r_hat[i])], axis=0)
            rhs = jnp.concatenate([kk[i] * a[i] * inv_w, k[i] * inv_w], axis=0)
            m.append(_dotf(lhs, rhs, _NT))
        x = [jnp.where(before, m[i][h2 * R:(h2 + 1) * R, 0:R], 0.0).astype(_BF) for i, h2 in G2]
        ak = [jnp.where(before, m[i][h2 * R:(h2 + 1) * R, R:2 * R], 0.0) for i, h2 in G2]
        rbk = [jnp.where(upto2, m[i][(2 + h2) * R:(3 + h2) * R, :], 0.0).astype(_BF) for i, h2 in G2]
        akv = [_dotf(ak[g], vb[g // 2]) for g in range(len(G2))]
        x2 = [_dotf(xx, xx).astype(_BF) for xx in x]
        x4 = [_dotf(xx, xx).astype(_BF) for xx in x2]
        x8 = [_dotf(xx, xx).astype(_BF) for xx in x4]
        t = [eye + xx.astype(jnp.float32) for xx in x]
        t = [t[g] + _dotf(t[g], x2[g]) for g in range(len(G2))]
        t = [t[g] + _dotf(t[g], x4[g]) for g in range(len(G2))]
        t = [t[g] + _dotf(t[g], x8[g]) for g in range(len(G2))]
        akv = pick(akv, 0)
        rhs2 = [jnp.concatenate([a_hat[i], akv[i]], axis=1).astype(_BF) for i in range(RW_GROUP)]
        ta = [_dotf(t[g], rhs2[g // 2]) for g in range(len(G2))]
        at = pick(ta, 0)
        u0 = pick(ta, R)
        rhs3 = [jnp.concatenate([jnp.concatenate([at[i], u0[i]], axis=1).astype(_BF),
                                 jnp.concatenate([jnp.zeros_like(vb[i]), vb[i]], axis=1)], axis=0)
                for i in range(RW_GROUP)]
        ry = [_dotf(rbk[g], rhs3[g // 2]) for g in range(len(G2))]
        rt = pick(ry, 0)
        y0 = pick(ry, R)
        for i, j in enumerate(js):
            w_aft = jnp.exp(aft[i])
            att_ref[0, 0, j] = at[i].T.astype(_BF)
            u0t_ref[0, 0, j] = u0[i].T
            vt_ref[0, 0, j] = v[i].T.astype(_BF)
            rt_ref[0, 0, j] = (r_hat[i] + rt[i]).astype(_BF)
            bw_ref[0, 0, j] = (kk[i] * a[i] * w_aft).astype(_BF)
            kw_ref[0, 0, j] = (k[i] * w_aft).astype(_BF)
            y0_ref[0, 0, j] = y0[i]
            tot = cum[i] + aft[i]
            wc_ref[0, 0, j] = jnp.exp(jnp.concatenate([tot[p * C:p * C + 1] for p in range(RW_PAIRS)], axis=0))
        return carry

    lax.fori_loop(0, n_sub // RW_GROUP, chunk_group, 0)


def _rwkv_b_kernel(*refs, n_sub):
    ins, (yf_ref, yb_ref, s_ref) = refs[:16], refs[16:]
    C = RW_CHUNK

    @pl.when(pl.program_id(1) == 0)
    def _():
        s_ref[...] = jnp.zeros_like(s_ref)

    row = lax.broadcasted_iota(jnp.int32, (LANES, LANES), 0)
    col = lax.broadcasted_iota(jnp.int32, (LANES, LANES), 1)
    diag = (row // RW_DH) == (col // RW_DH)
    pair_of_row = lax.broadcasted_iota(jnp.int32, (2 * RW_ROWS, LANES), 0) % RW_ROWS // C
    DP = [(d, p) for d in range(N_DIR) for p in range(RW_PAIRS)]

    def chunk(jj, carry):
        cjs = [jj, n_sub - 1 - jj]
        att, rt, bw, kw, vt, u0t, y0, wc = [[ins[8 * d + a][0, 0, cjs[d]] for d in range(N_DIR)] for a in range(8)]
        bk = [jnp.concatenate([bw[d], kw[d]], axis=0) for d in range(N_DIR)]
        sps = [s_ref[d, p] for d, p in DP]
        spb = [sp.astype(_BF) for sp in sps]
        uts = [jnp.dot(spb[i], att[d], preferred_element_type=jnp.float32) + u0t[d] for i, (d, p) in enumerate(DP)]
        for i, (d, p) in enumerate(DP):
            rows = slice(p * C, (p + 1) * C)
            y = lax.dot_general(rt[d][rows], spb[i], _NT, preferred_element_type=jnp.float32) + y0[d][rows]
            y_ref = yf_ref if d == 0 else yb_ref
            y_ref[0, pl.ds(pl.multiple_of(cjs[d] * C, C), C), p * LANES:(p + 1) * LANES] = y
        for i, (d, p) in enumerate(DP):
            lhs = jnp.concatenate([uts[i].astype(_BF), vt[d]], axis=1)
            rhs = jnp.where(pair_of_row == p, bk[d], jnp.zeros_like(bk[d]))
            upd = jnp.dot(lhs, rhs, preferred_element_type=jnp.float32)
            s_ref[d, p] = jnp.where(diag, wc[d][p:p + 1, :] * sps[i] + upd, 0.0)
        return carry

    lax.fori_loop(0, n_sub, chunk, 0)


def _rwkv_pallas(r, v, kk, k_dir, a_dir, lw_dir, n_ctx):
    bsz, t_len, _ = r.shape
    nb, cb = t_len // RW_TBLK, n_ctx // RW_TBLK
    assert t_len % RW_TBLK == 0 and n_ctx % RW_TBLK == 0
    n_sub = RW_TBLK // RW_CHUNK
    nc = t_len // RW_CHUNK
    sh_spec = pl.BlockSpec((1, RW_TBLK, RW_W), lambda d, b, i: (b, _scan_block(d, i, nb, cb), 0))
    dr_spec = pl.BlockSpec((1, 1, RW_TBLK, RW_W), lambda d, b, i: (d, b, _scan_block(d, i, nb, cb), 0))
    ch_spec = pl.BlockSpec((1, 1, n_sub, RW_ROWS, LANES), lambda d, b, i: (d, b, _scan_block(d, i, nb, cb), 0, 0))
    wc_spec = pl.BlockSpec((1, 1, n_sub, RW_PAIRS, LANES), lambda d, b, i: (d, b, _scan_block(d, i, nb, cb), 0, 0))
    ch_shape = lambda dt: jax.ShapeDtypeStruct((N_DIR, bsz, nc, RW_ROWS, LANES), dt)
    params = pltpu.CompilerParams(dimension_semantics=("parallel", "parallel", "arbitrary"),
                                  vmem_limit_bytes=VMEM_LIMIT_BYTES)
    chunk_local = pl.pallas_call(
        functools.partial(_rwkv_a_kernel, n_sub=n_sub),
        grid=(N_DIR, bsz, nb),
        in_specs=[sh_spec, dr_spec, sh_spec, sh_spec, dr_spec, dr_spec],
        out_specs=[ch_spec] * 7 + [wc_spec],
        out_shape=[ch_shape(_BF)] * 5 + [ch_shape(jnp.float32)] * 2
                  + [jax.ShapeDtypeStruct((N_DIR, bsz, nc, RW_PAIRS, LANES), jnp.float32)],
        compiler_params=params, name="rwkv_chunk_local",
    )(r, k_dir, v, kk, a_dir, lw_dir)
    dir_specs = []
    for d in range(N_DIR):
        blk = functools.partial(lambda d, b, i: (d, b, _scan_block(d, i, nb, cb), 0, 0), d)
        dir_specs += [pl.BlockSpec((1, 1, n_sub, RW_ROWS, LANES), blk)] * 7
        dir_specs += [pl.BlockSpec((1, 1, n_sub, RW_PAIRS, LANES), blk)]
    y_specs = [pl.BlockSpec((1, RW_TBLK, RW_W),
                            functools.partial(lambda d, b, i: (b, _scan_block(d, i, nb, cb), 0), d))
               for d in range(N_DIR)]
    return pl.pallas_call(
        functools.partial(_rwkv_b_kernel, n_sub=n_sub),
        grid=(bsz, nb),
        in_specs=dir_specs,
        out_specs=y_specs,
        out_shape=[jax.ShapeDtypeStruct((bsz, t_len, RW_W), jnp.float32)] * N_DIR,
        scratch_shapes=[pltpu.VMEM((N_DIR, RW_PAIRS, LANES, LANES), jnp.float32)],
        compiler_params=pltpu.CompilerParams(dimension_semantics=("parallel", "arbitrary"),
                                             vmem_limit_bytes=VMEM_LIMIT_BYTES),
        name="rwkv_state_scan",
    )(*chunk_local, *chunk_local)


def _layer_norm(x, eps=LN_EPS):
    mu = jnp.mean(x, axis=-1, keepdims=True)
    var = jnp.mean(jnp.square(x - mu), axis=-1, keepdims=True)
    return (x - mu) * lax.rsqrt(var + eps)


def _modulate(x, shift, scale):
    return _layer_norm(x) * (1.0 + scale) + shift


def _split_cols(z, widths):
    return jnp.split(z, np.cumsum(widths)[:-1].tolist(), axis=-1)


def _depthwise_conv3x3(z, w, b):
    ch = z.shape[-1]
    y = lax.conv_general_dilated(z, w[:, :, None, :], window_strides=(1, 1), padding='SAME',
                                 dimension_numbers=('NHWC', 'HWIO', 'NHWC'), feature_group_count=ch)
    return y + b


def _centred_shift(z):
    zp = jnp.pad(z, ((0, 0), (1, 1), (0, 0)))
    return 0.5 * (zp[:, :-2] + zp[:, 2:])


def _ctx_lat(fn, z, n_ctx):
    return jnp.concatenate([fn(z[:, :n_ctx]), fn(z[:, n_ctx:])], axis=1)


def _mlstm_branch(seg, n_ctx, conv_w, conv_b, ig_b, fg_b):
    q, k, v, ig, fg = seg
    bsz, t_len, _ = q.shape

    def conv(z):
        length = z.shape[1]
        rows, cols = (1, length) if length == n_ctx else (length // GRID_W, GRID_W)
        y = _depthwise_conv3x3(z.reshape(bsz, rows, cols, 2 * ML_W), conv_w, conv_b)
        return jax.nn.silu(y).reshape(bsz, length, 2 * ML_W)

    qk = jnp.concatenate([conv(jnp.concatenate([q[:, :n_ctx], k[:, :n_ctx]], axis=-1)),
                          conv(jnp.concatenate([q[:, n_ctx:], k[:, n_ctx:]], axis=-1))], axis=1)
    return _mlstm_pallas(qk[..., :ML_W], qk[..., ML_W:], v, jnp.concatenate([ig, fg], axis=-1), ig_b, fg_b, n_ctx)


def _rwkv7_branch(seg, n_ctx, mu, w0, w_up, a0, a_up, g_up, k_k, k_a, r_k):
    def hd(z):
        return z.reshape(z.shape[:-1] + (RW_HEADS, RW_DH))

    z = jnp.concatenate(seg, axis=-1)
    z = z + mu * (_ctx_lat(_centred_shift, z, n_ctx) - z)
    r, k, v, wd, ad, gd = _split_cols(z, RW_IN_WIDTHS)
    bsz, length, _ = r.shape
    wd = wd.reshape(bsz, length, N_DIR, RW_DECAY_RANK)
    ad = ad.reshape(bsz, length, N_DIR, RW_A_RANK)
    w_pre = w0[:, None, None, :] + jnp.einsum('bldr,drc->dblc', jnp.tanh(wd), w_up)
    log_decay = -jnp.exp(-jax.nn.softplus(-w_pre) - RW_DECAY_OFFSET)
    a = jax.nn.sigmoid(a0[:, None, None, :] + jnp.einsum('bldr,drc->dblc', ad, a_up))
    g = _mm_any(jax.nn.sigmoid(gd), g_up)
    kk = hd(k * k_k)
    kk = kk / jnp.maximum(jnp.sqrt(jnp.sum(jnp.square(kk), axis=-1, keepdims=True)), 1e-12)
    k_dir = k[None] * (1.0 + (a - 1.0) * k_a)
    bonus = jnp.sum(hd(r[None] * k_dir * r_k.reshape(-1)), axis=(0, 4))
    bonus = (bonus[..., None] * hd(v)).reshape(bsz, length, RW_W)
    y_f, y_b = _rwkv_pallas(r, v, kk.reshape(bsz, length, RW_W), k_dir, a, log_decay, n_ctx)
    return y_f, y_b, bonus, g


def _token_mixer(u, n_ctx, w_in,
                 ml_conv_w, ml_conv_b, ml_ig_b, ml_fg_b, ml_norm_g, ml_norm_b, ml_proj,
                 rw_mu, rw_w0, rw_w_up, rw_a0, rw_a_up, rw_g_up, rw_k_k, rw_k_a, rw_r_k,
                 rw_norm_g, rw_norm_b, rw_proj,
                 s5_lam_re, s5_lam_im, s5_log_dt, s5_b_re, s5_b_im, s5_c_re, s5_c_im, s5_d,
                 s5_w_val, s5_w_gate):
    bsz, t_len, _ = u.shape
    n = bsz * t_len
    p = _mm_any(u, _permute_w_in(w_in), keep_col_pad=True)

    def col(start, width):
        return p[..., start:start + width]

    n_gate = N_DIR * ML_HEADS
    h_dir = _mlstm_branch((col(P_QK, ML_W), col(P_QK + ML_W, ML_W), col(P_V, ML_W),
                           col(P_MLG, n_gate), col(P_MLG + n_gate, n_gate)),
                          n_ctx, ml_conv_w, ml_conv_b, ml_ig_b, ml_fg_b)
    y_f, y_b, bonus, g = _rwkv7_branch((col(P_RW, 3 * RW_W), col(P_RWLR, P_MLG - P_RWLR)), n_ctx, rw_mu, rw_w0,
                                       rw_w_up, rw_a0, rw_a_up, rw_g_up, rw_k_k, rw_k_a, rw_r_k)
    s5 = _s5_pallas(col(P_S5, S5_W), n_ctx, s5_lam_re, s5_lam_im, s5_log_dt, s5_b_re, s5_b_im, s5_c_re, s5_c_im,
                    s5_d)
    p2 = p.reshape(n, -1)
    flat = lambda z: z.reshape(n, -1)
    ml, rw = _post_scan(p2, h_dir.reshape(N_DIR, n, ML_W), flat(y_f), flat(y_b), flat(bonus), flat(g),
                        ml_norm_g, ml_norm_b, rw_norm_g, rw_norm_b)
    return _merge_pallas(p2, ml, rw, s5.reshape(n, -1),
                         ml_proj.astype(_BF), rw_proj.astype(_BF), s5_w_val.astype(_BF), s5_w_gate.astype(_BF))


def _permute_w_in(w_in):
    offs = np.cumsum((0,) + IN_WIDTHS)
    seg = lambda a, b: w_in[:, offs[a]:offs[b]]
    return jnp.concatenate([seg(0, 4), seg(6, 9), seg(12, 13), seg(13, 14), seg(9, 12), seg(4, 6)], axis=1)


def _moe_ffn(u, router_w, router_b, w_gate, w_up, w_down):
    gates_t = _router_pallas(u, router_w, router_b)
    return _moe_routed(u, gates_t, w_gate, w_up, w_down)


def kernel(x, c, ctx, c_ctx, ada_w, ada_b, w_in, ml_conv_w, ml_conv_b, ml_ig_b, ml_fg_b, ml_norm_g,
           ml_norm_b, ml_proj, rw_mu, rw_w0, rw_w_up, rw_a0, rw_a_up, rw_g_up, rw_k_k, rw_k_a, rw_r_k,
           rw_norm_g, rw_norm_b, rw_proj, s5_lam_re, s5_lam_im, s5_log_dt, s5_b_re, s5_b_im, s5_c_re,
           s5_c_im, s5_d, s5_w_val, s5_w_gate, w_out, ln1_g, ln1_b, ln2_g, ln2_b, router_w, router_b,
           exp_w_gate, exp_w_up, exp_w_down):
    bsz, n_ctx = ctx.shape[0], ctx.shape[1]
    t_len = n_ctx + x.shape[1]
    assert n_ctx % ROW_TM == 0 and t_len % ROW_TM == 0
    blocks_per_seq, ctx_blocks = t_len // ROW_TM, n_ctx // ROW_TM
    silu_c = jax.nn.silu(c)
    silu_cc = jax.nn.silu(c_ctx)[None, :]
    mods = []
    for i in range(DEPTH):
        mx = _mm_any(silu_c, ada_w[i]) + ada_b[i]
        mc = jnp.broadcast_to(_mm_any(silu_cc, ada_w[i]) + ada_b[i], mx.shape)
        mods.append(jnp.stack([mc, mx], axis=1).reshape(bsz, 2, N_MOD, 1, D_MODEL))
    xa = jnp.concatenate([ctx, x], axis=1).reshape(bsz * t_len, D_MODEL)
    m0 = mods[0]
    u = jnp.concatenate([_modulate(ctx, m0[:, 0, 0], m0[:, 0, 1]), _modulate(x, m0[:, 1, 0], m0[:, 1, 1])], axis=1)
    for i in range(DEPTH):
        z = _token_mixer(
            u, n_ctx, w_in[i],
            ml_conv_w[i], ml_conv_b[i], ml_ig_b[i], ml_fg_b[i], ml_norm_g[i], ml_norm_b[i], ml_proj[i],
            rw_mu[i], rw_w0[i], rw_w_up[i], rw_a0[i], rw_a_up[i], rw_g_up[i], rw_k_k[i], rw_k_a[i], rw_r_k[i],
            rw_norm_g[i], rw_norm_b[i], rw_proj[i],
            s5_lam_re[i], s5_lam_im[i], s5_log_dt[i], s5_b_re[i], s5_b_im[i], s5_c_re[i], s5_c_im[i], s5_d[i],
            s5_w_val[i], s5_w_gate[i])
        xa, u_ffn = _resid_norm_mod(z, w_out[i].astype(_BF), xa, mods[i], 2, ln1_g[i], ln1_b[i], mods[i], 3, 4,
                                    blocks_per_seq, ctx_blocks)
        ffn = _moe_ffn(u_ffn, router_w, router_b, exp_w_gate[i], exp_w_up[i], exp_w_down[i])
        xa, u = _resid_norm_mod(ffn, None, xa, mods[i], 5, ln2_g[i], ln2_b[i], mods[min(i + 1, DEPTH - 1)], 0, 1,
                                blocks_per_seq, ctx_blocks)
        u = u.reshape(bsz, t_len, D_MODEL)
    return xa.reshape(bsz, t_len, D_MODEL)[:, n_ctx:]
```

```python
import functools
import math

import jax
import jax.numpy as jnp
import numpy as np
from jax import lax
from jax.experimental import pallas as pl
from jax.experimental.pallas import tpu as pltpu

D_MODEL = 2048
DEPTH = 2
GRID_W = 64
N_DIR = 2
ML_HEADS = 4
ML_DH = 256
ML_W = ML_HEADS * ML_DH
ML_CHUNK = 64
ML_NORM_EPS = 1e-6
RW_HEADS = 16
RW_DH = 64
RW_W = RW_HEADS * RW_DH
RW_DECAY_RANK = 64
RW_A_RANK = 64
RW_G_RANK = 128
RW_DECAY_OFFSET = 0.5
RW_NORM_EPS = 64e-5
S5_W = 1024
S5_GROUP = 16
S5_GROUPS = S5_W // S5_GROUP
S5_STATE = 64
N_BRANCH = 3
N_GROUPS = 4
EXPERTS_PER_GROUP = 4
N_EXPERTS = N_GROUPS * EXPERTS_PER_GROUP
TOP_K = 2
D_EXPERT = 1024
DEEPNORM_ALPHA = (2.0 * DEPTH) ** 0.25
LN_EPS = 1e-5
N_MOD = 6
IN_WIDTHS = (ML_W, ML_W, ML_W, ML_W, N_DIR * ML_HEADS, N_DIR * ML_HEADS,
             RW_W, RW_W, RW_W, N_DIR * RW_DECAY_RANK, N_DIR * RW_A_RANK, RW_G_RANK,
             S5_W, N_BRANCH * D_MODEL)
D_IN = sum(IN_WIDTHS)
RW_IN_WIDTHS = (RW_W, RW_W, RW_W, N_DIR * RW_DECAY_RANK, N_DIR * RW_A_RANK, RW_G_RANK)

VMEM_LIMIT_BYTES = 56 * 1024 * 1024
LANES = 128

S5_CHUNK = 16
ML_TBLK = 256
RW_CHUNK = 16
RW_PAIRS = RW_HEADS // 2
RW_ROWS = RW_PAIRS * RW_CHUNK
RW_TBLK = 256
RW_GROUP = 8
RW_HALO = 8
MOE_TM = 256
ROW_TM = 256

P_QK, P_V, P_O = 0, 2 * ML_W, 3 * ML_W
P_RW = 4 * ML_W
P_S5 = P_RW + 3 * RW_W
P_GATE = P_S5 + S5_W
P_RWLR = P_GATE + N_BRANCH * D_MODEL
P_MLG = P_RWLR + 2 * N_DIR * RW_DECAY_RANK + RW_G_RANK
assert P_GATE % D_MODEL == 0 and P_MLG + 2 * N_DIR * ML_HEADS == D_IN and RW_DECAY_RANK == RW_A_RANK

_BF = jnp.bfloat16
_HI = lax.Precision.HIGHEST
_NT = (((1,), (1,)), ((), ()))
_TN = (((0,), (0,)), ((), ()))


def _scan_block(d, i, n_blocks, ctx_blocks):
    bwd = jnp.where(i < ctx_blocks, ctx_blocks - 1 - i, n_blocks - 1 + ctx_blocks - i)
    return jnp.where(d == 0, i, bwd)


def _mm_kernel(a_ref, w_ref, o_ref, abf_ref):
    @pl.when(pl.program_id(1) == 0)
    def _():
        abf_ref[...] = a_ref[...].astype(_BF)

    o_ref[...] = jnp.dot(abf_ref[...], w_ref[...], preferred_element_type=jnp.float32)


def _mm(a, w, tm, tn):
    m, k = a.shape
    n = w.shape[1]
    assert m % tm == 0 and n % tn == 0, (m, n, tm, tn)
    return pl.pallas_call(
        _mm_kernel,
        grid=(m // tm, n // tn),
        in_specs=[pl.BlockSpec((tm, k), lambda i, j: (i, 0)),
                  pl.BlockSpec((k, tn), lambda i, j: (0, j))],
        out_specs=pl.BlockSpec((tm, tn), lambda i, j: (i, j)),
        out_shape=jax.ShapeDtypeStruct((m, n), jnp.float32),
        scratch_shapes=[pltpu.VMEM((tm, k), _BF)],
        compiler_params=pltpu.CompilerParams(
            dimension_semantics=("parallel", "arbitrary"),
            vmem_limit_bytes=VMEM_LIMIT_BYTES),
        name="mm",
    )(a, w)


def _mm_any(a, w, tm=1024, tn=512, keep_col_pad=False):
    lead = a.shape[:-1]
    a2 = a.reshape(-1, a.shape[-1])
    m, n = a2.shape[0], w.shape[1]
    mp = -(-m // 8) * 8
    if mp > tm:
        mp = -(-m // tm) * tm
    tm = min(tm, mp)
    np_ = -(-n // LANES) * LANES
    if np_ > tn:
        np_ = -(-n // tn) * tn
    tn = min(tn, np_)
    if mp != m:
        a2 = jnp.pad(a2, ((0, mp - m), (0, 0)))
    wb = w.astype(_BF)
    if np_ != n:
        wb = jnp.pad(wb, ((0, 0), (0, np_ - n)))
    out = _mm(a2, wb, tm, tn)
    if keep_col_pad:
        n = np_
    if mp != m or np_ != n:
        out = out[:m, :n]
    return out.reshape(lead + (n,))


def _const_spec(shape):
    nd = len(shape)
    return pl.BlockSpec(shape, lambda i: (0,) * nd, pipeline_mode=pl.Buffered(1))


def _merge_kernel(ml_ref, rw_ref, s5_ref, g0_ref, g1_ref, g2_ref, wml_ref, wrw_ref, wval_ref, wgate_ref, z_ref):
    dot = lambda a_ref, w_ref: jnp.dot(a_ref[...].astype(_BF), w_ref[...], preferred_element_type=jnp.float32)
    sval = dot(s5_ref, wval_ref) * jax.nn.sigmoid(dot(s5_ref, wgate_ref))
    z = (jax.nn.sigmoid(g0_ref[...]) * dot(ml_ref, wml_ref) + jax.nn.sigmoid(g1_ref[...]) * dot(rw_ref, wrw_ref)
         + jax.nn.sigmoid(g2_ref[...]) * sval)
    z_ref[...] = z.astype(_BF)


def _merge_pallas(p2, ml, rw, s5, wml, wrw, wval, wgate):
    n, w = ml.shape
    tm = ROW_TM
    gb = P_GATE // D_MODEL
    row = lambda i: (i, 0)
    return pl.pallas_call(
        _merge_kernel,
        grid=(n // tm,),
        in_specs=[pl.BlockSpec((tm, w), row)] * 3
                 + [pl.BlockSpec((tm, D_MODEL), functools.partial(lambda j, i: (i, gb + j), j))
                    for j in range(N_BRANCH)]
                 + [_const_spec((w, D_MODEL))] * 4,
        out_specs=pl.BlockSpec((tm, D_MODEL), row),
        out_shape=jax.ShapeDtypeStruct((n, D_MODEL), _BF),
        compiler_params=pltpu.CompilerParams(dimension_semantics=("parallel",), vmem_limit_bytes=VMEM_LIMIT_BYTES),
        name="merge_gate_proj",
    )(ml, rw, s5, p2, p2, p2, wml, wrw, wval, wgate)


def _post_scan_kernel(hf_ref, hb_ref, o_ref, yf_ref, yb_ref, bonus_ref, g_ref, mlg_ref, mlb_ref, rwg_ref, rwb_ref,
                      ones_ref, ml_ref, rw_ref):
    h = hf_ref[0] + hb_ref[0]
    parts = []
    for hd in range(ML_HEADS):
        x = h[:, hd * ML_DH:(hd + 1) * ML_DH]
        xc = x - jnp.mean(x, axis=-1, keepdims=True)
        parts.append(xc * lax.rsqrt(jnp.mean(xc * xc, axis=-1, keepdims=True) + ML_NORM_EPS))
    hn = jnp.concatenate(parts, axis=1) * mlg_ref[...] + mlb_ref[...]
    ml_ref[...] = (jax.nn.sigmoid(o_ref[...]) * hn).astype(_BF)

    def head_mean(x):
        hi = x.astype(_BF)
        lo = (x - hi.astype(jnp.float32)).astype(_BF)
        s = (jnp.dot(hi, ones_ref[...], preferred_element_type=jnp.float32)
             + jnp.dot(lo, ones_ref[...], preferred_element_type=jnp.float32))
        return s * (1.0 / RW_DH)

    y = yf_ref[...] + yb_ref[...]
    yc = y - head_mean(y)
    yn = yc * lax.rsqrt(head_mean(yc * yc) + RW_NORM_EPS) * rwg_ref[...] + rwb_ref[...]
    rw_ref[...] = ((yn + bonus_ref[...]) * g_ref[...]).astype(_BF)


def _post_scan(p2, h_dir, y_f, y_b, bonus, g, ml_norm_g, ml_norm_b, rw_norm_g, rw_norm_b):
    n = y_f.shape[0]
    tm = ROW_TM
    head = jnp.arange(RW_W) // RW_DH
    ones_bd = (head[:, None] == head[None, :]).astype(_BF)
    vec = pl.BlockSpec((1, RW_W), lambda i: (0, 0))
    blk = pl.BlockSpec((tm, RW_W), lambda i: (i, 0))
    return pl.pallas_call(
        _post_scan_kernel,
        grid=(n // tm,),
        in_specs=[pl.BlockSpec((1, tm, ML_W), lambda i: (0, i, 0)), pl.BlockSpec((1, tm, ML_W), lambda i: (1, i, 0)),
                  pl.BlockSpec((tm, ML_W), lambda i: (i, P_O // ML_W)), blk, blk, blk, blk, vec, vec, vec, vec,
                  _const_spec((RW_W, RW_W))],
        out_specs=[blk, blk],
        out_shape=[jax.ShapeDtypeStruct((n, ML_W), _BF), jax.ShapeDtypeStruct((n, RW_W), _BF)],
        compiler_params=pltpu.CompilerParams(dimension_semantics=("parallel",), vmem_limit_bytes=VMEM_LIMIT_BYTES),
        name="post_scan_norm_gate",
    )(h_dir, h_dir, p2, y_f, y_b, bonus, g, ml_norm_g.reshape(1, -1), ml_norm_b.reshape(1, -1),
      rw_norm_g.reshape(1, -1), rw_norm_b.reshape(1, -1), ones_bd)


def _ln_rows(x, eps=LN_EPS):
    mu = jnp.mean(x, axis=-1, keepdims=True)
    xc = x - mu
    var = jnp.mean(xc * xc, axis=-1, keepdims=True)
    return xc * lax.rsqrt(var + eps)


def _resid_kernel(*refs, with_w):
    if with_w:
        d_ref, w_ref, x_ref, gate_ref, g_ref, b_ref, sh_ref, sc_ref, xo_ref, uo_ref = refs
        delta = jnp.dot(d_ref[...], w_ref[...], preferred_element_type=jnp.float32)
    else:
        d_ref, x_ref, gate_ref, g_ref, b_ref, sh_ref, sc_ref, xo_ref, uo_ref = refs
        delta = d_ref[...]
    xn = _ln_rows(DEEPNORM_ALPHA * x_ref[...] + gate_ref[0, 0, 0] * delta) * g_ref[...] + b_ref[...]
    xo_ref[...] = xn
    uo_ref[...] = _ln_rows(xn) * (1.0 + sc_ref[0, 0, 0]) + sh_ref[0, 0, 0]


def _resid_norm_mod(delta, w, x, mod_a, ia, ln_g, ln_b, mod_b, ish, isc, blocks_per_seq, ctx_blocks):
    n, d = x.shape
    tm = ROW_TM
    row = lambda i: (i, 0)

    def mod_spec(m):
        return pl.BlockSpec((1, 1, 1, 1, d), lambda i: (i // blocks_per_seq,
                                                       (i % blocks_per_seq >= ctx_blocks).astype(jnp.int32), m, 0, 0))

    vec = pl.BlockSpec((1, d), lambda i: (0, 0))
    if w is not None:
        in_specs = [pl.BlockSpec((tm, delta.shape[1]), row), _const_spec(w.shape), pl.BlockSpec((tm, d), row)]
        args = (delta, w, x)
    else:
        in_specs = [pl.BlockSpec((tm, d), row), pl.BlockSpec((tm, d), row)]
        args = (delta, x)
    return pl.pallas_call(
        functools.partial(_resid_kernel, with_w=w is not None),
        grid=(n // tm,),
        in_specs=in_specs + [mod_spec(ia), vec, vec, mod_spec(ish), mod_spec(isc)],
        out_specs=[pl.BlockSpec((tm, d), row)] * 2,
        out_shape=[jax.ShapeDtypeStruct((n, d), jnp.float32)] * 2,
        compiler_params=pltpu.CompilerParams(dimension_semantics=("parallel",), vmem_limit_bytes=VMEM_LIMIT_BYTES),
        name="resid_norm_mod",
    )(*args, mod_a, ln_g.reshape(1, d), ln_b.reshape(1, d), mod_b, mod_b)


def _moe_ffn_kernel(te_ref, tv_ref, x_ref, wg_ref, wu_ref, wd_ref, y_ref, wgb_ref, wub_ref, wdb_ref):
    t = pl.program_id(0)

    @pl.when((t == 0) | (te_ref[t] != te_ref[jnp.maximum(t - 1, 0)]))
    def _():
        wgb_ref[...] = wg_ref[0, 0].astype(_BF)
        wub_ref[...] = wu_ref[0, 0].astype(_BF)
        wdb_ref[...] = wd_ref[0, 0].astype(_BF)

    @pl.when(tv_ref[t] == 1)
    def _():
        x = x_ref[...].astype(_BF)
        hg = jnp.dot(x, wgb_ref[...], preferred_element_type=jnp.float32)
        hu = jnp.dot(x, wub_ref[...], preferred_element_type=jnp.float32)
        h = (hg * jax.nn.sigmoid(hg)) * hu
        y_ref[...] = jnp.dot(h.astype(_BF), wdb_ref[...], preferred_element_type=jnp.float32)

    @pl.when(tv_ref[t] == 0)
    def _():
        y_ref[...] = jnp.zeros_like(y_ref)


def _moe_routed(u, gates_t, layer, wg, wu, wd):
    n_tok, d = u.shape
    _, n_e, _, d_e = wg.shape
    tm = MOE_TM
    n_tiles = (TOP_K * n_tok) // tm + n_e
    n_slots = n_tiles * tm
    sel = gates_t > 0.0
    seli = sel.astype(jnp.int32)
    rank = jnp.cumsum(seli, axis=1) - 1
    cnt = jnp.sum(seli, axis=1)
    tiles_e = (cnt + tm - 1) // tm
    tile_end = jnp.cumsum(tiles_e)
    off = (tile_end - tiles_e) * tm
    slot = off[:, None] + rank
    order = jnp.cumsum(seli, axis=0)
    slots, gsel = [], []
    for j in range(TOP_K):
        pick = sel & (order == j + 1)
        slots.append(jnp.sum(jnp.where(pick, slot, 0), axis=0))
        gsel.append(jnp.sum(jnp.where(pick, gates_t, 0.0), axis=0))
    tok = jnp.arange(n_tok, dtype=jnp.int32)
    tok_of_slot = jnp.zeros((n_slots,), jnp.int32).at[jnp.concatenate(slots)].set(jnp.tile(tok, TOP_K))
    tile_ids = jnp.arange(n_tiles, dtype=jnp.int32)
    tile_valid = (tile_ids < tile_end[-1]).astype(jnp.int32)
    tile_expert = jnp.minimum(jnp.searchsorted(tile_end, tile_ids, side='right'), n_e - 1).astype(jnp.int32)
    last_e = jnp.max(jnp.where(cnt > 0, jnp.arange(n_e), 0)).astype(jnp.int32)
    tile_expert = jnp.where(tile_valid == 1, tile_expert, last_e)
    xs = u.at[tok_of_slot].get(mode='promise_in_bounds')
    w_index = lambda t, te, tv: (layer, te[t], 0, 0)
    grid_spec = pltpu.PrefetchScalarGridSpec(
        num_scalar_prefetch=2,
        grid=(n_tiles,),
        in_specs=[pl.BlockSpec((tm, d), lambda t, te, tv: (t, 0)),
                  pl.BlockSpec((1, 1, d, d_e), w_index, pipeline_mode=pl.Buffered(1)),
                  pl.BlockSpec((1, 1, d, d_e), w_index, pipeline_mode=pl.Buffered(1)),
                  pl.BlockSpec((1, 1, d_e, d), w_index, pipeline_mode=pl.Buffered(1))],
        out_specs=pl.BlockSpec((tm, d), lambda t, te, tv: (t, 0)),
        scratch_shapes=[pltpu.VMEM((d, d_e), _BF), pltpu.VMEM((d, d_e), _BF), pltpu.VMEM((d_e, d), _BF)])
    ys = pl.pallas_call(
        _moe_ffn_kernel, grid_spec=grid_spec,
        out_shape=jax.ShapeDtypeStruct((n_slots, d), jnp.float32),
        compiler_params=pltpu.CompilerParams(dimension_semantics=("arbitrary",),
                                             vmem_limit_bytes=VMEM_LIMIT_BYTES),
        name="moe_routed_ffn",
    )(tile_expert, tile_valid, xs, wg, wu, wd)
    return sum(gsel[j][:, None] * ys.at[slots[j]].get(mode='promise_in_bounds') for j in range(TOP_K))


def _router_kernel(u_ref, wt_ref, b_ref, g_ref):
    logits = lax.dot_general(wt_ref[...], u_ref[...], _NT, precision=_HI, preferred_element_type=jnp.float32)
    aff = jax.nn.sigmoid(logits)
    score = aff + b_ref[...]
    s = [score[e:e + 1] for e in range(N_EXPERTS)]
    a = [aff[e:e + 1] for e in range(N_EXPERTS)]
    gs = []
    for g in range(N_GROUPS):
        m = s[g * EXPERTS_PER_GROUP:(g + 1) * EXPERTS_PER_GROUP]
        best = None
        for i in range(EXPERTS_PER_GROUP):
            for j in range(i + 1, EXPERTS_PER_GROUP):
                best = m[i] + m[j] if best is None else jnp.maximum(best, m[i] + m[j])
        gs.append(best)
    best_val = gs[0]
    best_grp = jnp.zeros_like(gs[0], dtype=jnp.int32)
    for g in range(1, N_GROUPS):
        better = gs[g] > best_val
        best_grp = jnp.where(better, g, best_grp)
        best_val = jnp.where(better, gs[g], best_val)
    sel = []
    for e in range(N_EXPERTS):
        g = e // EXPERTS_PER_GROUP
        rank = jnp.zeros_like(best_grp)
        for j in range(g * EXPERTS_PER_GROUP, (g + 1) * EXPERTS_PER_GROUP):
            if j != e:
                ahead = (s[j] > s[e]) | ((s[j] == s[e]) & (j < e))
                rank = rank + ahead.astype(jnp.int32)
        sel.append((best_grp == g) & (rank < TOP_K))
    wsum = sum(jnp.where(sel[e], a[e], 0.0) for e in range(N_EXPERTS))
    g_ref[...] = jnp.concatenate([jnp.where(sel[e], a[e] / wsum, 0.0) for e in range(N_EXPERTS)], axis=0)


def _router_pallas(u, router_w, router_b, tm=512):
    n_tok, d = u.shape
    assert n_tok % tm == 0
    return pl.pallas_call(
        _router_kernel,
        grid=(n_tok // tm,),
        in_specs=[pl.BlockSpec((tm, d), lambda i: (i, 0)),
                  pl.BlockSpec((N_EXPERTS, d), lambda i: (0, 0)),
                  pl.BlockSpec((N_EXPERTS, 1), lambda i: (0, 0))],
        out_specs=pl.BlockSpec((N_EXPERTS, tm), lambda i: (0, i)),
        out_shape=jax.ShapeDtypeStruct((N_EXPERTS, n_tok), jnp.float32),
        compiler_params=pltpu.CompilerParams(dimension_semantics=("parallel",),
                                             vmem_limit_bytes=VMEM_LIMIT_BYTES),
        name="moe_router",
    )(u, router_w.T, router_b.reshape(N_EXPERTS, 1))


def _s5_mats(lam_re, lam_im, log_dt, b_re, b_im, c_re, c_im):
    L = S5_CHUNK
    lam = lax.complex(lam_re, lam_im)
    ldt = lam * jnp.exp(log_dt)[..., None]
    lam_bar = jnp.exp(ldt)
    b_bar = ((lam_bar - 1.0) / lam)[..., None] * lax.complex(b_re, b_im)
    c_mat = lax.complex(c_re, c_im)
    tau = jnp.arange(L + 1, dtype=jnp.float32)
    pw = jnp.exp(ldt[:, :, None, :] * tau[None, None, :, None])
    kern = jnp.real(jnp.einsum('dgon,dgtn,dgni->dgtoi', c_mat, pw[:, :, :L], b_bar))
    s_idx = jnp.arange(L)[:, None]
    t_idx = jnp.arange(L)[None, :]

    def toeplitz(k, lag, valid):
        m = k[:, jnp.clip(lag, 0, L - 1)] * valid[None, :, :, None, None]
        return jnp.transpose(m, (0, 1, 4, 2, 3)).reshape(-1, L * S5_GROUP, L * S5_GROUP)

    tsum = (toeplitz(kern[0], t_idx - s_idx, (t_idx >= s_idx).astype(jnp.float32))
            + toeplitz(kern[1], s_idx - t_idx, (s_idx >= t_idx).astype(jnp.float32)))
    pin_f = pw[0][:, L - 1 - jnp.arange(L)]
    pin_b = pw[1][:, jnp.arange(L)]
    in_f = jnp.einsum('gsn,gni->gsin', pin_f, b_bar[0]).reshape(-1, L * S5_GROUP, S5_STATE)
    in_b = jnp.einsum('gsn,gni->gsin', pin_b, b_bar[1]).reshape(-1, L * S5_GROUP, S5_STATE)
    icat = jnp.concatenate([jnp.real(in_f), jnp.imag(in_f), jnp.real(in_b), jnp.imag(in_b)], axis=-1)
    pout_f = pw[0][:, 1 + jnp.arange(L)]
    pout_b = pw[1][:, L - jnp.arange(L)]
    out_f = jnp.einsum('gon,gtn->gnto', c_mat[0], pout_f).reshape(-1, S5_STATE, L * S5_GROUP)
    out_b = jnp.einsum('gon,gtn->gnto', c_mat[1], pout_b).reshape(-1, S5_STATE, L * S5_GROUP)
    ocat = jnp.concatenate([jnp.real(out_f), -jnp.imag(out_f), jnp.real(out_b), -jnp.imag(out_b)], axis=1)
    lam_l = pw[:, :, L]
    lam_chunk = jnp.stack([jnp.real(lam_l[0]), jnp.imag(lam_l[0]), jnp.real(lam_l[1]), jnp.imag(lam_l[1])],
                          axis=1)
    return tsum, icat, ocat, lam_chunk


def _s5_kernel(u_ref, t_ref, i_ref, o_ref, lam_ref, d_ref, y_ref, v_ref, xfr_ref, xfi_ref, xbr_ref, xbi_ref,
               *, n_chunks, ctx_chunks, bsz):
    n = S5_STATE
    u = u_ref[0]
    ub = u.astype(_BF)
    v_ref[...] = jnp.dot(ub, i_ref[0], preferred_element_type=jnp.float32)
    lam = lam_ref[0]
    lfr = jnp.broadcast_to(lam[0:1], (bsz, n))
    lfi = jnp.broadcast_to(lam[1:2], (bsz, n))
    lbr = jnp.broadcast_to(lam[2:3], (bsz, n))
    lbi = jnp.broadcast_to(lam[3:4], (bsz, n))

    def cmul_add(lr, li, xr, xi, vr, vi):
        return lr * xr - li * xi + vr, lr * xi + li * xr + vi

    def step(j, carry):
        fr, fi, br, bi = carry
        rf = pl.multiple_of(j * (2 * bsz), 2 * bsz)
        pb = jnp.where(j < ctx_chunks // 2, ctx_chunks // 2 - 1 - j, (n_chunks + ctx_chunks) // 2 - 1 - j)
        rb = pl.multiple_of(pb * (2 * bsz), 2 * bsz)
        vf = v_ref[pl.ds(rf, 2 * bsz), :]
        vb = v_ref[pl.ds(rb, 2 * bsz), :]
        fr1, fi1 = cmul_add(lfr, lfi, fr, fi, vf[:bsz, 0:n], vf[:bsz, n:2 * n])
        fr2, fi2 = cmul_add(lfr, lfi, fr1, fi1, vf[bsz:, 0:n], vf[bsz:, n:2 * n])
        br1, bi1 = cmul_add(lbr, lbi, br, bi, vb[bsz:, 2 * n:3 * n], vb[bsz:, 3 * n:4 * n])
        br2, bi2 = cmul_add(lbr, lbi, br1, bi1, vb[:bsz, 2 * n:3 * n], vb[:bsz, 3 * n:4 * n])
        xfr_ref[pl.ds(rf, 2 * bsz), :] = jnp.concatenate([fr, fr1], axis=0)
        xfi_ref[pl.ds(rf, 2 * bsz), :] = jnp.concatenate([fi, fi1], axis=0)
        xbr_ref[pl.ds(rb, 2 * bsz), :] = jnp.concatenate([br1, br], axis=0)
        xbi_ref[pl.ds(rb, 2 * bsz), :] = jnp.concatenate([bi1, bi], axis=0)
        return fr2, fi2, br2, bi2

    z = jnp.zeros((bsz, n), jnp.float32)
    lax.fori_loop(0, n_chunks // 2, step, (z, z, z, z))
    o = o_ref[0]
    y = jnp.dot(ub, t_ref[0], preferred_element_type=jnp.float32)
    y += jnp.dot(xfr_ref[...].astype(_BF), o[0:n], preferred_element_type=jnp.float32)
    y += jnp.dot(xfi_ref[...].astype(_BF), o[n:2 * n], preferred_element_type=jnp.float32)
    y += jnp.dot(xbr_ref[...].astype(_BF), o[2 * n:3 * n], preferred_element_type=jnp.float32)
    y += jnp.dot(xbi_ref[...].astype(_BF), o[3 * n:4 * n], preferred_element_type=jnp.float32)
    y += d_ref[0] * u
    y_ref[0] = 0.5 * y * (1.0 + jnp.tanh(math.sqrt(2.0 / math.pi) * (y + 0.044715 * (y * y * y))))


def _s5_pallas(u, n_ctx, lam_re, lam_im, log_dt, b_re, b_im, c_re, c_im, d_skip):
    bsz, t_len, _ = u.shape
    L, G, C = S5_CHUNK, S5_GROUPS, S5_GROUP
    nc = t_len // L
    assert (2 * bsz) % 8 == 0 and nc % 2 == 0 and (n_ctx // L) % 2 == 0
    tsum, icat, ocat, lam_chunk = _s5_mats(lam_re, lam_im, log_dt, b_re, b_im, c_re, c_im)
    ug = jnp.transpose(u.reshape(bsz, nc, L, G, C), (3, 1, 0, 2, 4)).reshape(G, nc * bsz, L * C)
    dvec = jnp.tile(d_skip.reshape(G, 1, C), (1, L, 1)).reshape(G, 1, L * C)
    rows = nc * bsz
    wspec = pl.BlockSpec((1, L * C, L * C), lambda g: (g, 0, 0))
    yg = pl.pallas_call(
        functools.partial(_s5_kernel, n_chunks=nc, ctx_chunks=n_ctx // L, bsz=bsz),
        grid=(G,),
        in_specs=[pl.BlockSpec((1, rows, L * C), lambda g: (g, 0, 0)), wspec, wspec, wspec,
                  pl.BlockSpec((1, 4, S5_STATE), lambda g: (g, 0, 0)),
                  pl.BlockSpec((1, 1, L * C), lambda g: (g, 0, 0))],
        out_specs=pl.BlockSpec((1, rows, L * C), lambda g: (g, 0, 0)),
        out_shape=jax.ShapeDtypeStruct((G, rows, L * C), jnp.float32),
        scratch_shapes=[pltpu.VMEM((rows, 4 * S5_STATE), jnp.float32)]
                       + [pltpu.VMEM((rows, S5_STATE), jnp.float32)] * 4,
        compiler_params=pltpu.CompilerParams(dimension_semantics=("parallel",),
                                             vmem_limit_bytes=VMEM_LIMIT_BYTES),
        name="s5_scan",
    )(ug, tsum.astype(_BF), icat.astype(_BF), ocat.astype(_BF), lam_chunk, dvec)
    return jnp.transpose(yg.reshape(G, nc, bsz, L, C), (2, 1, 3, 0, 4)).reshape(bsz, t_len, S5_W)


def _log_sigmoid(x):
    return jnp.minimum(x, 0.0) - jnp.log(1.0 + jnp.exp(-jnp.abs(x)))


def _mlstm_kernel(igb_ref, fgb_ref, q_ref, k_ref, v_ref, gc_ref, gr_ref, h_ref, cmat_ref, nvec_ref, m_ref):
    d = pl.program_id(0)
    L = ML_CHUNK
    n_sub = ML_TBLK // L
    H = range(ML_HEADS)

    @pl.when(pl.program_id(2) == 0)
    def _():
        cmat_ref[...] = jnp.zeros_like(cmat_ref)
        nvec_ref[...] = jnp.zeros_like(nvec_ref)
        m_ref[...] = jnp.zeros_like(m_ref)

    igb = [igb_ref[d, h] for h in H]
    fgb = [fgb_ref[d, h] for h in H]
    row = lax.broadcasted_iota(jnp.int32, (L, L), 0)
    col = lax.broadcasted_iota(jnp.int32, (L, L), 1)
    sign = 1 - 2 * d
    seen = (row - col) * sign >= 0
    seen_f = seen.astype(jnp.float32)
    seen_t = ((col - row) * sign >= 0).astype(jnp.float32)
    scale = ML_DH ** -0.5

    def chunk(jj, carry):
        cj = jnp.where(d == 0, jj, n_sub - 1 - jj)
        r0 = pl.multiple_of(cj * L, L)
        hs = lambda h: slice(h * ML_DH, (h + 1) * ML_DH)
        q = [q_ref[0, pl.ds(r0, L), hs(h)] for h in H]
        k = [k_ref[0, pl.ds(r0, L), hs(h)] * scale for h in H]
        vb = [v_ref[0, pl.ds(r0, L), hs(h)].astype(_BF) for h in H]
        gc = [gc_ref[0, 0, h, pl.ds(r0, L), :] for h in H]
        gr = [gr_ref[0, 0, h, cj] for h in H]
        li_col = [gc[h][:, 0:1] + igb[h] for h in H]
        lf_col = [_log_sigmoid(gc[h][:, 1:2] + fgb[h]) for h in H]
        li_row = [gr[h][0:1, :] + igb[h] for h in H]
        lf_row = [_log_sigmoid(gr[h][1:2, :] + fgb[h]) for h in H]
        m_prev = [m_ref[h] for h in H]
        bcum_col = [jnp.dot(seen_f, jnp.broadcast_to(lf_col[h], (L, L)), precision=_HI,
                            preferred_element_type=jnp.float32) for h in H]
        bcum_row = [jnp.dot(jnp.broadcast_to(lf_row[h], (8, L)), seen_t, precision=_HI,
                            preferred_element_type=jnp.float32)[0:1] for h in H]
        qb = [q[h].astype(_BF) for h in H]
        qk = [lax.dot_general(qb[h], k[h].astype(_BF), _NT, preferred_element_type=jnp.float32) for h in H]
        qc = [jnp.dot(qb[h], cmat_ref[h].astype(_BF), preferred_element_type=jnp.float32) for h in H]
        log_d = [jnp.where(seen, bcum_col[h] - bcum_row[h] + li_row[h], -jnp.inf) for h in H]
        inter = [bcum_col[h][:, 0:1] + m_prev[h] for h in H]
        m_j = [jnp.maximum(jnp.max(log_d[h], axis=1, keepdims=True), inter[h]) for h in H]
        scores = [qk[h] * jnp.exp(log_d[h] - m_j[h]) for h in H]
        s_inter = [jnp.exp(inter[h] - m_j[h]) for h in H]
        sv = [jnp.dot(scores[h].astype(_BF), vb[h], preferred_element_type=jnp.float32) for h in H]
        b_last = [jnp.sum(lf_col[h], axis=0, keepdims=True) for h in H]
        log_w = [b_last[h] - bcum_col[h][:, 0:1] + li_col[h] for h in H]
        m_new = [jnp.maximum(b_last[h] + m_prev[h], jnp.max(log_w[h], axis=0, keepdims=True)) for h in H]
        kw = [k[h] * jnp.exp(log_w[h] - m_new[h]) for h in H]
        decay = [jnp.exp(b_last[h] + m_prev[h] - m_new[h]) for h in H]
        kv = [lax.dot_general(kw[h].astype(_BF), vb[h], _TN, preferred_element_type=jnp.float32) for h in H]
        for h in H:
            num = sv[h] + s_inter[h] * qc[h]
            den = (jnp.sum(scores[h], axis=1, keepdims=True)
                   + s_inter[h] * jnp.sum(q[h] * nvec_ref[h], axis=1, keepdims=True))
            h_ref[0, 0, pl.ds(r0, L), hs(h)] = num / jnp.maximum(jnp.abs(den), jnp.exp(-m_j[h]))
        for h in H:
            cmat_ref[h] = decay[h] * cmat_ref[h] + kv[h]
            nvec_ref[h] = decay[h] * nvec_ref[h] + jnp.sum(kw[h], axis=0, keepdims=True)
            m_ref[h] = m_new[h]
        return carry

    lax.fori_loop(0, n_sub, chunk, 0)


def _mlstm_pallas(q, k, v, gates, ig_b, fg_b, n_ctx):
    bsz, t_len, _ = q.shape
    nb = t_len // ML_TBLK
    cb = n_ctx // ML_TBLK
    assert t_len % ML_TBLK == 0 and n_ctx % ML_TBLK == 0
    g = gates.reshape(bsz, t_len, 2, N_DIR, ML_HEADS)
    gcol = jnp.transpose(g, (3, 0, 4, 1, 2))
    grow = jnp.transpose(g.reshape(bsz, t_len // ML_CHUNK, ML_CHUNK, 2, N_DIR, ML_HEADS),
                         (4, 0, 5, 1, 3, 2))
    blk = lambda d, i: _scan_block(d, i, nb, cb)
    grid_spec = pltpu.PrefetchScalarGridSpec(
        num_scalar_prefetch=2,
        grid=(N_DIR, bsz, nb),
        in_specs=[pl.BlockSpec((1, ML_TBLK, ML_W), lambda d, b, i, *_: (b, blk(d, i), 0))] * 3 + [
            pl.BlockSpec((1, 1, ML_HEADS, ML_TBLK, 2), lambda d, b, i, *_: (d, b, 0, blk(d, i), 0)),
            pl.BlockSpec((1, 1, ML_HEADS, ML_TBLK // ML_CHUNK, 2, ML_CHUNK),
                         lambda d, b, i, *_: (d, b, 0, blk(d, i), 0, 0))],
        out_specs=pl.BlockSpec((1, 1, ML_TBLK, ML_W), lambda d, b, i, *_: (d, b, blk(d, i), 0)),
        scratch_shapes=[pltpu.VMEM((ML_HEADS, ML_DH, ML_DH), jnp.float32),
                        pltpu.VMEM((ML_HEADS, 1, ML_DH), jnp.float32),
                        pltpu.VMEM((ML_HEADS, 1, 1), jnp.float32)])
    return pl.pallas_call(
        _mlstm_kernel, grid_spec=grid_spec,
        out_shape=jax.ShapeDtypeStruct((N_DIR, bsz, t_len, ML_W), jnp.float32),
        compiler_params=pltpu.CompilerParams(
            dimension_semantics=("parallel", "parallel", "arbitrary"),
            vmem_limit_bytes=VMEM_LIMIT_BYTES),
        name="mlstm_scan",
    )(ig_b, fg_b, q, k, v, gcol, grow)


def _to_pairs(x):
    return jnp.concatenate([x[:, p * LANES:(p + 1) * LANES] for p in range(RW_PAIRS)], axis=0)


def _dotf(a, b, dims=None):
    a = a.astype(_BF)
    b = b.astype(_BF)
    if dims is None:
        return jnp.dot(a, b, preferred_element_type=jnp.float32)
    return lax.dot_general(a, b, dims, preferred_element_type=jnp.float32)


def _rwkv_a_kernel(r_ref, k_ref, v_ref, kk_ref, a_ref, lw_ref,
                   att_ref, rt_ref, bw_ref, kw_ref, vt_ref, u0t_ref, y0_ref, wc_ref, *, n_sub):
    d = pl.program_id(0)
    C, R = RW_CHUNK, RW_ROWS
    row = lax.broadcasted_iota(jnp.int32, (R, R), 0)
    col = lax.broadcasted_iota(jnp.int32, (R, R), 1)
    same = (row // C) == (col // C)
    sign = 1 - 2 * d
    before = same & ((row - col) * sign > 0)
    upto = same & ((row - col) * sign >= 0)
    upto2 = jnp.concatenate([upto, upto], axis=1)
    eye = (row == col).astype(jnp.float32)
    first = col < RW_DH
    tpos = row % C

    def chunk_group(jg, carry):
        js = [jg * RW_GROUP + i for i in range(RW_GROUP)]
        G2 = [(i, h2) for i in range(RW_GROUP) for h2 in range(2)]
        pick = lambda lst, off: [jnp.where(first, lst[2 * i][:, off:off + R], lst[2 * i + 1][:, off:off + R])
                                 for i in range(RW_GROUP)]
        r0s = [pl.multiple_of(j * C, C) for j in js]
        ld = lambda ref: [_to_pairs(ref[0, pl.ds(r0, C), :]) for r0 in r0s]
        ldd = lambda ref: [_to_pairs(ref[0, 0, pl.ds(r0, C), :]) for r0 in r0s]
        r, v, kk = ld(r_ref), ld(v_ref), ld(kk_ref)
        k, a, lw = ldd(k_ref), ldd(a_ref), ldd(lw_ref)
        fwd = d == 0
        cum, aft = [], []
        for i in range(RW_GROUP):
            pre = lw[i]
            suf = lw[i]
            for s in (1, 2, 4, 8):
                pre = pre + jnp.where(tpos >= s, pltpu.roll(pre, s, axis=0), 0.0)
                suf = suf + jnp.where(tpos < C - s, pltpu.roll(suf, R - s, axis=0), 0.0)
            cum.append(jnp.where(fwd, pre, suf))
            aft.append(jnp.where(fwd, suf, pre) - lw[i])
        a_hat = [-kk[i] * jnp.exp(cum[i] - lw[i]) for i in range(RW_GROUP)]
        r_hat = [r[i] * jnp.exp(cum[i]) for i in range(RW_GROUP)]
        vb = [v[i].astype(_BF) for i in range(RW_GROUP)]
        m = []
        for i in range(RW_GROUP):
            inv_w = jnp.exp(-cum[i])
            lhs = jnp.concatenate([jnp.where(first, a_hat[i], 0.0), jnp.where(first, 0.0, a_hat[i]),
                                   jnp.where(first, r_hat[i], 0.0), jnp.where(first, 0.0, r_hat[i])], axis=0)
            rhs = jnp.concatenate([kk[i] * a[i] * inv_w, k[i] * inv_w], axis=0)
            m.append(_dotf(lhs, rhs, _NT))
        x = [jnp.where(before, m[i][h2 * R:(h2 + 1) * R, 0:R], 0.0).astype(_BF) for i, h2 in G2]
        ak = [jnp.where(before, m[i][h2 * R:(h2 + 1) * R, R:2 * R], 0.0) for i, h2 in G2]
        rbk = [jnp.where(upto2, m[i][(2 + h2) * R:(3 + h2) * R, :], 0.0).astype(_BF) for i, h2 in G2]
        akv = [_dotf(ak[g], vb[g // 2]) for g in range(len(G2))]
        x2 = [_dotf(xx, xx).astype(_BF) for xx in x]
        x4 = [_dotf(xx, xx).astype(_BF) for xx in x2]
        x8 = [_dotf(xx, xx).astype(_BF) for xx in x4]
        t = [eye + xx.astype(jnp.float32) for xx in x]
        t = [t[g] + _dotf(t[g], x2[g]) for g in range(len(G2))]
        t = [t[g] + _dotf(t[g], x4[g]) for g in range(len(G2))]
        t = [t[g] + _dotf(t[g], x8[g]) for g in range(len(G2))]
        akv = pick(akv, 0)
        rhs2 = [jnp.concatenate([a_hat[i], akv[i]], axis=1).astype(_BF) for i in range(RW_GROUP)]
        ta = [_dotf(t[g], rhs2[g // 2]) for g in range(len(G2))]
        at = pick(ta, 0)
        u0 = pick(ta, R)
        rhs3 = [jnp.concatenate([jnp.concatenate([at[i], u0[i]], axis=1).astype(_BF),
                                 jnp.concatenate([jnp.zeros_like(vb[i]), vb[i]], axis=1)], axis=0)
                for i in range(RW_GROUP)]
        ry = [_dotf(rbk[g], rhs3[g // 2]) for g in range(len(G2))]
        rt = pick(ry, 0)
        y0 = pick(ry, R)
        for i, j in enumerate(js):
            w_aft = jnp.exp(aft[i])
            att_ref[0, 0, j] = at[i].T.astype(_BF)
            u0t_ref[0, 0, j] = u0[i].T
            vt_ref[0, 0, j] = v[i].T.astype(_BF)
            rt_ref[0, 0, j] = (r_hat[i] + rt[i]).astype(_BF)
            bw_ref[0, 0, j] = (kk[i] * a[i] * w_aft).astype(_BF)
            kw_ref[0, 0, j] = (k[i] * w_aft).astype(_BF)
            y0_ref[0, 0, j] = y0[i]
            tot = cum[i] + aft[i]
            wc_ref[0, 0, j] = jnp.exp(jnp.concatenate([tot[p * C:p * C + 1] for p in range(RW_PAIRS)], axis=0))
        return carry

    lax.fori_loop(0, n_sub // RW_GROUP, chunk_group, 0)


def _rwkv_b_kernel(*refs, n_sub):
    ins, (yf_ref, yb_ref, s_ref) = refs[:16], refs[16:]
    C = RW_CHUNK

    @pl.when(pl.program_id(1) == 0)
    def _():
        s_ref[...] = jnp.zeros_like(s_ref)

    row = lax.broadcasted_iota(jnp.int32, (LANES, LANES), 0)
    col = lax.broadcasted_iota(jnp.int32, (LANES, LANES), 1)
    diag = (row // RW_DH) == (col // RW_DH)
    pair_of_row = lax.broadcasted_iota(jnp.int32, (2 * RW_ROWS, LANES), 0) % RW_ROWS // C
    DP = [(d, p) for d in range(N_DIR) for p in range(RW_PAIRS)]

    def chunk(jj, carry):
        cjs = [jj, n_sub - 1 - jj]
        att, rt, bw, kw, vt, u0t, y0, wc = [[ins[8 * d + a][0, 0, cjs[d]] for d in range(N_DIR)] for a in range(8)]
        bk = [jnp.concatenate([bw[d], kw[d]], axis=0) for d in range(N_DIR)]
        sps = [s_ref[d, p] for d, p in DP]
        spb = [sp.astype(_BF) for sp in sps]
        uts = [jnp.dot(spb[i], att[d], preferred_element_type=jnp.float32) + u0t[d] for i, (d, p) in enumerate(DP)]
        for i, (d, p) in enumerate(DP):
            rows = slice(p * C, (p + 1) * C)
            y = lax.dot_general(rt[d][rows], spb[i], _NT, preferred_element_type=jnp.float32) + y0[d][rows]
            y_ref = yf_ref if d == 0 else yb_ref
            y_ref[0, pl.ds(pl.multiple_of(cjs[d] * C, C), C), p * LANES:(p + 1) * LANES] = y
        for i, (d, p) in enumerate(DP):
            lhs = jnp.concatenate([uts[i].astype(_BF), vt[d]], axis=1)
            rhs = jnp.where(pair_of_row == p, bk[d], jnp.zeros_like(bk[d]))
            upd = jnp.dot(lhs, rhs, preferred_element_type=jnp.float32)
            s_ref[d, p] = jnp.where(diag, wc[d][p:p + 1, :] * sps[i] + upd, 0.0)
        return carry

    lax.fori_loop(0, n_sub, chunk, 0)


def _rwkv_pallas(r, v, kk, k_dir, a_dir, lw_dir, n_ctx):
    bsz, t_len, _ = r.shape
    nb, cb = t_len // RW_TBLK, n_ctx // RW_TBLK
    assert t_len % RW_TBLK == 0 and n_ctx % RW_TBLK == 0
    n_sub = RW_TBLK // RW_CHUNK
    nc = t_len // RW_CHUNK
    sh_spec = pl.BlockSpec((1, RW_TBLK, RW_W), lambda d, b, i: (b, _scan_block(d, i, nb, cb), 0))
    dr_spec = pl.BlockSpec((1, 1, RW_TBLK, RW_W), lambda d, b, i: (d, b, _scan_block(d, i, nb, cb), 0))
    ch_spec = pl.BlockSpec((1, 1, n_sub, RW_ROWS, LANES), lambda d, b, i: (d, b, _scan_block(d, i, nb, cb), 0, 0))
    wc_spec = pl.BlockSpec((1, 1, n_sub, RW_PAIRS, LANES), lambda d, b, i: (d, b, _scan_block(d, i, nb, cb), 0, 0))
    ch_shape = lambda dt: jax.ShapeDtypeStruct((N_DIR, bsz, nc, RW_ROWS, LANES), dt)
    params = pltpu.CompilerParams(dimension_semantics=("parallel", "parallel", "arbitrary"),
                                  vmem_limit_bytes=VMEM_LIMIT_BYTES)
    chunk_local = pl.pallas_call(
        functools.partial(_rwkv_a_kernel, n_sub=n_sub),
        grid=(N_DIR, bsz, nb),
        in_specs=[sh_spec, dr_spec, sh_spec, sh_spec, dr_spec, dr_spec],
        out_specs=[ch_spec] * 7 + [wc_spec],
        out_shape=[ch_shape(_BF)] * 5 + [ch_shape(jnp.float32)] * 2
                  + [jax.ShapeDtypeStruct((N_DIR, bsz, nc, RW_PAIRS, LANES), jnp.float32)],
        compiler_params=params, name="rwkv_chunk_local",
    )(r, k_dir, v, kk, a_dir, lw_dir)
    dir_specs = []
    for d in range(N_DIR):
        blk = functools.partial(lambda d, b, i: (d, b, _scan_block(d, i, nb, cb), 0, 0), d)
        dir_specs += [pl.BlockSpec((1, 1, n_sub, RW_ROWS, LANES), blk)] * 7
        dir_specs += [pl.BlockSpec((1, 1, n_sub, RW_PAIRS, LANES), blk)]
    y_specs = [pl.BlockSpec((1, RW_TBLK, RW_W),
                            functools.partial(lambda d, b, i: (b, _scan_block(d, i, nb, cb), 0), d))
               for d in range(N_DIR)]
    return pl.pallas_call(
        functools.partial(_rwkv_b_kernel, n_sub=n_sub),
        grid=(bsz, nb),
        in_specs=dir_specs,
        out_specs=y_specs,
        out_shape=[jax.ShapeDtypeStruct((bsz, t_len, RW_W), jnp.float32)] * N_DIR,
        scratch_shapes=[pltpu.VMEM((N_DIR, RW_PAIRS, LANES, LANES), jnp.float32)],
        compiler_params=pltpu.CompilerParams(dimension_semantics=("parallel", "arbitrary"),
                                             vmem_limit_bytes=VMEM_LIMIT_BYTES),
        name="rwkv_state_scan",
    )(*chunk_local, *chunk_local)


def _softplus(x):
    return jnp.maximum(x, 0.0) + jnp.log(1.0 + jnp.exp(-jnp.abs(x)))


def _rwkv_prep_kernel(*refs, blocks_per_seq, ctx_blocks):
    (zr, zk, zv, zw, za, zg, pr, pk, pv, pw, pa, pg, nr, nk, nv, nw, na, ng,
     mu_ref, w0_ref, a0_ref, wup_ref, aup_ref, gup_ref, kkw_ref, ka_ref, rk_ref, ones_ref,
     r_out, v_out, kk_out, kdir_out, a_out, lw_out, bonus_out, g_out) = refs
    blk = pl.program_id(0) % blocks_per_seq
    has_prev = (blk != 0) & (blk != ctx_blocks)
    has_next = (blk != ctx_blocks - 1) & (blk != blocks_per_seq - 1)
    tm = zr.shape[0]

    def lerp(z_ref, p_ref, n_ref, mu):
        z = z_ref[...]
        rows = lax.broadcasted_iota(jnp.int32, z.shape, 0)
        prev_row = jnp.where(has_prev, p_ref[RW_HALO - 1:RW_HALO, :], 0.0)
        next_row = jnp.where(has_next, n_ref[0:1, :], 0.0)
        before = jnp.where(rows == 0, prev_row, pltpu.roll(z, 1, axis=0))
        after = jnp.where(rows == tm - 1, next_row, pltpu.roll(z, tm - 1, axis=0))
        return z + mu * (0.5 * (before + after) - z)

    mu = mu_ref[...]
    r = lerp(zr, pr, nr, mu[:, 0:RW_W])
    k = lerp(zk, pk, nk, mu[:, RW_W:2 * RW_W])
    v = lerp(zv, pv, nv, mu[:, 2 * RW_W:3 * RW_W])
    wd = lerp(zw, pw, nw, mu[:, 3 * RW_W:3 * RW_W + LANES])
    ad = lerp(za, pa, na, mu[:, 3 * RW_W + LANES:3 * RW_W + 2 * LANES])
    gd = lerp(zg, pg, ng, mu[:, 3 * RW_W + 2 * LANES:3 * RW_W + 3 * LANES])
    dot = lambda a, w_ref: jnp.dot(a.astype(_BF), w_ref[...], preferred_element_type=jnp.float32)
    w_pre = dot(jnp.tanh(wd), wup_ref)
    a_pre = dot(ad, aup_ref)
    g_out[...] = dot(jax.nn.sigmoid(gd), gup_ref)

    def head_sum(x):
        hi = x.astype(_BF)
        lo = (x - hi.astype(jnp.float32)).astype(_BF)
        return (jnp.dot(hi, ones_ref[...], preferred_element_type=jnp.float32)
                + jnp.dot(lo, ones_ref[...], preferred_element_type=jnp.float32))

    kk = k * kkw_ref[...]
    kk_out[...] = kk / jnp.maximum(jnp.sqrt(head_sum(kk * kk)), 1e-12)
    r_out[...] = r
    v_out[...] = v
    rk = r * rk_ref[...]
    bonus = jnp.zeros_like(r)
    for d in range(N_DIR):
        cols = slice(d * RW_W, (d + 1) * RW_W)
        lw_out[d] = -jnp.exp(-_softplus(-(w0_ref[d:d + 1, :] + w_pre[:, cols])) - RW_DECAY_OFFSET)
        a = jax.nn.sigmoid(a0_ref[d:d + 1, :] + a_pre[:, cols])
        a_out[d] = a
        k_dir = k * (1.0 + (a - 1.0) * ka_ref[...])
        kdir_out[d] = k_dir
        bonus = bonus + rk * k_dir
    bonus_out[...] = head_sum(bonus) * v


def _rwkv_prep(p2, blocks_per_seq, ctx_blocks, mu, w0, w_up, a0, a_up, g_up, k_k, k_a, r_k):
    n = p2.shape[0]
    tm = ROW_TM
    hb = tm // RW_HALO
    n_hb = n // RW_HALO
    cw, cl = P_RW // RW_W, P_RWLR // LANES
    assert P_RW % RW_W == 0 and P_RWLR % LANES == 0 and N_DIR * RW_DECAY_RANK == LANES and RW_G_RANK == LANES

    def specs(rows, row_index):
        wide = [pl.BlockSpec((rows, RW_W), functools.partial(lambda j, i: (row_index(i), cw + j), j))
                for j in range(3)]
        return wide + [pl.BlockSpec((rows, LANES), functools.partial(lambda j, i: (row_index(i), cl + j), j))
                       for j in range(3)]

    main = specs(tm, lambda i: i)
    prev = specs(RW_HALO, lambda i: jnp.maximum(i * hb - 1, 0))
    nxt = specs(RW_HALO, lambda i: jnp.minimum((i + 1) * hb, n_hb - 1))
    zero = jnp.zeros((RW_DECAY_RANK, RW_W), jnp.float32)
    both_dirs = lambda up: jnp.concatenate([jnp.concatenate([up[0], zero], axis=1),
                                            jnp.concatenate([zero, up[1]], axis=1)], axis=0).astype(_BF)
    head = jnp.arange(RW_W) // RW_DH
    ones_bd = (head[:, None] == head[None, :]).astype(_BF)
    vec = lambda w: pl.BlockSpec((1, w), lambda i: (0, 0))
    dvec = pl.BlockSpec((N_DIR, RW_W), lambda i: (0, 0))
    consts = [vec(mu.shape[0]), dvec, dvec, _const_spec((LANES, N_DIR * RW_W)), _const_spec((LANES, N_DIR * RW_W)),
              _const_spec((RW_G_RANK, RW_W)), vec(RW_W), vec(RW_W), vec(RW_W), _const_spec((RW_W, RW_W))]
    row = pl.BlockSpec((tm, RW_W), lambda i: (i, 0))
    drow = pl.BlockSpec((N_DIR, tm, RW_W), lambda i: (0, i, 0))
    sh = jax.ShapeDtypeStruct((n, RW_W), jnp.float32)
    dsh = jax.ShapeDtypeStruct((N_DIR, n, RW_W), jnp.float32)
    return pl.pallas_call(
        functools.partial(_rwkv_prep_kernel, blocks_per_seq=blocks_per_seq, ctx_blocks=ctx_blocks),
        grid=(n // tm,),
        in_specs=main + prev + nxt + consts,
        out_specs=[row, row, row, drow, drow, drow, row, row],
        out_shape=[sh, sh, sh, dsh, dsh, dsh, sh, sh],
        compiler_params=pltpu.CompilerParams(dimension_semantics=("parallel",), vmem_limit_bytes=VMEM_LIMIT_BYTES),
        name="rwkv_prep",
    )(*([p2] * 18), mu.reshape(1, -1), w0, a0, both_dirs(w_up), both_dirs(a_up), g_up.astype(_BF),
      k_k.reshape(1, -1), k_a.reshape(1, -1), r_k.reshape(1, -1), ones_bd)


def _layer_norm(x, eps=LN_EPS):
    mu = jnp.mean(x, axis=-1, keepdims=True)
    var = jnp.mean(jnp.square(x - mu), axis=-1, keepdims=True)
    return (x - mu) * lax.rsqrt(var + eps)


def _modulate(x, shift, scale):
    return _layer_norm(x) * (1.0 + scale) + shift


def _depthwise_conv3x3(z, w, b):
    ch = z.shape[-1]
    y = lax.conv_general_dilated(z, w[:, :, None, :], window_strides=(1, 1), padding='SAME',
                                 dimension_numbers=('NHWC', 'HWIO', 'NHWC'), feature_group_count=ch)
    return y + b


def _mlstm_branch(seg, n_ctx, conv_w, conv_b, ig_b, fg_b):
    q, k, v, ig, fg = seg
    bsz, t_len, _ = q.shape

    def conv(z):
        length = z.shape[1]
        rows, cols = (1, length) if length == n_ctx else (length // GRID_W, GRID_W)
        y = _depthwise_conv3x3(z.reshape(bsz, rows, cols, 2 * ML_W), conv_w, conv_b)
        return jax.nn.silu(y).reshape(bsz, length, 2 * ML_W)

    qk = jnp.concatenate([conv(jnp.concatenate([q[:, :n_ctx], k[:, :n_ctx]], axis=-1)),
                          conv(jnp.concatenate([q[:, n_ctx:], k[:, n_ctx:]], axis=-1))], axis=1)
    return _mlstm_pallas(qk[..., :ML_W], qk[..., ML_W:], v, jnp.concatenate([ig, fg], axis=-1), ig_b, fg_b, n_ctx)


def _rwkv7_branch(p2, bsz, n_ctx, mu, w0, w_up, a0, a_up, g_up, k_k, k_a, r_k):
    t_len = p2.shape[0] // bsz
    r, v, kk, k_dir, a, log_decay, bonus, g = _rwkv_prep(p2, t_len // ROW_TM, n_ctx // ROW_TM, mu, w0, w_up, a0,
                                                         a_up, g_up, k_k, k_a, r_k)
    seq = lambda z: z.reshape(z.shape[:-2] + (bsz, t_len, RW_W))
    y_f, y_b = _rwkv_pallas(seq(r), seq(v), seq(kk), seq(k_dir), seq(a), seq(log_decay), n_ctx)
    return y_f.reshape(-1, RW_W), y_b.reshape(-1, RW_W), bonus, g


def _token_mixer(u, n_ctx, w_in,
                 ml_conv_w, ml_conv_b, ml_ig_b, ml_fg_b, ml_norm_g, ml_norm_b, ml_proj,
                 rw_mu, rw_w0, rw_w_up, rw_a0, rw_a_up, rw_g_up, rw_k_k, rw_k_a, rw_r_k,
                 rw_norm_g, rw_norm_b, rw_proj,
                 s5_lam_re, s5_lam_im, s5_log_dt, s5_b_re, s5_b_im, s5_c_re, s5_c_im, s5_d,
                 s5_w_val, s5_w_gate):
    bsz, t_len, _ = u.shape
    n = bsz * t_len
    p = _mm_any(u, _permute_w_in(w_in), keep_col_pad=True)

    def col(start, width):
        return p[..., start:start + width]

    n_gate = N_DIR * ML_HEADS
    h_dir = _mlstm_branch((col(P_QK, ML_W), col(P_QK + ML_W, ML_W), col(P_V, ML_W),
                           col(P_MLG, n_gate), col(P_MLG + n_gate, n_gate)),
                          n_ctx, ml_conv_w, ml_conv_b, ml_ig_b, ml_fg_b)
    p2 = p.reshape(n, -1)
    y_f, y_b, bonus, g = _rwkv7_branch(p2, bsz, n_ctx, rw_mu, rw_w0, rw_w_up, rw_a0, rw_a_up, rw_g_up,
                                       rw_k_k, rw_k_a, rw_r_k)
    s5 = _s5_pallas(col(P_S5, S5_W), n_ctx, s5_lam_re, s5_lam_im, s5_log_dt, s5_b_re, s5_b_im, s5_c_re, s5_c_im,
                    s5_d)
    ml, rw = _post_scan(p2, h_dir.reshape(N_DIR, n, ML_W), y_f, y_b, bonus, g,
                        ml_norm_g, ml_norm_b, rw_norm_g, rw_norm_b)
    return _merge_pallas(p2, ml, rw, s5.reshape(n, -1),
                         ml_proj.astype(_BF), rw_proj.astype(_BF), s5_w_val.astype(_BF), s5_w_gate.astype(_BF))


def _permute_w_in(w_in):
    offs = np.cumsum((0,) + IN_WIDTHS)
    seg = lambda a, b: w_in[:, offs[a]:offs[b]]
    return jnp.concatenate([seg(0, 4), seg(6, 9), seg(12, 13), seg(13, 14), seg(9, 12), seg(4, 6)], axis=1)


def _moe_ffn(u, router_w, router_b, layer, w_gate, w_up, w_down):
    gates_t = _router_pallas(u, router_w, router_b)
    return _moe_routed(u, gates_t, layer, w_gate, w_up, w_down)


def kernel(x, c, ctx, c_ctx, ada_w, ada_b, w_in, ml_conv_w, ml_conv_b, ml_ig_b, ml_fg_b, ml_norm_g,
           ml_norm_b, ml_proj, rw_mu, rw_w0, rw_w_up, rw_a0, rw_a_up, rw_g_up, rw_k_k, rw_k_a, rw_r_k,
           rw_norm_g, rw_norm_b, rw_proj, s5_lam_re, s5_lam_im, s5_log_dt, s5_b_re, s5_b_im, s5_c_re,
           s5_c_im, s5_d, s5_w_val, s5_w_gate, w_out, ln1_g, ln1_b, ln2_g, ln2_b, router_w, router_b,
           exp_w_gate, exp_w_up, exp_w_down):
    bsz, n_ctx = ctx.shape[0], ctx.shape[1]
    t_len = n_ctx + x.shape[1]
    assert n_ctx % ROW_TM == 0 and t_len % ROW_TM == 0
    blocks_per_seq, ctx_blocks = t_len // ROW_TM, n_ctx // ROW_TM
    silu_c = jax.nn.silu(c)
    silu_cc = jax.nn.silu(c_ctx)[None, :]
    mods = []
    for i in range(DEPTH):
        mx = _mm_any(silu_c, ada_w[i]) + ada_b[i]
        mc = jnp.broadcast_to(_mm_any(silu_cc, ada_w[i]) + ada_b[i], mx.shape)
        mods.append(jnp.stack([mc, mx], axis=1).reshape(bsz, 2, N_MOD, 1, D_MODEL))
    xa = jnp.concatenate([ctx, x], axis=1).reshape(bsz * t_len, D_MODEL)
    m0 = mods[0]
    u = jnp.concatenate([_modulate(ctx, m0[:, 0, 0], m0[:, 0, 1]), _modulate(x, m0[:, 1, 0], m0[:, 1, 1])], axis=1)
    for i in range(DEPTH):
        z = _token_mixer(
            u, n_ctx, w_in[i],
            ml_conv_w[i], ml_conv_b[i], ml_ig_b[i], ml_fg_b[i], ml_norm_g[i], ml_norm_b[i], ml_proj[i],
            rw_mu[i], rw_w0[i], rw_w_up[i], rw_a0[i], rw_a_up[i], rw_g_up[i], rw_k_k[i], rw_k_a[i], rw_r_k[i],
            rw_norm_g[i], rw_norm_b[i], rw_proj[i],
            s5_lam_re[i], s5_lam_im[i], s5_log_dt[i], s5_b_re[i], s5_b_im[i], s5_c_re[i], s5_c_im[i], s5_d[i],
            s5_w_val[i], s5_w_gate[i])
        xa, u_ffn = _resid_norm_mod(z, w_out[i].astype(_BF), xa, mods[i], 2, ln1_g[i], ln1_b[i], mods[i], 3, 4,
                                    blocks_per_seq, ctx_blocks)
        ffn = _moe_ffn(u_ffn, router_w, router_b, i, exp_w_gate, exp_w_up, exp_w_down)
        xa, u = _resid_norm_mod(ffn, None, xa, mods[i], 5, ln2_g[i], ln2_b[i], mods[min(i + 1, DEPTH - 1)], 0, 1,
                                blocks_per_seq, ctx_blocks)
        u = u.reshape(bsz, t_len, D_MODEL)
    return xa.reshape(bsz, t_len, D_MODEL)[:, n_ctx:]
```

```python
import functools
import math

import jax
import jax.numpy as jnp
import numpy as np
from jax import lax
from jax.experimental import pallas as pl
from jax.experimental.pallas import tpu as pltpu

D_MODEL = 2048
DEPTH = 2
GRID_W = 64
N_DIR = 2
ML_HEADS = 4
ML_DH = 256
ML_W = ML_HEADS * ML_DH
ML_CHUNK = 64
ML_NORM_EPS = 1e-6
RW_HEADS = 16
RW_DH = 64
RW_W = RW_HEADS * RW_DH
RW_DECAY_RANK = 64
RW_A_RANK = 64
RW_G_RANK = 128
RW_DECAY_OFFSET = 0.5
RW_NORM_EPS = 64e-5
S5_W = 1024
S5_GROUP = 16
S5_GROUPS = S5_W // S5_GROUP
S5_STATE = 64
N_BRANCH = 3
N_GROUPS = 4
EXPERTS_PER_GROUP = 4
N_EXPERTS = N_GROUPS * EXPERTS_PER_GROUP
TOP_K = 2
D_EXPERT = 1024
DEEPNORM_ALPHA = (2.0 * DEPTH) ** 0.25
LN_EPS = 1e-5
N_MOD = 6
IN_WIDTHS = (ML_W, ML_W, ML_W, ML_W, N_DIR * ML_HEADS, N_DIR * ML_HEADS,
             RW_W, RW_W, RW_W, N_DIR * RW_DECAY_RANK, N_DIR * RW_A_RANK, RW_G_RANK,
             S5_W, N_BRANCH * D_MODEL)
D_IN = sum(IN_WIDTHS)
RW_IN_WIDTHS = (RW_W, RW_W, RW_W, N_DIR * RW_DECAY_RANK, N_DIR * RW_A_RANK, RW_G_RANK)

VMEM_LIMIT_BYTES = 56 * 1024 * 1024
LANES = 128

S5_CHUNK = 16
S5_UNROLL = 8
RL_TBLK = 256
PROJ_TM = 1536
ML_TBLK = 256
RW_CHUNK = 16
RW_PAIRS = RW_HEADS // 2
RW_ROWS = RW_PAIRS * RW_CHUNK
RW_TBLK = 256
RW_GROUP = 8
RW_HALO = 8
MOE_TM = 256
ROW_TM = 256

P_QK, P_V, P_O = 0, 2 * ML_W, 3 * ML_W
P_RW = 4 * ML_W
P_S5 = P_RW + 3 * RW_W
P_GATE = P_S5 + S5_W
P_RWLR = P_GATE + N_BRANCH * D_MODEL
P_MLG = P_RWLR + 2 * N_DIR * RW_DECAY_RANK + RW_G_RANK
assert P_GATE % D_MODEL == 0 and P_MLG + 2 * N_DIR * ML_HEADS == D_IN and RW_DECAY_RANK == RW_A_RANK

_BF = jnp.bfloat16
_HI = lax.Precision.HIGHEST
_NT = (((1,), (1,)), ((), ()))
_TN = (((0,), (0,)), ((), ()))


def _scan_block(d, i, n_blocks, ctx_blocks):
    bwd = jnp.where(i < ctx_blocks, ctx_blocks - 1 - i, n_blocks - 1 + ctx_blocks - i)
    return jnp.where(d == 0, i, bwd)


def _mm_kernel(a_ref, w_ref, o_ref, abf_ref):
    @pl.when(pl.program_id(1) == 0)
    def _():
        abf_ref[...] = a_ref[...].astype(_BF)

    o_ref[...] = jnp.dot(abf_ref[...], w_ref[...], preferred_element_type=jnp.float32)


def _mm_bf16_kernel(a_ref, w_ref, o_ref):
    o_ref[...] = jnp.dot(a_ref[...], w_ref[...], preferred_element_type=jnp.float32)


def _mm(a, w, tm, tn):
    m, k = a.shape
    n = w.shape[1]
    assert m % tm == 0 and n % tn == 0, (m, n, tm, tn)
    if a.dtype == _BF:
        return pl.pallas_call(
            _mm_bf16_kernel,
            grid=(m // tm, n // tn),
            in_specs=[pl.BlockSpec((tm, k), lambda i, j: (i, 0)),
                      pl.BlockSpec((k, tn), lambda i, j: (0, j))],
            out_specs=pl.BlockSpec((tm, tn), lambda i, j: (i, j)),
            out_shape=jax.ShapeDtypeStruct((m, n), jnp.float32),
            compiler_params=pltpu.CompilerParams(
                dimension_semantics=("parallel", "arbitrary"),
                vmem_limit_bytes=VMEM_LIMIT_BYTES),
            name="mm_bf16",
        )(a, w)
    return pl.pallas_call(
        _mm_kernel,
        grid=(m // tm, n // tn),
        in_specs=[pl.BlockSpec((tm, k), lambda i, j: (i, 0)),
                  pl.BlockSpec((k, tn), lambda i, j: (0, j))],
        out_specs=pl.BlockSpec((tm, tn), lambda i, j: (i, j)),
        out_shape=jax.ShapeDtypeStruct((m, n), jnp.float32),
        scratch_shapes=[pltpu.VMEM((tm, k), _BF)],
        compiler_params=pltpu.CompilerParams(
            dimension_semantics=("parallel", "arbitrary"),
            vmem_limit_bytes=VMEM_LIMIT_BYTES),
        name="mm",
    )(a, w)


def _mm_any(a, w, tm=1024, tn=512, keep_col_pad=False):
    lead = a.shape[:-1]
    a2 = a.reshape(-1, a.shape[-1])
    m, n = a2.shape[0], w.shape[1]
    mp = -(-m // 8) * 8
    if mp > tm:
        mp = -(-m // tm) * tm
    tm = min(tm, mp)
    np_ = -(-n // LANES) * LANES
    if np_ > tn:
        np_ = -(-n // tn) * tn
    tn = min(tn, np_)
    if mp != m:
        a2 = jnp.pad(a2, ((0, mp - m), (0, 0)))
    wb = w.astype(_BF)
    if np_ != n:
        wb = jnp.pad(wb, ((0, 0), (0, np_ - n)))
    out = _mm(a2, wb, tm, tn)
    if keep_col_pad:
        n = np_
    if mp != m or np_ != n:
        out = out[:m, :n]
    return out.reshape(lead + (n,))


def _const_spec(shape):
    nd = len(shape)
    return pl.BlockSpec(shape, lambda i: (0,) * nd, pipeline_mode=pl.Buffered(1))


def _merge_kernel(ml_ref, rw_ref, s5_ref, g0_ref, g1_ref, g2_ref, wml_ref, wrw_ref, wval_ref, wgate_ref, z_ref):
    dot = lambda a, w_ref: jnp.dot(a.astype(_BF), w_ref[...], preferred_element_type=jnp.float32)
    s5 = jnp.concatenate([s5_ref[m] for m in range(S5_W // LANES)], axis=1)
    sval = dot(s5, wval_ref) * jax.nn.sigmoid(dot(s5, wgate_ref))
    z = (jax.nn.sigmoid(g0_ref[...]) * dot(ml_ref[...], wml_ref)
         + jax.nn.sigmoid(g1_ref[...]) * dot(rw_ref[...], wrw_ref)
         + jax.nn.sigmoid(g2_ref[...]) * sval)
    z_ref[...] = z.astype(_BF)


def _merge_pallas(p2, ml, rw, s5, wml, wrw, wval, wgate):
    n, w = ml.shape
    tm = ROW_TM
    gb = P_GATE // D_MODEL
    row = lambda i: (i, 0)
    return pl.pallas_call(
        _merge_kernel,
        grid=(n // tm,),
        in_specs=[pl.BlockSpec((tm, w), row)] * 2 + [pl.BlockSpec((S5_W // LANES, tm, LANES), lambda i: (0, i, 0))]
                 + [pl.BlockSpec((tm, D_MODEL), functools.partial(lambda j, i: (i, gb + j), j))
                    for j in range(N_BRANCH)]
                 + [_const_spec((w, D_MODEL))] * 4,
        out_specs=pl.BlockSpec((tm, D_MODEL), row),
        out_shape=jax.ShapeDtypeStruct((n, D_MODEL), _BF),
        compiler_params=pltpu.CompilerParams(dimension_semantics=("parallel",), vmem_limit_bytes=VMEM_LIMIT_BYTES),
        name="merge_gate_proj",
    )(ml, rw, s5, p2, p2, p2, wml, wrw, wval, wgate)


def _post_scan_kernel(hf_ref, hb_ref, o_ref, yf_ref, yb_ref, bonus_ref, g_ref, mlg_ref, mlb_ref, rwg_ref, rwb_ref,
                      ones_ref, ml_ref, rw_ref):
    h = hf_ref[0] + hb_ref[0]
    parts = []
    for hd in range(ML_HEADS):
        x = h[:, hd * ML_DH:(hd + 1) * ML_DH]
        xc = x - jnp.mean(x, axis=-1, keepdims=True)
        parts.append(xc * lax.rsqrt(jnp.mean(xc * xc, axis=-1, keepdims=True) + ML_NORM_EPS))
    hn = jnp.concatenate(parts, axis=1) * mlg_ref[...] + mlb_ref[...]
    ml_ref[...] = (jax.nn.sigmoid(o_ref[...]) * hn).astype(_BF)

    def head_mean(x):
        hi = x.astype(_BF)
        lo = (x - hi.astype(jnp.float32)).astype(_BF)
        s = (jnp.dot(hi, ones_ref[...], preferred_element_type=jnp.float32)
             + jnp.dot(lo, ones_ref[...], preferred_element_type=jnp.float32))
        return s * (1.0 / RW_DH)

    y = yf_ref[...] + yb_ref[...]
    yc = y - head_mean(y)
    yn = yc * lax.rsqrt(head_mean(yc * yc) + RW_NORM_EPS) * rwg_ref[...] + rwb_ref[...]
    rw_ref[...] = ((yn + bonus_ref[...]) * g_ref[...]).astype(_BF)


def _post_scan(p2, h_dir, y_f, y_b, bonus, g, ml_norm_g, ml_norm_b, rw_norm_g, rw_norm_b):
    n = y_f.shape[0]
    tm = ROW_TM
    head = jnp.arange(RW_W) // RW_DH
    ones_bd = (head[:, None] == head[None, :]).astype(_BF)
    vec = pl.BlockSpec((1, RW_W), lambda i: (0, 0))
    blk = pl.BlockSpec((tm, RW_W), lambda i: (i, 0))
    return pl.pallas_call(
        _post_scan_kernel,
        grid=(n // tm,),
        in_specs=[pl.BlockSpec((1, tm, ML_W), lambda i: (0, i, 0)), pl.BlockSpec((1, tm, ML_W), lambda i: (1, i, 0)),
                  pl.BlockSpec((tm, ML_W), lambda i: (i, P_O // ML_W)), blk, blk, blk, blk, vec, vec, vec, vec,
                  _const_spec((RW_W, RW_W))],
        out_specs=[blk, blk],
        out_shape=[jax.ShapeDtypeStruct((n, ML_W), _BF), jax.ShapeDtypeStruct((n, RW_W), _BF)],
        compiler_params=pltpu.CompilerParams(dimension_semantics=("parallel",), vmem_limit_bytes=VMEM_LIMIT_BYTES),
        name="post_scan_norm_gate",
    )(h_dir, h_dir, p2, y_f, y_b, bonus, g, ml_norm_g.reshape(1, -1), ml_norm_b.reshape(1, -1),
      rw_norm_g.reshape(1, -1), rw_norm_b.reshape(1, -1), ones_bd)


def _ln_rows(x, eps=LN_EPS):
    mu = jnp.mean(x, axis=-1, keepdims=True)
    xc = x - mu
    var = jnp.mean(xc * xc, axis=-1, keepdims=True)
    return xc * lax.rsqrt(var + eps)


def _resid_kernel(*refs, with_w):
    if with_w:
        d_ref, w_ref, x_ref, gate_ref, g_ref, b_ref, sh_ref, sc_ref, xo_ref, uo_ref = refs
        delta = jnp.dot(d_ref[...], w_ref[...], preferred_element_type=jnp.float32)
    else:
        d_ref, x_ref, gate_ref, g_ref, b_ref, sh_ref, sc_ref, xo_ref, uo_ref = refs
        delta = d_ref[...]
    xn = _ln_rows(DEEPNORM_ALPHA * x_ref[...] + gate_ref[0, 0, 0] * delta) * g_ref[...] + b_ref[...]
    xo_ref[...] = xn
    uo_ref[...] = (_ln_rows(xn) * (1.0 + sc_ref[0, 0, 0]) + sh_ref[0, 0, 0]).astype(uo_ref.dtype)


def _resid_norm_mod(delta, w, x, mod_a, ia, ln_g, ln_b, mod_b, ish, isc, blocks_per_seq, ctx_blocks,
                    u_dtype=jnp.float32):
    n, d = x.shape
    tm = ROW_TM
    row = lambda i: (i, 0)

    def mod_spec(m):
        return pl.BlockSpec((1, 1, 1, 1, d), lambda i: (i // blocks_per_seq,
                                                       (i % blocks_per_seq >= ctx_blocks).astype(jnp.int32), m, 0, 0))

    vec = pl.BlockSpec((1, d), lambda i: (0, 0))
    if w is not None:
        in_specs = [pl.BlockSpec((tm, delta.shape[1]), row), _const_spec(w.shape), pl.BlockSpec((tm, d), row)]
        args = (delta, w, x)
    else:
        in_specs = [pl.BlockSpec((tm, d), row), pl.BlockSpec((tm, d), row)]
        args = (delta, x)
    return pl.pallas_call(
        functools.partial(_resid_kernel, with_w=w is not None),
        grid=(n // tm,),
        in_specs=in_specs + [mod_spec(ia), vec, vec, mod_spec(ish), mod_spec(isc)],
        out_specs=[pl.BlockSpec((tm, d), row)] * 2,
        out_shape=[jax.ShapeDtypeStruct((n, d), jnp.float32), jax.ShapeDtypeStruct((n, d), u_dtype)],
        compiler_params=pltpu.CompilerParams(dimension_semantics=("parallel",), vmem_limit_bytes=VMEM_LIMIT_BYTES),
        name="resid_norm_mod",
    )(*args, mod_a, ln_g.reshape(1, d), ln_b.reshape(1, d), mod_b, mod_b)


def _moe_ffn_kernel(te_ref, tv_ref, x_ref, wg_ref, wu_ref, wd_ref, y_ref, wgb_ref, wub_ref, wdb_ref):
    t = pl.program_id(0)

    @pl.when((t == 0) | (te_ref[t] != te_ref[jnp.maximum(t - 1, 0)]))
    def _():
        wgb_ref[...] = wg_ref[0, 0].astype(_BF)
        wub_ref[...] = wu_ref[0, 0].astype(_BF)
        wdb_ref[...] = wd_ref[0, 0].astype(_BF)

    @pl.when(tv_ref[t] == 1)
    def _():
        x = x_ref[...].astype(_BF)
        hg = jnp.dot(x, wgb_ref[...], preferred_element_type=jnp.float32)
        hu = jnp.dot(x, wub_ref[...], preferred_element_type=jnp.float32)
        h = (hg * jax.nn.sigmoid(hg)) * hu
        y_ref[...] = jnp.dot(h.astype(_BF), wdb_ref[...], preferred_element_type=jnp.float32)

    @pl.when(tv_ref[t] == 0)
    def _():
        y_ref[...] = jnp.zeros_like(y_ref)


def _moe_routed(u, gates_t, layer, wg, wu, wd):
    n_tok, d = u.shape
    _, n_e, _, d_e = wg.shape
    tm = MOE_TM
    n_tiles = (TOP_K * n_tok) // tm + n_e
    n_slots = n_tiles * tm
    sel = gates_t > 0.0
    seli = sel.astype(jnp.int32)
    rank = jnp.cumsum(seli, axis=1) - 1
    cnt = jnp.sum(seli, axis=1)
    tiles_e = (cnt + tm - 1) // tm
    tile_end = jnp.cumsum(tiles_e)
    off = (tile_end - tiles_e) * tm
    slot = off[:, None] + rank
    order = jnp.cumsum(seli, axis=0)
    slots, gsel = [], []
    for j in range(TOP_K):
        pick = sel & (order == j + 1)
        slots.append(jnp.sum(jnp.where(pick, slot, 0), axis=0))
        gsel.append(jnp.sum(jnp.where(pick, gates_t, 0.0), axis=0))
    tok = jnp.arange(n_tok, dtype=jnp.int32)
    tok_of_slot = jnp.zeros((n_slots,), jnp.int32).at[jnp.concatenate(slots)].set(jnp.tile(tok, TOP_K))
    tile_ids = jnp.arange(n_tiles, dtype=jnp.int32)
    tile_valid = (tile_ids < tile_end[-1]).astype(jnp.int32)
    tile_expert = jnp.minimum(jnp.searchsorted(tile_end, tile_ids, side='right'), n_e - 1).astype(jnp.int32)
    last_e = jnp.max(jnp.where(cnt > 0, jnp.arange(n_e), 0)).astype(jnp.int32)
    tile_expert = jnp.where(tile_valid == 1, tile_expert, last_e)
    xs = u.at[tok_of_slot].get(mode='promise_in_bounds')
    w_index = lambda t, te, tv: (layer, te[t], 0, 0)
    grid_spec = pltpu.PrefetchScalarGridSpec(
        num_scalar_prefetch=2,
        grid=(n_tiles,),
        in_specs=[pl.BlockSpec((tm, d), lambda t, te, tv: (t, 0)),
                  pl.BlockSpec((1, 1, d, d_e), w_index, pipeline_mode=pl.Buffered(1)),
                  pl.BlockSpec((1, 1, d, d_e), w_index, pipeline_mode=pl.Buffered(1)),
                  pl.BlockSpec((1, 1, d_e, d), w_index, pipeline_mode=pl.Buffered(1))],
        out_specs=pl.BlockSpec((tm, d), lambda t, te, tv: (t, 0)),
        scratch_shapes=[pltpu.VMEM((d, d_e), _BF), pltpu.VMEM((d, d_e), _BF), pltpu.VMEM((d_e, d), _BF)])
    ys = pl.pallas_call(
        _moe_ffn_kernel, grid_spec=grid_spec,
        out_shape=jax.ShapeDtypeStruct((n_slots, d), jnp.float32),
        compiler_params=pltpu.CompilerParams(dimension_semantics=("arbitrary",),
                                             vmem_limit_bytes=VMEM_LIMIT_BYTES),
        name="moe_routed_ffn",
    )(tile_expert, tile_valid, xs, wg, wu, wd)
    return sum(gsel[j][:, None] * ys.at[slots[j]].get(mode='promise_in_bounds') for j in range(TOP_K))


def _router_kernel(u_ref, wt_ref, b_ref, g_ref):
    logits = lax.dot_general(wt_ref[...], u_ref[...], _NT, precision=_HI, preferred_element_type=jnp.float32)
    aff = jax.nn.sigmoid(logits)
    score = aff + b_ref[...]
    s = [score[e:e + 1] for e in range(N_EXPERTS)]
    a = [aff[e:e + 1] for e in range(N_EXPERTS)]
    gs = []
    for g in range(N_GROUPS):
        m = s[g * EXPERTS_PER_GROUP:(g + 1) * EXPERTS_PER_GROUP]
        best = None
        for i in range(EXPERTS_PER_GROUP):
            for j in range(i + 1, EXPERTS_PER_GROUP):
                best = m[i] + m[j] if best is None else jnp.maximum(best, m[i] + m[j])
        gs.append(best)
    best_val = gs[0]
    best_grp = jnp.zeros_like(gs[0], dtype=jnp.int32)
    for g in range(1, N_GROUPS):
        better = gs[g] > best_val
        best_grp = jnp.where(better, g, best_grp)
        best_val = jnp.where(better, gs[g], best_val)
    sel = []
    for e in range(N_EXPERTS):
        g = e // EXPERTS_PER_GROUP
        rank = jnp.zeros_like(best_grp)
        for j in range(g * EXPERTS_PER_GROUP, (g + 1) * EXPERTS_PER_GROUP):
            if j != e:
                ahead = (s[j] > s[e]) | ((s[j] == s[e]) & (j < e))
                rank = rank + ahead.astype(jnp.int32)
        sel.append((best_grp == g) & (rank < TOP_K))
    wsum = sum(jnp.where(sel[e], a[e], 0.0) for e in range(N_EXPERTS))
    g_ref[...] = jnp.concatenate([jnp.where(sel[e], a[e] / wsum, 0.0) for e in range(N_EXPERTS)], axis=0)


def _router_pallas(u, router_w, router_b, tm=512):
    n_tok, d = u.shape
    assert n_tok % tm == 0
    return pl.pallas_call(
        _router_kernel,
        grid=(n_tok // tm,),
        in_specs=[pl.BlockSpec((tm, d), lambda i: (i, 0)),
                  pl.BlockSpec((N_EXPERTS, d), lambda i: (0, 0)),
                  pl.BlockSpec((N_EXPERTS, 1), lambda i: (0, 0))],
        out_specs=pl.BlockSpec((N_EXPERTS, tm), lambda i: (0, i)),
        out_shape=jax.ShapeDtypeStruct((N_EXPERTS, n_tok), jnp.float32),
        compiler_params=pltpu.CompilerParams(dimension_semantics=("parallel",),
                                             vmem_limit_bytes=VMEM_LIMIT_BYTES),
        name="moe_router",
    )(u, router_w.T, router_b.reshape(N_EXPERTS, 1))


def _s5_mats(lam_re, lam_im, log_dt, b_re, b_im, c_re, c_im):
    L = S5_CHUNK
    lam = lax.complex(lam_re, lam_im)
    ldt = lam * jnp.exp(log_dt)[..., None]
    lam_bar = jnp.exp(ldt)
    b_bar = ((lam_bar - 1.0) / lam)[..., None] * lax.complex(b_re, b_im)
    c_mat = lax.complex(c_re, c_im)
    tau = jnp.arange(L + 1, dtype=jnp.float32)
    pw = jnp.exp(ldt[:, :, None, :] * tau[None, None, :, None])
    kern = jnp.real(jnp.einsum('dgon,dgtn,dgni->dgtoi', c_mat, pw[:, :, :L], b_bar))
    s_idx = jnp.arange(L)[:, None]
    t_idx = jnp.arange(L)[None, :]

    def toeplitz(k, lag, valid):
        m = k[:, jnp.clip(lag, 0, L - 1)] * valid[None, :, :, None, None]
        return jnp.transpose(m, (0, 1, 4, 2, 3)).reshape(-1, L * S5_GROUP, L * S5_GROUP)

    tsum = (toeplitz(kern[0], t_idx - s_idx, (t_idx >= s_idx).astype(jnp.float32))
            + toeplitz(kern[1], s_idx - t_idx, (s_idx >= t_idx).astype(jnp.float32)))
    pin_f = pw[0][:, L - 1 - jnp.arange(L)]
    pin_b = pw[1][:, jnp.arange(L)]
    in_f = jnp.einsum('gsn,gni->gsin', pin_f, b_bar[0]).reshape(-1, L * S5_GROUP, S5_STATE)
    in_b = jnp.einsum('gsn,gni->gsin', pin_b, b_bar[1]).reshape(-1, L * S5_GROUP, S5_STATE)
    icat = jnp.concatenate([jnp.real(in_f), jnp.imag(in_f), jnp.real(in_b), jnp.imag(in_b)], axis=-1)
    pout_f = pw[0][:, 1 + jnp.arange(L)]
    pout_b = pw[1][:, L - jnp.arange(L)]
    out_f = jnp.einsum('gon,gtn->gnto', c_mat[0], pout_f).reshape(-1, S5_STATE, L * S5_GROUP)
    out_b = jnp.einsum('gon,gtn->gnto', c_mat[1], pout_b).reshape(-1, S5_STATE, L * S5_GROUP)
    ocat = jnp.concatenate([jnp.real(out_f), -jnp.imag(out_f), jnp.real(out_b), -jnp.imag(out_b)], axis=1)
    lam_l = pw[:, :, L]
    lam_chunk = jnp.stack([jnp.real(lam_l[0]), jnp.imag(lam_l[0]), jnp.real(lam_l[1]), jnp.imag(lam_l[1])],
                          axis=1)
    return tsum, icat, ocat, lam_chunk


def _s5_kernel(u0_ref, u1_ref, t_ref, i_ref, o_ref, lam_ref, d_ref, y0_ref, y1_ref,
               v_ref, xfr_ref, xfi_ref, xbr_ref, xbi_ref, *, n_chunks, ctx_chunks, bsz):
    n = S5_STATE
    u = jnp.concatenate([u0_ref[0], u1_ref[0]], axis=1)
    ub = u.astype(_BF)
    v_ref[...] = jnp.dot(ub, i_ref[0], preferred_element_type=jnp.float32)
    lam = lam_ref[0]
    lfr = jnp.broadcast_to(lam[0:1], (bsz, n))
    lfi = jnp.broadcast_to(lam[1:2], (bsz, n))
    lbr = jnp.broadcast_to(lam[2:3], (bsz, n))
    lbi = jnp.broadcast_to(lam[3:4], (bsz, n))

    def cmul_add(lr, li, xr, xi, vr, vi):
        return lr * xr - li * xi + vr, lr * xi + li * xr + vi

    def step(j, carry):
        fr, fi, br, bi = carry
        rf = pl.multiple_of(j * (2 * bsz), 2 * bsz)
        pb = jnp.where(j < ctx_chunks // 2, ctx_chunks // 2 - 1 - j, (n_chunks + ctx_chunks) // 2 - 1 - j)
        rb = pl.multiple_of(pb * (2 * bsz), 2 * bsz)
        vf = v_ref[pl.ds(rf, 2 * bsz), :]
        vb = v_ref[pl.ds(rb, 2 * bsz), :]
        fr1, fi1 = cmul_add(lfr, lfi, fr, fi, vf[:bsz, 0:n], vf[:bsz, n:2 * n])
        fr2, fi2 = cmul_add(lfr, lfi, fr1, fi1, vf[bsz:, 0:n], vf[bsz:, n:2 * n])
        br1, bi1 = cmul_add(lbr, lbi, br, bi, vb[bsz:, 2 * n:3 * n], vb[bsz:, 3 * n:4 * n])
        br2, bi2 = cmul_add(lbr, lbi, br1, bi1, vb[:bsz, 2 * n:3 * n], vb[:bsz, 3 * n:4 * n])
        xfr_ref[pl.ds(rf, 2 * bsz), :] = jnp.concatenate([fr, fr1], axis=0)
        xfi_ref[pl.ds(rf, 2 * bsz), :] = jnp.concatenate([fi, fi1], axis=0)
        xbr_ref[pl.ds(rb, 2 * bsz), :] = jnp.concatenate([br1, br], axis=0)
        xbi_ref[pl.ds(rb, 2 * bsz), :] = jnp.concatenate([bi1, bi], axis=0)
        return fr2, fi2, br2, bi2

    z = jnp.zeros((bsz, n), jnp.float32)
    lax.fori_loop(0, n_chunks // 2, step, (z, z, z, z), unroll=S5_UNROLL)
    o = o_ref[0]
    y = jnp.dot(ub, t_ref[0], preferred_element_type=jnp.float32)
    y += jnp.dot(xfr_ref[...].astype(_BF), o[0:n], preferred_element_type=jnp.float32)
    y += jnp.dot(xfi_ref[...].astype(_BF), o[n:2 * n], preferred_element_type=jnp.float32)
    y += jnp.dot(xbr_ref[...].astype(_BF), o[2 * n:3 * n], preferred_element_type=jnp.float32)
    y += jnp.dot(xbi_ref[...].astype(_BF), o[3 * n:4 * n], preferred_element_type=jnp.float32)
    y += d_ref[0] * u
    y = 0.5 * y * (1.0 + jnp.tanh(math.sqrt(2.0 / math.pi) * (y + 0.044715 * (y * y * y))))
    y0_ref[0] = y[:, :LANES]
    y1_ref[0] = y[:, LANES:]


def _to_groups_kernel(*refs, bsz):
    L, C = S5_CHUNK, S5_GROUP
    nch = RL_TBLK // L
    gpt = LANES // C
    lane = lax.broadcasted_iota(jnp.int32, (nch, LANES), 1) // C
    x_refs, o_refs = refs[:S5_W // LANES], refs[S5_W // LANES:]
    for b in range(bsz):
        for m in range(S5_W // LANES):
            src = [x_refs[m][b, pl.ds(j, nch, stride=L), :] for j in range(L)]
            for gl in range(gpt):
                g = m * gpt + gl
                for q in range(L * C // LANES):
                    acc = jnp.zeros((nch, LANES), jnp.float32)
                    for jl in range(gpt):
                        j = q * gpt + jl
                        moved = src[j] if jl == gl else pltpu.roll(src[j], ((jl - gl) * C) % LANES, axis=1)
                        acc = jnp.where(lane == jl, moved, acc)
                    o_refs[q][g, pl.ds(b, nch, stride=bsz), :] = acc


def _from_groups_kernel(y0_ref, y1_ref, o_ref, *, bsz):
    L, C = S5_CHUNK, S5_GROUP
    nch = RL_TBLK // L
    gpt = LANES // C
    lane = lax.broadcasted_iota(jnp.int32, (nch, LANES), 1) // C
    y_refs = (y0_ref, y1_ref)
    for b in range(bsz):
        for m in range(S5_W // LANES):
            for q in range(L * C // LANES):
                src = [y_refs[q][m * gpt + gl, pl.ds(b, nch, stride=bsz), :] for gl in range(gpt)]
                for jl in range(gpt):
                    acc = jnp.zeros((nch, LANES), jnp.float32)
                    for gl in range(gpt):
                        moved = src[gl] if jl == gl else pltpu.roll(src[gl], ((gl - jl) * C) % LANES, axis=1)
                        acc = jnp.where(lane == gl, moved, acc)
                    o_ref[m, b, pl.ds(q * gpt + jl, nch, stride=L), :] = acc


def _s5_pallas(p3, n_ctx, lam_re, lam_im, log_dt, b_re, b_im, c_re, c_im, d_skip):
    bsz, t_len, _ = p3.shape
    L, G, C = S5_CHUNK, S5_GROUPS, S5_GROUP
    nc = t_len // L
    assert (2 * bsz) % 8 == 0 and nc % 2 == 0 and (n_ctx // L) % 2 == 0 and t_len % RL_TBLK == 0
    assert L * C == 2 * LANES and P_S5 % LANES == 0
    tsum, icat, ocat, lam_chunk = _s5_mats(lam_re, lam_im, log_dt, b_re, b_im, c_re, c_im)
    dvec = jnp.tile(d_skip.reshape(G, 1, C), (1, L, 1)).reshape(G, 1, L * C)
    rows = nc * bsz
    n_lt = S5_W // LANES
    nch = RL_TBLK // L
    relayout_params = pltpu.CompilerParams(dimension_semantics=("parallel",), vmem_limit_bytes=VMEM_LIMIT_BYTES)
    half_shape = jax.ShapeDtypeStruct((G, rows, LANES), jnp.float32)
    half_blk = pl.BlockSpec((G, nch * bsz, LANES), lambda i: (0, i, 0))
    u0, u1 = pl.pallas_call(
        functools.partial(_to_groups_kernel, bsz=bsz),
        grid=(t_len // RL_TBLK,),
        in_specs=[pl.BlockSpec((bsz, RL_TBLK, LANES), functools.partial(lambda m, i: (0, i, P_S5 // LANES + m), m))
                  for m in range(n_lt)],
        out_specs=[half_blk, half_blk],
        out_shape=[half_shape, half_shape],
        compiler_params=relayout_params, name="s5_to_groups",
    )(*([p3] * n_lt))
    wspec = pl.BlockSpec((1, L * C, L * C), lambda g: (g, 0, 0))
    half = pl.BlockSpec((1, rows, LANES), lambda g: (g, 0, 0))
    y0, y1 = pl.pallas_call(
        functools.partial(_s5_kernel, n_chunks=nc, ctx_chunks=n_ctx // L, bsz=bsz),
        grid=(G,),
        in_specs=[half, half, wspec, wspec, wspec,
                  pl.BlockSpec((1, 4, S5_STATE), lambda g: (g, 0, 0)),
                  pl.BlockSpec((1, 1, L * C), lambda g: (g, 0, 0))],
        out_specs=[half, half],
        out_shape=[half_shape, half_shape],
        scratch_shapes=[pltpu.VMEM((rows, 4 * S5_STATE), jnp.float32)]
                       + [pltpu.VMEM((rows, S5_STATE), jnp.float32)] * 4,
        compiler_params=pltpu.CompilerParams(dimension_semantics=("parallel",),
                                             vmem_limit_bytes=VMEM_LIMIT_BYTES),
        name="s5_scan",
    )(u0, u1, tsum.astype(_BF), icat.astype(_BF), ocat.astype(_BF), lam_chunk, dvec)
    return pl.pallas_call(
        functools.partial(_from_groups_kernel, bsz=bsz),
        grid=(t_len // RL_TBLK,),
        in_specs=[half_blk, half_blk],
        out_specs=pl.BlockSpec((n_lt, bsz, RL_TBLK, LANES), lambda i: (0, 0, i, 0)),
        out_shape=jax.ShapeDtypeStruct((n_lt, bsz, t_len, LANES), jnp.float32),
        compiler_params=relayout_params, name="s5_from_groups",
    )(y0, y1)


def _log_sigmoid(x):
    return jnp.minimum(x, 0.0) - jnp.log(1.0 + jnp.exp(-jnp.abs(x)))


def _mlstm_kernel(igb_ref, fgb_ref, q_ref, k_ref, v_ref, gc_ref, gr_ref, h_ref, cmat_ref, nvec_ref, m_ref):
    d = pl.program_id(0)
    L = ML_CHUNK
    n_sub = ML_TBLK // L
    H = range(ML_HEADS)

    @pl.when(pl.program_id(2) == 0)
    def _():
        cmat_ref[...] = jnp.zeros_like(cmat_ref)
        nvec_ref[...] = jnp.zeros_like(nvec_ref)
        m_ref[...] = jnp.zeros_like(m_ref)

    igb = [igb_ref[d, h] for h in H]
    fgb = [fgb_ref[d, h] for h in H]
    row = lax.broadcasted_iota(jnp.int32, (L, L), 0)
    col = lax.broadcasted_iota(jnp.int32, (L, L), 1)
    sign = 1 - 2 * d
    seen = (row - col) * sign >= 0
    seen_f = seen.astype(jnp.float32)
    seen_t = ((col - row) * sign >= 0).astype(jnp.float32)
    scale = ML_DH ** -0.5

    def chunk(jj, carry):
        cj = jnp.where(d == 0, jj, n_sub - 1 - jj)
        r0 = pl.multiple_of(cj * L, L)
        hs = lambda h: slice(h * ML_DH, (h + 1) * ML_DH)
        q = [q_ref[0, pl.ds(r0, L), hs(h)] for h in H]
        k = [k_ref[0, pl.ds(r0, L), hs(h)] * scale for h in H]
        vb = [v_ref[0, pl.ds(r0, L), hs(h)].astype(_BF) for h in H]
        gc = [gc_ref[0, 0, h, pl.ds(r0, L), :] for h in H]
        gr = [gr_ref[0, 0, h, cj] for h in H]
        li_col = [gc[h][:, 0:1] + igb[h] for h in H]
        lf_col = [_log_sigmoid(gc[h][:, 1:2] + fgb[h]) for h in H]
        li_row = [gr[h][0:1, :] + igb[h] for h in H]
        lf_row = [_log_sigmoid(gr[h][1:2, :] + fgb[h]) for h in H]
        m_prev = [m_ref[h] for h in H]
        bcum_col = [jnp.dot(seen_f, jnp.broadcast_to(lf_col[h], (L, L)), precision=_HI,
                            preferred_element_type=jnp.float32) for h in H]
        bcum_row = [jnp.dot(jnp.broadcast_to(lf_row[h], (8, L)), seen_t, precision=_HI,
                            preferred_element_type=jnp.float32)[0:1] for h in H]
        qb = [q[h].astype(_BF) for h in H]
        qk = [lax.dot_general(qb[h], k[h].astype(_BF), _NT, preferred_element_type=jnp.float32) for h in H]
        qc = [jnp.dot(qb[h], cmat_ref[h].astype(_BF), preferred_element_type=jnp.float32) for h in H]
        log_d = [jnp.where(seen, bcum_col[h] - bcum_row[h] + li_row[h], -jnp.inf) for h in H]
        inter = [bcum_col[h][:, 0:1] + m_prev[h] for h in H]
        m_j = [jnp.maximum(jnp.max(log_d[h], axis=1, keepdims=True), inter[h]) for h in H]
        scores = [qk[h] * jnp.exp(log_d[h] - m_j[h]) for h in H]
        s_inter = [jnp.exp(inter[h] - m_j[h]) for h in H]
        sv = [jnp.dot(scores[h].astype(_BF), vb[h], preferred_element_type=jnp.float32) for h in H]
        b_last = [jnp.sum(lf_col[h], axis=0, keepdims=True) for h in H]
        log_w = [b_last[h] - bcum_col[h][:, 0:1] + li_col[h] for h in H]
        m_new = [jnp.maximum(b_last[h] + m_prev[h], jnp.max(log_w[h], axis=0, keepdims=True)) for h in H]
        kw = [k[h] * jnp.exp(log_w[h] - m_new[h]) for h in H]
        decay = [jnp.exp(b_last[h] + m_prev[h] - m_new[h]) for h in H]
        kv = [lax.dot_general(kw[h].astype(_BF), vb[h], _TN, preferred_element_type=jnp.float32) for h in H]
        for h in H:
            num = sv[h] + s_inter[h] * qc[h]
            den = (jnp.sum(scores[h], axis=1, keepdims=True)
                   + s_inter[h] * jnp.sum(q[h] * nvec_ref[h], axis=1, keepdims=True))
            h_ref[0, 0, pl.ds(r0, L), hs(h)] = num / jnp.maximum(jnp.abs(den), jnp.exp(-m_j[h]))
        for h in H:
            cmat_ref[h] = decay[h] * cmat_ref[h] + kv[h]
            nvec_ref[h] = decay[h] * nvec_ref[h] + jnp.sum(kw[h], axis=0, keepdims=True)
            m_ref[h] = m_new[h]
        return carry

    lax.fori_loop(0, n_sub, chunk, 0)


def _mlstm_pallas(q, k, v, gates, ig_b, fg_b, n_ctx):
    bsz, t_len, _ = q.shape
    nb = t_len // ML_TBLK
    cb = n_ctx // ML_TBLK
    assert t_len % ML_TBLK == 0 and n_ctx % ML_TBLK == 0
    g = gates.reshape(bsz, t_len, 2, N_DIR, ML_HEADS)
    gcol = jnp.transpose(g, (3, 0, 4, 1, 2))
    grow = jnp.transpose(g.reshape(bsz, t_len // ML_CHUNK, ML_CHUNK, 2, N_DIR, ML_HEADS),
                         (4, 0, 5, 1, 3, 2))
    blk = lambda d, i: _scan_block(d, i, nb, cb)
    grid_spec = pltpu.PrefetchScalarGridSpec(
        num_scalar_prefetch=2,
        grid=(N_DIR, bsz, nb),
        in_specs=[pl.BlockSpec((1, ML_TBLK, ML_W), lambda d, b, i, *_: (b, blk(d, i), 0))] * 3 + [
            pl.BlockSpec((1, 1, ML_HEADS, ML_TBLK, 2), lambda d, b, i, *_: (d, b, 0, blk(d, i), 0)),
            pl.BlockSpec((1, 1, ML_HEADS, ML_TBLK // ML_CHUNK, 2, ML_CHUNK),
                         lambda d, b, i, *_: (d, b, 0, blk(d, i), 0, 0))],
        out_specs=pl.BlockSpec((1, 1, ML_TBLK, ML_W), lambda d, b, i, *_: (d, b, blk(d, i), 0)),
        scratch_shapes=[pltpu.VMEM((ML_HEADS, ML_DH, ML_DH), jnp.float32),
                        pltpu.VMEM((ML_HEADS, 1, ML_DH), jnp.float32),
                        pltpu.VMEM((ML_HEADS, 1, 1), jnp.float32)])
    return pl.pallas_call(
        _mlstm_kernel, grid_spec=grid_spec,
        out_shape=jax.ShapeDtypeStruct((N_DIR, bsz, t_len, ML_W), jnp.float32),
        compiler_params=pltpu.CompilerParams(
            dimension_semantics=("parallel", "parallel", "arbitrary"),
            vmem_limit_bytes=VMEM_LIMIT_BYTES),
        name="mlstm_scan",
    )(ig_b, fg_b, q, k, v, gcol, grow)


def _to_pairs(x):
    return jnp.concatenate([x[:, p * LANES:(p + 1) * LANES] for p in range(RW_PAIRS)], axis=0)


def _dotf(a, b, dims=None):
    a = a.astype(_BF)
    b = b.astype(_BF)
    if dims is None:
        return jnp.dot(a, b, preferred_element_type=jnp.float32)
    return lax.dot_general(a, b, dims, preferred_element_type=jnp.float32)


def _rwkv_a_kernel(r_ref, k_ref, v_ref, kk_ref, a_ref, lw_ref,
                   att_ref, rt_ref, bw_ref, kw_ref, vt_ref, u0t_ref, y0_ref, wc_ref, *, n_sub):
    d = pl.program_id(0)
    C, R = RW_CHUNK, RW_ROWS
    row = lax.broadcasted_iota(jnp.int32, (R, R), 0)
    col = lax.broadcasted_iota(jnp.int32, (R, R), 1)
    same = (row // C) == (col // C)
    sign = 1 - 2 * d
    before = same & ((row - col) * sign > 0)
    upto = same & ((row - col) * sign >= 0)
    upto2 = jnp.concatenate([upto, upto], axis=1)
    eye = (row == col).astype(jnp.float32)
    first = col < RW_DH
    tpos = row % C

    def chunk_group(jg, carry):
        js = [jg * RW_GROUP + i for i in range(RW_GROUP)]
        G2 = [(i, h2) for i in range(RW_GROUP) for h2 in range(2)]
        pick = lambda lst, off: [jnp.where(first, lst[2 * i][:, off:off + R], lst[2 * i + 1][:, off:off + R])
                                 for i in range(RW_GROUP)]
        r0s = [pl.multiple_of(j * C, C) for j in js]
        ld = lambda ref: [_to_pairs(ref[0, pl.ds(r0, C), :]) for r0 in r0s]
        ldd = lambda ref: [_to_pairs(ref[0, 0, pl.ds(r0, C), :]) for r0 in r0s]
        r, v, kk = ld(r_ref), ld(v_ref), ld(kk_ref)
        k, a, lw = ldd(k_ref), ldd(a_ref), ldd(lw_ref)
        fwd = d == 0
        cum, aft = [], []
        for i in range(RW_GROUP):
            pre = lw[i]
            suf = lw[i]
            for s in (1, 2, 4, 8):
                pre = pre + jnp.where(tpos >= s, pltpu.roll(pre, s, axis=0), 0.0)
                suf = suf + jnp.where(tpos < C - s, pltpu.roll(suf, R - s, axis=0), 0.0)
            cum.append(jnp.where(fwd, pre, suf))
            aft.append(jnp.where(fwd, suf, pre) - lw[i])
        a_hat = [-kk[i] * jnp.exp(cum[i] - lw[i]) for i in range(RW_GROUP)]
        r_hat = [r[i] * jnp.exp(cum[i]) for i in range(RW_GROUP)]
        vb = [v[i].astype(_BF) for i in range(RW_GROUP)]
        m = []
        for i in range(RW_GROUP):
            inv_w = jnp.exp(-cum[i])
            lhs = jnp.concatenate([jnp.where(first, a_hat[i], 0.0), jnp.where(first, 0.0, a_hat[i]),
                                   jnp.where(first, r_hat[i], 0.0), jnp.where(first, 0.0, r_hat[i])], axis=0)
            rhs = jnp.concatenate([kk[i] * a[i] * inv_w, k[i] * inv_w], axis=0)
            m.append(_dotf(lhs, rhs, _NT))
        x = [jnp.where(before, m[i][h2 * R:(h2 + 1) * R, 0:R], 0.0).astype(_BF) for i, h2 in G2]
        ak = [jnp.where(before, m[i][h2 * R:(h2 + 1) * R, R:2 * R], 0.0) for i, h2 in G2]
        rbk = [jnp.where(upto2, m[i][(2 + h2) * R:(3 + h2) * R, :], 0.0).astype(_BF) for i, h2 in G2]
        akv = [_dotf(ak[g], vb[g // 2]) for g in range(len(G2))]
        x2 = [_dotf(xx, xx).astype(_BF) for xx in x]
        x4 = [_dotf(xx, xx).astype(_BF) for xx in x2]
        x8 = [_dotf(xx, xx).astype(_BF) for xx in x4]
        t = [eye + xx.astype(jnp.float32) for xx in x]
        t = [t[g] + _dotf(t[g], x2[g]) for g in range(len(G2))]
        t = [t[g] + _dotf(t[g], x4[g]) for g in range(len(G2))]
        t = [t[g] + _dotf(t[g], x8[g]) for g in range(len(G2))]
        akv = pick(akv, 0)
        rhs2 = [jnp.concatenate([a_hat[i], akv[i]], axis=1).astype(_BF) for i in range(RW_GROUP)]
        ta = [_dotf(t[g], rhs2[g // 2]) for g in range(len(G2))]
        at = pick(ta, 0)
        u0 = pick(ta, R)
        rhs3 = [jnp.concatenate([jnp.concatenate([at[i], u0[i]], axis=1).astype(_BF),
                                 jnp.concatenate([jnp.zeros_like(vb[i]), vb[i]], axis=1)], axis=0)
                for i in range(RW_GROUP)]
        ry = [_dotf(rbk[g], rhs3[g // 2]) for g in range(len(G2))]
        rt = pick(ry, 0)
        y0 = pick(ry, R)
        for i, j in enumerate(js):
            w_aft = jnp.exp(aft[i])
            att_ref[0, 0, j] = at[i].T.astype(_BF)
            u0t_ref[0, 0, j] = u0[i].T
            vt_ref[0, 0, j] = v[i].T.astype(_BF)
            rt_ref[0, 0, j] = (r_hat[i] + rt[i]).astype(_BF)
            bw_ref[0, 0, j] = (kk[i] * a[i] * w_aft).astype(_BF)
            kw_ref[0, 0, j] = (k[i] * w_aft).astype(_BF)
            y0_ref[0, 0, j] = y0[i]
            tot = cum[i] + aft[i]
            wc_ref[0, 0, j] = jnp.exp(jnp.concatenate([tot[p * C:p * C + 1] for p in range(RW_PAIRS)], axis=0))
        return carry

    lax.fori_loop(0, n_sub // RW_GROUP, chunk_group, 0)


def _rwkv_b_kernel(*refs, n_sub):
    ins, (yf_ref, yb_ref, s_ref) = refs[:16], refs[16:]
    C = RW_CHUNK

    @pl.when(pl.program_id(1) == 0)
    def _():
        s_ref[...] = jnp.zeros_like(s_ref)

    row = lax.broadcasted_iota(jnp.int32, (LANES, LANES), 0)
    col = lax.broadcasted_iota(jnp.int32, (LANES, LANES), 1)
    diag = (row // RW_DH) == (col // RW_DH)
    pair_of_row = lax.broadcasted_iota(jnp.int32, (2 * RW_ROWS, LANES), 0) % RW_ROWS // C
    DP = [(d, p) for d in range(N_DIR) for p in range(RW_PAIRS)]

    def chunk(jj, carry):
        cjs = [jj, n_sub - 1 - jj]
        att, rt, bw, kw, vt, u0t, y0, wc = [[ins[8 * d + a][0, 0, cjs[d]] for d in range(N_DIR)] for a in range(8)]
        bk = [jnp.concatenate([bw[d], kw[d]], axis=0) for d in range(N_DIR)]
        sps = [s_ref[d, p] for d, p in DP]
        spb = [sp.astype(_BF) for sp in sps]
        uts = [jnp.dot(spb[i], att[d], preferred_element_type=jnp.float32) + u0t[d] for i, (d, p) in enumerate(DP)]
        for i, (d, p) in enumerate(DP):
            rows = slice(p * C, (p + 1) * C)
            y = lax.dot_general(rt[d][rows], spb[i], _NT, preferred_element_type=jnp.float32) + y0[d][rows]
            y_ref = yf_ref if d == 0 else yb_ref
            y_ref[0, pl.ds(pl.multiple_of(cjs[d] * C, C), C), p * LANES:(p + 1) * LANES] = y
        for i, (d, p) in enumerate(DP):
            lhs = jnp.concatenate([uts[i].astype(_BF), vt[d]], axis=1)
            rhs = jnp.where(pair_of_row == p, bk[d], jnp.zeros_like(bk[d]))
            upd = jnp.dot(lhs, rhs, preferred_element_type=jnp.float32)
            s_ref[d, p] = jnp.where(diag, wc[d][p:p + 1, :] * sps[i] + upd, 0.0)
        return carry

    lax.fori_loop(0, n_sub, chunk, 0)


def _rwkv_pallas(r, v, kk, k_dir, a_dir, lw_dir, n_ctx):
    bsz, t_len, _ = r.shape
    nb, cb = t_len // RW_TBLK, n_ctx // RW_TBLK
    assert t_len % RW_TBLK == 0 and n_ctx % RW_TBLK == 0
    n_sub = RW_TBLK // RW_CHUNK
    nc = t_len // RW_CHUNK
    sh_spec = pl.BlockSpec((1, RW_TBLK, RW_W), lambda d, b, i: (b, _scan_block(d, i, nb, cb), 0))
    dr_spec = pl.BlockSpec((1, 1, RW_TBLK, RW_W), lambda d, b, i: (d, b, _scan_block(d, i, nb, cb), 0))
    ch_spec = pl.BlockSpec((1, 1, n_sub, RW_ROWS, LANES), lambda d, b, i: (d, b, _scan_block(d, i, nb, cb), 0, 0))
    wc_spec = pl.BlockSpec((1, 1, n_sub, RW_PAIRS, LANES), lambda d, b, i: (d, b, _scan_block(d, i, nb, cb), 0, 0))
    ch_shape = lambda dt: jax.ShapeDtypeStruct((N_DIR, bsz, nc, RW_ROWS, LANES), dt)
    params = pltpu.CompilerParams(dimension_semantics=("parallel", "parallel", "arbitrary"),
                                  vmem_limit_bytes=VMEM_LIMIT_BYTES)
    chunk_local = pl.pallas_call(
        functools.partial(_rwkv_a_kernel, n_sub=n_sub),
        grid=(N_DIR, bsz, nb),
        in_specs=[sh_spec, dr_spec, sh_spec, sh_spec, dr_spec, dr_spec],
        out_specs=[ch_spec] * 7 + [wc_spec],
        out_shape=[ch_shape(_BF)] * 5 + [ch_shape(jnp.float32)] * 2
                  + [jax.ShapeDtypeStruct((N_DIR, bsz, nc, RW_PAIRS, LANES), jnp.float32)],
        compiler_params=params, name="rwkv_chunk_local",
    )(r, k_dir, v, kk, a_dir, lw_dir)
    dir_specs = []
    for d in range(N_DIR):
        blk = functools.partial(lambda d, b, i: (d, b, _scan_block(d, i, nb, cb), 0, 0), d)
        dir_specs += [pl.BlockSpec((1, 1, n_sub, RW_ROWS, LANES), blk)] * 7
        dir_specs += [pl.BlockSpec((1, 1, n_sub, RW_PAIRS, LANES), blk)]
    y_specs = [pl.BlockSpec((1, RW_TBLK, RW_W),
                            functools.partial(lambda d, b, i: (b, _scan_block(d, i, nb, cb), 0), d))
               for d in range(N_DIR)]
    return pl.pallas_call(
        functools.partial(_rwkv_b_kernel, n_sub=n_sub),
        grid=(bsz, nb),
        in_specs=dir_specs,
        out_specs=y_specs,
        out_shape=[jax.ShapeDtypeStruct((bsz, t_len, RW_W), jnp.float32)] * N_DIR,
        scratch_shapes=[pltpu.VMEM((N_DIR, RW_PAIRS, LANES, LANES), jnp.float32)],
        compiler_params=pltpu.CompilerParams(dimension_semantics=("parallel", "arbitrary"),
                                             vmem_limit_bytes=VMEM_LIMIT_BYTES),
        name="rwkv_state_scan",
    )(*chunk_local, *chunk_local)


def _softplus(x):
    return jnp.maximum(x, 0.0) + jnp.log(1.0 + jnp.exp(-jnp.abs(x)))


def _rwkv_prep_kernel(*refs, blocks_per_seq, ctx_blocks):
    (zr, zk, zv, zw, za, zg, pr, pk, pv, pw, pa, pg, nr, nk, nv, nw, na, ng,
     mu_ref, w0_ref, a0_ref, wup_ref, aup_ref, gup_ref, kkw_ref, ka_ref, rk_ref, ones_ref,
     r_out, v_out, kk_out, kdir_out, a_out, lw_out, bonus_out, g_out) = refs
    blk = pl.program_id(0) % blocks_per_seq
    has_prev = (blk != 0) & (blk != ctx_blocks)
    has_next = (blk != ctx_blocks - 1) & (blk != blocks_per_seq - 1)
    tm = zr.shape[0]

    def lerp(z_ref, p_ref, n_ref, mu):
        z = z_ref[...]
        rows = lax.broadcasted_iota(jnp.int32, z.shape, 0)
        prev_row = jnp.where(has_prev, p_ref[RW_HALO - 1:RW_HALO, :], 0.0)
        next_row = jnp.where(has_next, n_ref[0:1, :], 0.0)
        before = jnp.where(rows == 0, prev_row, pltpu.roll(z, 1, axis=0))
        after = jnp.where(rows == tm - 1, next_row, pltpu.roll(z, tm - 1, axis=0))
        return z + mu * (0.5 * (before + after) - z)

    mu = mu_ref[...]
    r = lerp(zr, pr, nr, mu[:, 0:RW_W])
    k = lerp(zk, pk, nk, mu[:, RW_W:2 * RW_W])
    v = lerp(zv, pv, nv, mu[:, 2 * RW_W:3 * RW_W])
    wd = lerp(zw, pw, nw, mu[:, 3 * RW_W:3 * RW_W + LANES])
    ad = lerp(za, pa, na, mu[:, 3 * RW_W + LANES:3 * RW_W + 2 * LANES])
    gd = lerp(zg, pg, ng, mu[:, 3 * RW_W + 2 * LANES:3 * RW_W + 3 * LANES])
    dot = lambda a, w_ref: jnp.dot(a.astype(_BF), w_ref[...], preferred_element_type=jnp.float32)
    w_pre = dot(jnp.tanh(wd), wup_ref)
    a_pre = dot(ad, aup_ref)
    g_out[...] = dot(jax.nn.sigmoid(gd), gup_ref)

    def head_sum(x):
        hi = x.astype(_BF)
        lo = (x - hi.astype(jnp.float32)).astype(_BF)
        return (jnp.dot(hi, ones_ref[...], preferred_element_type=jnp.float32)
                + jnp.dot(lo, ones_ref[...], preferred_element_type=jnp.float32))

    kk = k * kkw_ref[...]
    kk_out[...] = kk / jnp.maximum(jnp.sqrt(head_sum(kk * kk)), 1e-12)
    r_out[...] = r
    v_out[...] = v
    rk = r * rk_ref[...]
    bonus = jnp.zeros_like(r)
    for d in range(N_DIR):
        cols = slice(d * RW_W, (d + 1) * RW_W)
        lw_out[d] = -jnp.exp(-_softplus(-(w0_ref[d:d + 1, :] + w_pre[:, cols])) - RW_DECAY_OFFSET)
        a = jax.nn.sigmoid(a0_ref[d:d + 1, :] + a_pre[:, cols])
        a_out[d] = a
        k_dir = k * (1.0 + (a - 1.0) * ka_ref[...])
        kdir_out[d] = k_dir
        bonus = bonus + rk * k_dir
    bonus_out[...] = head_sum(bonus) * v


def _rwkv_prep(p2, blocks_per_seq, ctx_blocks, mu, w0, w_up, a0, a_up, g_up, k_k, k_a, r_k):
    n = p2.shape[0]
    tm = ROW_TM
    hb = tm // RW_HALO
    n_hb = n // RW_HALO
    cw, cl = P_RW // RW_W, P_RWLR // LANES
    assert P_RW % RW_W == 0 and P_RWLR % LANES == 0 and N_DIR * RW_DECAY_RANK == LANES and RW_G_RANK == LANES

    def specs(rows, row_index):
        wide = [pl.BlockSpec((rows, RW_W), functools.partial(lambda j, i: (row_index(i), cw + j), j))
                for j in range(3)]
        return wide + [pl.BlockSpec((rows, LANES), functools.partial(lambda j, i: (row_index(i), cl + j), j))
                       for j in range(3)]

    main = specs(tm, lambda i: i)
    prev = specs(RW_HALO, lambda i: jnp.maximum(i * hb - 1, 0))
    nxt = specs(RW_HALO, lambda i: jnp.minimum((i + 1) * hb, n_hb - 1))
    zero = jnp.zeros((RW_DECAY_RANK, RW_W), jnp.float32)
    both_dirs = lambda up: jnp.concatenate([jnp.concatenate([up[0], zero], axis=1),
                                            jnp.concatenate([zero, up[1]], axis=1)], axis=0).astype(_BF)
    head = jnp.arange(RW_W) // RW_DH
    ones_bd = (head[:, None] == head[None, :]).astype(_BF)
    vec = lambda w: pl.BlockSpec((1, w), lambda i: (0, 0))
    dvec = pl.BlockSpec((N_DIR, RW_W), lambda i: (0, 0))
    consts = [vec(mu.shape[0]), dvec, dvec, _const_spec((LANES, N_DIR * RW_W)), _const_spec((LANES, N_DIR * RW_W)),
              _const_spec((RW_G_RANK, RW_W)), vec(RW_W), vec(RW_W), vec(RW_W), _const_spec((RW_W, RW_W))]
    row = pl.BlockSpec((tm, RW_W), lambda i: (i, 0))
    drow = pl.BlockSpec((N_DIR, tm, RW_W), lambda i: (0, i, 0))
    sh = jax.ShapeDtypeStruct((n, RW_W), jnp.float32)
    dsh = jax.ShapeDtypeStruct((N_DIR, n, RW_W), jnp.float32)
    return pl.pallas_call(
        functools.partial(_rwkv_prep_kernel, blocks_per_seq=blocks_per_seq, ctx_blocks=ctx_blocks),
        grid=(n // tm,),
        in_specs=main + prev + nxt + consts,
        out_specs=[row, row, row, drow, drow, drow, row, row],
        out_shape=[sh, sh, sh, dsh, dsh, dsh, sh, sh],
        compiler_params=pltpu.CompilerParams(dimension_semantics=("parallel",), vmem_limit_bytes=VMEM_LIMIT_BYTES),
        name="rwkv_prep",
    )(*([p2] * 18), mu.reshape(1, -1), w0, a0, both_dirs(w_up), both_dirs(a_up), g_up.astype(_BF),
      k_k.reshape(1, -1), k_a.reshape(1, -1), r_k.reshape(1, -1), ones_bd)


def _layer_norm(x, eps=LN_EPS):
    mu = jnp.mean(x, axis=-1, keepdims=True)
    var = jnp.mean(jnp.square(x - mu), axis=-1, keepdims=True)
    return (x - mu) * lax.rsqrt(var + eps)


def _modulate(x, shift, scale):
    return _layer_norm(x) * (1.0 + scale) + shift


def _depthwise_conv3x3(z, w, b):
    ch = z.shape[-1]
    y = lax.conv_general_dilated(z, w[:, :, None, :], window_strides=(1, 1), padding='SAME',
                                 dimension_numbers=('NHWC', 'HWIO', 'NHWC'), feature_group_count=ch)
    return y + b


def _mlstm_branch(seg, n_ctx, conv_w, conv_b, ig_b, fg_b):
    q, k, v, ig, fg = seg
    bsz, t_len, _ = q.shape

    def conv(z):
        length = z.shape[1]
        rows, cols = (1, length) if length == n_ctx else (length // GRID_W, GRID_W)
        y = _depthwise_conv3x3(z.reshape(bsz, rows, cols, 2 * ML_W), conv_w, conv_b)
        return jax.nn.silu(y).reshape(bsz, length, 2 * ML_W)

    qk = jnp.concatenate([conv(jnp.concatenate([q[:, :n_ctx], k[:, :n_ctx]], axis=-1)),
                          conv(jnp.concatenate([q[:, n_ctx:], k[:, n_ctx:]], axis=-1))], axis=1)
    return _mlstm_pallas(qk[..., :ML_W], qk[..., ML_W:], v, jnp.concatenate([ig, fg], axis=-1), ig_b, fg_b, n_ctx)


def _rwkv7_branch(p2, bsz, n_ctx, mu, w0, w_up, a0, a_up, g_up, k_k, k_a, r_k):
    t_len = p2.shape[0] // bsz
    r, v, kk, k_dir, a, log_decay, bonus, g = _rwkv_prep(p2, t_len // ROW_TM, n_ctx // ROW_TM, mu, w0, w_up, a0,
                                                         a_up, g_up, k_k, k_a, r_k)
    seq = lambda z: z.reshape(z.shape[:-2] + (bsz, t_len, RW_W))
    y_f, y_b = _rwkv_pallas(seq(r), seq(v), seq(kk), seq(k_dir), seq(a), seq(log_decay), n_ctx)
    return y_f.reshape(-1, RW_W), y_b.reshape(-1, RW_W), bonus, g


def _token_mixer(u, n_ctx, w_in,
                 ml_conv_w, ml_conv_b, ml_ig_b, ml_fg_b, ml_norm_g, ml_norm_b, ml_proj,
                 rw_mu, rw_w0, rw_w_up, rw_a0, rw_a_up, rw_g_up, rw_k_k, rw_k_a, rw_r_k,
                 rw_norm_g, rw_norm_b, rw_proj,
                 s5_lam_re, s5_lam_im, s5_log_dt, s5_b_re, s5_b_im, s5_c_re, s5_c_im, s5_d,
                 s5_w_val, s5_w_gate):
    bsz, t_len, _ = u.shape
    n = bsz * t_len
    p = _mm_any(u, _permute_w_in(w_in), tm=PROJ_TM, keep_col_pad=True)

    def col(start, width):
        return p[..., start:start + width]

    n_gate = N_DIR * ML_HEADS
    h_dir = _mlstm_branch((col(P_QK, ML_W), col(P_QK + ML_W, ML_W), col(P_V, ML_W),
                           col(P_MLG, n_gate), col(P_MLG + n_gate, n_gate)),
                          n_ctx, ml_conv_w, ml_conv_b, ml_ig_b, ml_fg_b)
    p2 = p.reshape(n, -1)
    y_f, y_b, bonus, g = _rwkv7_branch(p2, bsz, n_ctx, rw_mu, rw_w0, rw_w_up, rw_a0, rw_a_up, rw_g_up,
                                       rw_k_k, rw_k_a, rw_r_k)
    s5 = _s5_pallas(p, n_ctx, s5_lam_re, s5_lam_im, s5_log_dt, s5_b_re, s5_b_im, s5_c_re, s5_c_im, s5_d)
    ml, rw = _post_scan(p2, h_dir.reshape(N_DIR, n, ML_W), y_f, y_b, bonus, g,
                        ml_norm_g, ml_norm_b, rw_norm_g, rw_norm_b)
    return _merge_pallas(p2, ml, rw, s5.reshape(S5_W // LANES, n, LANES),
                         ml_proj.astype(_BF), rw_proj.astype(_BF), s5_w_val.astype(_BF), s5_w_gate.astype(_BF))


def _permute_w_in(w_in):
    offs = np.cumsum((0,) + IN_WIDTHS)
    seg = lambda a, b: w_in[:, offs[a]:offs[b]]
    return jnp.concatenate([seg(0, 4), seg(6, 9), seg(12, 13), seg(13, 14), seg(9, 12), seg(4, 6)], axis=1)


def _moe_ffn(u, router_w, router_b, layer, w_gate, w_up, w_down):
    gates_t = _router_pallas(u, router_w, router_b)
    return _moe_routed(u, gates_t, layer, w_gate, w_up, w_down)


def kernel(x, c, ctx, c_ctx, ada_w, ada_b, w_in, ml_conv_w, ml_conv_b, ml_ig_b, ml_fg_b, ml_norm_g,
           ml_norm_b, ml_proj, rw_mu, rw_w0, rw_w_up, rw_a0, rw_a_up, rw_g_up, rw_k_k, rw_k_a, rw_r_k,
           rw_norm_g, rw_norm_b, rw_proj, s5_lam_re, s5_lam_im, s5_log_dt, s5_b_re, s5_b_im, s5_c_re,
           s5_c_im, s5_d, s5_w_val, s5_w_gate, w_out, ln1_g, ln1_b, ln2_g, ln2_b, router_w, router_b,
           exp_w_gate, exp_w_up, exp_w_down):
    bsz, n_ctx = ctx.shape[0], ctx.shape[1]
    t_len = n_ctx + x.shape[1]
    assert n_ctx % ROW_TM == 0 and t_len % ROW_TM == 0
    blocks_per_seq, ctx_blocks = t_len // ROW_TM, n_ctx // ROW_TM
    silu_c = jax.nn.silu(c)
    silu_cc = jax.nn.silu(c_ctx)[None, :]
    mods = []
    for i in range(DEPTH):
        mx = _mm_any(silu_c, ada_w[i]) + ada_b[i]
        mc = jnp.broadcast_to(_mm_any(silu_cc, ada_w[i]) + ada_b[i], mx.shape)
        mods.append(jnp.stack([mc, mx], axis=1).reshape(bsz, 2, N_MOD, 1, D_MODEL))
    xa = jnp.concatenate([ctx, x], axis=1).reshape(bsz * t_len, D_MODEL)
    m0 = mods[0]
    u = jnp.concatenate([_modulate(ctx, m0[:, 0, 0], m0[:, 0, 1]), _modulate(x, m0[:, 1, 0], m0[:, 1, 1])],
                        axis=1).astype(_BF)
    for i in range(DEPTH):
        z = _token_mixer(
            u, n_ctx, w_in[i],
            ml_conv_w[i], ml_conv_b[i], ml_ig_b[i], ml_fg_b[i], ml_norm_g[i], ml_norm_b[i], ml_proj[i],
            rw_mu[i], rw_w0[i], rw_w_up[i], rw_a0[i], rw_a_up[i], rw_g_up[i], rw_k_k[i], rw_k_a[i], rw_r_k[i],
            rw_norm_g[i], rw_norm_b[i], rw_proj[i],
            s5_lam_re[i], s5_lam_im[i], s5_log_dt[i], s5_b_re[i], s5_b_im[i], s5_c_re[i], s5_c_im[i], s5_d[i],
            s5_w_val[i], s5_w_gate[i])
        xa, u_ffn = _resid_norm_mod(z, w_out[i].astype(_BF), xa, mods[i], 2, ln1_g[i], ln1_b[i], mods[i], 3, 4,
                                    blocks_per_seq, ctx_blocks)
        ffn = _moe_ffn(u_ffn, router_w, router_b, i, exp_w_gate, exp_w_up, exp_w_down)
        xa, u = _resid_norm_mod(ffn, None, xa, mods[i], 5, ln2_g[i], ln2_b[i], mods[min(i + 1, DEPTH - 1)], 0, 1,
                                blocks_per_seq, ctx_blocks, u_dtype=_BF)
        u = u.reshape(bsz, t_len, D_MODEL)
    return xa.reshape(bsz, t_len, D_MODEL)[:, n_ctx:]
```

```python
import functools
import math

import jax
import jax.numpy as jnp
import numpy as np
from jax import lax
from jax.experimental import pallas as pl
from jax.experimental.pallas import tpu as pltpu

D_MODEL = 2048
DEPTH = 2
GRID_W = 64
N_DIR = 2
ML_HEADS = 4
ML_DH = 256
ML_W = ML_HEADS * ML_DH
ML_CHUNK = 64
ML_NORM_EPS = 1e-6
RW_HEADS = 16
RW_DH = 64
RW_W = RW_HEADS * RW_DH
RW_DECAY_RANK = 64
RW_A_RANK = 64
RW_G_RANK = 128
RW_DECAY_OFFSET = 0.5
RW_NORM_EPS = 64e-5
S5_W = 1024
S5_GROUP = 16
S5_GROUPS = S5_W // S5_GROUP
S5_STATE = 64
N_BRANCH = 3
N_GROUPS = 4
EXPERTS_PER_GROUP = 4
N_EXPERTS = N_GROUPS * EXPERTS_PER_GROUP
TOP_K = 2
D_EXPERT = 1024
DEEPNORM_ALPHA = (2.0 * DEPTH) ** 0.25
LN_EPS = 1e-5
N_MOD = 6
IN_WIDTHS = (ML_W, ML_W, ML_W, ML_W, N_DIR * ML_HEADS, N_DIR * ML_HEADS,
             RW_W, RW_W, RW_W, N_DIR * RW_DECAY_RANK, N_DIR * RW_A_RANK, RW_G_RANK,
             S5_W, N_BRANCH * D_MODEL)
D_IN = sum(IN_WIDTHS)
RW_IN_WIDTHS = (RW_W, RW_W, RW_W, N_DIR * RW_DECAY_RANK, N_DIR * RW_A_RANK, RW_G_RANK)

VMEM_LIMIT_BYTES = 56 * 1024 * 1024
LANES = 128

S5_CHUNK = 16
S5_UNROLL = 8
RL_TBLK = 256
PROJ_TM = 1536
ML_TBLK = 256
RW_CHUNK = 16
RW_PAIRS = RW_HEADS // 2
RW_ROWS = RW_PAIRS * RW_CHUNK
RW_TBLK = 256
RW_GROUP = 8
RW_HALO = 8
MOE_TM = 256
ROW_TM = 256

P_QK, P_V, P_O = 0, 2 * ML_W, 3 * ML_W
P_RW = 4 * ML_W
P_S5 = P_RW + 3 * RW_W
P_GATE = P_S5 + S5_W
P_RWLR = P_GATE + N_BRANCH * D_MODEL
P_MLG = P_RWLR + 2 * N_DIR * RW_DECAY_RANK + RW_G_RANK
assert P_GATE % D_MODEL == 0 and P_MLG + 2 * N_DIR * ML_HEADS == D_IN and RW_DECAY_RANK == RW_A_RANK

_BF = jnp.bfloat16
_HI = lax.Precision.HIGHEST
_NT = (((1,), (1,)), ((), ()))
_TN = (((0,), (0,)), ((), ()))


def _scan_block(d, i, n_blocks, ctx_blocks):
    bwd = jnp.where(i < ctx_blocks, ctx_blocks - 1 - i, n_blocks - 1 + ctx_blocks - i)
    return jnp.where(d == 0, i, bwd)


def _mm_kernel(a_ref, w_ref, o_ref, abf_ref):
    @pl.when(pl.program_id(1) == 0)
    def _():
        abf_ref[...] = a_ref[...].astype(_BF)

    o_ref[...] = jnp.dot(abf_ref[...], w_ref[...], preferred_element_type=jnp.float32)


def _mm_bf16_kernel(a_ref, w_ref, o_ref):
    o_ref[...] = jnp.dot(a_ref[...], w_ref[...], preferred_element_type=jnp.float32)


def _mm(a, w, tm, tn):
    m, k = a.shape
    n = w.shape[1]
    assert m % tm == 0 and n % tn == 0, (m, n, tm, tn)
    if a.dtype == _BF:
        return pl.pallas_call(
            _mm_bf16_kernel,
            grid=(m // tm, n // tn),
            in_specs=[pl.BlockSpec((tm, k), lambda i, j: (i, 0)),
                      pl.BlockSpec((k, tn), lambda i, j: (0, j))],
            out_specs=pl.BlockSpec((tm, tn), lambda i, j: (i, j)),
            out_shape=jax.ShapeDtypeStruct((m, n), jnp.float32),
            compiler_params=pltpu.CompilerParams(
                dimension_semantics=("parallel", "arbitrary"),
                vmem_limit_bytes=VMEM_LIMIT_BYTES),
            name="mm_bf16",
        )(a, w)
    return pl.pallas_call(
        _mm_kernel,
        grid=(m // tm, n // tn),
        in_specs=[pl.BlockSpec((tm, k), lambda i, j: (i, 0)),
                  pl.BlockSpec((k, tn), lambda i, j: (0, j))],
        out_specs=pl.BlockSpec((tm, tn), lambda i, j: (i, j)),
        out_shape=jax.ShapeDtypeStruct((m, n), jnp.float32),
        scratch_shapes=[pltpu.VMEM((tm, k), _BF)],
        compiler_params=pltpu.CompilerParams(
            dimension_semantics=("parallel", "arbitrary"),
            vmem_limit_bytes=VMEM_LIMIT_BYTES),
        name="mm",
    )(a, w)


def _mm_any(a, w, tm=1024, tn=512, keep_col_pad=False):
    lead = a.shape[:-1]
    a2 = a.reshape(-1, a.shape[-1])
    m, n = a2.shape[0], w.shape[1]
    mp = -(-m // 8) * 8
    if mp > tm:
        mp = -(-m // tm) * tm
    tm = min(tm, mp)
    np_ = -(-n // LANES) * LANES
    if np_ > tn:
        np_ = -(-n // tn) * tn
    tn = min(tn, np_)
    if mp != m:
        a2 = jnp.pad(a2, ((0, mp - m), (0, 0)))
    wb = w.astype(_BF)
    if np_ != n:
        wb = jnp.pad(wb, ((0, 0), (0, np_ - n)))
    out = _mm(a2, wb, tm, tn)
    if keep_col_pad:
        n = np_
    if mp != m or np_ != n:
        out = out[:m, :n]
    return out.reshape(lead + (n,))


def _const_spec(shape):
    nd = len(shape)
    return pl.BlockSpec(shape, lambda i: (0,) * nd, pipeline_mode=pl.Buffered(1))


def _merge_kernel(ml_ref, rw_ref, s5_ref, g0_ref, g1_ref, g2_ref, wml_ref, wrw_ref, wval_ref, wgate_ref, z_ref):
    dot = lambda a, w_ref: jnp.dot(a.astype(_BF), w_ref[...], preferred_element_type=jnp.float32)
    s5 = jnp.concatenate([s5_ref[m] for m in range(S5_W // LANES)], axis=1)
    sval = dot(s5, wval_ref) * jax.nn.sigmoid(dot(s5, wgate_ref))
    z = (jax.nn.sigmoid(g0_ref[...]) * dot(ml_ref[...], wml_ref)
         + jax.nn.sigmoid(g1_ref[...]) * dot(rw_ref[...], wrw_ref)
         + jax.nn.sigmoid(g2_ref[...]) * sval)
    z_ref[...] = z.astype(_BF)


def _merge_pallas(p2, ml, rw, s5, wml, wrw, wval, wgate):
    n, w = ml.shape
    tm = ROW_TM
    gb = P_GATE // D_MODEL
    row = lambda i: (i, 0)
    return pl.pallas_call(
        _merge_kernel,
        grid=(n // tm,),
        in_specs=[pl.BlockSpec((tm, w), row)] * 2 + [pl.BlockSpec((S5_W // LANES, tm, LANES), lambda i: (0, i, 0))]
                 + [pl.BlockSpec((tm, D_MODEL), functools.partial(lambda j, i: (i, gb + j), j))
                    for j in range(N_BRANCH)]
                 + [_const_spec((w, D_MODEL))] * 4,
        out_specs=pl.BlockSpec((tm, D_MODEL), row),
        out_shape=jax.ShapeDtypeStruct((n, D_MODEL), _BF),
        compiler_params=pltpu.CompilerParams(dimension_semantics=("parallel",), vmem_limit_bytes=VMEM_LIMIT_BYTES),
        name="merge_gate_proj",
    )(ml, rw, s5, p2, p2, p2, wml, wrw, wval, wgate)


def _post_scan_kernel(hf_ref, hb_ref, o_ref, yf_ref, yb_ref, bonus_ref, g_ref, mlg_ref, mlb_ref, rwg_ref, rwb_ref,
                      ones_ref, ml_ref, rw_ref):
    h = hf_ref[0] + hb_ref[0]
    parts = []
    for hd in range(ML_HEADS):
        x = h[:, hd * ML_DH:(hd + 1) * ML_DH]
        xc = x - jnp.mean(x, axis=-1, keepdims=True)
        parts.append(xc * lax.rsqrt(jnp.mean(xc * xc, axis=-1, keepdims=True) + ML_NORM_EPS))
    hn = jnp.concatenate(parts, axis=1) * mlg_ref[...] + mlb_ref[...]
    ml_ref[...] = (jax.nn.sigmoid(o_ref[...]) * hn).astype(_BF)

    def head_mean(x):
        hi = x.astype(_BF)
        lo = (x - hi.astype(jnp.float32)).astype(_BF)
        s = (jnp.dot(hi, ones_ref[...], preferred_element_type=jnp.float32)
             + jnp.dot(lo, ones_ref[...], preferred_element_type=jnp.float32))
        return s * (1.0 / RW_DH)

    y = yf_ref[...] + yb_ref[...]
    yc = y - head_mean(y)
    yn = yc * lax.rsqrt(head_mean(yc * yc) + RW_NORM_EPS) * rwg_ref[...] + rwb_ref[...]
    rw_ref[...] = ((yn + bonus_ref[...]) * g_ref[...]).astype(_BF)


def _post_scan(p2, h_dir, y_f, y_b, bonus, g, ml_norm_g, ml_norm_b, rw_norm_g, rw_norm_b):
    n = y_f.shape[0]
    tm = ROW_TM
    head = jnp.arange(RW_W) // RW_DH
    ones_bd = (head[:, None] == head[None, :]).astype(_BF)
    vec = pl.BlockSpec((1, RW_W), lambda i: (0, 0))
    blk = pl.BlockSpec((tm, RW_W), lambda i: (i, 0))
    return pl.pallas_call(
        _post_scan_kernel,
        grid=(n // tm,),
        in_specs=[pl.BlockSpec((1, tm, ML_W), lambda i: (0, i, 0)), pl.BlockSpec((1, tm, ML_W), lambda i: (1, i, 0)),
                  pl.BlockSpec((tm, ML_W), lambda i: (i, P_O // ML_W)), blk, blk, blk, blk, vec, vec, vec, vec,
                  _const_spec((RW_W, RW_W))],
        out_specs=[blk, blk],
        out_shape=[jax.ShapeDtypeStruct((n, ML_W), _BF), jax.ShapeDtypeStruct((n, RW_W), _BF)],
        compiler_params=pltpu.CompilerParams(dimension_semantics=("parallel",), vmem_limit_bytes=VMEM_LIMIT_BYTES),
        name="post_scan_norm_gate",
    )(h_dir, h_dir, p2, y_f, y_b, bonus, g, ml_norm_g.reshape(1, -1), ml_norm_b.reshape(1, -1),
      rw_norm_g.reshape(1, -1), rw_norm_b.reshape(1, -1), ones_bd)


def _ln_rows(x, eps=LN_EPS):
    mu = jnp.mean(x, axis=-1, keepdims=True)
    xc = x - mu
    var = jnp.mean(xc * xc, axis=-1, keepdims=True)
    return xc * lax.rsqrt(var + eps)


def _resid_kernel(*refs, with_w):
    if with_w:
        d_ref, w_ref, x_ref, gate_ref, g_ref, b_ref, sh_ref, sc_ref, xo_ref, uo_ref = refs
        delta = jnp.dot(d_ref[...], w_ref[...], preferred_element_type=jnp.float32)
    else:
        d_ref, x_ref, gate_ref, g_ref, b_ref, sh_ref, sc_ref, xo_ref, uo_ref = refs
        delta = d_ref[...]
    xn = _ln_rows(DEEPNORM_ALPHA * x_ref[...] + gate_ref[0, 0, 0] * delta) * g_ref[...] + b_ref[...]
    xo_ref[...] = xn
    uo_ref[...] = (_ln_rows(xn) * (1.0 + sc_ref[0, 0, 0]) + sh_ref[0, 0, 0]).astype(uo_ref.dtype)


def _resid_norm_mod(delta, w, x, mod_a, ia, ln_g, ln_b, mod_b, ish, isc, blocks_per_seq, ctx_blocks,
                    u_dtype=jnp.float32):
    n, d = x.shape
    tm = ROW_TM
    row = lambda i: (i, 0)

    def mod_spec(m):
        return pl.BlockSpec((1, 1, 1, 1, d), lambda i: (i // blocks_per_seq,
                                                       (i % blocks_per_seq >= ctx_blocks).astype(jnp.int32), m, 0, 0))

    vec = pl.BlockSpec((1, d), lambda i: (0, 0))
    if w is not None:
        in_specs = [pl.BlockSpec((tm, delta.shape[1]), row), _const_spec(w.shape), pl.BlockSpec((tm, d), row)]
        args = (delta, w, x)
    else:
        in_specs = [pl.BlockSpec((tm, d), row), pl.BlockSpec((tm, d), row)]
        args = (delta, x)
    return pl.pallas_call(
        functools.partial(_resid_kernel, with_w=w is not None),
        grid=(n // tm,),
        in_specs=in_specs + [mod_spec(ia), vec, vec, mod_spec(ish), mod_spec(isc)],
        out_specs=[pl.BlockSpec((tm, d), row)] * 2,
        out_shape=[jax.ShapeDtypeStruct((n, d), jnp.float32), jax.ShapeDtypeStruct((n, d), u_dtype)],
        compiler_params=pltpu.CompilerParams(dimension_semantics=("parallel",), vmem_limit_bytes=VMEM_LIMIT_BYTES),
        name="resid_norm_mod",
    )(*args, mod_a, ln_g.reshape(1, d), ln_b.reshape(1, d), mod_b, mod_b)


def _moe_ffn_kernel(te_ref, tv_ref, x_ref, wg_ref, wu_ref, wd_ref, y_ref, wgb_ref, wub_ref, wdb_ref):
    t = pl.program_id(0)

    @pl.when((t == 0) | (te_ref[t] != te_ref[jnp.maximum(t - 1, 0)]))
    def _():
        wgb_ref[...] = wg_ref[0, 0].astype(_BF)
        wub_ref[...] = wu_ref[0, 0].astype(_BF)
        wdb_ref[...] = wd_ref[0, 0].astype(_BF)

    @pl.when(tv_ref[t] == 1)
    def _():
        x = x_ref[...].astype(_BF)
        hg = jnp.dot(x, wgb_ref[...], preferred_element_type=jnp.float32)
        hu = jnp.dot(x, wub_ref[...], preferred_element_type=jnp.float32)
        h = (hg * jax.nn.sigmoid(hg)) * hu
        y_ref[...] = jnp.dot(h.astype(_BF), wdb_ref[...], preferred_element_type=jnp.float32)

    @pl.when(tv_ref[t] == 0)
    def _():
        y_ref[...] = jnp.zeros_like(y_ref)


def _moe_routed(u, gates_t, layer, wg, wu, wd):
    n_tok, d = u.shape
    _, n_e, _, d_e = wg.shape
    tm = MOE_TM
    n_tiles = (TOP_K * n_tok) // tm + n_e
    n_slots = n_tiles * tm
    sel = gates_t > 0.0
    seli = sel.astype(jnp.int32)
    rank = jnp.cumsum(seli, axis=1) - 1
    cnt = jnp.sum(seli, axis=1)
    tiles_e = (cnt + tm - 1) // tm
    tile_end = jnp.cumsum(tiles_e)
    off = (tile_end - tiles_e) * tm
    slot = off[:, None] + rank
    order = jnp.cumsum(seli, axis=0)
    slots, gsel = [], []
    for j in range(TOP_K):
        pick = sel & (order == j + 1)
        slots.append(jnp.sum(jnp.where(pick, slot, 0), axis=0))
        gsel.append(jnp.sum(jnp.where(pick, gates_t, 0.0), axis=0))
    tok = jnp.arange(n_tok, dtype=jnp.int32)
    tok_of_slot = jnp.zeros((n_slots,), jnp.int32).at[jnp.concatenate(slots)].set(jnp.tile(tok, TOP_K))
    tile_ids = jnp.arange(n_tiles, dtype=jnp.int32)
    tile_valid = (tile_ids < tile_end[-1]).astype(jnp.int32)
    tile_expert = jnp.minimum(jnp.searchsorted(tile_end, tile_ids, side='right'), n_e - 1).astype(jnp.int32)
    last_e = jnp.max(jnp.where(cnt > 0, jnp.arange(n_e), 0)).astype(jnp.int32)
    tile_expert = jnp.where(tile_valid == 1, tile_expert, last_e)
    xs = u.astype(_BF).at[tok_of_slot].get(mode='promise_in_bounds')
    w_index = lambda t, te, tv: (layer, te[t], 0, 0)
    grid_spec = pltpu.PrefetchScalarGridSpec(
        num_scalar_prefetch=2,
        grid=(n_tiles,),
        in_specs=[pl.BlockSpec((tm, d), lambda t, te, tv: (t, 0)),
                  pl.BlockSpec((1, 1, d, d_e), w_index, pipeline_mode=pl.Buffered(1)),
                  pl.BlockSpec((1, 1, d, d_e), w_index, pipeline_mode=pl.Buffered(1)),
                  pl.BlockSpec((1, 1, d_e, d), w_index, pipeline_mode=pl.Buffered(1))],
        out_specs=pl.BlockSpec((tm, d), lambda t, te, tv: (t, 0)),
        scratch_shapes=[pltpu.VMEM((d, d_e), _BF), pltpu.VMEM((d, d_e), _BF), pltpu.VMEM((d_e, d), _BF)])
    ys = pl.pallas_call(
        _moe_ffn_kernel, grid_spec=grid_spec,
        out_shape=jax.ShapeDtypeStruct((n_slots, d), jnp.float32),
        compiler_params=pltpu.CompilerParams(dimension_semantics=("arbitrary",),
                                             vmem_limit_bytes=VMEM_LIMIT_BYTES),
        name="moe_routed_ffn",
    )(tile_expert, tile_valid, xs, wg, wu, wd)
    return sum(gsel[j][:, None] * ys.at[slots[j]].get(mode='promise_in_bounds') for j in range(TOP_K))


def _router_kernel(u_ref, wt_ref, b_ref, g_ref):
    logits = lax.dot_general(wt_ref[...], u_ref[...], _NT, precision=_HI, preferred_element_type=jnp.float32)
    aff = jax.nn.sigmoid(logits)
    score = aff + b_ref[...]
    s = [score[e:e + 1] for e in range(N_EXPERTS)]
    a = [aff[e:e + 1] for e in range(N_EXPERTS)]
    gs = []
    for g in range(N_GROUPS):
        m = s[g * EXPERTS_PER_GROUP:(g + 1) * EXPERTS_PER_GROUP]
        best = None
        for i in range(EXPERTS_PER_GROUP):
            for j in range(i + 1, EXPERTS_PER_GROUP):
                best = m[i] + m[j] if best is None else jnp.maximum(best, m[i] + m[j])
        gs.append(best)
    best_val = gs[0]
    best_grp = jnp.zeros_like(gs[0], dtype=jnp.int32)
    for g in range(1, N_GROUPS):
        better = gs[g] > best_val
        best_grp = jnp.where(better, g, best_grp)
        best_val = jnp.where(better, gs[g], best_val)
    sel = []
    for e in range(N_EXPERTS):
        g = e // EXPERTS_PER_GROUP
        rank = jnp.zeros_like(best_grp)
        for j in range(g * EXPERTS_PER_GROUP, (g + 1) * EXPERTS_PER_GROUP):
            if j != e:
                ahead = (s[j] > s[e]) | ((s[j] == s[e]) & (j < e))
                rank = rank + ahead.astype(jnp.int32)
        sel.append((best_grp == g) & (rank < TOP_K))
    wsum = sum(jnp.where(sel[e], a[e], 0.0) for e in range(N_EXPERTS))
    g_ref[...] = jnp.concatenate([jnp.where(sel[e], a[e] / wsum, 0.0) for e in range(N_EXPERTS)], axis=0)


def _router_pallas(u, router_w, router_b, tm=512):
    n_tok, d = u.shape
    assert n_tok % tm == 0
    return pl.pallas_call(
        _router_kernel,
        grid=(n_tok // tm,),
        in_specs=[pl.BlockSpec((tm, d), lambda i: (i, 0)),
                  pl.BlockSpec((N_EXPERTS, d), lambda i: (0, 0)),
                  pl.BlockSpec((N_EXPERTS, 1), lambda i: (0, 0))],
        out_specs=pl.BlockSpec((N_EXPERTS, tm), lambda i: (0, i)),
        out_shape=jax.ShapeDtypeStruct((N_EXPERTS, n_tok), jnp.float32),
        compiler_params=pltpu.CompilerParams(dimension_semantics=("parallel",),
                                             vmem_limit_bytes=VMEM_LIMIT_BYTES),
        name="moe_router",
    )(u, router_w.T, router_b.reshape(N_EXPERTS, 1))


def _s5_mats(lam_re, lam_im, log_dt, b_re, b_im, c_re, c_im):
    L = S5_CHUNK
    lam = lax.complex(lam_re, lam_im)
    ldt = lam * jnp.exp(log_dt)[..., None]
    lam_bar = jnp.exp(ldt)
    b_bar = ((lam_bar - 1.0) / lam)[..., None] * lax.complex(b_re, b_im)
    c_mat = lax.complex(c_re, c_im)
    tau = jnp.arange(L + 1, dtype=jnp.float32)
    pw = jnp.exp(ldt[:, :, None, :] * tau[None, None, :, None])
    kern = jnp.real(jnp.einsum('dgon,dgtn,dgni->dgtoi', c_mat, pw[:, :, :L], b_bar))
    s_idx = jnp.arange(L)[:, None]
    t_idx = jnp.arange(L)[None, :]

    def toeplitz(k, lag, valid):
        m = k[:, jnp.clip(lag, 0, L - 1)] * valid[None, :, :, None, None]
        return jnp.transpose(m, (0, 1, 4, 2, 3)).reshape(-1, L * S5_GROUP, L * S5_GROUP)

    tsum = (toeplitz(kern[0], t_idx - s_idx, (t_idx >= s_idx).astype(jnp.float32))
            + toeplitz(kern[1], s_idx - t_idx, (s_idx >= t_idx).astype(jnp.float32)))
    pin_f = pw[0][:, L - 1 - jnp.arange(L)]
    pin_b = pw[1][:, jnp.arange(L)]
    in_f = jnp.einsum('gsn,gni->gsin', pin_f, b_bar[0]).reshape(-1, L * S5_GROUP, S5_STATE)
    in_b = jnp.einsum('gsn,gni->gsin', pin_b, b_bar[1]).reshape(-1, L * S5_GROUP, S5_STATE)
    icat = jnp.concatenate([jnp.real(in_f), jnp.imag(in_f), jnp.real(in_b), jnp.imag(in_b)], axis=-1)
    pout_f = pw[0][:, 1 + jnp.arange(L)]
    pout_b = pw[1][:, L - jnp.arange(L)]
    out_f = jnp.einsum('gon,gtn->gnto', c_mat[0], pout_f).reshape(-1, S5_STATE, L * S5_GROUP)
    out_b = jnp.einsum('gon,gtn->gnto', c_mat[1], pout_b).reshape(-1, S5_STATE, L * S5_GROUP)
    ocat = jnp.concatenate([jnp.real(out_f), -jnp.imag(out_f), jnp.real(out_b), -jnp.imag(out_b)], axis=1)
    lam_l = pw[:, :, L]
    lam_chunk = jnp.stack([jnp.real(lam_l[0]), jnp.imag(lam_l[0]), jnp.real(lam_l[1]), jnp.imag(lam_l[1])],
                          axis=1)
    return tsum, icat, ocat, lam_chunk


def _s5_kernel(u0_ref, u1_ref, t_ref, i_ref, o_ref, lam_ref, d_ref, y0_ref, y1_ref,
               v_ref, xfr_ref, xfi_ref, xbr_ref, xbi_ref, *, n_chunks, ctx_chunks, bsz):
    n = S5_STATE
    u = jnp.concatenate([u0_ref[0], u1_ref[0]], axis=1)
    ub = u.astype(_BF)
    v_ref[...] = jnp.dot(ub, i_ref[0], preferred_element_type=jnp.float32)
    lam = lam_ref[0]
    lfr = jnp.broadcast_to(lam[0:1], (bsz, n))
    lfi = jnp.broadcast_to(lam[1:2], (bsz, n))
    lbr = jnp.broadcast_to(lam[2:3], (bsz, n))
    lbi = jnp.broadcast_to(lam[3:4], (bsz, n))

    def cmul_add(lr, li, xr, xi, vr, vi):
        return lr * xr - li * xi + vr, lr * xi + li * xr + vi

    def step(j, carry):
        fr, fi, br, bi = carry
        rf = pl.multiple_of(j * (2 * bsz), 2 * bsz)
        pb = jnp.where(j < ctx_chunks // 2, ctx_chunks // 2 - 1 - j, (n_chunks + ctx_chunks) // 2 - 1 - j)
        rb = pl.multiple_of(pb * (2 * bsz), 2 * bsz)
        vf = v_ref[pl.ds(rf, 2 * bsz), :]
        vb = v_ref[pl.ds(rb, 2 * bsz), :]
        fr1, fi1 = cmul_add(lfr, lfi, fr, fi, vf[:bsz, 0:n], vf[:bsz, n:2 * n])
        fr2, fi2 = cmul_add(lfr, lfi, fr1, fi1, vf[bsz:, 0:n], vf[bsz:, n:2 * n])
        br1, bi1 = cmul_add(lbr, lbi, br, bi, vb[bsz:, 2 * n:3 * n], vb[bsz:, 3 * n:4 * n])
        br2, bi2 = cmul_add(lbr, lbi, br1, bi1, vb[:bsz, 2 * n:3 * n], vb[:bsz, 3 * n:4 * n])
        xfr_ref[pl.ds(rf, 2 * bsz), :] = jnp.concatenate([fr, fr1], axis=0)
        xfi_ref[pl.ds(rf, 2 * bsz), :] = jnp.concatenate([fi, fi1], axis=0)
        xbr_ref[pl.ds(rb, 2 * bsz), :] = jnp.concatenate([br1, br], axis=0)
        xbi_ref[pl.ds(rb, 2 * bsz), :] = jnp.concatenate([bi1, bi], axis=0)
        return fr2, fi2, br2, bi2

    z = jnp.zeros((bsz, n), jnp.float32)
    lax.fori_loop(0, n_chunks // 2, step, (z, z, z, z), unroll=S5_UNROLL)
    o = o_ref[0]
    y = jnp.dot(ub, t_ref[0], preferred_element_type=jnp.float32)
    y += jnp.dot(xfr_ref[...].astype(_BF), o[0:n], preferred_element_type=jnp.float32)
    y += jnp.dot(xfi_ref[...].astype(_BF), o[n:2 * n], preferred_element_type=jnp.float32)
    y += jnp.dot(xbr_ref[...].astype(_BF), o[2 * n:3 * n], preferred_element_type=jnp.float32)
    y += jnp.dot(xbi_ref[...].astype(_BF), o[3 * n:4 * n], preferred_element_type=jnp.float32)
    y += d_ref[0] * u
    y = 0.5 * y * (1.0 + jnp.tanh(math.sqrt(2.0 / math.pi) * (y + 0.044715 * (y * y * y))))
    y0_ref[0] = y[:, :LANES]
    y1_ref[0] = y[:, LANES:]


def _to_groups_kernel(*refs, bsz):
    L, C = S5_CHUNK, S5_GROUP
    nch = RL_TBLK // L
    gpt = LANES // C
    lane = lax.broadcasted_iota(jnp.int32, (nch, LANES), 1) // C
    x_refs, o_refs = refs[:S5_W // LANES], refs[S5_W // LANES:]
    for b in range(bsz):
        for m in range(S5_W // LANES):
            src = [x_refs[m][b, pl.ds(j, nch, stride=L), :] for j in range(L)]
            for gl in range(gpt):
                g = m * gpt + gl
                for q in range(L * C // LANES):
                    acc = jnp.zeros((nch, LANES), jnp.float32)
                    for jl in range(gpt):
                        j = q * gpt + jl
                        moved = src[j] if jl == gl else pltpu.roll(src[j], ((jl - gl) * C) % LANES, axis=1)
                        acc = jnp.where(lane == jl, moved, acc)
                    o_refs[q][g, pl.ds(b, nch, stride=bsz), :] = acc


def _from_groups_kernel(y0_ref, y1_ref, o_ref, *, bsz):
    L, C = S5_CHUNK, S5_GROUP
    nch = RL_TBLK // L
    gpt = LANES // C
    lane = lax.broadcasted_iota(jnp.int32, (nch, LANES), 1) // C
    y_refs = (y0_ref, y1_ref)
    for b in range(bsz):
        for m in range(S5_W // LANES):
            for q in range(L * C // LANES):
                src = [y_refs[q][m * gpt + gl, pl.ds(b, nch, stride=bsz), :] for gl in range(gpt)]
                for jl in range(gpt):
                    acc = jnp.zeros((nch, LANES), jnp.float32)
                    for gl in range(gpt):
                        moved = src[gl] if jl == gl else pltpu.roll(src[gl], ((gl - jl) * C) % LANES, axis=1)
                        acc = jnp.where(lane == gl, moved, acc)
                    o_ref[m, b, pl.ds(q * gpt + jl, nch, stride=L), :] = acc


def _s5_pallas(p3, n_ctx, lam_re, lam_im, log_dt, b_re, b_im, c_re, c_im, d_skip):
    bsz, t_len, _ = p3.shape
    L, G, C = S5_CHUNK, S5_GROUPS, S5_GROUP
    nc = t_len // L
    assert (2 * bsz) % 8 == 0 and nc % 2 == 0 and (n_ctx // L) % 2 == 0 and t_len % RL_TBLK == 0
    assert L * C == 2 * LANES and P_S5 % LANES == 0
    tsum, icat, ocat, lam_chunk = _s5_mats(lam_re, lam_im, log_dt, b_re, b_im, c_re, c_im)
    dvec = jnp.tile(d_skip.reshape(G, 1, C), (1, L, 1)).reshape(G, 1, L * C)
    rows = nc * bsz
    n_lt = S5_W // LANES
    nch = RL_TBLK // L
    relayout_params = pltpu.CompilerParams(dimension_semantics=("parallel",), vmem_limit_bytes=VMEM_LIMIT_BYTES)
    half_shape = jax.ShapeDtypeStruct((G, rows, LANES), jnp.float32)
    half_blk = pl.BlockSpec((G, nch * bsz, LANES), lambda i: (0, i, 0))
    u0, u1 = pl.pallas_call(
        functools.partial(_to_groups_kernel, bsz=bsz),
        grid=(t_len // RL_TBLK,),
        in_specs=[pl.BlockSpec((bsz, RL_TBLK, LANES), functools.partial(lambda m, i: (0, i, P_S5 // LANES + m), m))
                  for m in range(n_lt)],
        out_specs=[half_blk, half_blk],
        out_shape=[half_shape, half_shape],
        compiler_params=relayout_params, name="s5_to_groups",
    )(*([p3] * n_lt))
    wspec = pl.BlockSpec((1, L * C, L * C), lambda g: (g, 0, 0))
    half = pl.BlockSpec((1, rows, LANES), lambda g: (g, 0, 0))
    y0, y1 = pl.pallas_call(
        functools.partial(_s5_kernel, n_chunks=nc, ctx_chunks=n_ctx // L, bsz=bsz),
        grid=(G,),
        in_specs=[half, half, wspec, wspec, wspec,
                  pl.BlockSpec((1, 4, S5_STATE), lambda g: (g, 0, 0)),
                  pl.BlockSpec((1, 1, L * C), lambda g: (g, 0, 0))],
        out_specs=[half, half],
        out_shape=[half_shape, half_shape],
        scratch_shapes=[pltpu.VMEM((rows, 4 * S5_STATE), jnp.float32)]
                       + [pltpu.VMEM((rows, S5_STATE), jnp.float32)] * 4,
        compiler_params=pltpu.CompilerParams(dimension_semantics=("parallel",),
                                             vmem_limit_bytes=VMEM_LIMIT_BYTES),
        name="s5_scan",
    )(u0, u1, tsum.astype(_BF), icat.astype(_BF), ocat.astype(_BF), lam_chunk, dvec)
    return pl.pallas_call(
        functools.partial(_from_groups_kernel, bsz=bsz),
        grid=(t_len // RL_TBLK,),
        in_specs=[half_blk, half_blk],
        out_specs=pl.BlockSpec((n_lt, bsz, RL_TBLK, LANES), lambda i: (0, 0, i, 0)),
        out_shape=jax.ShapeDtypeStruct((n_lt, bsz, t_len, LANES), jnp.float32),
        compiler_params=relayout_params, name="s5_from_groups",
    )(y0, y1)


def _log_sigmoid(x):
    return jnp.minimum(x, 0.0) - jnp.log(1.0 + jnp.exp(-jnp.abs(x)))


def _mlstm_kernel(igb_ref, fgb_ref, q_ref, k_ref, v_ref, gc_ref, gr_ref, h_ref, cmat_ref, nvec_ref, m_ref):
    d = pl.program_id(0)
    L = ML_CHUNK
    n_sub = ML_TBLK // L
    H = range(ML_HEADS)

    @pl.when(pl.program_id(2) == 0)
    def _():
        cmat_ref[...] = jnp.zeros_like(cmat_ref)
        nvec_ref[...] = jnp.zeros_like(nvec_ref)
        m_ref[...] = jnp.zeros_like(m_ref)

    igb = [igb_ref[d, h] for h in H]
    fgb = [fgb_ref[d, h] for h in H]
    row = lax.broadcasted_iota(jnp.int32, (L, L), 0)
    col = lax.broadcasted_iota(jnp.int32, (L, L), 1)
    sign = 1 - 2 * d
    seen = (row - col) * sign >= 0
    seen_f = seen.astype(jnp.float32)
    seen_t = ((col - row) * sign >= 0).astype(jnp.float32)
    scale = ML_DH ** -0.5

    def chunk(jj, carry):
        cj = jnp.where(d == 0, jj, n_sub - 1 - jj)
        r0 = pl.multiple_of(cj * L, L)
        hs = lambda h: slice(h * ML_DH, (h + 1) * ML_DH)
        q = [q_ref[0, pl.ds(r0, L), hs(h)] for h in H]
        k = [k_ref[0, pl.ds(r0, L), hs(h)] * scale for h in H]
        vb = [v_ref[0, pl.ds(r0, L), hs(h)].astype(_BF) for h in H]
        gc = [gc_ref[0, 0, h, pl.ds(r0, L), :] for h in H]
        gr = [gr_ref[0, 0, h, cj] for h in H]
        li_col = [gc[h][:, 0:1] + igb[h] for h in H]
        lf_col = [_log_sigmoid(gc[h][:, 1:2] + fgb[h]) for h in H]
        li_row = [gr[h][0:1, :] + igb[h] for h in H]
        lf_row = [_log_sigmoid(gr[h][1:2, :] + fgb[h]) for h in H]
        m_prev = [m_ref[h] for h in H]
        bcum_col = [jnp.dot(seen_f, jnp.broadcast_to(lf_col[h], (L, L)), precision=_HI,
                            preferred_element_type=jnp.float32) for h in H]
        bcum_row = [jnp.dot(jnp.broadcast_to(lf_row[h], (8, L)), seen_t, precision=_HI,
                            preferred_element_type=jnp.float32)[0:1] for h in H]
        qb = [q[h].astype(_BF) for h in H]
        qk = [lax.dot_general(qb[h], k[h].astype(_BF), _NT, preferred_element_type=jnp.float32) for h in H]
        qc = [jnp.dot(qb[h], cmat_ref[h].astype(_BF), preferred_element_type=jnp.float32) for h in H]
        log_d = [jnp.where(seen, bcum_col[h] - bcum_row[h] + li_row[h], -jnp.inf) for h in H]
        inter = [bcum_col[h][:, 0:1] + m_prev[h] for h in H]
        m_j = [jnp.maximum(jnp.max(log_d[h], axis=1, keepdims=True), inter[h]) for h in H]
        scores = [qk[h] * jnp.exp(log_d[h] - m_j[h]) for h in H]
        s_inter = [jnp.exp(inter[h] - m_j[h]) for h in H]
        sv = [jnp.dot(scores[h].astype(_BF), vb[h], preferred_element_type=jnp.float32) for h in H]
        b_last = [jnp.sum(lf_col[h], axis=0, keepdims=True) for h in H]
        log_w = [b_last[h] - bcum_col[h][:, 0:1] + li_col[h] for h in H]
        m_new = [jnp.maximum(b_last[h] + m_prev[h], jnp.max(log_w[h], axis=0, keepdims=True)) for h in H]
        kw = [k[h] * jnp.exp(log_w[h] - m_new[h]) for h in H]
        decay = [jnp.exp(b_last[h] + m_prev[h] - m_new[h]) for h in H]
        kv = [lax.dot_general(kw[h].astype(_BF), vb[h], _TN, preferred_element_type=jnp.float32) for h in H]
        for h in H:
            num = sv[h] + s_inter[h] * qc[h]
            den = (jnp.sum(scores[h], axis=1, keepdims=True)
                   + s_inter[h] * jnp.sum(q[h] * nvec_ref[h], axis=1, keepdims=True))
            h_ref[0, 0, pl.ds(r0, L), hs(h)] = num / jnp.maximum(jnp.abs(den), jnp.exp(-m_j[h]))
        for h in H:
            cmat_ref[h] = decay[h] * cmat_ref[h] + kv[h]
            nvec_ref[h] = decay[h] * nvec_ref[h] + jnp.sum(kw[h], axis=0, keepdims=True)
            m_ref[h] = m_new[h]
        return carry

    lax.fori_loop(0, n_sub, chunk, 0)


def _mlstm_pallas(qk, p3, gates, ig_b, fg_b, n_ctx):
    bsz, t_len, _ = qk.shape
    nb = t_len // ML_TBLK
    cb = n_ctx // ML_TBLK
    assert t_len % ML_TBLK == 0 and n_ctx % ML_TBLK == 0
    g = gates.reshape(bsz, t_len, 2, N_DIR, ML_HEADS)
    gcol = jnp.transpose(g, (3, 0, 4, 1, 2))
    grow = jnp.transpose(g.reshape(bsz, t_len // ML_CHUNK, ML_CHUNK, 2, N_DIR, ML_HEADS),
                         (4, 0, 5, 1, 3, 2))
    blk = lambda d, i: _scan_block(d, i, nb, cb)
    grid_spec = pltpu.PrefetchScalarGridSpec(
        num_scalar_prefetch=2,
        grid=(N_DIR, bsz, nb),
        in_specs=[pl.BlockSpec((1, ML_TBLK, ML_W), lambda d, b, i, *_: (b, blk(d, i), 0)),
                  pl.BlockSpec((1, ML_TBLK, ML_W), lambda d, b, i, *_: (b, blk(d, i), 1)),
                  pl.BlockSpec((1, ML_TBLK, ML_W), lambda d, b, i, *_: (b, blk(d, i), P_V // ML_W))] + [
            pl.BlockSpec((1, 1, ML_HEADS, ML_TBLK, 2), lambda d, b, i, *_: (d, b, 0, blk(d, i), 0)),
            pl.BlockSpec((1, 1, ML_HEADS, ML_TBLK // ML_CHUNK, 2, ML_CHUNK),
                         lambda d, b, i, *_: (d, b, 0, blk(d, i), 0, 0))],
        out_specs=pl.BlockSpec((1, 1, ML_TBLK, ML_W), lambda d, b, i, *_: (d, b, blk(d, i), 0)),
        scratch_shapes=[pltpu.VMEM((ML_HEADS, ML_DH, ML_DH), jnp.float32),
                        pltpu.VMEM((ML_HEADS, 1, ML_DH), jnp.float32),
                        pltpu.VMEM((ML_HEADS, 1, 1), jnp.float32)])
    return pl.pallas_call(
        _mlstm_kernel, grid_spec=grid_spec,
        out_shape=jax.ShapeDtypeStruct((N_DIR, bsz, t_len, ML_W), jnp.float32),
        compiler_params=pltpu.CompilerParams(
            dimension_semantics=("parallel", "parallel", "arbitrary"),
            vmem_limit_bytes=VMEM_LIMIT_BYTES),
        name="mlstm_scan",
    )(ig_b, fg_b, qk, qk, p3, gcol, grow)


def _to_pairs(x):
    return jnp.concatenate([x[:, p * LANES:(p + 1) * LANES] for p in range(RW_PAIRS)], axis=0)


def _dotf(a, b, dims=None):
    a = a.astype(_BF)
    b = b.astype(_BF)
    if dims is None:
        return jnp.dot(a, b, preferred_element_type=jnp.float32)
    return lax.dot_general(a, b, dims, preferred_element_type=jnp.float32)


def _rwkv_a_kernel(r_ref, k_ref, v_ref, kk_ref, a_ref, lw_ref,
                   att_ref, rt_ref, bw_ref, kw_ref, vt_ref, u0t_ref, y0_ref, wc_ref, *, n_sub):
    d = pl.program_id(0)
    C, R = RW_CHUNK, RW_ROWS
    row = lax.broadcasted_iota(jnp.int32, (R, R), 0)
    col = lax.broadcasted_iota(jnp.int32, (R, R), 1)
    same = (row // C) == (col // C)
    sign = 1 - 2 * d
    before = same & ((row - col) * sign > 0)
    upto = same & ((row - col) * sign >= 0)
    upto2 = jnp.concatenate([upto, upto], axis=1)
    eye = (row == col).astype(jnp.float32)
    first = col < RW_DH
    tpos = row % C

    def chunk_group(jg, carry):
        js = [jg * RW_GROUP + i for i in range(RW_GROUP)]
        G2 = [(i, h2) for i in range(RW_GROUP) for h2 in range(2)]
        pick = lambda lst, off: [jnp.where(first, lst[2 * i][:, off:off + R], lst[2 * i + 1][:, off:off + R])
                                 for i in range(RW_GROUP)]
        r0s = [pl.multiple_of(j * C, C) for j in js]
        ld = lambda ref: [_to_pairs(ref[0, pl.ds(r0, C), :]) for r0 in r0s]
        ldd = lambda ref: [_to_pairs(ref[0, 0, pl.ds(r0, C), :]) for r0 in r0s]
        r, v, kk = ld(r_ref), ld(v_ref), ld(kk_ref)
        k, a, lw = ldd(k_ref), ldd(a_ref), ldd(lw_ref)
        fwd = d == 0
        cum, aft = [], []
        for i in range(RW_GROUP):
            pre = lw[i]
            suf = lw[i]
            for s in (1, 2, 4, 8):
                pre = pre + jnp.where(tpos >= s, pltpu.roll(pre, s, axis=0), 0.0)
                suf = suf + jnp.where(tpos < C - s, pltpu.roll(suf, R - s, axis=0), 0.0)
            cum.append(jnp.where(fwd, pre, suf))
            aft.append(jnp.where(fwd, suf, pre) - lw[i])
        a_hat = [-kk[i] * jnp.exp(cum[i] - lw[i]) for i in range(RW_GROUP)]
        r_hat = [r[i] * jnp.exp(cum[i]) for i in range(RW_GROUP)]
        vb = [v[i].astype(_BF) for i in range(RW_GROUP)]
        m = []
        for i in range(RW_GROUP):
            inv_w = jnp.exp(-cum[i])
            lhs = jnp.concatenate([jnp.where(first, a_hat[i], 0.0), jnp.where(first, 0.0, a_hat[i]),
                                   jnp.where(first, r_hat[i], 0.0), jnp.where(first, 0.0, r_hat[i])], axis=0)
            rhs = jnp.concatenate([kk[i] * a[i] * inv_w, k[i] * inv_w], axis=0)
            m.append(_dotf(lhs, rhs, _NT))
        x = [jnp.where(before, m[i][h2 * R:(h2 + 1) * R, 0:R], 0.0).astype(_BF) for i, h2 in G2]
        ak = [jnp.where(before, m[i][h2 * R:(h2 + 1) * R, R:2 * R], 0.0) for i, h2 in G2]
        rbk = [jnp.where(upto2, m[i][(2 + h2) * R:(3 + h2) * R, :], 0.0).astype(_BF) for i, h2 in G2]
        akv = [_dotf(ak[g], vb[g // 2]) for g in range(len(G2))]
        x2 = [_dotf(xx, xx).astype(_BF) for xx in x]
        x4 = [_dotf(xx, xx).astype(_BF) for xx in x2]
        x8 = [_dotf(xx, xx).astype(_BF) for xx in x4]
        t = [eye + xx.astype(jnp.float32) for xx in x]
        t = [t[g] + _dotf(t[g], x2[g]) for g in range(len(G2))]
        t = [t[g] + _dotf(t[g], x4[g]) for g in range(len(G2))]
        t = [t[g] + _dotf(t[g], x8[g]) for g in range(len(G2))]
        akv = pick(akv, 0)
        rhs2 = [jnp.concatenate([a_hat[i], akv[i]], axis=1).astype(_BF) for i in range(RW_GROUP)]
        ta = [_dotf(t[g], rhs2[g // 2]) for g in range(len(G2))]
        at = pick(ta, 0)
        u0 = pick(ta, R)
        rhs3 = [jnp.concatenate([jnp.concatenate([at[i], u0[i]], axis=1).astype(_BF),
                                 jnp.concatenate([jnp.zeros_like(vb[i]), vb[i]], axis=1)], axis=0)
                for i in range(RW_GROUP)]
        ry = [_dotf(rbk[g], rhs3[g // 2]) for g in range(len(G2))]
        rt = pick(ry, 0)
        y0 = pick(ry, R)
        for i, j in enumerate(js):
            w_aft = jnp.exp(aft[i])
            att_ref[0, 0, j] = at[i].T.astype(_BF)
            u0t_ref[0, 0, j] = u0[i].T
            vt_ref[0, 0, j] = v[i].T.astype(_BF)
            rt_ref[0, 0, j] = (r_hat[i] + rt[i]).astype(_BF)
            bw_ref[0, 0, j] = (kk[i] * a[i] * w_aft).astype(_BF)
            kw_ref[0, 0, j] = (k[i] * w_aft).astype(_BF)
            y0_ref[0, 0, j] = y0[i]
            tot = cum[i] + aft[i]
            wc_ref[0, 0, j] = jnp.exp(jnp.concatenate([tot[p * C:p * C + 1] for p in range(RW_PAIRS)], axis=0))
        return carry

    lax.fori_loop(0, n_sub // RW_GROUP, chunk_group, 0)


def _rwkv_b_kernel(*refs, n_sub):
    ins, (yf_ref, yb_ref, s_ref) = refs[:16], refs[16:]
    C = RW_CHUNK

    @pl.when(pl.program_id(1) == 0)
    def _():
        s_ref[...] = jnp.zeros_like(s_ref)

    row = lax.broadcasted_iota(jnp.int32, (LANES, LANES), 0)
    col = lax.broadcasted_iota(jnp.int32, (LANES, LANES), 1)
    diag = (row // RW_DH) == (col // RW_DH)
    pair_of_row = lax.broadcasted_iota(jnp.int32, (2 * RW_ROWS, LANES), 0) % RW_ROWS // C
    DP = [(d, p) for d in range(N_DIR) for p in range(RW_PAIRS)]

    def chunk(jj, carry):
        cjs = [jj, n_sub - 1 - jj]
        att, rt, bw, kw, vt, u0t, y0, wc = [[ins[8 * d + a][0, 0, cjs[d]] for d in range(N_DIR)] for a in range(8)]
        bk = [jnp.concatenate([bw[d], kw[d]], axis=0) for d in range(N_DIR)]
        sps = [s_ref[d, p] for d, p in DP]
        spb = [sp.astype(_BF) for sp in sps]
        uts = [jnp.dot(spb[i], att[d], preferred_element_type=jnp.float32) + u0t[d] for i, (d, p) in enumerate(DP)]
        for i, (d, p) in enumerate(DP):
            rows = slice(p * C, (p + 1) * C)
            y = lax.dot_general(rt[d][rows], spb[i], _NT, preferred_element_type=jnp.float32) + y0[d][rows]
            y_ref = yf_ref if d == 0 else yb_ref
            y_ref[0, pl.ds(pl.multiple_of(cjs[d] * C, C), C), p * LANES:(p + 1) * LANES] = y
        for i, (d, p) in enumerate(DP):
            lhs = jnp.concatenate([uts[i].astype(_BF), vt[d]], axis=1)
            rhs = jnp.where(pair_of_row == p, bk[d], jnp.zeros_like(bk[d]))
            upd = jnp.dot(lhs, rhs, preferred_element_type=jnp.float32)
            s_ref[d, p] = jnp.where(diag, wc[d][p:p + 1, :] * sps[i] + upd, 0.0)
        return carry

    lax.fori_loop(0, n_sub, chunk, 0)


def _rwkv_pallas(r, v, kk, k_dir, a_dir, lw_dir, n_ctx):
    bsz, t_len, _ = r.shape
    nb, cb = t_len // RW_TBLK, n_ctx // RW_TBLK
    assert t_len % RW_TBLK == 0 and n_ctx % RW_TBLK == 0
    n_sub = RW_TBLK // RW_CHUNK
    nc = t_len // RW_CHUNK
    sh_spec = pl.BlockSpec((1, RW_TBLK, RW_W), lambda d, b, i: (b, _scan_block(d, i, nb, cb), 0))
    dr_spec = pl.BlockSpec((1, 1, RW_TBLK, RW_W), lambda d, b, i: (d, b, _scan_block(d, i, nb, cb), 0))
    ch_spec = pl.BlockSpec((1, 1, n_sub, RW_ROWS, LANES), lambda d, b, i: (d, b, _scan_block(d, i, nb, cb), 0, 0))
    wc_spec = pl.BlockSpec((1, 1, n_sub, RW_PAIRS, LANES), lambda d, b, i: (d, b, _scan_block(d, i, nb, cb), 0, 0))
    ch_shape = lambda dt: jax.ShapeDtypeStruct((N_DIR, bsz, nc, RW_ROWS, LANES), dt)
    params = pltpu.CompilerParams(dimension_semantics=("parallel", "parallel", "arbitrary"),
                                  vmem_limit_bytes=VMEM_LIMIT_BYTES)
    chunk_local = pl.pallas_call(
        functools.partial(_rwkv_a_kernel, n_sub=n_sub),
        grid=(N_DIR, bsz, nb),
        in_specs=[sh_spec, dr_spec, sh_spec, sh_spec, dr_spec, dr_spec],
        out_specs=[ch_spec] * 7 + [wc_spec],
        out_shape=[ch_shape(_BF)] * 5 + [ch_shape(jnp.float32)] * 2
                  + [jax.ShapeDtypeStruct((N_DIR, bsz, nc, RW_PAIRS, LANES), jnp.float32)],
        compiler_params=params, name="rwkv_chunk_local",
    )(r, k_dir, v, kk, a_dir, lw_dir)
    dir_specs = []
    for d in range(N_DIR):
        blk = functools.partial(lambda d, b, i: (d, b, _scan_block(d, i, nb, cb), 0, 0), d)
        dir_specs += [pl.BlockSpec((1, 1, n_sub, RW_ROWS, LANES), blk)] * 7
        dir_specs += [pl.BlockSpec((1, 1, n_sub, RW_PAIRS, LANES), blk)]
    y_specs = [pl.BlockSpec((1, RW_TBLK, RW_W),
                            functools.partial(lambda d, b, i: (b, _scan_block(d, i, nb, cb), 0), d))
               for d in range(N_DIR)]
    return pl.pallas_call(
        functools.partial(_rwkv_b_kernel, n_sub=n_sub),
        grid=(bsz, nb),
        in_specs=dir_specs,
        out_specs=y_specs,
        out_shape=[jax.ShapeDtypeStruct((bsz, t_len, RW_W), jnp.float32)] * N_DIR,
        scratch_shapes=[pltpu.VMEM((N_DIR, RW_PAIRS, LANES, LANES), jnp.float32)],
        compiler_params=pltpu.CompilerParams(dimension_semantics=("parallel", "arbitrary"),
                                             vmem_limit_bytes=VMEM_LIMIT_BYTES),
        name="rwkv_state_scan",
    )(*chunk_local, *chunk_local)


def _softplus(x):
    return jnp.maximum(x, 0.0) + jnp.log(1.0 + jnp.exp(-jnp.abs(x)))


def _rwkv_prep_kernel(*refs, blocks_per_seq, ctx_blocks):
    (zr, zk, zv, zw, za, zg, pr, pk, pv, pw, pa, pg, nr, nk, nv, nw, na, ng,
     mu_ref, w0_ref, a0_ref, wup_ref, aup_ref, gup_ref, kkw_ref, ka_ref, rk_ref, ones_ref,
     r_out, v_out, kk_out, kdir_out, a_out, lw_out, bonus_out, g_out) = refs
    blk = pl.program_id(0) % blocks_per_seq
    has_prev = (blk != 0) & (blk != ctx_blocks)
    has_next = (blk != ctx_blocks - 1) & (blk != blocks_per_seq - 1)
    tm = zr.shape[0]

    def lerp(z_ref, p_ref, n_ref, mu):
        z = z_ref[...]
        rows = lax.broadcasted_iota(jnp.int32, z.shape, 0)
        prev_row = jnp.where(has_prev, p_ref[RW_HALO - 1:RW_HALO, :], 0.0)
        next_row = jnp.where(has_next, n_ref[0:1, :], 0.0)
        before = jnp.where(rows == 0, prev_row, pltpu.roll(z, 1, axis=0))
        after = jnp.where(rows == tm - 1, next_row, pltpu.roll(z, tm - 1, axis=0))
        return z + mu * (0.5 * (before + after) - z)

    mu = mu_ref[...]
    r = lerp(zr, pr, nr, mu[:, 0:RW_W])
    k = lerp(zk, pk, nk, mu[:, RW_W:2 * RW_W])
    v = lerp(zv, pv, nv, mu[:, 2 * RW_W:3 * RW_W])
    wd = lerp(zw, pw, nw, mu[:, 3 * RW_W:3 * RW_W + LANES])
    ad = lerp(za, pa, na, mu[:, 3 * RW_W + LANES:3 * RW_W + 2 * LANES])
    gd = lerp(zg, pg, ng, mu[:, 3 * RW_W + 2 * LANES:3 * RW_W + 3 * LANES])
    dot = lambda a, w_ref: jnp.dot(a.astype(_BF), w_ref[...], preferred_element_type=jnp.float32)
    w_pre = dot(jnp.tanh(wd), wup_ref)
    a_pre = dot(ad, aup_ref)
    g_out[...] = dot(jax.nn.sigmoid(gd), gup_ref)

    def head_sum(x):
        hi = x.astype(_BF)
        lo = (x - hi.astype(jnp.float32)).astype(_BF)
        return (jnp.dot(hi, ones_ref[...], preferred_element_type=jnp.float32)
                + jnp.dot(lo, ones_ref[...], preferred_element_type=jnp.float32))

    kk = k * kkw_ref[...]
    kk_out[...] = kk / jnp.maximum(jnp.sqrt(head_sum(kk * kk)), 1e-12)
    r_out[...] = r
    v_out[...] = v
    rk = r * rk_ref[...]
    bonus = jnp.zeros_like(r)
    for d in range(N_DIR):
        cols = slice(d * RW_W, (d + 1) * RW_W)
        lw_out[d] = -jnp.exp(-_softplus(-(w0_ref[d:d + 1, :] + w_pre[:, cols])) - RW_DECAY_OFFSET)
        a = jax.nn.sigmoid(a0_ref[d:d + 1, :] + a_pre[:, cols])
        a_out[d] = a
        k_dir = k * (1.0 + (a - 1.0) * ka_ref[...])
        kdir_out[d] = k_dir
        bonus = bonus + rk * k_dir
    bonus_out[...] = head_sum(bonus) * v


def _rwkv_prep(p2, blocks_per_seq, ctx_blocks, mu, w0, w_up, a0, a_up, g_up, k_k, k_a, r_k):
    n = p2.shape[0]
    tm = ROW_TM
    hb = tm // RW_HALO
    n_hb = n // RW_HALO
    cw, cl = P_RW // RW_W, P_RWLR // LANES
    assert P_RW % RW_W == 0 and P_RWLR % LANES == 0 and N_DIR * RW_DECAY_RANK == LANES and RW_G_RANK == LANES

    def specs(rows, row_index):
        wide = [pl.BlockSpec((rows, RW_W), functools.partial(lambda j, i: (row_index(i), cw + j), j))
                for j in range(3)]
        return wide + [pl.BlockSpec((rows, LANES), functools.partial(lambda j, i: (row_index(i), cl + j), j))
                       for j in range(3)]

    main = specs(tm, lambda i: i)
    prev = specs(RW_HALO, lambda i: jnp.maximum(i * hb - 1, 0))
    nxt = specs(RW_HALO, lambda i: jnp.minimum((i + 1) * hb, n_hb - 1))
    zero = jnp.zeros((RW_DECAY_RANK, RW_W), jnp.float32)
    both_dirs = lambda up: jnp.concatenate([jnp.concatenate([up[0], zero], axis=1),
                                            jnp.concatenate([zero, up[1]], axis=1)], axis=0).astype(_BF)
    head = jnp.arange(RW_W) // RW_DH
    ones_bd = (head[:, None] == head[None, :]).astype(_BF)
    vec = lambda w: pl.BlockSpec((1, w), lambda i: (0, 0))
    dvec = pl.BlockSpec((N_DIR, RW_W), lambda i: (0, 0))
    consts = [vec(mu.shape[0]), dvec, dvec, _const_spec((LANES, N_DIR * RW_W)), _const_spec((LANES, N_DIR * RW_W)),
              _const_spec((RW_G_RANK, RW_W)), vec(RW_W), vec(RW_W), vec(RW_W), _const_spec((RW_W, RW_W))]
    row = pl.BlockSpec((tm, RW_W), lambda i: (i, 0))
    drow = pl.BlockSpec((N_DIR, tm, RW_W), lambda i: (0, i, 0))
    sh = jax.ShapeDtypeStruct((n, RW_W), jnp.float32)
    dsh = jax.ShapeDtypeStruct((N_DIR, n, RW_W), jnp.float32)
    return pl.pallas_call(
        functools.partial(_rwkv_prep_kernel, blocks_per_seq=blocks_per_seq, ctx_blocks=ctx_blocks),
        grid=(n // tm,),
        in_specs=main + prev + nxt + consts,
        out_specs=[row, row, row, drow, drow, drow, row, row],
        out_shape=[sh, sh, sh, dsh, dsh, dsh, sh, sh],
        compiler_params=pltpu.CompilerParams(dimension_semantics=("parallel",), vmem_limit_bytes=VMEM_LIMIT_BYTES),
        name="rwkv_prep",
    )(*([p2] * 18), mu.reshape(1, -1), w0, a0, both_dirs(w_up), both_dirs(a_up), g_up.astype(_BF),
      k_k.reshape(1, -1), k_a.reshape(1, -1), r_k.reshape(1, -1), ones_bd)


def _layer_norm(x, eps=LN_EPS):
    mu = jnp.mean(x, axis=-1, keepdims=True)
    var = jnp.mean(jnp.square(x - mu), axis=-1, keepdims=True)
    return (x - mu) * lax.rsqrt(var + eps)


def _modulate(x, shift, scale):
    return _layer_norm(x) * (1.0 + scale) + shift


def _conv_silu_kernel(x_ref, prev_ref, next_ref, w_ref, b_ref, o_ref, *, blocks_per_seq, ctx_blocks):
    blk = pl.program_id(0) % blocks_per_seq
    is_lat = blk >= ctx_blocks
    has_prev = is_lat & (blk != ctx_blocks)
    has_next = is_lat & (blk != blocks_per_seq - 1)
    tm, width = x_ref.shape
    ext = jnp.concatenate([jnp.where(has_prev, prev_ref[...], 0.0), x_ref[...],
                           jnp.where(has_next, next_ref[...], 0.0)], axis=0)
    n_ext = ext.shape[0]
    t = lax.broadcasted_iota(jnp.int32, (tm, 1), 0)
    col = jnp.where(is_lat, t % GRID_W, t)
    last_col = jnp.where(is_lat, GRID_W - 1, tm - 1)
    w = w_ref[...]
    acc = jnp.zeros((tm, width), jnp.float32) + b_ref[...]
    for dc in (-1, 0, 1):
        shifted = ext if dc == 0 else pltpu.roll(ext, (-dc) % n_ext, axis=0)
        col_ok = (col != 0) if dc == -1 else ((col != last_col) if dc == 1 else None)
        for dr in (-1, 0, 1):
            src = shifted[GRID_W + dr * GRID_W:GRID_W + dr * GRID_W + tm]
            ok = col_ok if dr == 0 else (is_lat if col_ok is None else (col_ok & is_lat))
            if ok is not None:
                src = jnp.where(ok, src, 0.0)
            tap = (dr + 1) * 3 + (dc + 1)
            acc = acc + src * w[tap:tap + 1, :]
    o_ref[...] = acc * jax.nn.sigmoid(acc)


def _conv_silu(p2, blocks_per_seq, ctx_blocks, conv_w, conv_b):
    n = p2.shape[0]
    tm = ROW_TM
    width = 2 * ML_W
    assert ctx_blocks == 1 and tm % GRID_W == 0 and P_QK % width == 0
    hb = tm // GRID_W
    n_hb = n // GRID_W
    cb = P_QK // width
    return pl.pallas_call(
        functools.partial(_conv_silu_kernel, blocks_per_seq=blocks_per_seq, ctx_blocks=ctx_blocks),
        grid=(n // tm,),
        in_specs=[pl.BlockSpec((tm, width), lambda i: (i, cb)),
                  pl.BlockSpec((GRID_W, width), lambda i: (jnp.maximum(i * hb - 1, 0), cb)),
                  pl.BlockSpec((GRID_W, width), lambda i: (jnp.minimum((i + 1) * hb, n_hb - 1), cb)),
                  pl.BlockSpec((9, width), lambda i: (0, 0)), pl.BlockSpec((1, width), lambda i: (0, 0))],
        out_specs=pl.BlockSpec((tm, width), lambda i: (i, 0)),
        out_shape=jax.ShapeDtypeStruct((n, width), jnp.float32),
        compiler_params=pltpu.CompilerParams(dimension_semantics=("parallel",), vmem_limit_bytes=VMEM_LIMIT_BYTES),
        name="mlstm_conv_silu",
    )(p2, p2, p2, conv_w.reshape(9, width), conv_b.reshape(1, width))


def _rwkv7_branch(p2, bsz, n_ctx, mu, w0, w_up, a0, a_up, g_up, k_k, k_a, r_k):
    t_len = p2.shape[0] // bsz
    r, v, kk, k_dir, a, log_decay, bonus, g = _rwkv_prep(p2, t_len // ROW_TM, n_ctx // ROW_TM, mu, w0, w_up, a0,
                                                         a_up, g_up, k_k, k_a, r_k)
    seq = lambda z: z.reshape(z.shape[:-2] + (bsz, t_len, RW_W))
    y_f, y_b = _rwkv_pallas(seq(r), seq(v), seq(kk), seq(k_dir), seq(a), seq(log_decay), n_ctx)
    return y_f.reshape(-1, RW_W), y_b.reshape(-1, RW_W), bonus, g


def _token_mixer(u, n_ctx, w_in,
                 ml_conv_w, ml_conv_b, ml_ig_b, ml_fg_b, ml_norm_g, ml_norm_b, ml_proj,
                 rw_mu, rw_w0, rw_w_up, rw_a0, rw_a_up, rw_g_up, rw_k_k, rw_k_a, rw_r_k,
                 rw_norm_g, rw_norm_b, rw_proj,
                 s5_lam_re, s5_lam_im, s5_log_dt, s5_b_re, s5_b_im, s5_c_re, s5_c_im, s5_d,
                 s5_w_val, s5_w_gate):
    bsz, t_len, _ = u.shape
    n = bsz * t_len
    p = _mm_any(u, _permute_w_in(w_in), tm=PROJ_TM, keep_col_pad=True)

    def col(start, width):
        return p[..., start:start + width]

    n_gate = N_DIR * ML_HEADS
    p2 = p.reshape(n, -1)
    qk = _conv_silu(p2, t_len // ROW_TM, n_ctx // ROW_TM, ml_conv_w, ml_conv_b)
    h_dir = _mlstm_pallas(qk.reshape(bsz, t_len, 2 * ML_W), p, col(P_MLG, 2 * n_gate), ml_ig_b, ml_fg_b, n_ctx)
    y_f, y_b, bonus, g = _rwkv7_branch(p2, bsz, n_ctx, rw_mu, rw_w0, rw_w_up, rw_a0, rw_a_up, rw_g_up,
                                       rw_k_k, rw_k_a, rw_r_k)
    s5 = _s5_pallas(p, n_ctx, s5_lam_re, s5_lam_im, s5_log_dt, s5_b_re, s5_b_im, s5_c_re, s5_c_im, s5_d)
    ml, rw = _post_scan(p2, h_dir.reshape(N_DIR, n, ML_W), y_f, y_b, bonus, g,
                        ml_norm_g, ml_norm_b, rw_norm_g, rw_norm_b)
    return _merge_pallas(p2, ml, rw, s5.reshape(S5_W // LANES, n, LANES),
                         ml_proj.astype(_BF), rw_proj.astype(_BF), s5_w_val.astype(_BF), s5_w_gate.astype(_BF))


def _permute_w_in(w_in):
    offs = np.cumsum((0,) + IN_WIDTHS)
    seg = lambda a, b: w_in[:, offs[a]:offs[b]]
    return jnp.concatenate([seg(0, 4), seg(6, 9), seg(12, 13), seg(13, 14), seg(9, 12), seg(4, 6)], axis=1)


def _moe_ffn(u, router_w, router_b, layer, w_gate, w_up, w_down):
    gates_t = _router_pallas(u, router_w, router_b)
    return _moe_routed(u, gates_t, layer, w_gate, w_up, w_down)


def kernel(x, c, ctx, c_ctx, ada_w, ada_b, w_in, ml_conv_w, ml_conv_b, ml_ig_b, ml_fg_b, ml_norm_g,
           ml_norm_b, ml_proj, rw_mu, rw_w0, rw_w_up, rw_a0, rw_a_up, rw_g_up, rw_k_k, rw_k_a, rw_r_k,
           rw_norm_g, rw_norm_b, rw_proj, s5_lam_re, s5_lam_im, s5_log_dt, s5_b_re, s5_b_im, s5_c_re,
           s5_c_im, s5_d, s5_w_val, s5_w_gate, w_out, ln1_g, ln1_b, ln2_g, ln2_b, router_w, router_b,
           exp_w_gate, exp_w_up, exp_w_down):
    bsz, n_ctx = ctx.shape[0], ctx.shape[1]
    t_len = n_ctx + x.shape[1]
    assert n_ctx % ROW_TM == 0 and t_len % ROW_TM == 0
    blocks_per_seq, ctx_blocks = t_len // ROW_TM, n_ctx // ROW_TM
    silu_c = jax.nn.silu(c)
    silu_cc = jax.nn.silu(c_ctx)[None, :]
    mods = []
    for i in range(DEPTH):
        mx = _mm_any(silu_c, ada_w[i]) + ada_b[i]
        mc = jnp.broadcast_to(_mm_any(silu_cc, ada_w[i]) + ada_b[i], mx.shape)
        mods.append(jnp.stack([mc, mx], axis=1).reshape(bsz, 2, N_MOD, 1, D_MODEL))
    xa = jnp.concatenate([ctx, x], axis=1).reshape(bsz * t_len, D_MODEL)
    m0 = mods[0]
    u = jnp.concatenate([_modulate(ctx, m0[:, 0, 0], m0[:, 0, 1]), _modulate(x, m0[:, 1, 0], m0[:, 1, 1])],
                        axis=1).astype(_BF)
    for i in range(DEPTH):
        z = _token_mixer(
            u, n_ctx, w_in[i],
            ml_conv_w[i], ml_conv_b[i], ml_ig_b[i], ml_fg_b[i], ml_norm_g[i], ml_norm_b[i], ml_proj[i],
            rw_mu[i], rw_w0[i], rw_w_up[i], rw_a0[i], rw_a_up[i], rw_g_up[i], rw_k_k[i], rw_k_a[i], rw_r_k[i],
            rw_norm_g[i], rw_norm_b[i], rw_proj[i],
            s5_lam_re[i], s5_lam_im[i], s5_log_dt[i], s5_b_re[i], s5_b_im[i], s5_c_re[i], s5_c_im[i], s5_d[i],
            s5_w_val[i], s5_w_gate[i])
        xa, u_ffn = _resid_norm_mod(z, w_out[i].astype(_BF), xa, mods[i], 2, ln1_g[i], ln1_b[i], mods[i], 3, 4,
                                    blocks_per_seq, ctx_blocks)
        ffn = _moe_ffn(u_ffn, router_w, router_b, i, exp_w_gate, exp_w_up, exp_w_down)
        xa, u = _resid_norm_mod(ffn, None, xa, mods[i], 5, ln2_g[i], ln2_b[i], mods[min(i + 1, DEPTH - 1)], 0, 1,
                                blocks_per_seq, ctx_blocks, u_dtype=_BF)
        u = u.reshape(bsz, t_len, D_MODEL)
    return xa.reshape(bsz, t_len, D_MODEL)[:, n_ctx:]
```

```python
import functools
import math

import jax
import jax.numpy as jnp
import numpy as np
from jax import lax
from jax.experimental import pallas as pl
from jax.experimental.pallas import tpu as pltpu

D_MODEL = 2048
DEPTH = 2
GRID_W = 64
N_DIR = 2
ML_HEADS = 4
ML_DH = 256
ML_W = ML_HEADS * ML_DH
ML_CHUNK = 64
ML_NORM_EPS = 1e-6
RW_HEADS = 16
RW_DH = 64
RW_W = RW_HEADS * RW_DH
RW_DECAY_RANK = 64
RW_A_RANK = 64
RW_G_RANK = 128
RW_DECAY_OFFSET = 0.5
RW_NORM_EPS = 64e-5
S5_W = 1024
S5_GROUP = 16
S5_GROUPS = S5_W // S5_GROUP
S5_STATE = 64
N_BRANCH = 3
N_GROUPS = 4
EXPERTS_PER_GROUP = 4
N_EXPERTS = N_GROUPS * EXPERTS_PER_GROUP
TOP_K = 2
D_EXPERT = 1024
DEEPNORM_ALPHA = (2.0 * DEPTH) ** 0.25
LN_EPS = 1e-5
N_MOD = 6
IN_WIDTHS = (ML_W, ML_W, ML_W, ML_W, N_DIR * ML_HEADS, N_DIR * ML_HEADS,
             RW_W, RW_W, RW_W, N_DIR * RW_DECAY_RANK, N_DIR * RW_A_RANK, RW_G_RANK,
             S5_W, N_BRANCH * D_MODEL)
D_IN = sum(IN_WIDTHS)
RW_IN_WIDTHS = (RW_W, RW_W, RW_W, N_DIR * RW_DECAY_RANK, N_DIR * RW_A_RANK, RW_G_RANK)

VMEM_LIMIT_BYTES = 56 * 1024 * 1024
LANES = 128

S5_CHUNK = 16
S5_UNROLL = 8
RL_TBLK = 256
PROJ_TM = 1536
ML_TBLK = 256
RW_CHUNK = 16
RW_PAIRS = RW_HEADS // 2
RW_ROWS = RW_PAIRS * RW_CHUNK
RW_TBLK = 256
RW_GROUP = 8
RW_HALO = 8
MOE_TM = 256
ROW_TM = 256

P_QK, P_V, P_O = 0, 2 * ML_W, 3 * ML_W
P_RW = 4 * ML_W
P_S5 = P_RW + 3 * RW_W
P_GATE = P_S5 + S5_W
P_RWLR = P_GATE + N_BRANCH * D_MODEL
P_MLG = P_RWLR + 2 * N_DIR * RW_DECAY_RANK + RW_G_RANK
assert P_GATE % D_MODEL == 0 and P_MLG + 2 * N_DIR * ML_HEADS == D_IN and RW_DECAY_RANK == RW_A_RANK

_BF = jnp.bfloat16
_HI = lax.Precision.HIGHEST
_NT = (((1,), (1,)), ((), ()))
_TN = (((0,), (0,)), ((), ()))


def _scan_block(d, i, n_blocks, ctx_blocks):
    bwd = jnp.where(i < ctx_blocks, ctx_blocks - 1 - i, n_blocks - 1 + ctx_blocks - i)
    return jnp.where(d == 0, i, bwd)


def _mm_kernel(a_ref, w_ref, o_ref, abf_ref):
    @pl.when(pl.program_id(1) == 0)
    def _():
        abf_ref[...] = a_ref[...].astype(_BF)

    o_ref[...] = jnp.dot(abf_ref[...], w_ref[...], preferred_element_type=jnp.float32)


def _mm_bf16_kernel(a_ref, w_ref, o_ref):
    o_ref[...] = jnp.dot(a_ref[...], w_ref[...], preferred_element_type=jnp.float32)


def _mm(a, w, tm, tn):
    m, k = a.shape
    n = w.shape[1]
    assert m % tm == 0 and n % tn == 0, (m, n, tm, tn)
    if a.dtype == _BF:
        return pl.pallas_call(
            _mm_bf16_kernel,
            grid=(m // tm, n // tn),
            in_specs=[pl.BlockSpec((tm, k), lambda i, j: (i, 0)),
                      pl.BlockSpec((k, tn), lambda i, j: (0, j))],
            out_specs=pl.BlockSpec((tm, tn), lambda i, j: (i, j)),
            out_shape=jax.ShapeDtypeStruct((m, n), jnp.float32),
            compiler_params=pltpu.CompilerParams(
                dimension_semantics=("parallel", "arbitrary"),
                vmem_limit_bytes=VMEM_LIMIT_BYTES),
            name="mm_bf16",
        )(a, w)
    return pl.pallas_call(
        _mm_kernel,
        grid=(m // tm, n // tn),
        in_specs=[pl.BlockSpec((tm, k), lambda i, j: (i, 0)),
                  pl.BlockSpec((k, tn), lambda i, j: (0, j))],
        out_specs=pl.BlockSpec((tm, tn), lambda i, j: (i, j)),
        out_shape=jax.ShapeDtypeStruct((m, n), jnp.float32),
        scratch_shapes=[pltpu.VMEM((tm, k), _BF)],
        compiler_params=pltpu.CompilerParams(
            dimension_semantics=("parallel", "arbitrary"),
            vmem_limit_bytes=VMEM_LIMIT_BYTES),
        name="mm",
    )(a, w)


def _mm_any(a, w, tm=1024, tn=512, keep_col_pad=False):
    lead = a.shape[:-1]
    a2 = a.reshape(-1, a.shape[-1])
    m, n = a2.shape[0], w.shape[1]
    mp = -(-m // 8) * 8
    if mp > tm:
        mp = -(-m // tm) * tm
    tm = min(tm, mp)
    np_ = -(-n // LANES) * LANES
    if np_ > tn:
        np_ = -(-n // tn) * tn
    tn = min(tn, np_)
    if mp != m:
        a2 = jnp.pad(a2, ((0, mp - m), (0, 0)))
    wb = w.astype(_BF)
    if np_ != n:
        wb = jnp.pad(wb, ((0, 0), (0, np_ - n)))
    out = _mm(a2, wb, tm, tn)
    if keep_col_pad:
        n = np_
    if mp != m or np_ != n:
        out = out[:m, :n]
    return out.reshape(lead + (n,))


def _const_spec(shape):
    nd = len(shape)
    return pl.BlockSpec(shape, lambda i: (0,) * nd, pipeline_mode=pl.Buffered(1))


def _merge_kernel(ml_ref, rw_ref, s5_ref, g0_ref, g1_ref, g2_ref, wml_ref, wrw_ref, wval_ref, wgate_ref, z_ref):
    dot = lambda a, w_ref: jnp.dot(a.astype(_BF), w_ref[...], preferred_element_type=jnp.float32)
    s5 = jnp.concatenate([s5_ref[m] for m in range(S5_W // LANES)], axis=1)
    sval = dot(s5, wval_ref) * jax.nn.sigmoid(dot(s5, wgate_ref))
    z = (jax.nn.sigmoid(g0_ref[...]) * dot(ml_ref[...], wml_ref)
         + jax.nn.sigmoid(g1_ref[...]) * dot(rw_ref[...], wrw_ref)
         + jax.nn.sigmoid(g2_ref[...]) * sval)
    z_ref[...] = z.astype(_BF)


def _merge_pallas(p2, ml, rw, s5, wml, wrw, wval, wgate):
    n, w = ml.shape
    tm = ROW_TM
    gb = P_GATE // D_MODEL
    row = lambda i: (i, 0)
    return pl.pallas_call(
        _merge_kernel,
        grid=(n // tm,),
        in_specs=[pl.BlockSpec((tm, w), row)] * 2 + [pl.BlockSpec((S5_W // LANES, tm, LANES), lambda i: (0, i, 0))]
                 + [pl.BlockSpec((tm, D_MODEL), functools.partial(lambda j, i: (i, gb + j), j))
                    for j in range(N_BRANCH)]
                 + [_const_spec((w, D_MODEL))] * 4,
        out_specs=pl.BlockSpec((tm, D_MODEL), row),
        out_shape=jax.ShapeDtypeStruct((n, D_MODEL), _BF),
        compiler_params=pltpu.CompilerParams(dimension_semantics=("parallel",), vmem_limit_bytes=VMEM_LIMIT_BYTES),
        name="merge_gate_proj",
    )(ml, rw, s5, p2, p2, p2, wml, wrw, wval, wgate)


def _post_scan_kernel(hf_ref, hb_ref, o_ref, yf_ref, yb_ref, bonus_ref, g_ref, mlg_ref, mlb_ref, rwg_ref, rwb_ref,
                      ones_ref, ml_ref, rw_ref):
    h = hf_ref[0] + hb_ref[0]
    parts = []
    for hd in range(ML_HEADS):
        x = h[:, hd * ML_DH:(hd + 1) * ML_DH]
        xc = x - jnp.mean(x, axis=-1, keepdims=True)
        parts.append(xc * lax.rsqrt(jnp.mean(xc * xc, axis=-1, keepdims=True) + ML_NORM_EPS))
    hn = jnp.concatenate(parts, axis=1) * mlg_ref[...] + mlb_ref[...]
    ml_ref[...] = (jax.nn.sigmoid(o_ref[...]) * hn).astype(_BF)

    def head_mean(x):
        hi = x.astype(_BF)
        lo = (x - hi.astype(jnp.float32)).astype(_BF)
        s = (jnp.dot(hi, ones_ref[...], preferred_element_type=jnp.float32)
             + jnp.dot(lo, ones_ref[...], preferred_element_type=jnp.float32))
        return s * (1.0 / RW_DH)

    y = yf_ref[...] + yb_ref[...]
    yc = y - head_mean(y)
    yn = yc * lax.rsqrt(head_mean(yc * yc) + RW_NORM_EPS) * rwg_ref[...] + rwb_ref[...]
    rw_ref[...] = ((yn + bonus_ref[...]) * g_ref[...]).astype(_BF)


def _post_scan(p2, h_dir, y_f, y_b, bonus, g, ml_norm_g, ml_norm_b, rw_norm_g, rw_norm_b):
    n = y_f.shape[0]
    tm = ROW_TM
    head = jnp.arange(RW_W) // RW_DH
    ones_bd = (head[:, None] == head[None, :]).astype(_BF)
    vec = pl.BlockSpec((1, RW_W), lambda i: (0, 0))
    blk = pl.BlockSpec((tm, RW_W), lambda i: (i, 0))
    return pl.pallas_call(
        _post_scan_kernel,
        grid=(n // tm,),
        in_specs=[pl.BlockSpec((1, tm, ML_W), lambda i: (0, i, 0)), pl.BlockSpec((1, tm, ML_W), lambda i: (1, i, 0)),
                  pl.BlockSpec((tm, ML_W), lambda i: (i, P_O // ML_W)), blk, blk, blk, blk, vec, vec, vec, vec,
                  _const_spec((RW_W, RW_W))],
        out_specs=[blk, blk],
        out_shape=[jax.ShapeDtypeStruct((n, ML_W), _BF), jax.ShapeDtypeStruct((n, RW_W), _BF)],
        compiler_params=pltpu.CompilerParams(dimension_semantics=("parallel",), vmem_limit_bytes=VMEM_LIMIT_BYTES),
        name="post_scan_norm_gate",
    )(h_dir, h_dir, p2, y_f, y_b, bonus, g, ml_norm_g.reshape(1, -1), ml_norm_b.reshape(1, -1),
      rw_norm_g.reshape(1, -1), rw_norm_b.reshape(1, -1), ones_bd)


def _ln_rows(x, eps=LN_EPS):
    mu = jnp.mean(x, axis=-1, keepdims=True)
    xc = x - mu
    var = jnp.mean(xc * xc, axis=-1, keepdims=True)
    return xc * lax.rsqrt(var + eps)


def _resid_kernel(*refs, with_w):
    if with_w:
        d_ref, w_ref, x_ref, gate_ref, g_ref, b_ref, sh_ref, sc_ref, xo_ref, uo_ref = refs
        delta = jnp.dot(d_ref[...], w_ref[...], preferred_element_type=jnp.float32)
    else:
        d_ref, x_ref, gate_ref, g_ref, b_ref, sh_ref, sc_ref, xo_ref, uo_ref = refs
        delta = d_ref[...]
    xn = _ln_rows(DEEPNORM_ALPHA * x_ref[...] + gate_ref[0, 0, 0] * delta) * g_ref[...] + b_ref[...]
    xo_ref[...] = xn
    uo_ref[...] = (_ln_rows(xn) * (1.0 + sc_ref[0, 0, 0]) + sh_ref[0, 0, 0]).astype(uo_ref.dtype)


def _resid_norm_mod(delta, w, x, mod_a, ia, ln_g, ln_b, mod_b, ish, isc, blocks_per_seq, ctx_blocks,
                    u_dtype=jnp.float32):
    n, d = x.shape
    tm = ROW_TM
    row = lambda i: (i, 0)

    def mod_spec(m):
        return pl.BlockSpec((1, 1, 1, 1, d), lambda i: (i // blocks_per_seq,
                                                       (i % blocks_per_seq >= ctx_blocks).astype(jnp.int32), m, 0, 0))

    vec = pl.BlockSpec((1, d), lambda i: (0, 0))
    if w is not None:
        in_specs = [pl.BlockSpec((tm, delta.shape[1]), row), _const_spec(w.shape), pl.BlockSpec((tm, d), row)]
        args = (delta, w, x)
    else:
        in_specs = [pl.BlockSpec((tm, d), row), pl.BlockSpec((tm, d), row)]
        args = (delta, x)
    return pl.pallas_call(
        functools.partial(_resid_kernel, with_w=w is not None),
        grid=(n // tm,),
        in_specs=in_specs + [mod_spec(ia), vec, vec, mod_spec(ish), mod_spec(isc)],
        out_specs=[pl.BlockSpec((tm, d), row)] * 2,
        out_shape=[jax.ShapeDtypeStruct((n, d), jnp.float32), jax.ShapeDtypeStruct((n, d), u_dtype)],
        compiler_params=pltpu.CompilerParams(dimension_semantics=("parallel",), vmem_limit_bytes=VMEM_LIMIT_BYTES),
        name="resid_norm_mod",
    )(*args, mod_a, ln_g.reshape(1, d), ln_b.reshape(1, d), mod_b, mod_b)


def _moe_ffn_kernel(te_ref, tv_ref, x_ref, wg_ref, wu_ref, wd_ref, y_ref, wgb_ref, wub_ref, wdb_ref):
    t = pl.program_id(0)

    @pl.when((t == 0) | (te_ref[t] != te_ref[jnp.maximum(t - 1, 0)]))
    def _():
        wgb_ref[...] = wg_ref[0, 0].astype(_BF)
        wub_ref[...] = wu_ref[0, 0].astype(_BF)
        wdb_ref[...] = wd_ref[0, 0].astype(_BF)

    @pl.when(tv_ref[t] == 1)
    def _():
        x = x_ref[...].astype(_BF)
        hg = jnp.dot(x, wgb_ref[...], preferred_element_type=jnp.float32)
        hu = jnp.dot(x, wub_ref[...], preferred_element_type=jnp.float32)
        h = (hg * jax.nn.sigmoid(hg)) * hu
        y_ref[...] = jnp.dot(h.astype(_BF), wdb_ref[...], preferred_element_type=jnp.float32)

    @pl.when(tv_ref[t] == 0)
    def _():
        y_ref[...] = jnp.zeros_like(y_ref)


def _moe_routed(u, gates_t, layer, wg, wu, wd):
    n_tok, d = u.shape
    _, n_e, _, d_e = wg.shape
    tm = MOE_TM
    n_tiles = (TOP_K * n_tok) // tm + n_e
    n_slots = n_tiles * tm
    sel = gates_t > 0.0
    seli = sel.astype(jnp.int32)
    rank = jnp.cumsum(seli, axis=1) - 1
    cnt = jnp.sum(seli, axis=1)
    tiles_e = (cnt + tm - 1) // tm
    tile_end = jnp.cumsum(tiles_e)
    off = (tile_end - tiles_e) * tm
    slot = off[:, None] + rank
    order = jnp.cumsum(seli, axis=0)
    slots, owned, gsel = [], [], []
    for j in range(TOP_K):
        pick = sel & (order == j + 1)
        slots.append(jnp.sum(jnp.where(pick, slot, 0), axis=0))
        owned.append(jnp.where(jnp.any(pick, axis=0), slots[j], n_slots))
        gsel.append(jnp.sum(jnp.where(pick, gates_t, 0.0), axis=0))
    tok = jnp.arange(n_tok, dtype=jnp.int32)
    tok_of_slot = jnp.zeros((n_slots,), jnp.int32).at[jnp.concatenate(owned)].set(jnp.tile(tok, TOP_K), mode='drop')
    tile_ids = jnp.arange(n_tiles, dtype=jnp.int32)
    tile_valid = (tile_ids < tile_end[-1]).astype(jnp.int32)
    tile_expert = jnp.minimum(jnp.searchsorted(tile_end, tile_ids, side='right'), n_e - 1).astype(jnp.int32)
    last_e = jnp.max(jnp.where(cnt > 0, jnp.arange(n_e), 0)).astype(jnp.int32)
    tile_expert = jnp.where(tile_valid == 1, tile_expert, last_e)
    xs = u.astype(_BF).at[tok_of_slot].get(mode='promise_in_bounds')
    w_index = lambda t, te, tv: (layer, te[t], 0, 0)
    grid_spec = pltpu.PrefetchScalarGridSpec(
        num_scalar_prefetch=2,
        grid=(n_tiles,),
        in_specs=[pl.BlockSpec((tm, d), lambda t, te, tv: (t, 0)),
                  pl.BlockSpec((1, 1, d, d_e), w_index, pipeline_mode=pl.Buffered(1)),
                  pl.BlockSpec((1, 1, d, d_e), w_index, pipeline_mode=pl.Buffered(1)),
                  pl.BlockSpec((1, 1, d_e, d), w_index, pipeline_mode=pl.Buffered(1))],
        out_specs=pl.BlockSpec((tm, d), lambda t, te, tv: (t, 0)),
        scratch_shapes=[pltpu.VMEM((d, d_e), _BF), pltpu.VMEM((d, d_e), _BF), pltpu.VMEM((d_e, d), _BF)])
    ys = pl.pallas_call(
        _moe_ffn_kernel, grid_spec=grid_spec,
        out_shape=jax.ShapeDtypeStruct((n_slots, d), jnp.float32),
        compiler_params=pltpu.CompilerParams(dimension_semantics=("arbitrary",),
                                             vmem_limit_bytes=VMEM_LIMIT_BYTES),
        name="moe_routed_ffn",
    )(tile_expert, tile_valid, xs, wg, wu, wd)
    return sum(gsel[j][:, None] * ys.at[slots[j]].get(mode='promise_in_bounds') for j in range(TOP_K))


def _router_kernel(u_ref, wt_ref, b_ref, g_ref):
    logits = lax.dot_general(wt_ref[...], u_ref[...], _NT, precision=_HI, preferred_element_type=jnp.float32)
    aff = jax.nn.sigmoid(logits)
    score = aff + b_ref[...]
    s = [score[e:e + 1] for e in range(N_EXPERTS)]
    a = [aff[e:e + 1] for e in range(N_EXPERTS)]
    gs = []
    for g in range(N_GROUPS):
        m = s[g * EXPERTS_PER_GROUP:(g + 1) * EXPERTS_PER_GROUP]
        best = None
        for i in range(EXPERTS_PER_GROUP):
            for j in range(i + 1, EXPERTS_PER_GROUP):
                best = m[i] + m[j] if best is None else jnp.maximum(best, m[i] + m[j])
        gs.append(best)
    best_val = gs[0]
    best_grp = jnp.zeros_like(gs[0], dtype=jnp.int32)
    for g in range(1, N_GROUPS):
        better = gs[g] > best_val
        best_grp = jnp.where(better, g, best_grp)
        best_val = jnp.where(better, gs[g], best_val)
    sel = []
    for e in range(N_EXPERTS):
        g = e // EXPERTS_PER_GROUP
        rank = jnp.zeros_like(best_grp)
        for j in range(g * EXPERTS_PER_GROUP, (g + 1) * EXPERTS_PER_GROUP):
            if j != e:
                ahead = (s[j] > s[e]) | ((s[j] == s[e]) & (j < e))
                rank = rank + ahead.astype(jnp.int32)
        sel.append((best_grp == g) & (rank < TOP_K))
    wsum = sum(jnp.where(sel[e], a[e], 0.0) for e in range(N_EXPERTS))
    g_ref[...] = jnp.concatenate([jnp.where(sel[e], a[e] / wsum, 0.0) for e in range(N_EXPERTS)], axis=0)


def _router_pallas(u, router_w, router_b, tm=512):
    n_tok, d = u.shape
    assert n_tok % tm == 0
    return pl.pallas_call(
        _router_kernel,
        grid=(n_tok // tm,),
        in_specs=[pl.BlockSpec((tm, d), lambda i: (i, 0)),
                  pl.BlockSpec((N_EXPERTS, d), lambda i: (0, 0)),
                  pl.BlockSpec((N_EXPERTS, 1), lambda i: (0, 0))],
        out_specs=pl.BlockSpec((N_EXPERTS, tm), lambda i: (0, i)),
        out_shape=jax.ShapeDtypeStruct((N_EXPERTS, n_tok), jnp.float32),
        compiler_params=pltpu.CompilerParams(dimension_semantics=("parallel",),
                                             vmem_limit_bytes=VMEM_LIMIT_BYTES),
        name="moe_router",
    )(u, router_w.T, router_b.reshape(N_EXPERTS, 1))


def _s5_mats(lam_re, lam_im, log_dt, b_re, b_im, c_re, c_im):
    L = S5_CHUNK
    lam = lax.complex(lam_re, lam_im)
    ldt = lam * jnp.exp(log_dt)[..., None]
    lam_bar = jnp.exp(ldt)
    b_bar = ((lam_bar - 1.0) / lam)[..., None] * lax.complex(b_re, b_im)
    c_mat = lax.complex(c_re, c_im)
    tau = jnp.arange(L + 1, dtype=jnp.float32)
    pw = jnp.exp(ldt[:, :, None, :] * tau[None, None, :, None])
    kern = jnp.real(jnp.einsum('dgon,dgtn,dgni->dgtoi', c_mat, pw[:, :, :L], b_bar))
    s_idx = jnp.arange(L)[:, None]
    t_idx = jnp.arange(L)[None, :]

    def toeplitz(k, lag, valid):
        m = k[:, jnp.clip(lag, 0, L - 1)] * valid[None, :, :, None, None]
        return jnp.transpose(m, (0, 1, 4, 2, 3)).reshape(-1, L * S5_GROUP, L * S5_GROUP)

    tsum = (toeplitz(kern[0], t_idx - s_idx, (t_idx >= s_idx).astype(jnp.float32))
            + toeplitz(kern[1], s_idx - t_idx, (s_idx >= t_idx).astype(jnp.float32)))
    pin_f = pw[0][:, L - 1 - jnp.arange(L)]
    pin_b = pw[1][:, jnp.arange(L)]
    in_f = jnp.einsum('gsn,gni->gsin', pin_f, b_bar[0]).reshape(-1, L * S5_GROUP, S5_STATE)
    in_b = jnp.einsum('gsn,gni->gsin', pin_b, b_bar[1]).reshape(-1, L * S5_GROUP, S5_STATE)
    icat = jnp.concatenate([jnp.real(in_f), jnp.imag(in_f), jnp.real(in_b), jnp.imag(in_b)], axis=-1)
    pout_f = pw[0][:, 1 + jnp.arange(L)]
    pout_b = pw[1][:, L - jnp.arange(L)]
    out_f = jnp.einsum('gon,gtn->gnto', c_mat[0], pout_f).reshape(-1, S5_STATE, L * S5_GROUP)
    out_b = jnp.einsum('gon,gtn->gnto', c_mat[1], pout_b).reshape(-1, S5_STATE, L * S5_GROUP)
    ocat = jnp.concatenate([jnp.real(out_f), -jnp.imag(out_f), jnp.real(out_b), -jnp.imag(out_b)], axis=1)
    lam_l = pw[:, :, L]
    lam_chunk = jnp.stack([jnp.real(lam_l[0]), jnp.imag(lam_l[0]), jnp.real(lam_l[1]), jnp.imag(lam_l[1])],
                          axis=1)
    return tsum, icat, ocat, lam_chunk


def _s5_kernel(u0_ref, u1_ref, t_ref, i_ref, o_ref, lam_ref, d_ref, y0_ref, y1_ref,
               v_ref, xfr_ref, xfi_ref, xbr_ref, xbi_ref, *, n_chunks, ctx_chunks, bsz):
    n = S5_STATE
    u = jnp.concatenate([u0_ref[0], u1_ref[0]], axis=1)
    ub = u.astype(_BF)
    v_ref[...] = jnp.dot(ub, i_ref[0], preferred_element_type=jnp.float32)
    lam = lam_ref[0]
    lfr = jnp.broadcast_to(lam[0:1], (bsz, n))
    lfi = jnp.broadcast_to(lam[1:2], (bsz, n))
    lbr = jnp.broadcast_to(lam[2:3], (bsz, n))
    lbi = jnp.broadcast_to(lam[3:4], (bsz, n))

    def cmul_add(lr, li, xr, xi, vr, vi):
        return lr * xr - li * xi + vr, lr * xi + li * xr + vi

    def step(j, carry):
        fr, fi, br, bi = carry
        rf = pl.multiple_of(j * (2 * bsz), 2 * bsz)
        pb = jnp.where(j < ctx_chunks // 2, ctx_chunks // 2 - 1 - j, (n_chunks + ctx_chunks) // 2 - 1 - j)
        rb = pl.multiple_of(pb * (2 * bsz), 2 * bsz)
        vf = v_ref[pl.ds(rf, 2 * bsz), :]
        vb = v_ref[pl.ds(rb, 2 * bsz), :]
        fr1, fi1 = cmul_add(lfr, lfi, fr, fi, vf[:bsz, 0:n], vf[:bsz, n:2 * n])
        fr2, fi2 = cmul_add(lfr, lfi, fr1, fi1, vf[bsz:, 0:n], vf[bsz:, n:2 * n])
        br1, bi1 = cmul_add(lbr, lbi, br, bi, vb[bsz:, 2 * n:3 * n], vb[bsz:, 3 * n:4 * n])
        br2, bi2 = cmul_add(lbr, lbi, br1, bi1, vb[:bsz, 2 * n:3 * n], vb[:bsz, 3 * n:4 * n])
        xfr_ref[pl.ds(rf, 2 * bsz), :] = jnp.concatenate([fr, fr1], axis=0)
        xfi_ref[pl.ds(rf, 2 * bsz), :] = jnp.concatenate([fi, fi1], axis=0)
        xbr_ref[pl.ds(rb, 2 * bsz), :] = jnp.concatenate([br1, br], axis=0)
        xbi_ref[pl.ds(rb, 2 * bsz), :] = jnp.concatenate([bi1, bi], axis=0)
        return fr2, fi2, br2, bi2

    z = jnp.zeros((bsz, n), jnp.float32)
    lax.fori_loop(0, n_chunks // 2, step, (z, z, z, z), unroll=S5_UNROLL)
    o = o_ref[0]
    y = jnp.dot(ub, t_ref[0], preferred_element_type=jnp.float32)
    y += jnp.dot(xfr_ref[...].astype(_BF), o[0:n], preferred_element_type=jnp.float32)
    y += jnp.dot(xfi_ref[...].astype(_BF), o[n:2 * n], preferred_element_type=jnp.float32)
    y += jnp.dot(xbr_ref[...].astype(_BF), o[2 * n:3 * n], preferred_element_type=jnp.float32)
    y += jnp.dot(xbi_ref[...].astype(_BF), o[3 * n:4 * n], preferred_element_type=jnp.float32)
    y += d_ref[0] * u
    y = 0.5 * y * (1.0 + jnp.tanh(math.sqrt(2.0 / math.pi) * (y + 0.044715 * (y * y * y))))
    y0_ref[0] = y[:, :LANES]
    y1_ref[0] = y[:, LANES:]


def _to_groups_kernel(*refs, bsz):
    L, C = S5_CHUNK, S5_GROUP
    nch = RL_TBLK // L
    gpt = LANES // C
    lane = lax.broadcasted_iota(jnp.int32, (nch, LANES), 1) // C
    x_refs, o_refs = refs[:S5_W // LANES], refs[S5_W // LANES:]
    for b in range(bsz):
        for m in range(S5_W // LANES):
            src = [x_refs[m][b, pl.ds(j, nch, stride=L), :] for j in range(L)]
            for gl in range(gpt):
                g = m * gpt + gl
                for q in range(L * C // LANES):
                    acc = jnp.zeros((nch, LANES), jnp.float32)
                    for jl in range(gpt):
                        j = q * gpt + jl
                        moved = src[j] if jl == gl else pltpu.roll(src[j], ((jl - gl) * C) % LANES, axis=1)
                        acc = jnp.where(lane == jl, moved, acc)
                    o_refs[q][g, pl.ds(b, nch, stride=bsz), :] = acc


def _from_groups_kernel(y0_ref, y1_ref, o_ref, *, bsz):
    L, C = S5_CHUNK, S5_GROUP
    nch = RL_TBLK // L
    gpt = LANES // C
    lane = lax.broadcasted_iota(jnp.int32, (nch, LANES), 1) // C
    y_refs = (y0_ref, y1_ref)
    for b in range(bsz):
        for m in range(S5_W // LANES):
            for q in range(L * C // LANES):
                src = [y_refs[q][m * gpt + gl, pl.ds(b, nch, stride=bsz), :] for gl in range(gpt)]
                for jl in range(gpt):
                    acc = jnp.zeros((nch, LANES), jnp.float32)
                    for gl in range(gpt):
                        moved = src[gl] if jl == gl else pltpu.roll(src[gl], ((gl - jl) * C) % LANES, axis=1)
                        acc = jnp.where(lane == gl, moved, acc)
                    o_ref[m, b, pl.ds(q * gpt + jl, nch, stride=L), :] = acc


def _s5_pallas(p3, n_ctx, lam_re, lam_im, log_dt, b_re, b_im, c_re, c_im, d_skip):
    bsz, t_len, _ = p3.shape
    L, G, C = S5_CHUNK, S5_GROUPS, S5_GROUP
    nc = t_len // L
    assert (2 * bsz) % 8 == 0 and nc % 2 == 0 and (n_ctx // L) % 2 == 0 and t_len % RL_TBLK == 0
    assert L * C == 2 * LANES and P_S5 % LANES == 0
    tsum, icat, ocat, lam_chunk = _s5_mats(lam_re, lam_im, log_dt, b_re, b_im, c_re, c_im)
    dvec = jnp.tile(d_skip.reshape(G, 1, C), (1, L, 1)).reshape(G, 1, L * C)
    rows = nc * bsz
    n_lt = S5_W // LANES
    nch = RL_TBLK // L
    relayout_params = pltpu.CompilerParams(dimension_semantics=("parallel",), vmem_limit_bytes=VMEM_LIMIT_BYTES)
    half_shape = jax.ShapeDtypeStruct((G, rows, LANES), jnp.float32)
    half_blk = pl.BlockSpec((G, nch * bsz, LANES), lambda i: (0, i, 0))
    u0, u1 = pl.pallas_call(
        functools.partial(_to_groups_kernel, bsz=bsz),
        grid=(t_len // RL_TBLK,),
        in_specs=[pl.BlockSpec((bsz, RL_TBLK, LANES), functools.partial(lambda m, i: (0, i, P_S5 // LANES + m), m))
                  for m in range(n_lt)],
        out_specs=[half_blk, half_blk],
        out_shape=[half_shape, half_shape],
        compiler_params=relayout_params, name="s5_to_groups",
    )(*([p3] * n_lt))
    wspec = pl.BlockSpec((1, L * C, L * C), lambda g: (g, 0, 0))
    half = pl.BlockSpec((1, rows, LANES), lambda g: (g, 0, 0))
    y0, y1 = pl.pallas_call(
        functools.partial(_s5_kernel, n_chunks=nc, ctx_chunks=n_ctx // L, bsz=bsz),
        grid=(G,),
        in_specs=[half, half, wspec, wspec, wspec,
                  pl.BlockSpec((1, 4, S5_STATE), lambda g: (g, 0, 0)),
                  pl.BlockSpec((1, 1, L * C), lambda g: (g, 0, 0))],
        out_specs=[half, half],
        out_shape=[half_shape, half_shape],
        scratch_shapes=[pltpu.VMEM((rows, 4 * S5_STATE), jnp.float32)]
                       + [pltpu.VMEM((rows, S5_STATE), jnp.float32)] * 4,
        compiler_params=pltpu.CompilerParams(dimension_semantics=("parallel",),
                                             vmem_limit_bytes=VMEM_LIMIT_BYTES),
        name="s5_scan",
    )(u0, u1, tsum.astype(_BF), icat.astype(_BF), ocat.astype(_BF), lam_chunk, dvec)
    return pl.pallas_call(
        functools.partial(_from_groups_kernel, bsz=bsz),
        grid=(t_len // RL_TBLK,),
        in_specs=[half_blk, half_blk],
        out_specs=pl.BlockSpec((n_lt, bsz, RL_TBLK, LANES), lambda i: (0, 0, i, 0)),
        out_shape=jax.ShapeDtypeStruct((n_lt, bsz, t_len, LANES), jnp.float32),
        compiler_params=relayout_params, name="s5_from_groups",
    )(y0, y1)


def _log_sigmoid(x):
    return jnp.minimum(x, 0.0) - jnp.log(1.0 + jnp.exp(-jnp.abs(x)))


def _mlstm_kernel(igb_ref, fgb_ref, q_ref, k_ref, v_ref, gc_ref, gr_ref, h_ref, cmat_ref, nvec_ref, m_ref):
    d = pl.program_id(0)
    L = ML_CHUNK
    n_sub = ML_TBLK // L
    H = range(ML_HEADS)

    @pl.when(pl.program_id(2) == 0)
    def _():
        cmat_ref[...] = jnp.zeros_like(cmat_ref)
        nvec_ref[...] = jnp.zeros_like(nvec_ref)
        m_ref[...] = jnp.zeros_like(m_ref)

    igb = [igb_ref[d, h] for h in H]
    fgb = [fgb_ref[d, h] for h in H]
    row = lax.broadcasted_iota(jnp.int32, (L, L), 0)
    col = lax.broadcasted_iota(jnp.int32, (L, L), 1)
    sign = 1 - 2 * d
    seen = (row - col) * sign >= 0
    seen_f = seen.astype(jnp.float32)
    seen_t = ((col - row) * sign >= 0).astype(jnp.float32)
    scale = ML_DH ** -0.5

    def chunk(jj, carry):
        cj = jnp.where(d == 0, jj, n_sub - 1 - jj)
        r0 = pl.multiple_of(cj * L, L)
        hs = lambda h: slice(h * ML_DH, (h + 1) * ML_DH)
        q = [q_ref[0, pl.ds(r0, L), hs(h)] for h in H]
        k = [k_ref[0, pl.ds(r0, L), hs(h)] * scale for h in H]
        vb = [v_ref[0, pl.ds(r0, L), hs(h)].astype(_BF) for h in H]
        gc = [gc_ref[0, 0, h, pl.ds(r0, L), :] for h in H]
        gr = [gr_ref[0, 0, h, cj] for h in H]
        li_col = [gc[h][:, 0:1] + igb[h] for h in H]
        lf_col = [_log_sigmoid(gc[h][:, 1:2] + fgb[h]) for h in H]
        li_row = [gr[h][0:1, :] + igb[h] for h in H]
        lf_row = [_log_sigmoid(gr[h][1:2, :] + fgb[h]) for h in H]
        m_prev = [m_ref[h] for h in H]
        bcum_col = [jnp.dot(seen_f, jnp.broadcast_to(lf_col[h], (L, L)), precision=_HI,
                            preferred_element_type=jnp.float32) for h in H]
        bcum_row = [jnp.dot(jnp.broadcast_to(lf_row[h], (8, L)), seen_t, precision=_HI,
                            preferred_element_type=jnp.float32)[0:1] for h in H]
        qb = [q[h].astype(_BF) for h in H]
        qk = [lax.dot_general(qb[h], k[h].astype(_BF), _NT, preferred_element_type=jnp.float32) for h in H]
        qc = [jnp.dot(qb[h], cmat_ref[h].astype(_BF), preferred_element_type=jnp.float32) for h in H]
        log_d = [jnp.where(seen, bcum_col[h] - bcum_row[h] + li_row[h], -jnp.inf) for h in H]
        inter = [bcum_col[h][:, 0:1] + m_prev[h] for h in H]
        m_j = [jnp.maximum(jnp.max(log_d[h], axis=1, keepdims=True), inter[h]) for h in H]
        scores = [qk[h] * jnp.exp(log_d[h] - m_j[h]) for h in H]
        s_inter = [jnp.exp(inter[h] - m_j[h]) for h in H]
        sv = [jnp.dot(scores[h].astype(_BF), vb[h], preferred_element_type=jnp.float32) for h in H]
        b_last = [jnp.sum(lf_col[h], axis=0, keepdims=True) for h in H]
        log_w = [b_last[h] - bcum_col[h][:, 0:1] + li_col[h] for h in H]
        m_new = [jnp.maximum(b_last[h] + m_prev[h], jnp.max(log_w[h], axis=0, keepdims=True)) for h in H]
        kw = [k[h] * jnp.exp(log_w[h] - m_new[h]) for h in H]
        decay = [jnp.exp(b_last[h] + m_prev[h] - m_new[h]) for h in H]
        kv = [lax.dot_general(kw[h].astype(_BF), vb[h], _TN, preferred_element_type=jnp.float32) for h in H]
        for h in H:
            num = sv[h] + s_inter[h] * qc[h]
            den = (jnp.sum(scores[h], axis=1, keepdims=True)
                   + s_inter[h] * jnp.sum(q[h] * nvec_ref[h], axis=1, keepdims=True))
            h_ref[0, 0, pl.ds(r0, L), hs(h)] = num / jnp.maximum(jnp.abs(den), jnp.exp(-m_j[h]))
        for h in H:
            cmat_ref[h] = decay[h] * cmat_ref[h] + kv[h]
            nvec_ref[h] = decay[h] * nvec_ref[h] + jnp.sum(kw[h], axis=0, keepdims=True)
            m_ref[h] = m_new[h]
        return carry

    lax.fori_loop(0, n_sub, chunk, 0)


def _mlstm_pallas(qk, p3, gates, ig_b, fg_b, n_ctx):
    bsz, t_len, _ = qk.shape
    nb = t_len // ML_TBLK
    cb = n_ctx // ML_TBLK
    assert t_len % ML_TBLK == 0 and n_ctx % ML_TBLK == 0
    g = gates.reshape(bsz, t_len, 2, N_DIR, ML_HEADS)
    gcol = jnp.transpose(g, (3, 0, 4, 1, 2))
    grow = jnp.transpose(g.reshape(bsz, t_len // ML_CHUNK, ML_CHUNK, 2, N_DIR, ML_HEADS),
                         (4, 0, 5, 1, 3, 2))
    blk = lambda d, i: _scan_block(d, i, nb, cb)
    grid_spec = pltpu.PrefetchScalarGridSpec(
        num_scalar_prefetch=2,
        grid=(N_DIR, bsz, nb),
        in_specs=[pl.BlockSpec((1, ML_TBLK, ML_W), lambda d, b, i, *_: (b, blk(d, i), 0)),
                  pl.BlockSpec((1, ML_TBLK, ML_W), lambda d, b, i, *_: (b, blk(d, i), 1)),
                  pl.BlockSpec((1, ML_TBLK, ML_W), lambda d, b, i, *_: (b, blk(d, i), P_V // ML_W))] + [
            pl.BlockSpec((1, 1, ML_HEADS, ML_TBLK, 2), lambda d, b, i, *_: (d, b, 0, blk(d, i), 0)),
            pl.BlockSpec((1, 1, ML_HEADS, ML_TBLK // ML_CHUNK, 2, ML_CHUNK),
                         lambda d, b, i, *_: (d, b, 0, blk(d, i), 0, 0))],
        out_specs=pl.BlockSpec((1, 1, ML_TBLK, ML_W), lambda d, b, i, *_: (d, b, blk(d, i), 0)),
        scratch_shapes=[pltpu.VMEM((ML_HEADS, ML_DH, ML_DH), jnp.float32),
                        pltpu.VMEM((ML_HEADS, 1, ML_DH), jnp.float32),
                        pltpu.VMEM((ML_HEADS, 1, 1), jnp.float32)])
    return pl.pallas_call(
        _mlstm_kernel, grid_spec=grid_spec,
        out_shape=jax.ShapeDtypeStruct((N_DIR, bsz, t_len, ML_W), jnp.float32),
        compiler_params=pltpu.CompilerParams(
            dimension_semantics=("parallel", "parallel", "arbitrary"),
            vmem_limit_bytes=VMEM_LIMIT_BYTES),
        name="mlstm_scan",
    )(ig_b, fg_b, qk, qk, p3, gcol, grow)


def _to_pairs(x):
    return jnp.concatenate([x[:, p * LANES:(p + 1) * LANES] for p in range(RW_PAIRS)], axis=0)


def _dotf(a, b, dims=None):
    a = a.astype(_BF)
    b = b.astype(_BF)
    if dims is None:
        return jnp.dot(a, b, preferred_element_type=jnp.float32)
    return lax.dot_general(a, b, dims, preferred_element_type=jnp.float32)


def _rwkv_a_kernel(r_ref, k_ref, v_ref, kk_ref, a_ref, lw_ref,
                   att_ref, rt_ref, bw_ref, kw_ref, vt_ref, u0t_ref, y0_ref, wc_ref, *, n_sub):
    d = pl.program_id(0)
    C, R = RW_CHUNK, RW_ROWS
    row = lax.broadcasted_iota(jnp.int32, (R, R), 0)
    col = lax.broadcasted_iota(jnp.int32, (R, R), 1)
    same = (row // C) == (col // C)
    sign = 1 - 2 * d
    before = same & ((row - col) * sign > 0)
    upto = same & ((row - col) * sign >= 0)
    upto2 = jnp.concatenate([upto, upto], axis=1)
    eye = (row == col).astype(jnp.float32)
    first = col < RW_DH
    tpos = row % C

    def chunk_group(jg, carry):
        js = [jg * RW_GROUP + i for i in range(RW_GROUP)]
        G2 = [(i, h2) for i in range(RW_GROUP) for h2 in range(2)]
        pick = lambda lst, off: [jnp.where(first, lst[2 * i][:, off:off + R], lst[2 * i + 1][:, off:off + R])
                                 for i in range(RW_GROUP)]
        r0s = [pl.multiple_of(j * C, C) for j in js]
        ld = lambda ref: [_to_pairs(ref[0, pl.ds(r0, C), :]) for r0 in r0s]
        ldd = lambda ref: [_to_pairs(ref[0, 0, pl.ds(r0, C), :]) for r0 in r0s]
        r, v, kk = ld(r_ref), ld(v_ref), ld(kk_ref)
        k, a, lw = ldd(k_ref), ldd(a_ref), ldd(lw_ref)
        fwd = d == 0
        cum, aft = [], []
        for i in range(RW_GROUP):
            pre = lw[i]
            suf = lw[i]
            for s in (1, 2, 4, 8):
                pre = pre + jnp.where(tpos >= s, pltpu.roll(pre, s, axis=0), 0.0)
                suf = suf + jnp.where(tpos < C - s, pltpu.roll(suf, R - s, axis=0), 0.0)
            cum.append(jnp.where(fwd, pre, suf))
            aft.append(jnp.where(fwd, suf, pre) - lw[i])
        a_hat = [-kk[i] * jnp.exp(cum[i] - lw[i]) for i in range(RW_GROUP)]
        r_hat = [r[i] * jnp.exp(cum[i]) for i in range(RW_GROUP)]
        vb = [v[i].astype(_BF) for i in range(RW_GROUP)]
        m = []
        for i in range(RW_GROUP):
            inv_w = jnp.exp(-cum[i])
            lhs = jnp.concatenate([jnp.where(first, a_hat[i], 0.0), jnp.where(first, 0.0, a_hat[i]),
                                   jnp.where(first, r_hat[i], 0.0), jnp.where(first, 0.0, r_hat[i])], axis=0)
            rhs = jnp.concatenate([kk[i] * a[i] * inv_w, k[i] * inv_w], axis=0)
            m.append(_dotf(lhs, rhs, _NT))
        x = [jnp.where(before, m[i][h2 * R:(h2 + 1) * R, 0:R], 0.0).astype(_BF) for i, h2 in G2]
        ak = [jnp.where(before, m[i][h2 * R:(h2 + 1) * R, R:2 * R], 0.0) for i, h2 in G2]
        rbk = [jnp.where(upto2, m[i][(2 + h2) * R:(3 + h2) * R, :], 0.0).astype(_BF) for i, h2 in G2]
        akv = [_dotf(ak[g], vb[g // 2]) for g in range(len(G2))]
        x2 = [_dotf(xx, xx).astype(_BF) for xx in x]
        x4 = [_dotf(xx, xx).astype(_BF) for xx in x2]
        x8 = [_dotf(xx, xx).astype(_BF) for xx in x4]
        t = [eye + xx.astype(jnp.float32) for xx in x]
        t = [t[g] + _dotf(t[g], x2[g]) for g in range(len(G2))]
        t = [t[g] + _dotf(t[g], x4[g]) for g in range(len(G2))]
        t = [t[g] + _dotf(t[g], x8[g]) for g in range(len(G2))]
        akv = pick(akv, 0)
        rhs2 = [jnp.concatenate([a_hat[i], akv[i]], axis=1).astype(_BF) for i in range(RW_GROUP)]
        ta = [_dotf(t[g], rhs2[g // 2]) for g in range(len(G2))]
        at = pick(ta, 0)
        u0 = pick(ta, R)
        rhs3 = [jnp.concatenate([jnp.concatenate([at[i], u0[i]], axis=1).astype(_BF),
                                 jnp.concatenate([jnp.zeros_like(vb[i]), vb[i]], axis=1)], axis=0)
                for i in range(RW_GROUP)]
        ry = [_dotf(rbk[g], rhs3[g // 2]) for g in range(len(G2))]
        rt = pick(ry, 0)
        y0 = pick(ry, R)
        for i, j in enumerate(js):
            w_aft = jnp.exp(aft[i])
            att_ref[0, 0, j] = at[i].T.astype(_BF)
            u0t_ref[0, 0, j] = u0[i].T
            vt_ref[0, 0, j] = v[i].T.astype(_BF)
            rt_ref[0, 0, j] = (r_hat[i] + rt[i]).astype(_BF)
            bw_ref[0, 0, j] = (kk[i] * a[i] * w_aft).astype(_BF)
            kw_ref[0, 0, j] = (k[i] * w_aft).astype(_BF)
            y0_ref[0, 0, j] = y0[i]
            tot = cum[i] + aft[i]
            wc_ref[0, 0, j] = jnp.exp(jnp.concatenate([tot[p * C:p * C + 1] for p in range(RW_PAIRS)], axis=0))
        return carry

    lax.fori_loop(0, n_sub // RW_GROUP, chunk_group, 0)


def _rwkv_b_kernel(*refs, n_sub):
    ins, (yf_ref, yb_ref, s_ref) = refs[:16], refs[16:]
    C = RW_CHUNK

    @pl.when(pl.program_id(1) == 0)
    def _():
        s_ref[...] = jnp.zeros_like(s_ref)

    row = lax.broadcasted_iota(jnp.int32, (LANES, LANES), 0)
    col = lax.broadcasted_iota(jnp.int32, (LANES, LANES), 1)
    diag = (row // RW_DH) == (col // RW_DH)
    pair_of_row = lax.broadcasted_iota(jnp.int32, (2 * RW_ROWS, LANES), 0) % RW_ROWS // C
    DP = [(d, p) for d in range(N_DIR) for p in range(RW_PAIRS)]

    def chunk(jj, carry):
        cjs = [jj, n_sub - 1 - jj]
        att, rt, bw, kw, vt, u0t, y0, wc = [[ins[8 * d + a][0, 0, cjs[d]] for d in range(N_DIR)] for a in range(8)]
        bk = [jnp.concatenate([bw[d], kw[d]], axis=0) for d in range(N_DIR)]
        sps = [s_ref[d, p] for d, p in DP]
        spb = [sp.astype(_BF) for sp in sps]
        uts = [jnp.dot(spb[i], att[d], preferred_element_type=jnp.float32) + u0t[d] for i, (d, p) in enumerate(DP)]
        for i, (d, p) in enumerate(DP):
            rows = slice(p * C, (p + 1) * C)
            y = lax.dot_general(rt[d][rows], spb[i], _NT, preferred_element_type=jnp.float32) + y0[d][rows]
            y_ref = yf_ref if d == 0 else yb_ref
            y_ref[0, pl.ds(pl.multiple_of(cjs[d] * C, C), C), p * LANES:(p + 1) * LANES] = y
        for i, (d, p) in enumerate(DP):
            lhs = jnp.concatenate([uts[i].astype(_BF), vt[d]], axis=1)
            rhs = jnp.where(pair_of_row == p, bk[d], jnp.zeros_like(bk[d]))
            upd = jnp.dot(lhs, rhs, preferred_element_type=jnp.float32)
            s_ref[d, p] = jnp.where(diag, wc[d][p:p + 1, :] * sps[i] + upd, 0.0)
        return carry

    lax.fori_loop(0, n_sub, chunk, 0)


def _rwkv_pallas(r, v, kk, k_dir, a_dir, lw_dir, n_ctx):
    bsz, t_len, _ = r.shape
    nb, cb = t_len // RW_TBLK, n_ctx // RW_TBLK
    assert t_len % RW_TBLK == 0 and n_ctx % RW_TBLK == 0
    n_sub = RW_TBLK // RW_CHUNK
    nc = t_len // RW_CHUNK
    sh_spec = pl.BlockSpec((1, RW_TBLK, RW_W), lambda d, b, i: (b, _scan_block(d, i, nb, cb), 0))
    dr_spec = pl.BlockSpec((1, 1, RW_TBLK, RW_W), lambda d, b, i: (d, b, _scan_block(d, i, nb, cb), 0))
    ch_spec = pl.BlockSpec((1, 1, n_sub, RW_ROWS, LANES), lambda d, b, i: (d, b, _scan_block(d, i, nb, cb), 0, 0))
    wc_spec = pl.BlockSpec((1, 1, n_sub, RW_PAIRS, LANES), lambda d, b, i: (d, b, _scan_block(d, i, nb, cb), 0, 0))
    ch_shape = lambda dt: jax.ShapeDtypeStruct((N_DIR, bsz, nc, RW_ROWS, LANES), dt)
    params = pltpu.CompilerParams(dimension_semantics=("parallel", "parallel", "arbitrary"),
                                  vmem_limit_bytes=VMEM_LIMIT_BYTES)
    chunk_local = pl.pallas_call(
        functools.partial(_rwkv_a_kernel, n_sub=n_sub),
        grid=(N_DIR, bsz, nb),
        in_specs=[sh_spec, dr_spec, sh_spec, sh_spec, dr_spec, dr_spec],
        out_specs=[ch_spec] * 7 + [wc_spec],
        out_shape=[ch_shape(_BF)] * 5 + [ch_shape(jnp.float32)] * 2
                  + [jax.ShapeDtypeStruct((N_DIR, bsz, nc, RW_PAIRS, LANES), jnp.float32)],
        compiler_params=params, name="rwkv_chunk_local",
    )(r, k_dir, v, kk, a_dir, lw_dir)
    dir_specs = []
    for d in range(N_DIR):
        blk = functools.partial(lambda d, b, i: (d, b, _scan_block(d, i, nb, cb), 0, 0), d)
        dir_specs += [pl.BlockSpec((1, 1, n_sub, RW_ROWS, LANES), blk)] * 7
        dir_specs += [pl.BlockSpec((1, 1, n_sub, RW_PAIRS, LANES), blk)]
    y_specs = [pl.BlockSpec((1, RW_TBLK, RW_W),
                            functools.partial(lambda d, b, i: (b, _scan_block(d, i, nb, cb), 0), d))
               for d in range(N_DIR)]
    return pl.pallas_call(
        functools.partial(_rwkv_b_kernel, n_sub=n_sub),
        grid=(bsz, nb),
        in_specs=dir_specs,
        out_specs=y_specs,
        out_shape=[jax.ShapeDtypeStruct((bsz, t_len, RW_W), jnp.float32)] * N_DIR,
        scratch_shapes=[pltpu.VMEM((N_DIR, RW_PAIRS, LANES, LANES), jnp.float32)],
        compiler_params=pltpu.CompilerParams(dimension_semantics=("parallel", "arbitrary"),
                                             vmem_limit_bytes=VMEM_LIMIT_BYTES),
        name="rwkv_state_scan",
    )(*chunk_local, *chunk_local)


def _softplus(x):
    return jnp.maximum(x, 0.0) + jnp.log(1.0 + jnp.exp(-jnp.abs(x)))


def _rwkv_prep_kernel(*refs, blocks_per_seq, ctx_blocks):
    (zr, zk, zv, zw, za, zg, pr, pk, pv, pw, pa, pg, nr, nk, nv, nw, na, ng,
     mu_ref, w0_ref, a0_ref, wup_ref, aup_ref, gup_ref, kkw_ref, ka_ref, rk_ref, ones_ref,
     r_out, v_out, kk_out, kdir_out, a_out, lw_out, bonus_out, g_out) = refs
    blk = pl.program_id(0) % blocks_per_seq
    has_prev = (blk != 0) & (blk != ctx_blocks)
    has_next = (blk != ctx_blocks - 1) & (blk != blocks_per_seq - 1)
    tm = zr.shape[0]

    def lerp(z_ref, p_ref, n_ref, mu):
        z = z_ref[...]
        rows = lax.broadcasted_iota(jnp.int32, z.shape, 0)
        prev_row = jnp.where(has_prev, p_ref[RW_HALO - 1:RW_HALO, :], 0.0)
        next_row = jnp.where(has_next, n_ref[0:1, :], 0.0)
        before = jnp.where(rows == 0, prev_row, pltpu.roll(z, 1, axis=0))
        after = jnp.where(rows == tm - 1, next_row, pltpu.roll(z, tm - 1, axis=0))
        return z + mu * (0.5 * (before + after) - z)

    mu = mu_ref[...]
    r = lerp(zr, pr, nr, mu[:, 0:RW_W])
    k = lerp(zk, pk, nk, mu[:, RW_W:2 * RW_W])
    v = lerp(zv, pv, nv, mu[:, 2 * RW_W:3 * RW_W])
    wd = lerp(zw, pw, nw, mu[:, 3 * RW_W:3 * RW_W + LANES])
    ad = lerp(za, pa, na, mu[:, 3 * RW_W + LANES:3 * RW_W + 2 * LANES])
    gd = lerp(zg, pg, ng, mu[:, 3 * RW_W + 2 * LANES:3 * RW_W + 3 * LANES])
    dot = lambda a, w_ref: jnp.dot(a.astype(_BF), w_ref[...], preferred_element_type=jnp.float32)
    w_pre = dot(jnp.tanh(wd), wup_ref)
    a_pre = dot(ad, aup_ref)
    g_out[...] = dot(jax.nn.sigmoid(gd), gup_ref)

    def head_sum(x):
        hi = x.astype(_BF)
        lo = (x - hi.astype(jnp.float32)).astype(_BF)
        return (jnp.dot(hi, ones_ref[...], preferred_element_type=jnp.float32)
                + jnp.dot(lo, ones_ref[...], preferred_element_type=jnp.float32))

    kk = k * kkw_ref[...]
    kk_out[...] = kk / jnp.maximum(jnp.sqrt(head_sum(kk * kk)), 1e-12)
    r_out[...] = r
    v_out[...] = v
    rk = r * rk_ref[...]
    bonus = jnp.zeros_like(r)
    for d in range(N_DIR):
        cols = slice(d * RW_W, (d + 1) * RW_W)
        lw_out[d] = -jnp.exp(-_softplus(-(w0_ref[d:d + 1, :] + w_pre[:, cols])) - RW_DECAY_OFFSET)
        a = jax.nn.sigmoid(a0_ref[d:d + 1, :] + a_pre[:, cols])
        a_out[d] = a
        k_dir = k * (1.0 + (a - 1.0) * ka_ref[...])
        kdir_out[d] = k_dir
        bonus = bonus + rk * k_dir
    bonus_out[...] = head_sum(bonus) * v


def _rwkv_prep(p2, blocks_per_seq, ctx_blocks, mu, w0, w_up, a0, a_up, g_up, k_k, k_a, r_k):
    n = p2.shape[0]
    tm = ROW_TM
    hb = tm // RW_HALO
    n_hb = n // RW_HALO
    cw, cl = P_RW // RW_W, P_RWLR // LANES
    assert P_RW % RW_W == 0 and P_RWLR % LANES == 0 and N_DIR * RW_DECAY_RANK == LANES and RW_G_RANK == LANES

    def specs(rows, row_index):
        wide = [pl.BlockSpec((rows, RW_W), functools.partial(lambda j, i: (row_index(i), cw + j), j))
                for j in range(3)]
        return wide + [pl.BlockSpec((rows, LANES), functools.partial(lambda j, i: (row_index(i), cl + j), j))
                       for j in range(3)]

    main = specs(tm, lambda i: i)
    prev = specs(RW_HALO, lambda i: jnp.maximum(i * hb - 1, 0))
    nxt = specs(RW_HALO, lambda i: jnp.minimum((i + 1) * hb, n_hb - 1))
    zero = jnp.zeros((RW_DECAY_RANK, RW_W), jnp.float32)
    both_dirs = lambda up: jnp.concatenate([jnp.concatenate([up[0], zero], axis=1),
                                            jnp.concatenate([zero, up[1]], axis=1)], axis=0).astype(_BF)
    head = jnp.arange(RW_W) // RW_DH
    ones_bd = (head[:, None] == head[None, :]).astype(_BF)
    vec = lambda w: pl.BlockSpec((1, w), lambda i: (0, 0))
    dvec = pl.BlockSpec((N_DIR, RW_W), lambda i: (0, 0))
    consts = [vec(mu.shape[0]), dvec, dvec, _const_spec((LANES, N_DIR * RW_W)), _const_spec((LANES, N_DIR * RW_W)),
              _const_spec((RW_G_RANK, RW_W)), vec(RW_W), vec(RW_W), vec(RW_W), _const_spec((RW_W, RW_W))]
    row = pl.BlockSpec((tm, RW_W), lambda i: (i, 0))
    drow = pl.BlockSpec((N_DIR, tm, RW_W), lambda i: (0, i, 0))
    sh = jax.ShapeDtypeStruct((n, RW_W), jnp.float32)
    dsh = jax.ShapeDtypeStruct((N_DIR, n, RW_W), jnp.float32)
    return pl.pallas_call(
        functools.partial(_rwkv_prep_kernel, blocks_per_seq=blocks_per_seq, ctx_blocks=ctx_blocks),
        grid=(n // tm,),
        in_specs=main + prev + nxt + consts,
        out_specs=[row, row, row, drow, drow, drow, row, row],
        out_shape=[sh, sh, sh, dsh, dsh, dsh, sh, sh],
        compiler_params=pltpu.CompilerParams(dimension_semantics=("parallel",), vmem_limit_bytes=VMEM_LIMIT_BYTES),
        name="rwkv_prep",
    )(*([p2] * 18), mu.reshape(1, -1), w0, a0, both_dirs(w_up), both_dirs(a_up), g_up.astype(_BF),
      k_k.reshape(1, -1), k_a.reshape(1, -1), r_k.reshape(1, -1), ones_bd)


def _layer_norm(x, eps=LN_EPS):
    mu = jnp.mean(x, axis=-1, keepdims=True)
    var = jnp.mean(jnp.square(x - mu), axis=-1, keepdims=True)
    return (x - mu) * lax.rsqrt(var + eps)


def _modulate(x, shift, scale):
    return _layer_norm(x) * (1.0 + scale) + shift


def _conv_silu_kernel(x_ref, prev_ref, next_ref, w_ref, b_ref, o_ref, *, blocks_per_seq, ctx_blocks):
    blk = pl.program_id(0) % blocks_per_seq
    is_lat = blk >= ctx_blocks
    has_prev = is_lat & (blk != ctx_blocks)
    has_next = is_lat & (blk != blocks_per_seq - 1)
    tm, width = x_ref.shape
    ext = jnp.concatenate([jnp.where(has_prev, prev_ref[...], 0.0), x_ref[...],
                           jnp.where(has_next, next_ref[...], 0.0)], axis=0)
    n_ext = ext.shape[0]
    t = lax.broadcasted_iota(jnp.int32, (tm, 1), 0)
    col = jnp.where(is_lat, t % GRID_W, t)
    last_col = jnp.where(is_lat, GRID_W - 1, tm - 1)
    w = w_ref[...]
    acc = jnp.zeros((tm, width), jnp.float32) + b_ref[...]
    for dc in (-1, 0, 1):
        shifted = ext if dc == 0 else pltpu.roll(ext, (-dc) % n_ext, axis=0)
        col_ok = (col != 0) if dc == -1 else ((col != last_col) if dc == 1 else None)
        for dr in (-1, 0, 1):
            src = shifted[GRID_W + dr * GRID_W:GRID_W + dr * GRID_W + tm]
            ok = col_ok if dr == 0 else (is_lat if col_ok is None else (col_ok & is_lat))
            if ok is not None:
                src = jnp.where(ok, src, 0.0)
            tap = (dr + 1) * 3 + (dc + 1)
            acc = acc + src * w[tap:tap + 1, :]
    o_ref[...] = acc * jax.nn.sigmoid(acc)


def _conv_silu(p2, blocks_per_seq, ctx_blocks, conv_w, conv_b):
    n = p2.shape[0]
    tm = ROW_TM
    width = 2 * ML_W
    assert ctx_blocks == 1 and tm % GRID_W == 0 and P_QK % width == 0
    hb = tm // GRID_W
    n_hb = n // GRID_W
    cb = P_QK // width
    return pl.pallas_call(
        functools.partial(_conv_silu_kernel, blocks_per_seq=blocks_per_seq, ctx_blocks=ctx_blocks),
        grid=(n // tm,),
        in_specs=[pl.BlockSpec((tm, width), lambda i: (i, cb)),
                  pl.BlockSpec((GRID_W, width), lambda i: (jnp.maximum(i * hb - 1, 0), cb)),
                  pl.BlockSpec((GRID_W, width), lambda i: (jnp.minimum((i + 1) * hb, n_hb - 1), cb)),
                  pl.BlockSpec((9, width), lambda i: (0, 0)), pl.BlockSpec((1, width), lambda i: (0, 0))],
        out_specs=pl.BlockSpec((tm, width), lambda i: (i, 0)),
        out_shape=jax.ShapeDtypeStruct((n, width), jnp.float32),
        compiler_params=pltpu.CompilerParams(dimension_semantics=("parallel",), vmem_limit_bytes=VMEM_LIMIT_BYTES),
        name="mlstm_conv_silu",
    )(p2, p2, p2, conv_w.reshape(9, width), conv_b.reshape(1, width))


def _rwkv7_branch(p2, bsz, n_ctx, mu, w0, w_up, a0, a_up, g_up, k_k, k_a, r_k):
    t_len = p2.shape[0] // bsz
    r, v, kk, k_dir, a, log_decay, bonus, g = _rwkv_prep(p2, t_len // ROW_TM, n_ctx // ROW_TM, mu, w0, w_up, a0,
                                                         a_up, g_up, k_k, k_a, r_k)
    seq = lambda z: z.reshape(z.shape[:-2] + (bsz, t_len, RW_W))
    y_f, y_b = _rwkv_pallas(seq(r), seq(v), seq(kk), seq(k_dir), seq(a), seq(log_decay), n_ctx)
    return y_f.reshape(-1, RW_W), y_b.reshape(-1, RW_W), bonus, g


def _token_mixer(u, n_ctx, w_in,
                 ml_conv_w, ml_conv_b, ml_ig_b, ml_fg_b, ml_norm_g, ml_norm_b, ml_proj,
                 rw_mu, rw_w0, rw_w_up, rw_a0, rw_a_up, rw_g_up, rw_k_k, rw_k_a, rw_r_k,
                 rw_norm_g, rw_norm_b, rw_proj,
                 s5_lam_re, s5_lam_im, s5_log_dt, s5_b_re, s5_b_im, s5_c_re, s5_c_im, s5_d,
                 s5_w_val, s5_w_gate):
    bsz, t_len, _ = u.shape
    n = bsz * t_len
    p = _mm_any(u, _permute_w_in(w_in), tm=PROJ_TM, keep_col_pad=True)

    def col(start, width):
        return p[..., start:start + width]

    n_gate = N_DIR * ML_HEADS
    p2 = p.reshape(n, -1)
    qk = _conv_silu(p2, t_len // ROW_TM, n_ctx // ROW_TM, ml_conv_w, ml_conv_b)
    h_dir = _mlstm_pallas(qk.reshape(bsz, t_len, 2 * ML_W), p, col(P_MLG, 2 * n_gate), ml_ig_b, ml_fg_b, n_ctx)
    y_f, y_b, bonus, g = _rwkv7_branch(p2, bsz, n_ctx, rw_mu, rw_w0, rw_w_up, rw_a0, rw_a_up, rw_g_up,
                                       rw_k_k, rw_k_a, rw_r_k)
    s5 = _s5_pallas(p, n_ctx, s5_lam_re, s5_lam_im, s5_log_dt, s5_b_re, s5_b_im, s5_c_re, s5_c_im, s5_d)
    ml, rw = _post_scan(p2, h_dir.reshape(N_DIR, n, ML_W), y_f, y_b, bonus, g,
                        ml_norm_g, ml_norm_b, rw_norm_g, rw_norm_b)
    return _merge_pallas(p2, ml, rw, s5.reshape(S5_W // LANES, n, LANES),
                         ml_proj.astype(_BF), rw_proj.astype(_BF), s5_w_val.astype(_BF), s5_w_gate.astype(_BF))


def _permute_w_in(w_in):
    offs = np.cumsum((0,) + IN_WIDTHS)
    seg = lambda a, b: w_in[:, offs[a]:offs[b]]
    return jnp.concatenate([seg(0, 4), seg(6, 9), seg(12, 13), seg(13, 14), seg(9, 12), seg(4, 6)], axis=1)


def _moe_ffn(u, router_w, router_b, layer, w_gate, w_up, w_down):
    gates_t = _router_pallas(u, router_w, router_b)
    return _moe_routed(u, gates_t, layer, w_gate, w_up, w_down)


def kernel(x, c, ctx, c_ctx, ada_w, ada_b, w_in, ml_conv_w, ml_conv_b, ml_ig_b, ml_fg_b, ml_norm_g,
           ml_norm_b, ml_proj, rw_mu, rw_w0, rw_w_up, rw_a0, rw_a_up, rw_g_up, rw_k_k, rw_k_a, rw_r_k,
           rw_norm_g, rw_norm_b, rw_proj, s5_lam_re, s5_lam_im, s5_log_dt, s5_b_re, s5_b_im, s5_c_re,
           s5_c_im, s5_d, s5_w_val, s5_w_gate, w_out, ln1_g, ln1_b, ln2_g, ln2_b, router_w, router_b,
           exp_w_gate, exp_w_up, exp_w_down):
    bsz, n_ctx = ctx.shape[0], ctx.shape[1]
    t_len = n_ctx + x.shape[1]
    assert n_ctx % ROW_TM == 0 and t_len % ROW_TM == 0
    blocks_per_seq, ctx_blocks = t_len // ROW_TM, n_ctx // ROW_TM
    silu_c = jax.nn.silu(c)
    silu_cc = jax.nn.silu(c_ctx)[None, :]
    mods = []
    for i in range(DEPTH):
        mx = _mm_any(silu_c, ada_w[i]) + ada_b[i]
        mc = jnp.broadcast_to(_mm_any(silu_cc, ada_w[i]) + ada_b[i], mx.shape)
        mods.append(jnp.stack([mc, mx], axis=1).reshape(bsz, 2, N_MOD, 1, D_MODEL))
    xa = jnp.concatenate([ctx, x], axis=1).reshape(bsz * t_len, D_MODEL)
    m0 = mods[0]
    u = jnp.concatenate([_modulate(ctx, m0[:, 0, 0], m0[:, 0, 1]), _modulate(x, m0[:, 1, 0], m0[:, 1, 1])],
                        axis=1).astype(_BF)
    for i in range(DEPTH):
        z = _token_mixer(
            u, n_ctx, w_in[i],
            ml_conv_w[i], ml_conv_b[i], ml_ig_b[i], ml_fg_b[i], ml_norm_g[i], ml_norm_b[i], ml_proj[i],
            rw_mu[i], rw_w0[i], rw_w_up[i], rw_a0[i], rw_a_up[i], rw_g_up[i], rw_k_k[i], rw_k_a[i], rw_r_k[i],
            rw_norm_g[i], rw_norm_b[i], rw_proj[i],
            s5_lam_re[i], s5_lam_im[i], s5_log_dt[i], s5_b_re[i], s5_b_im[i], s5_c_re[i], s5_c_im[i], s5_d[i],
            s5_w_val[i], s5_w_gate[i])
        xa, u_ffn = _resid_norm_mod(z, w_out[i].astype(_BF), xa, mods[i], 2, ln1_g[i], ln1_b[i], mods[i], 3, 4,
                                    blocks_per_seq, ctx_blocks)
        ffn = _moe_ffn(u_ffn, router_w, router_b, i, exp_w_gate, exp_w_up, exp_w_down)
        xa, u = _resid_norm_mod(ffn, None, xa, mods[i], 5, ln2_g[i], ln2_b[i], mods[min(i + 1, DEPTH - 1)], 0, 1,
                                blocks_per_seq, ctx_blocks, u_dtype=_BF)
        u = u.reshape(bsz, t_len, D_MODEL)
    return xa.reshape(bsz, t_len, D_MODEL)[:, n_ctx:]
```

```python
import functools
import math

import jax
import jax.numpy as jnp
import numpy as np
from jax import lax
from jax.experimental import pallas as pl
from jax.experimental.pallas import tpu as pltpu

D_MODEL = 2048
DEPTH = 2
GRID_W = 64
N_DIR = 2
ML_HEADS = 4
ML_DH = 256
ML_W = ML_HEADS * ML_DH
ML_CHUNK = 64
ML_NORM_EPS = 1e-6
RW_HEADS = 16
RW_DH = 64
RW_W = RW_HEADS * RW_DH
RW_DECAY_RANK = 64
RW_A_RANK = 64
RW_G_RANK = 128
RW_DECAY_OFFSET = 0.5
RW_NORM_EPS = 64e-5
S5_W = 1024
S5_GROUP = 16
S5_GROUPS = S5_W // S5_GROUP
S5_STATE = 64
N_BRANCH = 3
N_GROUPS = 4
EXPERTS_PER_GROUP = 4
N_EXPERTS = N_GROUPS * EXPERTS_PER_GROUP
TOP_K = 2
D_EXPERT = 1024
DEEPNORM_ALPHA = (2.0 * DEPTH) ** 0.25
LN_EPS = 1e-5
N_MOD = 6
IN_WIDTHS = (ML_W, ML_W, ML_W, ML_W, N_DIR * ML_HEADS, N_DIR * ML_HEADS,
             RW_W, RW_W, RW_W, N_DIR * RW_DECAY_RANK, N_DIR * RW_A_RANK, RW_G_RANK,
             S5_W, N_BRANCH * D_MODEL)
D_IN = sum(IN_WIDTHS)
RW_IN_WIDTHS = (RW_W, RW_W, RW_W, N_DIR * RW_DECAY_RANK, N_DIR * RW_A_RANK, RW_G_RANK)

VMEM_LIMIT_BYTES = 56 * 1024 * 1024
LANES = 128

S5_CHUNK = 16
S5_UNROLL = 8
RL_TBLK = 256
PROJ_TM = 1536
ML_TBLK = 256
RW_CHUNK = 16
RW_PAIRS = RW_HEADS // 2
RW_ROWS = RW_PAIRS * RW_CHUNK
RW_TBLK = 256
RW_GROUP = 8
RW_HALO = 8
MOE_TM = 512
ROW_TM = 256

P_QK, P_V, P_O = 0, 2 * ML_W, 3 * ML_W
P_RW = 4 * ML_W
P_S5 = P_RW + 3 * RW_W
P_GATE = P_S5 + S5_W
P_RWLR = P_GATE + N_BRANCH * D_MODEL
P_MLG = P_RWLR + 2 * N_DIR * RW_DECAY_RANK + RW_G_RANK
assert P_GATE % D_MODEL == 0 and P_MLG + 2 * N_DIR * ML_HEADS == D_IN and RW_DECAY_RANK == RW_A_RANK

_BF = jnp.bfloat16
_HI = lax.Precision.HIGHEST
_NT = (((1,), (1,)), ((), ()))
_TN = (((0,), (0,)), ((), ()))


def _scan_block(d, i, n_blocks, ctx_blocks):
    bwd = jnp.where(i < ctx_blocks, ctx_blocks - 1 - i, n_blocks - 1 + ctx_blocks - i)
    return jnp.where(d == 0, i, bwd)


def _mm_kernel(a_ref, w_ref, o_ref, abf_ref):
    @pl.when(pl.program_id(1) == 0)
    def _():
        abf_ref[...] = a_ref[...].astype(_BF)

    o_ref[...] = jnp.dot(abf_ref[...], w_ref[...], preferred_element_type=jnp.float32)


def _mm_bf16_kernel(a_ref, w_ref, o_ref):
    o_ref[...] = jnp.dot(a_ref[...], w_ref[...], preferred_element_type=jnp.float32)


def _mm(a, w, tm, tn):
    m, k = a.shape
    n = w.shape[1]
    assert m % tm == 0 and n % tn == 0, (m, n, tm, tn)
    if a.dtype == _BF:
        return pl.pallas_call(
            _mm_bf16_kernel,
            grid=(m // tm, n // tn),
            in_specs=[pl.BlockSpec((tm, k), lambda i, j: (i, 0)),
                      pl.BlockSpec((k, tn), lambda i, j: (0, j))],
            out_specs=pl.BlockSpec((tm, tn), lambda i, j: (i, j)),
            out_shape=jax.ShapeDtypeStruct((m, n), jnp.float32),
            compiler_params=pltpu.CompilerParams(
                dimension_semantics=("parallel", "arbitrary"),
                vmem_limit_bytes=VMEM_LIMIT_BYTES),
            name="mm_bf16",
        )(a, w)
    return pl.pallas_call(
        _mm_kernel,
        grid=(m // tm, n // tn),
        in_specs=[pl.BlockSpec((tm, k), lambda i, j: (i, 0)),
                  pl.BlockSpec((k, tn), lambda i, j: (0, j))],
        out_specs=pl.BlockSpec((tm, tn), lambda i, j: (i, j)),
        out_shape=jax.ShapeDtypeStruct((m, n), jnp.float32),
        scratch_shapes=[pltpu.VMEM((tm, k), _BF)],
        compiler_params=pltpu.CompilerParams(
            dimension_semantics=("parallel", "arbitrary"),
            vmem_limit_bytes=VMEM_LIMIT_BYTES),
        name="mm",
    )(a, w)


def _mm_any(a, w, tm=1024, tn=512, keep_col_pad=False):
    lead = a.shape[:-1]
    a2 = a.reshape(-1, a.shape[-1])
    m, n = a2.shape[0], w.shape[1]
    mp = -(-m // 8) * 8
    if mp > tm:
        mp = -(-m // tm) * tm
    tm = min(tm, mp)
    np_ = -(-n // LANES) * LANES
    if np_ > tn:
        np_ = -(-n // tn) * tn
    tn = min(tn, np_)
    if mp != m:
        a2 = jnp.pad(a2, ((0, mp - m), (0, 0)))
    wb = w.astype(_BF)
    if np_ != n:
        wb = jnp.pad(wb, ((0, 0), (0, np_ - n)))
    out = _mm(a2, wb, tm, tn)
    if keep_col_pad:
        n = np_
    if mp != m or np_ != n:
        out = out[:m, :n]
    return out.reshape(lead + (n,))


def _const_spec(shape):
    nd = len(shape)
    return pl.BlockSpec(shape, lambda i: (0,) * nd, pipeline_mode=pl.Buffered(1))


def _merge_kernel(ml_ref, rw_ref, s5_ref, g0_ref, g1_ref, g2_ref, wml_ref, wrw_ref, wval_ref, wgate_ref, z_ref):
    dot = lambda a, w_ref: jnp.dot(a.astype(_BF), w_ref[...], preferred_element_type=jnp.float32)
    s5 = jnp.concatenate([s5_ref[m] for m in range(S5_W // LANES)], axis=1)
    sval = dot(s5, wval_ref) * jax.nn.sigmoid(dot(s5, wgate_ref))
    z = (jax.nn.sigmoid(g0_ref[...]) * dot(ml_ref[...], wml_ref)
         + jax.nn.sigmoid(g1_ref[...]) * dot(rw_ref[...], wrw_ref)
         + jax.nn.sigmoid(g2_ref[...]) * sval)
    z_ref[...] = z.astype(_BF)


def _merge_pallas(p2, ml, rw, s5, wml, wrw, wval, wgate):
    n, w = ml.shape
    tm = ROW_TM
    gb = P_GATE // D_MODEL
    row = lambda i: (i, 0)
    return pl.pallas_call(
        _merge_kernel,
        grid=(n // tm,),
        in_specs=[pl.BlockSpec((tm, w), row)] * 2 + [pl.BlockSpec((S5_W // LANES, tm, LANES), lambda i: (0, i, 0))]
                 + [pl.BlockSpec((tm, D_MODEL), functools.partial(lambda j, i: (i, gb + j), j))
                    for j in range(N_BRANCH)]
                 + [_const_spec((w, D_MODEL))] * 4,
        out_specs=pl.BlockSpec((tm, D_MODEL), row),
        out_shape=jax.ShapeDtypeStruct((n, D_MODEL), _BF),
        compiler_params=pltpu.CompilerParams(dimension_semantics=("parallel",), vmem_limit_bytes=VMEM_LIMIT_BYTES),
        name="merge_gate_proj",
    )(ml, rw, s5, p2, p2, p2, wml, wrw, wval, wgate)


def _post_scan_kernel(hf_ref, hb_ref, o_ref, yf_ref, yb_ref, bonus_ref, g_ref, mlg_ref, mlb_ref, rwg_ref, rwb_ref,
                      ones_ref, ml_ref, rw_ref):
    h = hf_ref[0] + hb_ref[0]
    parts = []
    for hd in range(ML_HEADS):
        x = h[:, hd * ML_DH:(hd + 1) * ML_DH]
        xc = x - jnp.mean(x, axis=-1, keepdims=True)
        parts.append(xc * lax.rsqrt(jnp.mean(xc * xc, axis=-1, keepdims=True) + ML_NORM_EPS))
    hn = jnp.concatenate(parts, axis=1) * mlg_ref[...] + mlb_ref[...]
    ml_ref[...] = (jax.nn.sigmoid(o_ref[...]) * hn).astype(_BF)

    def head_mean(x):
        hi = x.astype(_BF)
        lo = (x - hi.astype(jnp.float32)).astype(_BF)
        s = (jnp.dot(hi, ones_ref[...], preferred_element_type=jnp.float32)
             + jnp.dot(lo, ones_ref[...], preferred_element_type=jnp.float32))
        return s * (1.0 / RW_DH)

    y = yf_ref[...] + yb_ref[...]
    yc = y - head_mean(y)
    yn = yc * lax.rsqrt(head_mean(yc * yc) + RW_NORM_EPS) * rwg_ref[...] + rwb_ref[...]
    rw_ref[...] = ((yn + bonus_ref[...]) * g_ref[...]).astype(_BF)


def _post_scan(p2, h_dir, y_f, y_b, bonus, g, ml_norm_g, ml_norm_b, rw_norm_g, rw_norm_b):
    n = y_f.shape[0]
    tm = ROW_TM
    head = jnp.arange(RW_W) // RW_DH
    ones_bd = (head[:, None] == head[None, :]).astype(_BF)
    vec = pl.BlockSpec((1, RW_W), lambda i: (0, 0))
    blk = pl.BlockSpec((tm, RW_W), lambda i: (i, 0))
    return pl.pallas_call(
        _post_scan_kernel,
        grid=(n // tm,),
        in_specs=[pl.BlockSpec((1, tm, ML_W), lambda i: (0, i, 0)), pl.BlockSpec((1, tm, ML_W), lambda i: (1, i, 0)),
                  pl.BlockSpec((tm, ML_W), lambda i: (i, P_O // ML_W)), blk, blk, blk, blk, vec, vec, vec, vec,
                  _const_spec((RW_W, RW_W))],
        out_specs=[blk, blk],
        out_shape=[jax.ShapeDtypeStruct((n, ML_W), _BF), jax.ShapeDtypeStruct((n, RW_W), _BF)],
        compiler_params=pltpu.CompilerParams(dimension_semantics=("parallel",), vmem_limit_bytes=VMEM_LIMIT_BYTES),
        name="post_scan_norm_gate",
    )(h_dir, h_dir, p2, y_f, y_b, bonus, g, ml_norm_g.reshape(1, -1), ml_norm_b.reshape(1, -1),
      rw_norm_g.reshape(1, -1), rw_norm_b.reshape(1, -1), ones_bd)


def _ln_rows(x, eps=LN_EPS):
    mu = jnp.mean(x, axis=-1, keepdims=True)
    xc = x - mu
    var = jnp.mean(xc * xc, axis=-1, keepdims=True)
    return xc * lax.rsqrt(var + eps)


def _resid_kernel(*refs, with_w):
    if with_w:
        d_ref, w_ref, x_ref, gate_ref, g_ref, b_ref, sh_ref, sc_ref, xo_ref, uo_ref = refs
        delta = jnp.dot(d_ref[...], w_ref[...], preferred_element_type=jnp.float32)
    else:
        d_ref, x_ref, gate_ref, g_ref, b_ref, sh_ref, sc_ref, xo_ref, uo_ref = refs
        delta = d_ref[...]
    xn = _ln_rows(DEEPNORM_ALPHA * x_ref[...] + gate_ref[0, 0, 0] * delta) * g_ref[...] + b_ref[...]
    xo_ref[...] = xn
    uo_ref[...] = (_ln_rows(xn) * (1.0 + sc_ref[0, 0, 0]) + sh_ref[0, 0, 0]).astype(uo_ref.dtype)


def _resid_norm_mod(delta, w, x, mod_a, ia, ln_g, ln_b, mod_b, ish, isc, blocks_per_seq, ctx_blocks,
                    u_dtype=jnp.float32):
    n, d = x.shape
    tm = ROW_TM
    row = lambda i: (i, 0)

    def mod_spec(m):
        return pl.BlockSpec((1, 1, 1, 1, d), lambda i: (i // blocks_per_seq,
                                                       (i % blocks_per_seq >= ctx_blocks).astype(jnp.int32), m, 0, 0))

    vec = pl.BlockSpec((1, d), lambda i: (0, 0))
    if w is not None:
        in_specs = [pl.BlockSpec((tm, delta.shape[1]), row), _const_spec(w.shape), pl.BlockSpec((tm, d), row)]
        args = (delta, w, x)
    else:
        in_specs = [pl.BlockSpec((tm, d), row), pl.BlockSpec((tm, d), row)]
        args = (delta, x)
    return pl.pallas_call(
        functools.partial(_resid_kernel, with_w=w is not None),
        grid=(n // tm,),
        in_specs=in_specs + [mod_spec(ia), vec, vec, mod_spec(ish), mod_spec(isc)],
        out_specs=[pl.BlockSpec((tm, d), row)] * 2,
        out_shape=[jax.ShapeDtypeStruct((n, d), jnp.float32), jax.ShapeDtypeStruct((n, d), u_dtype)],
        compiler_params=pltpu.CompilerParams(dimension_semantics=("parallel",), vmem_limit_bytes=VMEM_LIMIT_BYTES),
        name="resid_norm_mod",
    )(*args, mod_a, ln_g.reshape(1, d), ln_b.reshape(1, d), mod_b, mod_b)


def _moe_ffn_kernel(te_ref, tv_ref, x_ref, wg_ref, wu_ref, wd_ref, y_ref, wgb_ref, wub_ref, wdb_ref):
    t = pl.program_id(0)

    @pl.when((t == 0) | (te_ref[t] != te_ref[jnp.maximum(t - 1, 0)]))
    def _():
        wgb_ref[...] = wg_ref[0, 0].astype(_BF)
        wub_ref[...] = wu_ref[0, 0].astype(_BF)
        wdb_ref[...] = wd_ref[0, 0].astype(_BF)

    @pl.when(tv_ref[t] == 1)
    def _():
        x = x_ref[...].astype(_BF)
        hg = jnp.dot(x, wgb_ref[...], preferred_element_type=jnp.float32)
        hu = jnp.dot(x, wub_ref[...], preferred_element_type=jnp.float32)
        h = (hg * jax.nn.sigmoid(hg)) * hu
        y_ref[...] = jnp.dot(h.astype(_BF), wdb_ref[...], preferred_element_type=jnp.float32)

    @pl.when(tv_ref[t] == 0)
    def _():
        y_ref[...] = jnp.zeros_like(y_ref)


def _moe_routed(u, gates_t, layer, wg, wu, wd):
    n_tok, d = u.shape
    _, n_e, _, d_e = wg.shape
    tm = MOE_TM
    n_tiles = (TOP_K * n_tok) // tm + n_e
    n_slots = n_tiles * tm
    sel = gates_t > 0.0
    seli = sel.astype(jnp.int32)
    rank = jnp.cumsum(seli, axis=1) - 1
    cnt = jnp.sum(seli, axis=1)
    tiles_e = (cnt + tm - 1) // tm
    tile_end = jnp.cumsum(tiles_e)
    off = (tile_end - tiles_e) * tm
    slot = off[:, None] + rank
    order = jnp.cumsum(seli, axis=0)
    slots, owned, gsel = [], [], []
    for j in range(TOP_K):
        pick = sel & (order == j + 1)
        slots.append(jnp.sum(jnp.where(pick, slot, 0), axis=0))
        owned.append(jnp.where(jnp.any(pick, axis=0), slots[j], n_slots))
        gsel.append(jnp.sum(jnp.where(pick, gates_t, 0.0), axis=0))
    tok = jnp.arange(n_tok, dtype=jnp.int32)
    tok_of_slot = jnp.zeros((n_slots,), jnp.int32).at[jnp.concatenate(owned)].set(jnp.tile(tok, TOP_K), mode='drop')
    tile_ids = jnp.arange(n_tiles, dtype=jnp.int32)
    tile_valid = (tile_ids < tile_end[-1]).astype(jnp.int32)
    tile_expert = jnp.minimum(jnp.searchsorted(tile_end, tile_ids, side='right'), n_e - 1).astype(jnp.int32)
    last_e = jnp.max(jnp.where(cnt > 0, jnp.arange(n_e), 0)).astype(jnp.int32)
    tile_expert = jnp.where(tile_valid == 1, tile_expert, last_e)
    xs = u.astype(_BF).at[tok_of_slot].get(mode='promise_in_bounds')
    w_index = lambda t, te, tv: (layer, te[t], 0, 0)
    grid_spec = pltpu.PrefetchScalarGridSpec(
        num_scalar_prefetch=2,
        grid=(n_tiles,),
        in_specs=[pl.BlockSpec((tm, d), lambda t, te, tv: (t, 0)),
                  pl.BlockSpec((1, 1, d, d_e), w_index, pipeline_mode=pl.Buffered(1)),
                  pl.BlockSpec((1, 1, d, d_e), w_index, pipeline_mode=pl.Buffered(1)),
                  pl.BlockSpec((1, 1, d_e, d), w_index, pipeline_mode=pl.Buffered(1))],
        out_specs=pl.BlockSpec((tm, d), lambda t, te, tv: (t, 0)),
        scratch_shapes=[pltpu.VMEM((d, d_e), _BF), pltpu.VMEM((d, d_e), _BF), pltpu.VMEM((d_e, d), _BF)])
    ys = pl.pallas_call(
        _moe_ffn_kernel, grid_spec=grid_spec,
        out_shape=jax.ShapeDtypeStruct((n_slots, d), jnp.float32),
        compiler_params=pltpu.CompilerParams(dimension_semantics=("arbitrary",),
                                             vmem_limit_bytes=VMEM_LIMIT_BYTES),
        name="moe_routed_ffn",
    )(tile_expert, tile_valid, xs, wg, wu, wd)
    return sum(gsel[j][:, None] * ys.at[slots[j]].get(mode='promise_in_bounds') for j in range(TOP_K))


def _router_kernel(u_ref, wt_ref, b_ref, g_ref):
    logits = lax.dot_general(wt_ref[...], u_ref[...], _NT, precision=_HI, preferred_element_type=jnp.float32)
    aff = jax.nn.sigmoid(logits)
    score = aff + b_ref[...]
    s = [score[e:e + 1] for e in range(N_EXPERTS)]
    a = [aff[e:e + 1] for e in range(N_EXPERTS)]
    gs = []
    for g in range(N_GROUPS):
        m = s[g * EXPERTS_PER_GROUP:(g + 1) * EXPERTS_PER_GROUP]
        best = None
        for i in range(EXPERTS_PER_GROUP):
            for j in range(i + 1, EXPERTS_PER_GROUP):
                best = m[i] + m[j] if best is None else jnp.maximum(best, m[i] + m[j])
        gs.append(best)
    best_val = gs[0]
    best_grp = jnp.zeros_like(gs[0], dtype=jnp.int32)
    for g in range(1, N_GROUPS):
        better = gs[g] > best_val
        best_grp = jnp.where(better, g, best_grp)
        best_val = jnp.where(better, gs[g], best_val)
    sel = []
    for e in range(N_EXPERTS):
        g = e // EXPERTS_PER_GROUP
        rank = jnp.zeros_like(best_grp)
        for j in range(g * EXPERTS_PER_GROUP, (g + 1) * EXPERTS_PER_GROUP):
            if j != e:
                ahead = (s[j] > s[e]) | ((s[j] == s[e]) & (j < e))
                rank = rank + ahead.astype(jnp.int32)
        sel.append((best_grp == g) & (rank < TOP_K))
    wsum = sum(jnp.where(sel[e], a[e], 0.0) for e in range(N_EXPERTS))
    g_ref[...] = jnp.concatenate([jnp.where(sel[e], a[e] / wsum, 0.0) for e in range(N_EXPERTS)], axis=0)


def _router_pallas(u, router_w, router_b, tm=512):
    n_tok, d = u.shape
    assert n_tok % tm == 0
    return pl.pallas_call(
        _router_kernel,
        grid=(n_tok // tm,),
        in_specs=[pl.BlockSpec((tm, d), lambda i: (i, 0)),
                  pl.BlockSpec((N_EXPERTS, d), lambda i: (0, 0)),
                  pl.BlockSpec((N_EXPERTS, 1), lambda i: (0, 0))],
        out_specs=pl.BlockSpec((N_EXPERTS, tm), lambda i: (0, i)),
        out_shape=jax.ShapeDtypeStruct((N_EXPERTS, n_tok), jnp.float32),
        compiler_params=pltpu.CompilerParams(dimension_semantics=("parallel",),
                                             vmem_limit_bytes=VMEM_LIMIT_BYTES),
        name="moe_router",
    )(u, router_w.T, router_b.reshape(N_EXPERTS, 1))


def _s5_mats(lam_re, lam_im, log_dt, b_re, b_im, c_re, c_im):
    L = S5_CHUNK
    lam = lax.complex(lam_re, lam_im)
    ldt = lam * jnp.exp(log_dt)[..., None]
    lam_bar = jnp.exp(ldt)
    b_bar = ((lam_bar - 1.0) / lam)[..., None] * lax.complex(b_re, b_im)
    c_mat = lax.complex(c_re, c_im)
    tau = jnp.arange(L + 1, dtype=jnp.float32)
    pw = jnp.exp(ldt[:, :, None, :] * tau[None, None, :, None])
    kern = jnp.real(jnp.einsum('dgon,dgtn,dgni->dgtoi', c_mat, pw[:, :, :L], b_bar))
    s_idx = jnp.arange(L)[:, None]
    t_idx = jnp.arange(L)[None, :]

    def toeplitz(k, lag, valid):
        m = k[:, jnp.clip(lag, 0, L - 1)] * valid[None, :, :, None, None]
        return jnp.transpose(m, (0, 1, 4, 2, 3)).reshape(-1, L * S5_GROUP, L * S5_GROUP)

    tsum = (toeplitz(kern[0], t_idx - s_idx, (t_idx >= s_idx).astype(jnp.float32))
            + toeplitz(kern[1], s_idx - t_idx, (s_idx >= t_idx).astype(jnp.float32)))
    pin_f = pw[0][:, L - 1 - jnp.arange(L)]
    pin_b = pw[1][:, jnp.arange(L)]
    in_f = jnp.einsum('gsn,gni->gsin', pin_f, b_bar[0]).reshape(-1, L * S5_GROUP, S5_STATE)
    in_b = jnp.einsum('gsn,gni->gsin', pin_b, b_bar[1]).reshape(-1, L * S5_GROUP, S5_STATE)
    icat = jnp.concatenate([jnp.real(in_f), jnp.imag(in_f), jnp.real(in_b), jnp.imag(in_b)], axis=-1)
    pout_f = pw[0][:, 1 + jnp.arange(L)]
    pout_b = pw[1][:, L - jnp.arange(L)]
    out_f = jnp.einsum('gon,gtn->gnto', c_mat[0], pout_f).reshape(-1, S5_STATE, L * S5_GROUP)
    out_b = jnp.einsum('gon,gtn->gnto', c_mat[1], pout_b).reshape(-1, S5_STATE, L * S5_GROUP)
    ocat = jnp.concatenate([jnp.real(out_f), -jnp.imag(out_f), jnp.real(out_b), -jnp.imag(out_b)], axis=1)
    lam_l = pw[:, :, L]
    lam_chunk = jnp.stack([jnp.real(lam_l[0]), jnp.imag(lam_l[0]), jnp.real(lam_l[1]), jnp.imag(lam_l[1])],
                          axis=1)
    return tsum, icat, ocat, lam_chunk


def _s5_kernel(u0_ref, u1_ref, t_ref, i_ref, o_ref, lam_ref, d_ref, y0_ref, y1_ref,
               v_ref, xfr_ref, xfi_ref, xbr_ref, xbi_ref, *, n_chunks, ctx_chunks, bsz):
    n = S5_STATE
    u = jnp.concatenate([u0_ref[0], u1_ref[0]], axis=1)
    ub = u.astype(_BF)
    v_ref[...] = jnp.dot(ub, i_ref[0], preferred_element_type=jnp.float32)
    lam = lam_ref[0]
    lfr = jnp.broadcast_to(lam[0:1], (bsz, n))
    lfi = jnp.broadcast_to(lam[1:2], (bsz, n))
    lbr = jnp.broadcast_to(lam[2:3], (bsz, n))
    lbi = jnp.broadcast_to(lam[3:4], (bsz, n))

    def cmul_add(lr, li, xr, xi, vr, vi):
        return lr * xr - li * xi + vr, lr * xi + li * xr + vi

    def step(j, carry):
        fr, fi, br, bi = carry
        rf = pl.multiple_of(j * (2 * bsz), 2 * bsz)
        pb = jnp.where(j < ctx_chunks // 2, ctx_chunks // 2 - 1 - j, (n_chunks + ctx_chunks) // 2 - 1 - j)
        rb = pl.multiple_of(pb * (2 * bsz), 2 * bsz)
        vf = v_ref[pl.ds(rf, 2 * bsz), :]
        vb = v_ref[pl.ds(rb, 2 * bsz), :]
        fr1, fi1 = cmul_add(lfr, lfi, fr, fi, vf[:bsz, 0:n], vf[:bsz, n:2 * n])
        fr2, fi2 = cmul_add(lfr, lfi, fr1, fi1, vf[bsz:, 0:n], vf[bsz:, n:2 * n])
        br1, bi1 = cmul_add(lbr, lbi, br, bi, vb[bsz:, 2 * n:3 * n], vb[bsz:, 3 * n:4 * n])
        br2, bi2 = cmul_add(lbr, lbi, br1, bi1, vb[:bsz, 2 * n:3 * n], vb[:bsz, 3 * n:4 * n])
        xfr_ref[pl.ds(rf, 2 * bsz), :] = jnp.concatenate([fr, fr1], axis=0)
        xfi_ref[pl.ds(rf, 2 * bsz), :] = jnp.concatenate([fi, fi1], axis=0)
        xbr_ref[pl.ds(rb, 2 * bsz), :] = jnp.concatenate([br1, br], axis=0)
        xbi_ref[pl.ds(rb, 2 * bsz), :] = jnp.concatenate([bi1, bi], axis=0)
        return fr2, fi2, br2, bi2

    z = jnp.zeros((bsz, n), jnp.float32)
    lax.fori_loop(0, n_chunks // 2, step, (z, z, z, z), unroll=S5_UNROLL)
    o = o_ref[0]
    y = jnp.dot(ub, t_ref[0], preferred_element_type=jnp.float32)
    y += jnp.dot(xfr_ref[...].astype(_BF), o[0:n], preferred_element_type=jnp.float32)
    y += jnp.dot(xfi_ref[...].astype(_BF), o[n:2 * n], preferred_element_type=jnp.float32)
    y += jnp.dot(xbr_ref[...].astype(_BF), o[2 * n:3 * n], preferred_element_type=jnp.float32)
    y += jnp.dot(xbi_ref[...].astype(_BF), o[3 * n:4 * n], preferred_element_type=jnp.float32)
    y += d_ref[0] * u
    y = 0.5 * y * (1.0 + jnp.tanh(math.sqrt(2.0 / math.pi) * (y + 0.044715 * (y * y * y))))
    y0_ref[0] = y[:, :LANES]
    y1_ref[0] = y[:, LANES:]


def _to_groups_kernel(*refs, bsz):
    L, C = S5_CHUNK, S5_GROUP
    nch = RL_TBLK // L
    gpt = LANES // C
    lane = lax.broadcasted_iota(jnp.int32, (nch, LANES), 1) // C
    x_refs, o_refs = refs[:S5_W // LANES], refs[S5_W // LANES:]
    for b in range(bsz):
        for m in range(S5_W // LANES):
            src = [x_refs[m][b, pl.ds(j, nch, stride=L), :] for j in range(L)]
            for gl in range(gpt):
                g = m * gpt + gl
                for q in range(L * C // LANES):
                    acc = jnp.zeros((nch, LANES), jnp.float32)
                    for jl in range(gpt):
                        j = q * gpt + jl
                        moved = src[j] if jl == gl else pltpu.roll(src[j], ((jl - gl) * C) % LANES, axis=1)
                        acc = jnp.where(lane == jl, moved, acc)
                    o_refs[q][g, pl.ds(b, nch, stride=bsz), :] = acc


def _from_groups_kernel(y0_ref, y1_ref, o_ref, *, bsz):
    L, C = S5_CHUNK, S5_GROUP
    nch = RL_TBLK // L
    gpt = LANES // C
    lane = lax.broadcasted_iota(jnp.int32, (nch, LANES), 1) // C
    y_refs = (y0_ref, y1_ref)
    for b in range(bsz):
        for m in range(S5_W // LANES):
            for q in range(L * C // LANES):
                src = [y_refs[q][m * gpt + gl, pl.ds(b, nch, stride=bsz), :] for gl in range(gpt)]
                for jl in range(gpt):
                    acc = jnp.zeros((nch, LANES), jnp.float32)
                    for gl in range(gpt):
                        moved = src[gl] if jl == gl else pltpu.roll(src[gl], ((gl - jl) * C) % LANES, axis=1)
                        acc = jnp.where(lane == gl, moved, acc)
                    o_ref[m, b, pl.ds(q * gpt + jl, nch, stride=L), :] = acc


def _s5_pallas(p3, n_ctx, lam_re, lam_im, log_dt, b_re, b_im, c_re, c_im, d_skip):
    bsz, t_len, _ = p3.shape
    L, G, C = S5_CHUNK, S5_GROUPS, S5_GROUP
    nc = t_len // L
    assert (2 * bsz) % 8 == 0 and nc % 2 == 0 and (n_ctx // L) % 2 == 0 and t_len % RL_TBLK == 0
    assert L * C == 2 * LANES and P_S5 % LANES == 0
    tsum, icat, ocat, lam_chunk = _s5_mats(lam_re, lam_im, log_dt, b_re, b_im, c_re, c_im)
    dvec = jnp.tile(d_skip.reshape(G, 1, C), (1, L, 1)).reshape(G, 1, L * C)
    rows = nc * bsz
    n_lt = S5_W // LANES
    nch = RL_TBLK // L
    relayout_params = pltpu.CompilerParams(dimension_semantics=("parallel",), vmem_limit_bytes=VMEM_LIMIT_BYTES)
    half_shape = jax.ShapeDtypeStruct((G, rows, LANES), jnp.float32)
    half_blk = pl.BlockSpec((G, nch * bsz, LANES), lambda i: (0, i, 0))
    u0, u1 = pl.pallas_call(
        functools.partial(_to_groups_kernel, bsz=bsz),
        grid=(t_len // RL_TBLK,),
        in_specs=[pl.BlockSpec((bsz, RL_TBLK, LANES), functools.partial(lambda m, i: (0, i, P_S5 // LANES + m), m))
                  for m in range(n_lt)],
        out_specs=[half_blk, half_blk],
        out_shape=[half_shape, half_shape],
        compiler_params=relayout_params, name="s5_to_groups",
    )(*([p3] * n_lt))
    wspec = pl.BlockSpec((1, L * C, L * C), lambda g: (g, 0, 0))
    half = pl.BlockSpec((1, rows, LANES), lambda g: (g, 0, 0))
    y0, y1 = pl.pallas_call(
        functools.partial(_s5_kernel, n_chunks=nc, ctx_chunks=n_ctx // L, bsz=bsz),
        grid=(G,),
        in_specs=[half, half, wspec, wspec, wspec,
                  pl.BlockSpec((1, 4, S5_STATE), lambda g: (g, 0, 0)),
                  pl.BlockSpec((1, 1, L * C), lambda g: (g, 0, 0))],
        out_specs=[half, half],
        out_shape=[half_shape, half_shape],
        scratch_shapes=[pltpu.VMEM((rows, 4 * S5_STATE), jnp.float32)]
                       + [pltpu.VMEM((rows, S5_STATE), jnp.float32)] * 4,
        compiler_params=pltpu.CompilerParams(dimension_semantics=("parallel",),
                                             vmem_limit_bytes=VMEM_LIMIT_BYTES),
        name="s5_scan",
    )(u0, u1, tsum.astype(_BF), icat.astype(_BF), ocat.astype(_BF), lam_chunk, dvec)
    return pl.pallas_call(
        functools.partial(_from_groups_kernel, bsz=bsz),
        grid=(t_len // RL_TBLK,),
        in_specs=[half_blk, half_blk],
        out_specs=pl.BlockSpec((n_lt, bsz, RL_TBLK, LANES), lambda i: (0, 0, i, 0)),
        out_shape=jax.ShapeDtypeStruct((n_lt, bsz, t_len, LANES), jnp.float32),
        compiler_params=relayout_params, name="s5_from_groups",
    )(y0, y1)


def _log_sigmoid(x):
    return jnp.minimum(x, 0.0) - jnp.log(1.0 + jnp.exp(-jnp.abs(x)))


def _mlstm_kernel(igb_ref, fgb_ref, q_ref, k_ref, v_ref, gc_ref, gr_ref, h_ref, cmat_ref, nvec_ref, m_ref):
    d = pl.program_id(0)
    L = ML_CHUNK
    n_sub = ML_TBLK // L
    H = range(ML_HEADS)

    @pl.when(pl.program_id(2) == 0)
    def _():
        cmat_ref[...] = jnp.zeros_like(cmat_ref)
        nvec_ref[...] = jnp.zeros_like(nvec_ref)
        m_ref[...] = jnp.zeros_like(m_ref)

    igb = [igb_ref[d, h] for h in H]
    fgb = [fgb_ref[d, h] for h in H]
    row = lax.broadcasted_iota(jnp.int32, (L, L), 0)
    col = lax.broadcasted_iota(jnp.int32, (L, L), 1)
    sign = 1 - 2 * d
    seen = (row - col) * sign >= 0
    seen_f = seen.astype(jnp.float32)
    seen_t = ((col - row) * sign >= 0).astype(jnp.float32)
    scale = ML_DH ** -0.5

    def chunk(jj, carry):
        cj = jnp.where(d == 0, jj, n_sub - 1 - jj)
        r0 = pl.multiple_of(cj * L, L)
        hs = lambda h: slice(h * ML_DH, (h + 1) * ML_DH)
        q = [q_ref[0, pl.ds(r0, L), hs(h)] for h in H]
        k = [k_ref[0, pl.ds(r0, L), hs(h)] * scale for h in H]
        vb = [v_ref[0, pl.ds(r0, L), hs(h)].astype(_BF) for h in H]
        gc = [gc_ref[0, 0, h, pl.ds(r0, L), :] for h in H]
        gr = [gr_ref[0, 0, h, cj] for h in H]
        li_col = [gc[h][:, 0:1] + igb[h] for h in H]
        lf_col = [_log_sigmoid(gc[h][:, 1:2] + fgb[h]) for h in H]
        li_row = [gr[h][0:1, :] + igb[h] for h in H]
        lf_row = [_log_sigmoid(gr[h][1:2, :] + fgb[h]) for h in H]
        m_prev = [m_ref[h] for h in H]
        bcum_col = [jnp.dot(seen_f, jnp.broadcast_to(lf_col[h], (L, L)), precision=_HI,
                            preferred_element_type=jnp.float32) for h in H]
        bcum_row = [jnp.dot(jnp.broadcast_to(lf_row[h], (8, L)), seen_t, precision=_HI,
                            preferred_element_type=jnp.float32)[0:1] for h in H]
        qb = [q[h].astype(_BF) for h in H]
        qk = [lax.dot_general(qb[h], k[h].astype(_BF), _NT, preferred_element_type=jnp.float32) for h in H]
        qc = [jnp.dot(qb[h], cmat_ref[h].astype(_BF), preferred_element_type=jnp.float32) for h in H]
        log_d = [jnp.where(seen, bcum_col[h] - bcum_row[h] + li_row[h], -jnp.inf) for h in H]
        inter = [bcum_col[h][:, 0:1] + m_prev[h] for h in H]
        m_j = [jnp.maximum(jnp.max(log_d[h], axis=1, keepdims=True), inter[h]) for h in H]
        scores = [qk[h] * jnp.exp(log_d[h] - m_j[h]) for h in H]
        s_inter = [jnp.exp(inter[h] - m_j[h]) for h in H]
        sv = [jnp.dot(scores[h].astype(_BF), vb[h], preferred_element_type=jnp.float32) for h in H]
        b_last = [jnp.sum(lf_col[h], axis=0, keepdims=True) for h in H]
        log_w = [b_last[h] - bcum_col[h][:, 0:1] + li_col[h] for h in H]
        m_new = [jnp.maximum(b_last[h] + m_prev[h], jnp.max(log_w[h], axis=0, keepdims=True)) for h in H]
        kw = [k[h] * jnp.exp(log_w[h] - m_new[h]) for h in H]
        decay = [jnp.exp(b_last[h] + m_prev[h] - m_new[h]) for h in H]
        kv = [lax.dot_general(kw[h].astype(_BF), vb[h], _TN, preferred_element_type=jnp.float32) for h in H]
        for h in H:
            num = sv[h] + s_inter[h] * qc[h]
            den = (jnp.sum(scores[h], axis=1, keepdims=True)
                   + s_inter[h] * jnp.sum(q[h] * nvec_ref[h], axis=1, keepdims=True))
            h_ref[0, 0, pl.ds(r0, L), hs(h)] = num / jnp.maximum(jnp.abs(den), jnp.exp(-m_j[h]))
        for h in H:
            cmat_ref[h] = decay[h] * cmat_ref[h] + kv[h]
            nvec_ref[h] = decay[h] * nvec_ref[h] + jnp.sum(kw[h], axis=0, keepdims=True)
            m_ref[h] = m_new[h]
        return carry

    lax.fori_loop(0, n_sub, chunk, 0)


def _mlstm_pallas(qk, p3, gates, ig_b, fg_b, n_ctx):
    bsz, t_len, _ = qk.shape
    nb = t_len // ML_TBLK
    cb = n_ctx // ML_TBLK
    assert t_len % ML_TBLK == 0 and n_ctx % ML_TBLK == 0
    g = gates.reshape(bsz, t_len, 2, N_DIR, ML_HEADS)
    gcol = jnp.transpose(g, (3, 0, 4, 1, 2))
    grow = jnp.transpose(g.reshape(bsz, t_len // ML_CHUNK, ML_CHUNK, 2, N_DIR, ML_HEADS),
                         (4, 0, 5, 1, 3, 2))
    blk = lambda d, i: _scan_block(d, i, nb, cb)
    grid_spec = pltpu.PrefetchScalarGridSpec(
        num_scalar_prefetch=2,
        grid=(N_DIR, bsz, nb),
        in_specs=[pl.BlockSpec((1, ML_TBLK, ML_W), lambda d, b, i, *_: (b, blk(d, i), 0)),
                  pl.BlockSpec((1, ML_TBLK, ML_W), lambda d, b, i, *_: (b, blk(d, i), 1)),
                  pl.BlockSpec((1, ML_TBLK, ML_W), lambda d, b, i, *_: (b, blk(d, i), P_V // ML_W))] + [
            pl.BlockSpec((1, 1, ML_HEADS, ML_TBLK, 2), lambda d, b, i, *_: (d, b, 0, blk(d, i), 0)),
            pl.BlockSpec((1, 1, ML_HEADS, ML_TBLK // ML_CHUNK, 2, ML_CHUNK),
                         lambda d, b, i, *_: (d, b, 0, blk(d, i), 0, 0))],
        out_specs=pl.BlockSpec((1, 1, ML_TBLK, ML_W), lambda d, b, i, *_: (d, b, blk(d, i), 0)),
        scratch_shapes=[pltpu.VMEM((ML_HEADS, ML_DH, ML_DH), jnp.float32),
                        pltpu.VMEM((ML_HEADS, 1, ML_DH), jnp.float32),
                        pltpu.VMEM((ML_HEADS, 1, 1), jnp.float32)])
    return pl.pallas_call(
        _mlstm_kernel, grid_spec=grid_spec,
        out_shape=jax.ShapeDtypeStruct((N_DIR, bsz, t_len, ML_W), jnp.float32),
        compiler_params=pltpu.CompilerParams(
            dimension_semantics=("parallel", "parallel", "arbitrary"),
            vmem_limit_bytes=VMEM_LIMIT_BYTES),
        name="mlstm_scan",
    )(ig_b, fg_b, qk, qk, p3, gcol, grow)


def _to_pairs(x):
    return jnp.concatenate([x[:, p * LANES:(p + 1) * LANES] for p in range(RW_PAIRS)], axis=0)


def _dotf(a, b, dims=None):
    a = a.astype(_BF)
    b = b.astype(_BF)
    if dims is None:
        return jnp.dot(a, b, preferred_element_type=jnp.float32)
    return lax.dot_general(a, b, dims, preferred_element_type=jnp.float32)


def _rwkv_a_kernel(r_ref, k_ref, v_ref, kk_ref, a_ref, lw_ref,
                   att_ref, rt_ref, bw_ref, kw_ref, vt_ref, u0t_ref, y0_ref, wc_ref, *, n_sub):
    d = pl.program_id(0)
    C, R = RW_CHUNK, RW_ROWS
    row = lax.broadcasted_iota(jnp.int32, (R, R), 0)
    col = lax.broadcasted_iota(jnp.int32, (R, R), 1)
    same = (row // C) == (col // C)
    sign = 1 - 2 * d
    before = same & ((row - col) * sign > 0)
    upto = same & ((row - col) * sign >= 0)
    upto2 = jnp.concatenate([upto, upto], axis=1)
    eye = (row == col).astype(jnp.float32)
    first = col < RW_DH
    tpos = row % C

    def chunk_group(jg, carry):
        js = [jg * RW_GROUP + i for i in range(RW_GROUP)]
        G2 = [(i, h2) for i in range(RW_GROUP) for h2 in range(2)]
        pick = lambda lst, off: [jnp.where(first, lst[2 * i][:, off:off + R], lst[2 * i + 1][:, off:off + R])
                                 for i in range(RW_GROUP)]
        r0s = [pl.multiple_of(j * C, C) for j in js]
        ld = lambda ref: [_to_pairs(ref[0, pl.ds(r0, C), :]) for r0 in r0s]
        ldd = lambda ref: [_to_pairs(ref[0, 0, pl.ds(r0, C), :]) for r0 in r0s]
        r, v, kk = ld(r_ref), ld(v_ref), ld(kk_ref)
        k, a, lw = ldd(k_ref), ldd(a_ref), ldd(lw_ref)
        fwd = d == 0
        cum, aft = [], []
        for i in range(RW_GROUP):
            pre = lw[i]
            suf = lw[i]
            for s in (1, 2, 4, 8):
                pre = pre + jnp.where(tpos >= s, pltpu.roll(pre, s, axis=0), 0.0)
                suf = suf + jnp.where(tpos < C - s, pltpu.roll(suf, R - s, axis=0), 0.0)
            cum.append(jnp.where(fwd, pre, suf))
            aft.append(jnp.where(fwd, suf, pre) - lw[i])
        a_hat = [-kk[i] * jnp.exp(cum[i] - lw[i]) for i in range(RW_GROUP)]
        r_hat = [r[i] * jnp.exp(cum[i]) for i in range(RW_GROUP)]
        vb = [v[i].astype(_BF) for i in range(RW_GROUP)]
        m = []
        for i in range(RW_GROUP):
            inv_w = jnp.exp(-cum[i])
            lhs = jnp.concatenate([jnp.where(first, a_hat[i], 0.0), jnp.where(first, 0.0, a_hat[i]),
                                   jnp.where(first, r_hat[i], 0.0), jnp.where(first, 0.0, r_hat[i])], axis=0)
            rhs = jnp.concatenate([kk[i] * a[i] * inv_w, k[i] * inv_w], axis=0)
            m.append(_dotf(lhs, rhs, _NT))
        x = [jnp.where(before, m[i][h2 * R:(h2 + 1) * R, 0:R], 0.0).astype(_BF) for i, h2 in G2]
        ak = [jnp.where(before, m[i][h2 * R:(h2 + 1) * R, R:2 * R], 0.0) for i, h2 in G2]
        rbk = [jnp.where(upto2, m[i][(2 + h2) * R:(3 + h2) * R, :], 0.0).astype(_BF) for i, h2 in G2]
        akv = [_dotf(ak[g], vb[g // 2]) for g in range(len(G2))]
        x2 = [_dotf(xx, xx).astype(_BF) for xx in x]
        x4 = [_dotf(xx, xx).astype(_BF) for xx in x2]
        x8 = [_dotf(xx, xx).astype(_BF) for xx in x4]
        t = [eye + xx.astype(jnp.float32) for xx in x]
        t = [t[g] + _dotf(t[g], x2[g]) for g in range(len(G2))]
        t = [t[g] + _dotf(t[g], x4[g]) for g in range(len(G2))]
        t = [t[g] + _dotf(t[g], x8[g]) for g in range(len(G2))]
        akv = pick(akv, 0)
        rhs2 = [jnp.concatenate([a_hat[i], akv[i]], axis=1).astype(_BF) for i in range(RW_GROUP)]
        ta = [_dotf(t[g], rhs2[g // 2]) for g in range(len(G2))]
        at = pick(ta, 0)
        u0 = pick(ta, R)
        rhs3 = [jnp.concatenate([jnp.concatenate([at[i], u0[i]], axis=1).astype(_BF),
                                 jnp.concatenate([jnp.zeros_like(vb[i]), vb[i]], axis=1)], axis=0)
                for i in range(RW_GROUP)]
        ry = [_dotf(rbk[g], rhs3[g // 2]) for g in range(len(G2))]
        rt = pick(ry, 0)
        y0 = pick(ry, R)
        for i, j in enumerate(js):
            w_aft = jnp.exp(aft[i])
            att_ref[0, 0, j] = at[i].T.astype(_BF)
            u0t_ref[0, 0, j] = u0[i].T
            vt_ref[0, 0, j] = v[i].T.astype(_BF)
            rt_ref[0, 0, j] = (r_hat[i] + rt[i]).astype(_BF)
            bw_ref[0, 0, j] = (kk[i] * a[i] * w_aft).astype(_BF)
            kw_ref[0, 0, j] = (k[i] * w_aft).astype(_BF)
            y0_ref[0, 0, j] = y0[i]
            tot = cum[i] + aft[i]
            wc_ref[0, 0, j] = jnp.exp(jnp.concatenate([tot[p * C:p * C + 1] for p in range(RW_PAIRS)], axis=0))
        return carry

    lax.fori_loop(0, n_sub // RW_GROUP, chunk_group, 0)


def _rwkv_b_kernel(*refs, n_sub):
    ins, (yf_ref, yb_ref, s_ref) = refs[:16], refs[16:]
    C = RW_CHUNK

    @pl.when(pl.program_id(1) == 0)
    def _():
        s_ref[...] = jnp.zeros_like(s_ref)

    row = lax.broadcasted_iota(jnp.int32, (LANES, LANES), 0)
    col = lax.broadcasted_iota(jnp.int32, (LANES, LANES), 1)
    diag = (row // RW_DH) == (col // RW_DH)
    pair_of_row = lax.broadcasted_iota(jnp.int32, (2 * RW_ROWS, LANES), 0) % RW_ROWS // C
    DP = [(d, p) for d in range(N_DIR) for p in range(RW_PAIRS)]

    def chunk(jj, carry):
        cjs = [jj, n_sub - 1 - jj]
        att, rt, bw, kw, vt, u0t, y0, wc = [[ins[8 * d + a][0, 0, cjs[d]] for d in range(N_DIR)] for a in range(8)]
        bk = [jnp.concatenate([bw[d], kw[d]], axis=0) for d in range(N_DIR)]
        sps = [s_ref[d, p] for d, p in DP]
        spb = [sp.astype(_BF) for sp in sps]
        uts = [jnp.dot(spb[i], att[d], preferred_element_type=jnp.float32) + u0t[d] for i, (d, p) in enumerate(DP)]
        for i, (d, p) in enumerate(DP):
            rows = slice(p * C, (p + 1) * C)
            y = lax.dot_general(rt[d][rows], spb[i], _NT, preferred_element_type=jnp.float32) + y0[d][rows]
            y_ref = yf_ref if d == 0 else yb_ref
            y_ref[0, pl.ds(pl.multiple_of(cjs[d] * C, C), C), p * LANES:(p + 1) * LANES] = y
        for i, (d, p) in enumerate(DP):
            lhs = jnp.concatenate([uts[i].astype(_BF), vt[d]], axis=1)
            rhs = jnp.where(pair_of_row == p, bk[d], jnp.zeros_like(bk[d]))
            upd = jnp.dot(lhs, rhs, preferred_element_type=jnp.float32)
            s_ref[d, p] = jnp.where(diag, wc[d][p:p + 1, :] * sps[i] + upd, 0.0)
        return carry

    lax.fori_loop(0, n_sub, chunk, 0)


def _rwkv_pallas(r, v, kk, k_dir, a_dir, lw_dir, n_ctx):
    bsz, t_len, _ = r.shape
    nb, cb = t_len // RW_TBLK, n_ctx // RW_TBLK
    assert t_len % RW_TBLK == 0 and n_ctx % RW_TBLK == 0
    n_sub = RW_TBLK // RW_CHUNK
    nc = t_len // RW_CHUNK
    sh_spec = pl.BlockSpec((1, RW_TBLK, RW_W), lambda d, b, i: (b, _scan_block(d, i, nb, cb), 0))
    dr_spec = pl.BlockSpec((1, 1, RW_TBLK, RW_W), lambda d, b, i: (d, b, _scan_block(d, i, nb, cb), 0))
    ch_spec = pl.BlockSpec((1, 1, n_sub, RW_ROWS, LANES), lambda d, b, i: (d, b, _scan_block(d, i, nb, cb), 0, 0))
    wc_spec = pl.BlockSpec((1, 1, n_sub, RW_PAIRS, LANES), lambda d, b, i: (d, b, _scan_block(d, i, nb, cb), 0, 0))
    ch_shape = lambda dt: jax.ShapeDtypeStruct((N_DIR, bsz, nc, RW_ROWS, LANES), dt)
    params = pltpu.CompilerParams(dimension_semantics=("parallel", "parallel", "arbitrary"),
                                  vmem_limit_bytes=VMEM_LIMIT_BYTES)
    chunk_local = pl.pallas_call(
        functools.partial(_rwkv_a_kernel, n_sub=n_sub),
        grid=(N_DIR, bsz, nb),
        in_specs=[sh_spec, dr_spec, sh_spec, sh_spec, dr_spec, dr_spec],
        out_specs=[ch_spec] * 7 + [wc_spec],
        out_shape=[ch_shape(_BF)] * 5 + [ch_shape(jnp.float32)] * 2
                  + [jax.ShapeDtypeStruct((N_DIR, bsz, nc, RW_PAIRS, LANES), jnp.float32)],
        compiler_params=params, name="rwkv_chunk_local",
    )(r, k_dir, v, kk, a_dir, lw_dir)
    dir_specs = []
    for d in range(N_DIR):
        blk = functools.partial(lambda d, b, i: (d, b, _scan_block(d, i, nb, cb), 0, 0), d)
        dir_specs += [pl.BlockSpec((1, 1, n_sub, RW_ROWS, LANES), blk)] * 7
        dir_specs += [pl.BlockSpec((1, 1, n_sub, RW_PAIRS, LANES), blk)]
    y_specs = [pl.BlockSpec((1, RW_TBLK, RW_W),
                            functools.partial(lambda d, b, i: (b, _scan_block(d, i, nb, cb), 0), d))
               for d in range(N_DIR)]
    return pl.pallas_call(
        functools.partial(_rwkv_b_kernel, n_sub=n_sub),
        grid=(bsz, nb),
        in_specs=dir_specs,
        out_specs=y_specs,
        out_shape=[jax.ShapeDtypeStruct((bsz, t_len, RW_W), jnp.float32)] * N_DIR,
        scratch_shapes=[pltpu.VMEM((N_DIR, RW_PAIRS, LANES, LANES), jnp.float32)],
        compiler_params=pltpu.CompilerParams(dimension_semantics=("parallel", "arbitrary"),
                                             vmem_limit_bytes=VMEM_LIMIT_BYTES),
        name="rwkv_state_scan",
    )(*chunk_local, *chunk_local)


def _softplus(x):
    return jnp.maximum(x, 0.0) + jnp.log(1.0 + jnp.exp(-jnp.abs(x)))


def _rwkv_prep_kernel(*refs, blocks_per_seq, ctx_blocks):
    (zr, zk, zv, zw, za, zg, pr, pk, pv, pw, pa, pg, nr, nk, nv, nw, na, ng,
     mu_ref, w0_ref, a0_ref, wup_ref, aup_ref, gup_ref, kkw_ref, ka_ref, rk_ref, ones_ref,
     r_out, v_out, kk_out, kdir_out, a_out, lw_out, bonus_out, g_out) = refs
    blk = pl.program_id(0) % blocks_per_seq
    has_prev = (blk != 0) & (blk != ctx_blocks)
    has_next = (blk != ctx_blocks - 1) & (blk != blocks_per_seq - 1)
    tm = zr.shape[0]

    def lerp(z_ref, p_ref, n_ref, mu):
        z = z_ref[...]
        rows = lax.broadcasted_iota(jnp.int32, z.shape, 0)
        prev_row = jnp.where(has_prev, p_ref[RW_HALO - 1:RW_HALO, :], 0.0)
        next_row = jnp.where(has_next, n_ref[0:1, :], 0.0)
        before = jnp.where(rows == 0, prev_row, pltpu.roll(z, 1, axis=0))
        after = jnp.where(rows == tm - 1, next_row, pltpu.roll(z, tm - 1, axis=0))
        return z + mu * (0.5 * (before + after) - z)

    mu = mu_ref[...]
    r = lerp(zr, pr, nr, mu[:, 0:RW_W])
    k = lerp(zk, pk, nk, mu[:, RW_W:2 * RW_W])
    v = lerp(zv, pv, nv, mu[:, 2 * RW_W:3 * RW_W])
    wd = lerp(zw, pw, nw, mu[:, 3 * RW_W:3 * RW_W + LANES])
    ad = lerp(za, pa, na, mu[:, 3 * RW_W + LANES:3 * RW_W + 2 * LANES])
    gd = lerp(zg, pg, ng, mu[:, 3 * RW_W + 2 * LANES:3 * RW_W + 3 * LANES])
    dot = lambda a, w_ref: jnp.dot(a.astype(_BF), w_ref[...], preferred_element_type=jnp.float32)
    w_pre = dot(jnp.tanh(wd), wup_ref)
    a_pre = dot(ad, aup_ref)
    g_out[...] = dot(jax.nn.sigmoid(gd), gup_ref)

    def head_sum(x):
        hi = x.astype(_BF)
        lo = (x - hi.astype(jnp.float32)).astype(_BF)
        return (jnp.dot(hi, ones_ref[...], preferred_element_type=jnp.float32)
                + jnp.dot(lo, ones_ref[...], preferred_element_type=jnp.float32))

    kk = k * kkw_ref[...]
    kk_out[...] = kk / jnp.maximum(jnp.sqrt(head_sum(kk * kk)), 1e-12)
    r_out[...] = r
    v_out[...] = v
    rk = r * rk_ref[...]
    bonus = jnp.zeros_like(r)
    for d in range(N_DIR):
        cols = slice(d * RW_W, (d + 1) * RW_W)
        lw_out[d] = -jnp.exp(-_softplus(-(w0_ref[d:d + 1, :] + w_pre[:, cols])) - RW_DECAY_OFFSET)
        a = jax.nn.sigmoid(a0_ref[d:d + 1, :] + a_pre[:, cols])
        a_out[d] = a
        k_dir = k * (1.0 + (a - 1.0) * ka_ref[...])
        kdir_out[d] = k_dir
        bonus = bonus + rk * k_dir
    bonus_out[...] = head_sum(bonus) * v


def _rwkv_prep(p2, blocks_per_seq, ctx_blocks, mu, w0, w_up, a0, a_up, g_up, k_k, k_a, r_k):
    n = p2.shape[0]
    tm = ROW_TM
    hb = tm // RW_HALO
    n_hb = n // RW_HALO
    cw, cl = P_RW // RW_W, P_RWLR // LANES
    assert P_RW % RW_W == 0 and P_RWLR % LANES == 0 and N_DIR * RW_DECAY_RANK == LANES and RW_G_RANK == LANES

    def specs(rows, row_index):
        wide = [pl.BlockSpec((rows, RW_W), functools.partial(lambda j, i: (row_index(i), cw + j), j))
                for j in range(3)]
        return wide + [pl.BlockSpec((rows, LANES), functools.partial(lambda j, i: (row_index(i), cl + j), j))
                       for j in range(3)]

    main = specs(tm, lambda i: i)
    prev = specs(RW_HALO, lambda i: jnp.maximum(i * hb - 1, 0))
    nxt = specs(RW_HALO, lambda i: jnp.minimum((i + 1) * hb, n_hb - 1))
    zero = jnp.zeros((RW_DECAY_RANK, RW_W), jnp.float32)
    both_dirs = lambda up: jnp.concatenate([jnp.concatenate([up[0], zero], axis=1),
                                            jnp.concatenate([zero, up[1]], axis=1)], axis=0).astype(_BF)
    head = jnp.arange(RW_W) // RW_DH
    ones_bd = (head[:, None] == head[None, :]).astype(_BF)
    vec = lambda w: pl.BlockSpec((1, w), lambda i: (0, 0))
    dvec = pl.BlockSpec((N_DIR, RW_W), lambda i: (0, 0))
    consts = [vec(mu.shape[0]), dvec, dvec, _const_spec((LANES, N_DIR * RW_W)), _const_spec((LANES, N_DIR * RW_W)),
              _const_spec((RW_G_RANK, RW_W)), vec(RW_W), vec(RW_W), vec(RW_W), _const_spec((RW_W, RW_W))]
    row = pl.BlockSpec((tm, RW_W), lambda i: (i, 0))
    drow = pl.BlockSpec((N_DIR, tm, RW_W), lambda i: (0, i, 0))
    sh = jax.ShapeDtypeStruct((n, RW_W), jnp.float32)
    dsh = jax.ShapeDtypeStruct((N_DIR, n, RW_W), jnp.float32)
    return pl.pallas_call(
        functools.partial(_rwkv_prep_kernel, blocks_per_seq=blocks_per_seq, ctx_blocks=ctx_blocks),
        grid=(n // tm,),
        in_specs=main + prev + nxt + consts,
        out_specs=[row, row, row, drow, drow, drow, row, row],
        out_shape=[sh, sh, sh, dsh, dsh, dsh, sh, sh],
        compiler_params=pltpu.CompilerParams(dimension_semantics=("parallel",), vmem_limit_bytes=VMEM_LIMIT_BYTES),
        name="rwkv_prep",
    )(*([p2] * 18), mu.reshape(1, -1), w0, a0, both_dirs(w_up), both_dirs(a_up), g_up.astype(_BF),
      k_k.reshape(1, -1), k_a.reshape(1, -1), r_k.reshape(1, -1), ones_bd)


def _layer_norm(x, eps=LN_EPS):
    mu = jnp.mean(x, axis=-1, keepdims=True)
    var = jnp.mean(jnp.square(x - mu), axis=-1, keepdims=True)
    return (x - mu) * lax.rsqrt(var + eps)


def _modulate(x, shift, scale):
    return _layer_norm(x) * (1.0 + scale) + shift


def _conv_silu_kernel(x_ref, prev_ref, next_ref, w_ref, b_ref, o_ref, *, blocks_per_seq, ctx_blocks):
    blk = pl.program_id(0) % blocks_per_seq
    is_lat = blk >= ctx_blocks
    has_prev = is_lat & (blk != ctx_blocks)
    has_next = is_lat & (blk != blocks_per_seq - 1)
    tm, width = x_ref.shape
    ext = jnp.concatenate([jnp.where(has_prev, prev_ref[...], 0.0), x_ref[...],
                           jnp.where(has_next, next_ref[...], 0.0)], axis=0)
    n_ext = ext.shape[0]
    t = lax.broadcasted_iota(jnp.int32, (tm, 1), 0)
    col = jnp.where(is_lat, t % GRID_W, t)
    last_col = jnp.where(is_lat, GRID_W - 1, tm - 1)
    w = w_ref[...]
    acc = jnp.zeros((tm, width), jnp.float32) + b_ref[...]
    for dc in (-1, 0, 1):
        shifted = ext if dc == 0 else pltpu.roll(ext, (-dc) % n_ext, axis=0)
        col_ok = (col != 0) if dc == -1 else ((col != last_col) if dc == 1 else None)
        for dr in (-1, 0, 1):
            src = shifted[GRID_W + dr * GRID_W:GRID_W + dr * GRID_W + tm]
            ok = col_ok if dr == 0 else (is_lat if col_ok is None else (col_ok & is_lat))
            if ok is not None:
                src = jnp.where(ok, src, 0.0)
            tap = (dr + 1) * 3 + (dc + 1)
            acc = acc + src * w[tap:tap + 1, :]
    o_ref[...] = acc * jax.nn.sigmoid(acc)


def _conv_silu(p2, blocks_per_seq, ctx_blocks, conv_w, conv_b):
    n = p2.shape[0]
    tm = ROW_TM
    width = 2 * ML_W
    assert ctx_blocks == 1 and tm % GRID_W == 0 and P_QK % width == 0
    hb = tm // GRID_W
    n_hb = n // GRID_W
    cb = P_QK // width
    return pl.pallas_call(
        functools.partial(_conv_silu_kernel, blocks_per_seq=blocks_per_seq, ctx_blocks=ctx_blocks),
        grid=(n // tm,),
        in_specs=[pl.BlockSpec((tm, width), lambda i: (i, cb)),
                  pl.BlockSpec((GRID_W, width), lambda i: (jnp.maximum(i * hb - 1, 0), cb)),
                  pl.BlockSpec((GRID_W, width), lambda i: (jnp.minimum((i + 1) * hb, n_hb - 1), cb)),
                  pl.BlockSpec((9, width), lambda i: (0, 0)), pl.BlockSpec((1, width), lambda i: (0, 0))],
        out_specs=pl.BlockSpec((tm, width), lambda i: (i, 0)),
        out_shape=jax.ShapeDtypeStruct((n, width), jnp.float32),
        compiler_params=pltpu.CompilerParams(dimension_semantics=("parallel",), vmem_limit_bytes=VMEM_LIMIT_BYTES),
        name="mlstm_conv_silu",
    )(p2, p2, p2, conv_w.reshape(9, width), conv_b.reshape(1, width))


def _rwkv7_branch(p2, bsz, n_ctx, mu, w0, w_up, a0, a_up, g_up, k_k, k_a, r_k):
    t_len = p2.shape[0] // bsz
    r, v, kk, k_dir, a, log_decay, bonus, g = _rwkv_prep(p2, t_len // ROW_TM, n_ctx // ROW_TM, mu, w0, w_up, a0,
                                                         a_up, g_up, k_k, k_a, r_k)
    seq = lambda z: z.reshape(z.shape[:-2] + (bsz, t_len, RW_W))
    y_f, y_b = _rwkv_pallas(seq(r), seq(v), seq(kk), seq(k_dir), seq(a), seq(log_decay), n_ctx)
    return y_f.reshape(-1, RW_W), y_b.reshape(-1, RW_W), bonus, g


def _token_mixer(u, n_ctx, w_in,
                 ml_conv_w, ml_conv_b, ml_ig_b, ml_fg_b, ml_norm_g, ml_norm_b, ml_proj,
                 rw_mu, rw_w0, rw_w_up, rw_a0, rw_a_up, rw_g_up, rw_k_k, rw_k_a, rw_r_k,
                 rw_norm_g, rw_norm_b, rw_proj,
                 s5_lam_re, s5_lam_im, s5_log_dt, s5_b_re, s5_b_im, s5_c_re, s5_c_im, s5_d,
                 s5_w_val, s5_w_gate):
    bsz, t_len, _ = u.shape
    n = bsz * t_len
    p = _mm_any(u, _permute_w_in(w_in), tm=PROJ_TM, keep_col_pad=True)

    def col(start, width):
        return p[..., start:start + width]

    n_gate = N_DIR * ML_HEADS
    p2 = p.reshape(n, -1)
    qk = _conv_silu(p2, t_len // ROW_TM, n_ctx // ROW_TM, ml_conv_w, ml_conv_b)
    h_dir = _mlstm_pallas(qk.reshape(bsz, t_len, 2 * ML_W), p, col(P_MLG, 2 * n_gate), ml_ig_b, ml_fg_b, n_ctx)
    y_f, y_b, bonus, g = _rwkv7_branch(p2, bsz, n_ctx, rw_mu, rw_w0, rw_w_up, rw_a0, rw_a_up, rw_g_up,
                                       rw_k_k, rw_k_a, rw_r_k)
    s5 = _s5_pallas(p, n_ctx, s5_lam_re, s5_lam_im, s5_log_dt, s5_b_re, s5_b_im, s5_c_re, s5_c_im, s5_d)
    ml, rw = _post_scan(p2, h_dir.reshape(N_DIR, n, ML_W), y_f, y_b, bonus, g,
                        ml_norm_g, ml_norm_b, rw_norm_g, rw_norm_b)
    return _merge_pallas(p2, ml, rw, s5.reshape(S5_W // LANES, n, LANES),
                         ml_proj.astype(_BF), rw_proj.astype(_BF), s5_w_val.astype(_BF), s5_w_gate.astype(_BF))


def _permute_w_in(w_in):
    offs = np.cumsum((0,) + IN_WIDTHS)
    seg = lambda a, b: w_in[:, offs[a]:offs[b]]
    return jnp.concatenate([seg(0, 4), seg(6, 9), seg(12, 13), seg(13, 14), seg(9, 12), seg(4, 6)], axis=1)


def _moe_ffn(u, router_w, router_b, layer, w_gate, w_up, w_down):
    gates_t = _router_pallas(u, router_w, router_b)
    return _moe_routed(u, gates_t, layer, w_gate, w_up, w_down)


def kernel(x, c, ctx, c_ctx, ada_w, ada_b, w_in, ml_conv_w, ml_conv_b, ml_ig_b, ml_fg_b, ml_norm_g,
           ml_norm_b, ml_proj, rw_mu, rw_w0, rw_w_up, rw_a0, rw_a_up, rw_g_up, rw_k_k, rw_k_a, rw_r_k,
           rw_norm_g, rw_norm_b, rw_proj, s5_lam_re, s5_lam_im, s5_log_dt, s5_b_re, s5_b_im, s5_c_re,
           s5_c_im, s5_d, s5_w_val, s5_w_gate, w_out, ln1_g, ln1_b, ln2_g, ln2_b, router_w, router_b,
           exp_w_gate, exp_w_up, exp_w_down):
    bsz, n_ctx = ctx.shape[0], ctx.shape[1]
    t_len = n_ctx + x.shape[1]
    assert n_ctx % ROW_TM == 0 and t_len % ROW_TM == 0
    blocks_per_seq, ctx_blocks = t_len // ROW_TM, n_ctx // ROW_TM
    silu_c = jax.nn.silu(c)
    silu_cc = jax.nn.silu(c_ctx)[None, :]
    mods = []
    for i in range(DEPTH):
        mx = _mm_any(silu_c, ada_w[i]) + ada_b[i]
        mc = jnp.broadcast_to(_mm_any(silu_cc, ada_w[i]) + ada_b[i], mx.shape)
        mods.append(jnp.stack([mc, mx], axis=1).reshape(bsz, 2, N_MOD, 1, D_MODEL))
    xa = jnp.concatenate([ctx, x], axis=1).reshape(bsz * t_len, D_MODEL)
    m0 = mods[0]
    u = jnp.concatenate([_modulate(ctx, m0[:, 0, 0], m0[:, 0, 1]), _modulate(x, m0[:, 1, 0], m0[:, 1, 1])],
                        axis=1).astype(_BF)
    for i in range(DEPTH):
        z = _token_mixer(
            u, n_ctx, w_in[i],
            ml_conv_w[i], ml_conv_b[i], ml_ig_b[i], ml_fg_b[i], ml_norm_g[i], ml_norm_b[i], ml_proj[i],
            rw_mu[i], rw_w0[i], rw_w_up[i], rw_a0[i], rw_a_up[i], rw_g_up[i], rw_k_k[i], rw_k_a[i], rw_r_k[i],
            rw_norm_g[i], rw_norm_b[i], rw_proj[i],
            s5_lam_re[i], s5_lam_im[i], s5_log_dt[i], s5_b_re[i], s5_b_im[i], s5_c_re[i], s5_c_im[i], s5_d[i],
            s5_w_val[i], s5_w_gate[i])
        xa, u_ffn = _resid_norm_mod(z, w_out[i].astype(_BF), xa, mods[i], 2, ln1_g[i], ln1_b[i], mods[i], 3, 4,
                                    blocks_per_seq, ctx_blocks)
        ffn = _moe_ffn(u_ffn, router_w, router_b, i, exp_w_gate, exp_w_up, exp_w_down)
        xa, u = _resid_norm_mod(ffn, None, xa, mods[i], 5, ln2_g[i], ln2_b[i], mods[min(i + 1, DEPTH - 1)], 0, 1,
                                blocks_per_seq, ctx_blocks, u_dtype=_BF)
        u = u.reshape(bsz, t_len, D_MODEL)
    return xa.reshape(bsz, t_len, D_MODEL)[:, n_ctx:]
```

```python
import functools
import math

import jax
import jax.numpy as jnp
import numpy as np
from jax import lax
from jax.experimental import pallas as pl
from jax.experimental.pallas import tpu as pltpu

D_MODEL = 2048
DEPTH = 2
GRID_W = 64
N_DIR = 2
ML_HEADS = 4
ML_DH = 256
ML_W = ML_HEADS * ML_DH
ML_CHUNK = 64
ML_NORM_EPS = 1e-6
RW_HEADS = 16
RW_DH = 64
RW_W = RW_HEADS * RW_DH
RW_DECAY_RANK = 64
RW_A_RANK = 64
RW_G_RANK = 128
RW_DECAY_OFFSET = 0.5
RW_NORM_EPS = 64e-5
S5_W = 1024
S5_GROUP = 16
S5_GROUPS = S5_W // S5_GROUP
S5_STATE = 64
N_BRANCH = 3
N_GROUPS = 4
EXPERTS_PER_GROUP = 4
N_EXPERTS = N_GROUPS * EXPERTS_PER_GROUP
TOP_K = 2
D_EXPERT = 1024
DEEPNORM_ALPHA = (2.0 * DEPTH) ** 0.25
LN_EPS = 1e-5
N_MOD = 6
IN_WIDTHS = (ML_W, ML_W, ML_W, ML_W, N_DIR * ML_HEADS, N_DIR * ML_HEADS,
             RW_W, RW_W, RW_W, N_DIR * RW_DECAY_RANK, N_DIR * RW_A_RANK, RW_G_RANK,
             S5_W, N_BRANCH * D_MODEL)
D_IN = sum(IN_WIDTHS)
RW_IN_WIDTHS = (RW_W, RW_W, RW_W, N_DIR * RW_DECAY_RANK, N_DIR * RW_A_RANK, RW_G_RANK)

VMEM_LIMIT_BYTES = 56 * 1024 * 1024
LANES = 128

S5_CHUNK = 16
S5_UNROLL = 8
RL_TBLK = 256
PROJ_TM = 1536
ML_TBLK = 256
RW_CHUNK = 16
RW_PAIRS = RW_HEADS // 2
RW_ROWS = RW_PAIRS * RW_CHUNK
RW_TBLK = 256
RW_GROUP = 8
RW_HALO = 8
MOE_TM = 256
ROW_TM = 256

P_QK, P_V, P_O = 0, 2 * ML_W, 3 * ML_W
P_RW = 4 * ML_W
P_S5 = P_RW + 3 * RW_W
P_GATE = P_S5 + S5_W
P_RWLR = P_GATE + N_BRANCH * D_MODEL
P_MLG = P_RWLR + 2 * N_DIR * RW_DECAY_RANK + RW_G_RANK
assert P_GATE % D_MODEL == 0 and P_MLG + 2 * N_DIR * ML_HEADS == D_IN and RW_DECAY_RANK == RW_A_RANK

_BF = jnp.bfloat16
_HI = lax.Precision.HIGHEST
_NT = (((1,), (1,)), ((), ()))
_TN = (((0,), (0,)), ((), ()))


def _scan_block(d, i, n_blocks, ctx_blocks):
    bwd = jnp.where(i < ctx_blocks, ctx_blocks - 1 - i, n_blocks - 1 + ctx_blocks - i)
    return jnp.where(d == 0, i, bwd)


def _mm_kernel(a_ref, w_ref, o_ref, abf_ref):
    @pl.when(pl.program_id(1) == 0)
    def _():
        abf_ref[...] = a_ref[...].astype(_BF)

    o_ref[...] = jnp.dot(abf_ref[...], w_ref[...], preferred_element_type=jnp.float32)


def _mm_bf16_kernel(a_ref, w_ref, o_ref):
    o_ref[...] = jnp.dot(a_ref[...], w_ref[...], preferred_element_type=jnp.float32)


def _mm(a, w, tm, tn):
    m, k = a.shape
    n = w.shape[1]
    assert m % tm == 0 and n % tn == 0, (m, n, tm, tn)
    if a.dtype == _BF:
        return pl.pallas_call(
            _mm_bf16_kernel,
            grid=(m // tm, n // tn),
            in_specs=[pl.BlockSpec((tm, k), lambda i, j: (i, 0)),
                      pl.BlockSpec((k, tn), lambda i, j: (0, j))],
            out_specs=pl.BlockSpec((tm, tn), lambda i, j: (i, j)),
            out_shape=jax.ShapeDtypeStruct((m, n), jnp.float32),
            compiler_params=pltpu.CompilerParams(
                dimension_semantics=("parallel", "arbitrary"),
                vmem_limit_bytes=VMEM_LIMIT_BYTES),
            name="mm_bf16",
        )(a, w)
    return pl.pallas_call(
        _mm_kernel,
        grid=(m // tm, n // tn),
        in_specs=[pl.BlockSpec((tm, k), lambda i, j: (i, 0)),
                  pl.BlockSpec((k, tn), lambda i, j: (0, j))],
        out_specs=pl.BlockSpec((tm, tn), lambda i, j: (i, j)),
        out_shape=jax.ShapeDtypeStruct((m, n), jnp.float32),
        scratch_shapes=[pltpu.VMEM((tm, k), _BF)],
        compiler_params=pltpu.CompilerParams(
            dimension_semantics=("parallel", "arbitrary"),
            vmem_limit_bytes=VMEM_LIMIT_BYTES),
        name="mm",
    )(a, w)


def _mm_any(a, w, tm=1024, tn=512, keep_col_pad=False):
    lead = a.shape[:-1]
    a2 = a.reshape(-1, a.shape[-1])
    m, n = a2.shape[0], w.shape[1]
    mp = -(-m // 8) * 8
    if mp > tm:
        mp = -(-m // tm) * tm
    tm = min(tm, mp)
    np_ = -(-n // LANES) * LANES
    if np_ > tn:
        np_ = -(-n // tn) * tn
    tn = min(tn, np_)
    if mp != m:
        a2 = jnp.pad(a2, ((0, mp - m), (0, 0)))
    wb = w.astype(_BF)
    if np_ != n:
        wb = jnp.pad(wb, ((0, 0), (0, np_ - n)))
    out = _mm(a2, wb, tm, tn)
    if keep_col_pad:
        n = np_
    if mp != m or np_ != n:
        out = out[:m, :n]
    return out.reshape(lead + (n,))


def _const_spec(shape):
    nd = len(shape)
    return pl.BlockSpec(shape, lambda i: (0,) * nd, pipeline_mode=pl.Buffered(1))


def _merge_kernel(ml_ref, rw_ref, s5_ref, g0_ref, g1_ref, g2_ref, wml_ref, wrw_ref, wval_ref, wgate_ref, z_ref):
    dot = lambda a, w_ref: jnp.dot(a.astype(_BF), w_ref[...], preferred_element_type=jnp.float32)
    s5 = jnp.concatenate([s5_ref[m] for m in range(S5_W // LANES)], axis=1)
    sval = dot(s5, wval_ref) * jax.nn.sigmoid(dot(s5, wgate_ref))
    z = (jax.nn.sigmoid(g0_ref[...]) * dot(ml_ref[...], wml_ref)
         + jax.nn.sigmoid(g1_ref[...]) * dot(rw_ref[...], wrw_ref)
         + jax.nn.sigmoid(g2_ref[...]) * sval)
    z_ref[...] = z.astype(_BF)


def _merge_pallas(p2, ml, rw, s5, wml, wrw, wval, wgate):
    n, w = ml.shape
    tm = ROW_TM
    gb = P_GATE // D_MODEL
    row = lambda i: (i, 0)
    return pl.pallas_call(
        _merge_kernel,
        grid=(n // tm,),
        in_specs=[pl.BlockSpec((tm, w), row)] * 2 + [pl.BlockSpec((S5_W // LANES, tm, LANES), lambda i: (0, i, 0))]
                 + [pl.BlockSpec((tm, D_MODEL), functools.partial(lambda j, i: (i, gb + j), j))
                    for j in range(N_BRANCH)]
                 + [_const_spec((w, D_MODEL))] * 4,
        out_specs=pl.BlockSpec((tm, D_MODEL), row),
        out_shape=jax.ShapeDtypeStruct((n, D_MODEL), _BF),
        compiler_params=pltpu.CompilerParams(dimension_semantics=("parallel",), vmem_limit_bytes=VMEM_LIMIT_BYTES),
        name="merge_gate_proj",
    )(ml, rw, s5, p2, p2, p2, wml, wrw, wval, wgate)


def _post_scan_kernel(hf_ref, hb_ref, o_ref, yf_ref, yb_ref, bonus_ref, g_ref, mlg_ref, mlb_ref, rwg_ref, rwb_ref,
                      ones_ref, ml_ref, rw_ref):
    h = hf_ref[...] + hb_ref[...]
    parts = []
    for hd in range(ML_HEADS):
        x = h[:, hd * ML_DH:(hd + 1) * ML_DH]
        xc = x - jnp.mean(x, axis=-1, keepdims=True)
        parts.append(xc * lax.rsqrt(jnp.mean(xc * xc, axis=-1, keepdims=True) + ML_NORM_EPS))
    hn = jnp.concatenate(parts, axis=1) * mlg_ref[...] + mlb_ref[...]
    ml_ref[...] = (jax.nn.sigmoid(o_ref[...]) * hn).astype(_BF)

    def head_mean(x):
        hi = x.astype(_BF)
        lo = (x - hi.astype(jnp.float32)).astype(_BF)
        s = (jnp.dot(hi, ones_ref[...], preferred_element_type=jnp.float32)
             + jnp.dot(lo, ones_ref[...], preferred_element_type=jnp.float32))
        return s * (1.0 / RW_DH)

    y = yf_ref[...] + yb_ref[...]
    yc = y - head_mean(y)
    yn = yc * lax.rsqrt(head_mean(yc * yc) + RW_NORM_EPS) * rwg_ref[...] + rwb_ref[...]
    rw_ref[...] = ((yn + bonus_ref[...]) * g_ref[...]).astype(_BF)


def _post_scan(p2, h_f, h_b, y_f, y_b, bonus, g, ml_norm_g, ml_norm_b, rw_norm_g, rw_norm_b):
    n = y_f.shape[0]
    tm = ROW_TM
    head = jnp.arange(RW_W) // RW_DH
    ones_bd = (head[:, None] == head[None, :]).astype(_BF)
    vec = pl.BlockSpec((1, RW_W), lambda i: (0, 0))
    blk = pl.BlockSpec((tm, RW_W), lambda i: (i, 0))
    return pl.pallas_call(
        _post_scan_kernel,
        grid=(n // tm,),
        in_specs=[blk, blk,
                  pl.BlockSpec((tm, ML_W), lambda i: (i, P_O // ML_W)), blk, blk, blk, blk, vec, vec, vec, vec,
                  _const_spec((RW_W, RW_W))],
        out_specs=[blk, blk],
        out_shape=[jax.ShapeDtypeStruct((n, ML_W), _BF), jax.ShapeDtypeStruct((n, RW_W), _BF)],
        compiler_params=pltpu.CompilerParams(dimension_semantics=("parallel",), vmem_limit_bytes=VMEM_LIMIT_BYTES),
        name="post_scan_norm_gate",
    )(h_f, h_b, p2, y_f, y_b, bonus, g, ml_norm_g.reshape(1, -1), ml_norm_b.reshape(1, -1),
      rw_norm_g.reshape(1, -1), rw_norm_b.reshape(1, -1), ones_bd)


def _ln_rows(x, eps=LN_EPS):
    mu = jnp.mean(x, axis=-1, keepdims=True)
    xc = x - mu
    var = jnp.mean(xc * xc, axis=-1, keepdims=True)
    return xc * lax.rsqrt(var + eps)


def _resid_kernel(*refs, with_w):
    if with_w:
        d_ref, w_ref, x_ref, gate_ref, g_ref, b_ref, sh_ref, sc_ref, xo_ref, uo_ref = refs
        delta = jnp.dot(d_ref[...], w_ref[...], preferred_element_type=jnp.float32)
    else:
        d_ref, x_ref, gate_ref, g_ref, b_ref, sh_ref, sc_ref, xo_ref, uo_ref = refs
        delta = d_ref[...]
    xn = _ln_rows(DEEPNORM_ALPHA * x_ref[...] + gate_ref[0, 0, 0] * delta) * g_ref[...] + b_ref[...]
    xo_ref[...] = xn
    uo_ref[...] = (_ln_rows(xn) * (1.0 + sc_ref[0, 0, 0]) + sh_ref[0, 0, 0]).astype(uo_ref.dtype)


def _resid_norm_mod(delta, w, x, mod_a, ia, ln_g, ln_b, mod_b, ish, isc, blocks_per_seq, ctx_blocks,
                    u_dtype=jnp.float32):
    n, d = x.shape
    tm = ROW_TM
    row = lambda i: (i, 0)

    def mod_spec(m):
        return pl.BlockSpec((1, 1, 1, 1, d), lambda i: (i // blocks_per_seq,
                                                       (i % blocks_per_seq >= ctx_blocks).astype(jnp.int32), m, 0, 0))

    vec = pl.BlockSpec((1, d), lambda i: (0, 0))
    if w is not None:
        in_specs = [pl.BlockSpec((tm, delta.shape[1]), row), _const_spec(w.shape), pl.BlockSpec((tm, d), row)]
        args = (delta, w, x)
    else:
        in_specs = [pl.BlockSpec((tm, d), row), pl.BlockSpec((tm, d), row)]
        args = (delta, x)
    return pl.pallas_call(
        functools.partial(_resid_kernel, with_w=w is not None),
        grid=(n // tm,),
        in_specs=in_specs + [mod_spec(ia), vec, vec, mod_spec(ish), mod_spec(isc)],
        out_specs=[pl.BlockSpec((tm, d), row)] * 2,
        out_shape=[jax.ShapeDtypeStruct((n, d), jnp.float32), jax.ShapeDtypeStruct((n, d), u_dtype)],
        compiler_params=pltpu.CompilerParams(dimension_semantics=("parallel",), vmem_limit_bytes=VMEM_LIMIT_BYTES),
        name="resid_norm_mod",
    )(*args, mod_a, ln_g.reshape(1, d), ln_b.reshape(1, d), mod_b, mod_b)


def _moe_ffn_kernel(te_ref, tv_ref, x_ref, wg_ref, wu_ref, wd_ref, y_ref, wgb_ref, wub_ref, wdb_ref):
    t = pl.program_id(0)

    @pl.when((t == 0) | (te_ref[t] != te_ref[jnp.maximum(t - 1, 0)]))
    def _():
        wgb_ref[...] = wg_ref[0, 0].astype(_BF)
        wub_ref[...] = wu_ref[0, 0].astype(_BF)
        wdb_ref[...] = wd_ref[0, 0].astype(_BF)

    @pl.when(tv_ref[t] == 1)
    def _():
        x = x_ref[...].astype(_BF)
        hg = jnp.dot(x, wgb_ref[...], preferred_element_type=jnp.float32)
        hu = jnp.dot(x, wub_ref[...], preferred_element_type=jnp.float32)
        h = (hg * jax.nn.sigmoid(hg)) * hu
        y_ref[...] = jnp.dot(h.astype(_BF), wdb_ref[...], preferred_element_type=jnp.float32)

    @pl.when(tv_ref[t] == 0)
    def _():
        y_ref[...] = jnp.zeros_like(y_ref)


def _moe_routed(u, gates_t, layer, wg, wu, wd):
    n_tok, d = u.shape
    _, n_e, _, d_e = wg.shape
    tm = MOE_TM
    n_tiles = (TOP_K * n_tok) // tm + n_e
    n_slots = n_tiles * tm
    sel = gates_t > 0.0
    seli = sel.astype(jnp.int32)
    rank = jnp.cumsum(seli, axis=1) - 1
    cnt = jnp.sum(seli, axis=1)
    tiles_e = (cnt + tm - 1) // tm
    tile_end = jnp.cumsum(tiles_e)
    off = (tile_end - tiles_e) * tm
    slot = off[:, None] + rank
    order = jnp.cumsum(seli, axis=0)
    slots, owned, gsel = [], [], []
    for j in range(TOP_K):
        pick = sel & (order == j + 1)
        slots.append(jnp.sum(jnp.where(pick, slot, 0), axis=0))
        owned.append(jnp.where(jnp.any(pick, axis=0), slots[j], n_slots))
        gsel.append(jnp.sum(jnp.where(pick, gates_t, 0.0), axis=0))
    tok = jnp.arange(n_tok, dtype=jnp.int32)
    tok_of_slot = jnp.zeros((n_slots,), jnp.int32).at[jnp.concatenate(owned)].set(jnp.tile(tok, TOP_K), mode='drop')
    tile_ids = jnp.arange(n_tiles, dtype=jnp.int32)
    tile_valid = (tile_ids < tile_end[-1]).astype(jnp.int32)
    tile_expert = jnp.minimum(jnp.searchsorted(tile_end, tile_ids, side='right'), n_e - 1).astype(jnp.int32)
    last_e = jnp.max(jnp.where(cnt > 0, jnp.arange(n_e), 0)).astype(jnp.int32)
    tile_expert = jnp.where(tile_valid == 1, tile_expert, last_e)
    xs = u.astype(_BF).at[tok_of_slot].get(mode='promise_in_bounds')
    w_index = lambda t, te, tv: (layer, te[t], 0, 0)
    grid_spec = pltpu.PrefetchScalarGridSpec(
        num_scalar_prefetch=2,
        grid=(n_tiles,),
        in_specs=[pl.BlockSpec((tm, d), lambda t, te, tv: (t, 0)),
                  pl.BlockSpec((1, 1, d, d_e), w_index, pipeline_mode=pl.Buffered(1)),
                  pl.BlockSpec((1, 1, d, d_e), w_index, pipeline_mode=pl.Buffered(1)),
                  pl.BlockSpec((1, 1, d_e, d), w_index, pipeline_mode=pl.Buffered(1))],
        out_specs=pl.BlockSpec((tm, d), lambda t, te, tv: (t, 0)),
        scratch_shapes=[pltpu.VMEM((d, d_e), _BF), pltpu.VMEM((d, d_e), _BF), pltpu.VMEM((d_e, d), _BF)])
    ys = pl.pallas_call(
        _moe_ffn_kernel, grid_spec=grid_spec,
        out_shape=jax.ShapeDtypeStruct((n_slots, d), jnp.float32),
        compiler_params=pltpu.CompilerParams(dimension_semantics=("arbitrary",),
                                             vmem_limit_bytes=VMEM_LIMIT_BYTES),
        name="moe_routed_ffn",
    )(tile_expert, tile_valid, xs, wg, wu, wd)
    return sum(gsel[j][:, None] * ys.at[slots[j]].get(mode='promise_in_bounds') for j in range(TOP_K))


def _router_kernel(u_ref, wt_ref, b_ref, g_ref):
    logits = lax.dot_general(wt_ref[...], u_ref[...], _NT, precision=_HI, preferred_element_type=jnp.float32)
    aff = jax.nn.sigmoid(logits)
    score = aff + b_ref[...]
    s = [score[e:e + 1] for e in range(N_EXPERTS)]
    a = [aff[e:e + 1] for e in range(N_EXPERTS)]
    gs = []
    for g in range(N_GROUPS):
        m = s[g * EXPERTS_PER_GROUP:(g + 1) * EXPERTS_PER_GROUP]
        best = None
        for i in range(EXPERTS_PER_GROUP):
            for j in range(i + 1, EXPERTS_PER_GROUP):
                best = m[i] + m[j] if best is None else jnp.maximum(best, m[i] + m[j])
        gs.append(best)
    best_val = gs[0]
    best_grp = jnp.zeros_like(gs[0], dtype=jnp.int32)
    for g in range(1, N_GROUPS):
        better = gs[g] > best_val
        best_grp = jnp.where(better, g, best_grp)
        best_val = jnp.where(better, gs[g], best_val)
    sel = []
    for e in range(N_EXPERTS):
        g = e // EXPERTS_PER_GROUP
        rank = jnp.zeros_like(best_grp)
        for j in range(g * EXPERTS_PER_GROUP, (g + 1) * EXPERTS_PER_GROUP):
            if j != e:
                ahead = (s[j] > s[e]) | ((s[j] == s[e]) & (j < e))
                rank = rank + ahead.astype(jnp.int32)
        sel.append((best_grp == g) & (rank < TOP_K))
    wsum = sum(jnp.where(sel[e], a[e], 0.0) for e in range(N_EXPERTS))
    g_ref[...] = jnp.concatenate([jnp.where(sel[e], a[e] / wsum, 0.0) for e in range(N_EXPERTS)], axis=0)


def _router_pallas(u, router_w, router_b, tm=512):
    n_tok, d = u.shape
    assert n_tok % tm == 0
    return pl.pallas_call(
        _router_kernel,
        grid=(n_tok // tm,),
        in_specs=[pl.BlockSpec((tm, d), lambda i: (i, 0)),
                  pl.BlockSpec((N_EXPERTS, d), lambda i: (0, 0)),
                  pl.BlockSpec((N_EXPERTS, 1), lambda i: (0, 0))],
        out_specs=pl.BlockSpec((N_EXPERTS, tm), lambda i: (0, i)),
        out_shape=jax.ShapeDtypeStruct((N_EXPERTS, n_tok), jnp.float32),
        compiler_params=pltpu.CompilerParams(dimension_semantics=("parallel",),
                                             vmem_limit_bytes=VMEM_LIMIT_BYTES),
        name="moe_router",
    )(u, router_w.T, router_b.reshape(N_EXPERTS, 1))


def _s5_mats(lam_re, lam_im, log_dt, b_re, b_im, c_re, c_im):
    L = S5_CHUNK
    lam = lax.complex(lam_re, lam_im)
    ldt = lam * jnp.exp(log_dt)[..., None]
    lam_bar = jnp.exp(ldt)
    b_bar = ((lam_bar - 1.0) / lam)[..., None] * lax.complex(b_re, b_im)
    c_mat = lax.complex(c_re, c_im)
    tau = jnp.arange(L + 1, dtype=jnp.float32)
    pw = jnp.exp(ldt[:, :, None, :] * tau[None, None, :, None])
    kern = jnp.real(jnp.einsum('dgon,dgtn,dgni->dgtoi', c_mat, pw[:, :, :L], b_bar))
    s_idx = jnp.arange(L)[:, None]
    t_idx = jnp.arange(L)[None, :]

    def toeplitz(k, lag, valid):
        m = k[:, jnp.clip(lag, 0, L - 1)] * valid[None, :, :, None, None]
        return jnp.transpose(m, (0, 1, 4, 2, 3)).reshape(-1, L * S5_GROUP, L * S5_GROUP)

    tsum = (toeplitz(kern[0], t_idx - s_idx, (t_idx >= s_idx).astype(jnp.float32))
            + toeplitz(kern[1], s_idx - t_idx, (s_idx >= t_idx).astype(jnp.float32)))
    pin_f = pw[0][:, L - 1 - jnp.arange(L)]
    pin_b = pw[1][:, jnp.arange(L)]
    in_f = jnp.einsum('gsn,gni->gsin', pin_f, b_bar[0]).reshape(-1, L * S5_GROUP, S5_STATE)
    in_b = jnp.einsum('gsn,gni->gsin', pin_b, b_bar[1]).reshape(-1, L * S5_GROUP, S5_STATE)
    icat = jnp.concatenate([jnp.real(in_f), jnp.imag(in_f), jnp.real(in_b), jnp.imag(in_b)], axis=-1)
    pout_f = pw[0][:, 1 + jnp.arange(L)]
    pout_b = pw[1][:, L - jnp.arange(L)]
    out_f = jnp.einsum('gon,gtn->gnto', c_mat[0], pout_f).reshape(-1, S5_STATE, L * S5_GROUP)
    out_b = jnp.einsum('gon,gtn->gnto', c_mat[1], pout_b).reshape(-1, S5_STATE, L * S5_GROUP)
    ocat = jnp.concatenate([jnp.real(out_f), -jnp.imag(out_f), jnp.real(out_b), -jnp.imag(out_b)], axis=1)
    lam_l = pw[:, :, L]
    lam_chunk = jnp.stack([jnp.real(lam_l[0]), jnp.imag(lam_l[0]), jnp.real(lam_l[1]), jnp.imag(lam_l[1])],
                          axis=1)
    return tsum, icat, ocat, lam_chunk


def _s5_kernel(u0_ref, u1_ref, t_ref, i_ref, o_ref, lam_ref, d_ref, y0_ref, y1_ref,
               v_ref, xfr_ref, xfi_ref, xbr_ref, xbi_ref, *, n_chunks, ctx_chunks, bsz):
    n = S5_STATE
    u = jnp.concatenate([u0_ref[0], u1_ref[0]], axis=1)
    ub = u.astype(_BF)
    v_ref[...] = jnp.dot(ub, i_ref[0], preferred_element_type=jnp.float32)
    lam = lam_ref[0]
    lfr = jnp.broadcast_to(lam[0:1], (bsz, n))
    lfi = jnp.broadcast_to(lam[1:2], (bsz, n))
    lbr = jnp.broadcast_to(lam[2:3], (bsz, n))
    lbi = jnp.broadcast_to(lam[3:4], (bsz, n))

    def cmul_add(lr, li, xr, xi, vr, vi):
        return lr * xr - li * xi + vr, lr * xi + li * xr + vi

    def step(j, carry):
        fr, fi, br, bi = carry
        rf = pl.multiple_of(j * (2 * bsz), 2 * bsz)
        pb = jnp.where(j < ctx_chunks // 2, ctx_chunks // 2 - 1 - j, (n_chunks + ctx_chunks) // 2 - 1 - j)
        rb = pl.multiple_of(pb * (2 * bsz), 2 * bsz)
        vf = v_ref[pl.ds(rf, 2 * bsz), :]
        vb = v_ref[pl.ds(rb, 2 * bsz), :]
        fr1, fi1 = cmul_add(lfr, lfi, fr, fi, vf[:bsz, 0:n], vf[:bsz, n:2 * n])
        fr2, fi2 = cmul_add(lfr, lfi, fr1, fi1, vf[bsz:, 0:n], vf[bsz:, n:2 * n])
        br1, bi1 = cmul_add(lbr, lbi, br, bi, vb[bsz:, 2 * n:3 * n], vb[bsz:, 3 * n:4 * n])
        br2, bi2 = cmul_add(lbr, lbi, br1, bi1, vb[:bsz, 2 * n:3 * n], vb[:bsz, 3 * n:4 * n])
        xfr_ref[pl.ds(rf, 2 * bsz), :] = jnp.concatenate([fr, fr1], axis=0)
        xfi_ref[pl.ds(rf, 2 * bsz), :] = jnp.concatenate([fi, fi1], axis=0)
        xbr_ref[pl.ds(rb, 2 * bsz), :] = jnp.concatenate([br1, br], axis=0)
        xbi_ref[pl.ds(rb, 2 * bsz), :] = jnp.concatenate([bi1, bi], axis=0)
        return fr2, fi2, br2, bi2

    z = jnp.zeros((bsz, n), jnp.float32)
    lax.fori_loop(0, n_chunks // 2, step, (z, z, z, z), unroll=S5_UNROLL)
    o = o_ref[0]
    y = jnp.dot(ub, t_ref[0], preferred_element_type=jnp.float32)
    y += jnp.dot(xfr_ref[...].astype(_BF), o[0:n], preferred_element_type=jnp.float32)
    y += jnp.dot(xfi_ref[...].astype(_BF), o[n:2 * n], preferred_element_type=jnp.float32)
    y += jnp.dot(xbr_ref[...].astype(_BF), o[2 * n:3 * n], preferred_element_type=jnp.float32)
    y += jnp.dot(xbi_ref[...].astype(_BF), o[3 * n:4 * n], preferred_element_type=jnp.float32)
    y += d_ref[0] * u
    y = 0.5 * y * (1.0 + jnp.tanh(math.sqrt(2.0 / math.pi) * (y + 0.044715 * (y * y * y))))
    y0_ref[0] = y[:, :LANES]
    y1_ref[0] = y[:, LANES:]


def _to_groups_kernel(*refs, bsz):
    L, C = S5_CHUNK, S5_GROUP
    nch = RL_TBLK // L
    gpt = LANES // C
    lane = lax.broadcasted_iota(jnp.int32, (nch, LANES), 1) // C
    x_refs, o_refs = refs[:S5_W // LANES], refs[S5_W // LANES:]
    for b in range(bsz):
        for m in range(S5_W // LANES):
            src = [x_refs[m][b, pl.ds(j, nch, stride=L), :] for j in range(L)]
            for gl in range(gpt):
                g = m * gpt + gl
                for q in range(L * C // LANES):
                    acc = jnp.zeros((nch, LANES), jnp.float32)
                    for jl in range(gpt):
                        j = q * gpt + jl
                        moved = src[j] if jl == gl else pltpu.roll(src[j], ((jl - gl) * C) % LANES, axis=1)
                        acc = jnp.where(lane == jl, moved, acc)
                    o_refs[q][g, pl.ds(b, nch, stride=bsz), :] = acc


def _from_groups_kernel(y0_ref, y1_ref, o_ref, *, bsz):
    L, C = S5_CHUNK, S5_GROUP
    nch = RL_TBLK // L
    gpt = LANES // C
    lane = lax.broadcasted_iota(jnp.int32, (nch, LANES), 1) // C
    y_refs = (y0_ref, y1_ref)
    for b in range(bsz):
        for m in range(S5_W // LANES):
            for q in range(L * C // LANES):
                src = [y_refs[q][m * gpt + gl, pl.ds(b, nch, stride=bsz), :] for gl in range(gpt)]
                for jl in range(gpt):
                    acc = jnp.zeros((nch, LANES), jnp.float32)
                    for gl in range(gpt):
                        moved = src[gl] if jl == gl else pltpu.roll(src[gl], ((gl - jl) * C) % LANES, axis=1)
                        acc = jnp.where(lane == gl, moved, acc)
                    o_ref[m, b, pl.ds(q * gpt + jl, nch, stride=L), :] = acc


def _s5_pallas(p3, n_ctx, lam_re, lam_im, log_dt, b_re, b_im, c_re, c_im, d_skip):
    bsz, t_len, _ = p3.shape
    L, G, C = S5_CHUNK, S5_GROUPS, S5_GROUP
    nc = t_len // L
    assert (2 * bsz) % 8 == 0 and nc % 2 == 0 and (n_ctx // L) % 2 == 0 and t_len % RL_TBLK == 0
    assert L * C == 2 * LANES and P_S5 % LANES == 0
    tsum, icat, ocat, lam_chunk = _s5_mats(lam_re, lam_im, log_dt, b_re, b_im, c_re, c_im)
    dvec = jnp.tile(d_skip.reshape(G, 1, C), (1, L, 1)).reshape(G, 1, L * C)
    rows = nc * bsz
    n_lt = S5_W // LANES
    nch = RL_TBLK // L
    relayout_params = pltpu.CompilerParams(dimension_semantics=("parallel",), vmem_limit_bytes=VMEM_LIMIT_BYTES)
    half_shape = jax.ShapeDtypeStruct((G, rows, LANES), jnp.float32)
    half_blk = pl.BlockSpec((G, nch * bsz, LANES), lambda i: (0, i, 0))
    u0, u1 = pl.pallas_call(
        functools.partial(_to_groups_kernel, bsz=bsz),
        grid=(t_len // RL_TBLK,),
        in_specs=[pl.BlockSpec((bsz, RL_TBLK, LANES), functools.partial(lambda m, i: (0, i, P_S5 // LANES + m), m))
                  for m in range(n_lt)],
        out_specs=[half_blk, half_blk],
        out_shape=[half_shape, half_shape],
        compiler_params=relayout_params, name="s5_to_groups",
    )(*([p3] * n_lt))
    wspec = pl.BlockSpec((1, L * C, L * C), lambda g: (g, 0, 0))
    half = pl.BlockSpec((1, rows, LANES), lambda g: (g, 0, 0))
    y0, y1 = pl.pallas_call(
        functools.partial(_s5_kernel, n_chunks=nc, ctx_chunks=n_ctx // L, bsz=bsz),
        grid=(G,),
        in_specs=[half, half, wspec, wspec, wspec,
                  pl.BlockSpec((1, 4, S5_STATE), lambda g: (g, 0, 0)),
                  pl.BlockSpec((1, 1, L * C), lambda g: (g, 0, 0))],
        out_specs=[half, half],
        out_shape=[half_shape, half_shape],
        scratch_shapes=[pltpu.VMEM((rows, 4 * S5_STATE), jnp.float32)]
                       + [pltpu.VMEM((rows, S5_STATE), jnp.float32)] * 4,
        compiler_params=pltpu.CompilerParams(dimension_semantics=("parallel",),
                                             vmem_limit_bytes=VMEM_LIMIT_BYTES),
        name="s5_scan",
    )(u0, u1, tsum.astype(_BF), icat.astype(_BF), ocat.astype(_BF), lam_chunk, dvec)
    return pl.pallas_call(
        functools.partial(_from_groups_kernel, bsz=bsz),
        grid=(t_len // RL_TBLK,),
        in_specs=[half_blk, half_blk],
        out_specs=pl.BlockSpec((n_lt, bsz, RL_TBLK, LANES), lambda i: (0, 0, i, 0)),
        out_shape=jax.ShapeDtypeStruct((n_lt, bsz, t_len, LANES), jnp.float32),
        compiler_params=relayout_params, name="s5_from_groups",
    )(y0, y1)


def _log_sigmoid(x):
    return jnp.minimum(x, 0.0) - jnp.log(1.0 + jnp.exp(-jnp.abs(x)))


def _mlstm_kernel(igb_ref, fgb_ref, *refs):
    ins, (hf_ref, hb_ref, cmat_ref, nvec_ref, m_ref) = refs[:5 * N_DIR], refs[5 * N_DIR:]
    h_refs = (hf_ref, hb_ref)
    L = ML_CHUNK
    n_sub = ML_TBLK // L
    H = range(N_DIR * ML_HEADS)
    dof = lambda c: c // ML_HEADS
    hof = lambda c: c % ML_HEADS

    @pl.when(pl.program_id(1) == 0)
    def _():
        cmat_ref[...] = jnp.zeros_like(cmat_ref)
        nvec_ref[...] = jnp.zeros_like(nvec_ref)
        m_ref[...] = jnp.zeros_like(m_ref)

    igb = [igb_ref[dof(h), hof(h)] for h in H]
    fgb = [fgb_ref[dof(h), hof(h)] for h in H]
    row = lax.broadcasted_iota(jnp.int32, (L, L), 0)
    col = lax.broadcasted_iota(jnp.int32, (L, L), 1)
    seen_d = [row >= col, row <= col]
    seen_fd = [s.astype(jnp.float32) for s in seen_d]
    seen_td = [seen_fd[1], seen_fd[0]]
    scale = ML_DH ** -0.5

    def chunk(jj, carry):
        cjs = [jj, n_sub - 1 - jj]
        r0s = [pl.multiple_of(cj * L, L) for cj in cjs]
        hs = lambda h: slice(hof(h) * ML_DH, (hof(h) + 1) * ML_DH)
        q_ref, k_ref, v_ref, gc_ref, gr_ref = [[ins[5 * dof(h) + a] for h in H] for a in range(5)]
        q = [q_ref[h][0, pl.ds(r0s[dof(h)], L), hs(h)] for h in H]
        k = [k_ref[h][0, pl.ds(r0s[dof(h)], L), hs(h)] * scale for h in H]
        vb = [v_ref[h][0, pl.ds(r0s[dof(h)], L), hs(h)].astype(_BF) for h in H]
        gc = [gc_ref[h][0, 0, hof(h), pl.ds(r0s[dof(h)], L), :] for h in H]
        gr = [gr_ref[h][0, 0, hof(h), cjs[dof(h)]] for h in H]
        seen, seen_f, seen_t = seen_d, seen_fd, seen_td
        li_col = [gc[h][:, 0:1] + igb[h] for h in H]
        lf_col = [_log_sigmoid(gc[h][:, 1:2] + fgb[h]) for h in H]
        li_row = [gr[h][0:1, :] + igb[h] for h in H]
        lf_row = [_log_sigmoid(gr[h][1:2, :] + fgb[h]) for h in H]
        m_prev = [m_ref[h] for h in H]
        bcum_col = [jnp.dot(seen_f[dof(h)], jnp.broadcast_to(lf_col[h], (L, L)), precision=_HI,
                            preferred_element_type=jnp.float32) for h in H]
        bcum_row = [jnp.dot(jnp.broadcast_to(lf_row[h], (8, L)), seen_t[dof(h)], precision=_HI,
                            preferred_element_type=jnp.float32)[0:1] for h in H]
        qb = [q[h].astype(_BF) for h in H]
        qk = [lax.dot_general(qb[h], k[h].astype(_BF), _NT, preferred_element_type=jnp.float32) for h in H]
        qc = [jnp.dot(qb[h], cmat_ref[h].astype(_BF), preferred_element_type=jnp.float32) for h in H]
        log_d = [jnp.where(seen[dof(h)], bcum_col[h] - bcum_row[h] + li_row[h], -jnp.inf) for h in H]
        inter = [bcum_col[h][:, 0:1] + m_prev[h] for h in H]
        m_j = [jnp.maximum(jnp.max(log_d[h], axis=1, keepdims=True), inter[h]) for h in H]
        scores = [qk[h] * jnp.exp(log_d[h] - m_j[h]) for h in H]
        s_inter = [jnp.exp(inter[h] - m_j[h]) for h in H]
        sv = [jnp.dot(scores[h].astype(_BF), vb[h], preferred_element_type=jnp.float32) for h in H]
        b_last = [jnp.sum(lf_col[h], axis=0, keepdims=True) for h in H]
        log_w = [b_last[h] - bcum_col[h][:, 0:1] + li_col[h] for h in H]
        m_new = [jnp.maximum(b_last[h] + m_prev[h], jnp.max(log_w[h], axis=0, keepdims=True)) for h in H]
        kw = [k[h] * jnp.exp(log_w[h] - m_new[h]) for h in H]
        decay = [jnp.exp(b_last[h] + m_prev[h] - m_new[h]) for h in H]
        kv = [lax.dot_general(kw[h].astype(_BF), vb[h], _TN, preferred_element_type=jnp.float32) for h in H]
        for h in H:
            num = sv[h] + s_inter[h] * qc[h]
            den = (jnp.sum(scores[h], axis=1, keepdims=True)
                   + s_inter[h] * jnp.sum(q[h] * nvec_ref[h], axis=1, keepdims=True))
            h_refs[dof(h)][0, pl.ds(r0s[dof(h)], L), hs(h)] = num / jnp.maximum(jnp.abs(den), jnp.exp(-m_j[h]))
        for h in H:
            cmat_ref[h] = decay[h] * cmat_ref[h] + kv[h]
            nvec_ref[h] = decay[h] * nvec_ref[h] + jnp.sum(kw[h], axis=0, keepdims=True)
            m_ref[h] = m_new[h]
        return carry

    lax.fori_loop(0, n_sub, chunk, 0)


def _mlstm_pallas(qk, p3, gates, ig_b, fg_b, n_ctx):
    bsz, t_len, _ = qk.shape
    nb = t_len // ML_TBLK
    cb = n_ctx // ML_TBLK
    assert t_len % ML_TBLK == 0 and n_ctx % ML_TBLK == 0
    g = gates.reshape(bsz, t_len, 2, N_DIR, ML_HEADS)
    gcol = jnp.transpose(g, (3, 0, 4, 1, 2))
    grow = jnp.transpose(g.reshape(bsz, t_len // ML_CHUNK, ML_CHUNK, 2, N_DIR, ML_HEADS),
                         (4, 0, 5, 1, 3, 2))
    in_specs, out_specs = [], []
    for d in range(N_DIR):
        blk = functools.partial(lambda d, i: _scan_block(d, i, nb, cb), d)
        tok = lambda c, blk=blk: pl.BlockSpec((1, ML_TBLK, ML_W), lambda b, i, *_: (b, blk(i), c))
        in_specs += [tok(0), tok(1), tok(P_V // ML_W),
                     pl.BlockSpec((1, 1, ML_HEADS, ML_TBLK, 2),
                                  functools.partial(lambda d, blk, b, i, *_: (d, b, 0, blk(i), 0), d, blk)),
                     pl.BlockSpec((1, 1, ML_HEADS, ML_TBLK // ML_CHUNK, 2, ML_CHUNK),
                                  functools.partial(lambda d, blk, b, i, *_: (d, b, 0, blk(i), 0, 0), d, blk))]
        out_specs.append(tok(0))
    n_chain = N_DIR * ML_HEADS
    grid_spec = pltpu.PrefetchScalarGridSpec(
        num_scalar_prefetch=2,
        grid=(bsz, nb),
        in_specs=in_specs,
        out_specs=out_specs,
        scratch_shapes=[pltpu.VMEM((n_chain, ML_DH, ML_DH), jnp.float32),
                        pltpu.VMEM((n_chain, 1, ML_DH), jnp.float32),
                        pltpu.VMEM((n_chain, 1, 1), jnp.float32)])
    return pl.pallas_call(
        _mlstm_kernel, grid_spec=grid_spec,
        out_shape=[jax.ShapeDtypeStruct((bsz, t_len, ML_W), jnp.float32)] * N_DIR,
        compiler_params=pltpu.CompilerParams(
            dimension_semantics=("parallel", "arbitrary"),
            vmem_limit_bytes=VMEM_LIMIT_BYTES),
        name="mlstm_scan",
    )(ig_b, fg_b, *([qk, qk, p3, gcol, grow] * N_DIR))


def _to_pairs(x):
    return jnp.concatenate([x[:, p * LANES:(p + 1) * LANES] for p in range(RW_PAIRS)], axis=0)


def _dotf(a, b, dims=None):
    a = a.astype(_BF)
    b = b.astype(_BF)
    if dims is None:
        return jnp.dot(a, b, preferred_element_type=jnp.float32)
    return lax.dot_general(a, b, dims, preferred_element_type=jnp.float32)


def _rwkv_a_kernel(r_ref, k_ref, v_ref, kk_ref, a_ref, lw_ref,
                   att_ref, rt_ref, bw_ref, kw_ref, vt_ref, u0t_ref, y0_ref, wc_ref, *, n_sub):
    d = pl.program_id(0)
    C, R = RW_CHUNK, RW_ROWS
    row = lax.broadcasted_iota(jnp.int32, (R, R), 0)
    col = lax.broadcasted_iota(jnp.int32, (R, R), 1)
    same = (row // C) == (col // C)
    sign = 1 - 2 * d
    before = same & ((row - col) * sign > 0)
    upto = same & ((row - col) * sign >= 0)
    upto2 = jnp.concatenate([upto, upto], axis=1)
    eye = (row == col).astype(jnp.float32)
    first = col < RW_DH
    tpos = row % C

    def chunk_group(jg, carry):
        js = [jg * RW_GROUP + i for i in range(RW_GROUP)]
        G2 = [(i, h2) for i in range(RW_GROUP) for h2 in range(2)]
        pick = lambda lst, off: [jnp.where(first, lst[2 * i][:, off:off + R], lst[2 * i + 1][:, off:off + R])
                                 for i in range(RW_GROUP)]
        r0s = [pl.multiple_of(j * C, C) for j in js]
        ld = lambda ref: [_to_pairs(ref[0, pl.ds(r0, C), :]) for r0 in r0s]
        ldd = lambda ref: [_to_pairs(ref[0, 0, pl.ds(r0, C), :]) for r0 in r0s]
        r, v, kk = ld(r_ref), ld(v_ref), ld(kk_ref)
        k, a, lw = ldd(k_ref), ldd(a_ref), ldd(lw_ref)
        fwd = d == 0
        cum, aft = [], []
        for i in range(RW_GROUP):
            pre = lw[i]
            suf = lw[i]
            for s in (1, 2, 4, 8):
                pre = pre + jnp.where(tpos >= s, pltpu.roll(pre, s, axis=0), 0.0)
                suf = suf + jnp.where(tpos < C - s, pltpu.roll(suf, R - s, axis=0), 0.0)
            cum.append(jnp.where(fwd, pre, suf))
            aft.append(jnp.where(fwd, suf, pre) - lw[i])
        a_hat = [-kk[i] * jnp.exp(cum[i] - lw[i]) for i in range(RW_GROUP)]
        r_hat = [r[i] * jnp.exp(cum[i]) for i in range(RW_GROUP)]
        vb = [v[i].astype(_BF) for i in range(RW_GROUP)]
        m = []
        for i in range(RW_GROUP):
            inv_w = jnp.exp(-cum[i])
            lhs = jnp.concatenate([jnp.where(first, a_hat[i], 0.0), jnp.where(first, 0.0, a_hat[i]),
                                   jnp.where(first, r_hat[i], 0.0), jnp.where(first, 0.0, r_hat[i])], axis=0)
            rhs = jnp.concatenate([kk[i] * a[i] * inv_w, k[i] * inv_w], axis=0)
            m.append(_dotf(lhs, rhs, _NT))
        x = [jnp.where(before, m[i][h2 * R:(h2 + 1) * R, 0:R], 0.0).astype(_BF) for i, h2 in G2]
        ak = [jnp.where(before, m[i][h2 * R:(h2 + 1) * R, R:2 * R], 0.0) for i, h2 in G2]
        rbk = [jnp.where(upto2, m[i][(2 + h2) * R:(3 + h2) * R, :], 0.0).astype(_BF) for i, h2 in G2]
        akv = [_dotf(ak[g], vb[g // 2]) for g in range(len(G2))]
        x2 = [_dotf(xx, xx).astype(_BF) for xx in x]
        x4 = [_dotf(xx, xx).astype(_BF) for xx in x2]
        x8 = [_dotf(xx, xx).astype(_BF) for xx in x4]
        t = [eye + xx.astype(jnp.float32) for xx in x]
        t = [t[g] + _dotf(t[g], x2[g]) for g in range(len(G2))]
        t = [t[g] + _dotf(t[g], x4[g]) for g in range(len(G2))]
        t = [t[g] + _dotf(t[g], x8[g]) for g in range(len(G2))]
        akv = pick(akv, 0)
        rhs2 = [jnp.concatenate([a_hat[i], akv[i]], axis=1).astype(_BF) for i in range(RW_GROUP)]
        ta = [_dotf(t[g], rhs2[g // 2]) for g in range(len(G2))]
        at = pick(ta, 0)
        u0 = pick(ta, R)
        rhs3 = [jnp.concatenate([jnp.concatenate([at[i], u0[i]], axis=1).astype(_BF),
                                 jnp.concatenate([jnp.zeros_like(vb[i]), vb[i]], axis=1)], axis=0)
                for i in range(RW_GROUP)]
        ry = [_dotf(rbk[g], rhs3[g // 2]) for g in range(len(G2))]
        rt = pick(ry, 0)
        y0 = pick(ry, R)
        for i, j in enumerate(js):
            w_aft = jnp.exp(aft[i])
            att_ref[0, 0, j] = at[i].T.astype(_BF)
            u0t_ref[0, 0, j] = u0[i].T
            vt_ref[0, 0, j] = v[i].T.astype(_BF)
            rt_ref[0, 0, j] = (r_hat[i] + rt[i]).astype(_BF)
            bw_ref[0, 0, j] = (kk[i] * a[i] * w_aft).astype(_BF)
            kw_ref[0, 0, j] = (k[i] * w_aft).astype(_BF)
            y0_ref[0, 0, j] = y0[i]
            tot = cum[i] + aft[i]
            wc_ref[0, 0, j] = jnp.exp(jnp.concatenate([tot[p * C:p * C + 1] for p in range(RW_PAIRS)], axis=0))
        return carry

    lax.fori_loop(0, n_sub // RW_GROUP, chunk_group, 0)


def _rwkv_b_kernel(*refs, n_sub):
    ins, (yf_ref, yb_ref, s_ref) = refs[:16], refs[16:]
    C = RW_CHUNK

    @pl.when(pl.program_id(1) == 0)
    def _():
        s_ref[...] = jnp.zeros_like(s_ref)

    row = lax.broadcasted_iota(jnp.int32, (LANES, LANES), 0)
    col = lax.broadcasted_iota(jnp.int32, (LANES, LANES), 1)
    diag = (row // RW_DH) == (col // RW_DH)
    pair_of_row = lax.broadcasted_iota(jnp.int32, (2 * RW_ROWS, LANES), 0) % RW_ROWS // C
    DP = [(d, p) for d in range(N_DIR) for p in range(RW_PAIRS)]

    def chunk(jj, carry):
        cjs = [jj, n_sub - 1 - jj]
        att, rt, bw, kw, vt, u0t, y0, wc = [[ins[8 * d + a][0, 0, cjs[d]] for d in range(N_DIR)] for a in range(8)]
        bk = [jnp.concatenate([bw[d], kw[d]], axis=0) for d in range(N_DIR)]
        sps = [s_ref[d, p] for d, p in DP]
        spb = [sp.astype(_BF) for sp in sps]
        uts = [jnp.dot(spb[i], att[d], preferred_element_type=jnp.float32) + u0t[d] for i, (d, p) in enumerate(DP)]
        for i, (d, p) in enumerate(DP):
            rows = slice(p * C, (p + 1) * C)
            y = lax.dot_general(rt[d][rows], spb[i], _NT, preferred_element_type=jnp.float32) + y0[d][rows]
            y_ref = yf_ref if d == 0 else yb_ref
            y_ref[0, pl.ds(pl.multiple_of(cjs[d] * C, C), C), p * LANES:(p + 1) * LANES] = y
        for i, (d, p) in enumerate(DP):
            lhs = jnp.concatenate([uts[i].astype(_BF), vt[d]], axis=1)
            rhs = jnp.where(pair_of_row == p, bk[d], jnp.zeros_like(bk[d]))
            upd = jnp.dot(lhs, rhs, preferred_element_type=jnp.float32)
            s_ref[d, p] = jnp.where(diag, wc[d][p:p + 1, :] * sps[i] + upd, 0.0)
        return carry

    lax.fori_loop(0, n_sub, chunk, 0)


def _rwkv_pallas(r, v, kk, k_dir, a_dir, lw_dir, n_ctx):
    bsz, t_len, _ = r.shape
    nb, cb = t_len // RW_TBLK, n_ctx // RW_TBLK
    assert t_len % RW_TBLK == 0 and n_ctx % RW_TBLK == 0
    n_sub = RW_TBLK // RW_CHUNK
    nc = t_len // RW_CHUNK
    sh_spec = pl.BlockSpec((1, RW_TBLK, RW_W), lambda d, b, i: (b, _scan_block(d, i, nb, cb), 0))
    dr_spec = pl.BlockSpec((1, 1, RW_TBLK, RW_W), lambda d, b, i: (d, b, _scan_block(d, i, nb, cb), 0))
    ch_spec = pl.BlockSpec((1, 1, n_sub, RW_ROWS, LANES), lambda d, b, i: (d, b, _scan_block(d, i, nb, cb), 0, 0))
    wc_spec = pl.BlockSpec((1, 1, n_sub, RW_PAIRS, LANES), lambda d, b, i: (d, b, _scan_block(d, i, nb, cb), 0, 0))
    ch_shape = lambda dt: jax.ShapeDtypeStruct((N_DIR, bsz, nc, RW_ROWS, LANES), dt)
    params = pltpu.CompilerParams(dimension_semantics=("parallel", "parallel", "arbitrary"),
                                  vmem_limit_bytes=VMEM_LIMIT_BYTES)
    chunk_local = pl.pallas_call(
        functools.partial(_rwkv_a_kernel, n_sub=n_sub),
        grid=(N_DIR, bsz, nb),
        in_specs=[sh_spec, dr_spec, sh_spec, sh_spec, dr_spec, dr_spec],
        out_specs=[ch_spec] * 7 + [wc_spec],
        out_shape=[ch_shape(_BF)] * 5 + [ch_shape(jnp.float32)] * 2
                  + [jax.ShapeDtypeStruct((N_DIR, bsz, nc, RW_PAIRS, LANES), jnp.float32)],
        compiler_params=params, name="rwkv_chunk_local",
    )(r, k_dir, v, kk, a_dir, lw_dir)
    dir_specs = []
    for d in range(N_DIR):
        blk = functools.partial(lambda d, b, i: (d, b, _scan_block(d, i, nb, cb), 0, 0), d)
        dir_specs += [pl.BlockSpec((1, 1, n_sub, RW_ROWS, LANES), blk)] * 7
        dir_specs += [pl.BlockSpec((1, 1, n_sub, RW_PAIRS, LANES), blk)]
    y_specs = [pl.BlockSpec((1, RW_TBLK, RW_W),
                            functools.partial(lambda d, b, i: (b, _scan_block(d, i, nb, cb), 0), d))
               for d in range(N_DIR)]
    return pl.pallas_call(
        functools.partial(_rwkv_b_kernel, n_sub=n_sub),
        grid=(bsz, nb),
        in_specs=dir_specs,
        out_specs=y_specs,
        out_shape=[jax.ShapeDtypeStruct((bsz, t_len, RW_W), jnp.float32)] * N_DIR,
        scratch_shapes=[pltpu.VMEM((N_DIR, RW_PAIRS, LANES, LANES), jnp.float32)],
        compiler_params=pltpu.CompilerParams(dimension_semantics=("parallel", "arbitrary"),
                                             vmem_limit_bytes=VMEM_LIMIT_BYTES),
        name="rwkv_state_scan",
    )(*chunk_local, *chunk_local)


def _softplus(x):
    return jnp.maximum(x, 0.0) + jnp.log(1.0 + jnp.exp(-jnp.abs(x)))


def _rwkv_prep_kernel(*refs, blocks_per_seq, ctx_blocks):
    (zr, zk, zv, zw, za, zg, pr, pk, pv, pw, pa, pg, nr, nk, nv, nw, na, ng,
     mu_ref, w0_ref, a0_ref, wup_ref, aup_ref, gup_ref, kkw_ref, ka_ref, rk_ref, ones_ref,
     r_out, v_out, kk_out, kdir_out, a_out, lw_out, bonus_out, g_out) = refs
    blk = pl.program_id(0) % blocks_per_seq
    has_prev = (blk != 0) & (blk != ctx_blocks)
    has_next = (blk != ctx_blocks - 1) & (blk != blocks_per_seq - 1)
    tm = zr.shape[0]

    def lerp(z_ref, p_ref, n_ref, mu):
        z = z_ref[...]
        rows = lax.broadcasted_iota(jnp.int32, z.shape, 0)
        prev_row = jnp.where(has_prev, p_ref[RW_HALO - 1:RW_HALO, :], 0.0)
        next_row = jnp.where(has_next, n_ref[0:1, :], 0.0)
        before = jnp.where(rows == 0, prev_row, pltpu.roll(z, 1, axis=0))
        after = jnp.where(rows == tm - 1, next_row, pltpu.roll(z, tm - 1, axis=0))
        return z + mu * (0.5 * (before + after) - z)

    mu = mu_ref[...]
    r = lerp(zr, pr, nr, mu[:, 0:RW_W])
    k = lerp(zk, pk, nk, mu[:, RW_W:2 * RW_W])
    v = lerp(zv, pv, nv, mu[:, 2 * RW_W:3 * RW_W])
    wd = lerp(zw, pw, nw, mu[:, 3 * RW_W:3 * RW_W + LANES])
    ad = lerp(za, pa, na, mu[:, 3 * RW_W + LANES:3 * RW_W + 2 * LANES])
    gd = lerp(zg, pg, ng, mu[:, 3 * RW_W + 2 * LANES:3 * RW_W + 3 * LANES])
    dot = lambda a, w_ref: jnp.dot(a.astype(_BF), w_ref[...], preferred_element_type=jnp.float32)
    w_pre = dot(jnp.tanh(wd), wup_ref)
    a_pre = dot(ad, aup_ref)
    g_out[...] = dot(jax.nn.sigmoid(gd), gup_ref)

    def head_sum(x):
        hi = x.astype(_BF)
        lo = (x - hi.astype(jnp.float32)).astype(_BF)
        return (jnp.dot(hi, ones_ref[...], preferred_element_type=jnp.float32)
                + jnp.dot(lo, ones_ref[...], preferred_element_type=jnp.float32))

    kk = k * kkw_ref[...]
    kk_out[...] = kk / jnp.maximum(jnp.sqrt(head_sum(kk * kk)), 1e-12)
    r_out[...] = r
    v_out[...] = v
    rk = r * rk_ref[...]
    bonus = jnp.zeros_like(r)
    for d in range(N_DIR):
        cols = slice(d * RW_W, (d + 1) * RW_W)
        lw_out[d] = -jnp.exp(-_softplus(-(w0_ref[d:d + 1, :] + w_pre[:, cols])) - RW_DECAY_OFFSET)
        a = jax.nn.sigmoid(a0_ref[d:d + 1, :] + a_pre[:, cols])
        a_out[d] = a
        k_dir = k * (1.0 + (a - 1.0) * ka_ref[...])
        kdir_out[d] = k_dir
        bonus = bonus + rk * k_dir
    bonus_out[...] = head_sum(bonus) * v


def _rwkv_prep(p2, blocks_per_seq, ctx_blocks, mu, w0, w_up, a0, a_up, g_up, k_k, k_a, r_k):
    n = p2.shape[0]
    tm = ROW_TM
    hb = tm // RW_HALO
    n_hb = n // RW_HALO
    cw, cl = P_RW // RW_W, P_RWLR // LANES
    assert P_RW % RW_W == 0 and P_RWLR % LANES == 0 and N_DIR * RW_DECAY_RANK == LANES and RW_G_RANK == LANES

    def specs(rows, row_index):
        wide = [pl.BlockSpec((rows, RW_W), functools.partial(lambda j, i: (row_index(i), cw + j), j))
                for j in range(3)]
        return wide + [pl.BlockSpec((rows, LANES), functools.partial(lambda j, i: (row_index(i), cl + j), j))
                       for j in range(3)]

    main = specs(tm, lambda i: i)
    prev = specs(RW_HALO, lambda i: jnp.maximum(i * hb - 1, 0))
    nxt = specs(RW_HALO, lambda i: jnp.minimum((i + 1) * hb, n_hb - 1))
    zero = jnp.zeros((RW_DECAY_RANK, RW_W), jnp.float32)
    both_dirs = lambda up: jnp.concatenate([jnp.concatenate([up[0], zero], axis=1),
                                            jnp.concatenate([zero, up[1]], axis=1)], axis=0).astype(_BF)
    head = jnp.arange(RW_W) // RW_DH
    ones_bd = (head[:, None] == head[None, :]).astype(_BF)
    vec = lambda w: pl.BlockSpec((1, w), lambda i: (0, 0))
    dvec = pl.BlockSpec((N_DIR, RW_W), lambda i: (0, 0))
    consts = [vec(mu.shape[0]), dvec, dvec, _const_spec((LANES, N_DIR * RW_W)), _const_spec((LANES, N_DIR * RW_W)),
              _const_spec((RW_G_RANK, RW_W)), vec(RW_W), vec(RW_W), vec(RW_W), _const_spec((RW_W, RW_W))]
    row = pl.BlockSpec((tm, RW_W), lambda i: (i, 0))
    drow = pl.BlockSpec((N_DIR, tm, RW_W), lambda i: (0, i, 0))
    sh = jax.ShapeDtypeStruct((n, RW_W), jnp.float32)
    dsh = jax.ShapeDtypeStruct((N_DIR, n, RW_W), jnp.float32)
    return pl.pallas_call(
        functools.partial(_rwkv_prep_kernel, blocks_per_seq=blocks_per_seq, ctx_blocks=ctx_blocks),
        grid=(n // tm,),
        in_specs=main + prev + nxt + consts,
        out_specs=[row, row, row, drow, drow, drow, row, row],
        out_shape=[sh, sh, sh, dsh, dsh, dsh, sh, sh],
        compiler_params=pltpu.CompilerParams(dimension_semantics=("parallel",), vmem_limit_bytes=VMEM_LIMIT_BYTES),
        name="rwkv_prep",
    )(*([p2] * 18), mu.reshape(1, -1), w0, a0, both_dirs(w_up), both_dirs(a_up), g_up.astype(_BF),
      k_k.reshape(1, -1), k_a.reshape(1, -1), r_k.reshape(1, -1), ones_bd)


def _layer_norm(x, eps=LN_EPS):
    mu = jnp.mean(x, axis=-1, keepdims=True)
    var = jnp.mean(jnp.square(x - mu), axis=-1, keepdims=True)
    return (x - mu) * lax.rsqrt(var + eps)


def _modulate(x, shift, scale):
    return _layer_norm(x) * (1.0 + scale) + shift


def _conv_silu_kernel(x_ref, prev_ref, next_ref, w_ref, b_ref, o_ref, *, blocks_per_seq, ctx_blocks):
    blk = pl.program_id(0) % blocks_per_seq
    is_lat = blk >= ctx_blocks
    has_prev = is_lat & (blk != ctx_blocks)
    has_next = is_lat & (blk != blocks_per_seq - 1)
    tm, width = x_ref.shape
    ext = jnp.concatenate([jnp.where(has_prev, prev_ref[...], 0.0), x_ref[...],
                           jnp.where(has_next, next_ref[...], 0.0)], axis=0)
    n_ext = ext.shape[0]
    t = lax.broadcasted_iota(jnp.int32, (tm, 1), 0)
    col = jnp.where(is_lat, t % GRID_W, t)
    last_col = jnp.where(is_lat, GRID_W - 1, tm - 1)
    w = w_ref[...]
    acc = jnp.zeros((tm, width), jnp.float32) + b_ref[...]
    for dc in (-1, 0, 1):
        shifted = ext if dc == 0 else pltpu.roll(ext, (-dc) % n_ext, axis=0)
        col_ok = (col != 0) if dc == -1 else ((col != last_col) if dc == 1 else None)
        for dr in (-1, 0, 1):
            src = shifted[GRID_W + dr * GRID_W:GRID_W + dr * GRID_W + tm]
            ok = col_ok if dr == 0 else (is_lat if col_ok is None else (col_ok & is_lat))
            if ok is not None:
                src = jnp.where(ok, src, 0.0)
            tap = (dr + 1) * 3 + (dc + 1)
            acc = acc + src * w[tap:tap + 1, :]
    o_ref[...] = acc * jax.nn.sigmoid(acc)


def _conv_silu(p2, blocks_per_seq, ctx_blocks, conv_w, conv_b):
    n = p2.shape[0]
    tm = ROW_TM
    width = 2 * ML_W
    assert ctx_blocks == 1 and tm % GRID_W == 0 and P_QK % width == 0
    hb = tm // GRID_W
    n_hb = n // GRID_W
    cb = P_QK // width
    return pl.pallas_call(
        functools.partial(_conv_silu_kernel, blocks_per_seq=blocks_per_seq, ctx_blocks=ctx_blocks),
        grid=(n // tm,),
        in_specs=[pl.BlockSpec((tm, width), lambda i: (i, cb)),
                  pl.BlockSpec((GRID_W, width), lambda i: (jnp.maximum(i * hb - 1, 0), cb)),
                  pl.BlockSpec((GRID_W, width), lambda i: (jnp.minimum((i + 1) * hb, n_hb - 1), cb)),
                  pl.BlockSpec((9, width), lambda i: (0, 0)), pl.BlockSpec((1, width), lambda i: (0, 0))],
        out_specs=pl.BlockSpec((tm, width), lambda i: (i, 0)),
        out_shape=jax.ShapeDtypeStruct((n, width), jnp.float32),
        compiler_params=pltpu.CompilerParams(dimension_semantics=("parallel",), vmem_limit_bytes=VMEM_LIMIT_BYTES),
        name="mlstm_conv_silu",
    )(p2, p2, p2, conv_w.reshape(9, width), conv_b.reshape(1, width))


def _rwkv7_branch(p2, bsz, n_ctx, mu, w0, w_up, a0, a_up, g_up, k_k, k_a, r_k):
    t_len = p2.shape[0] // bsz
    r, v, kk, k_dir, a, log_decay, bonus, g = _rwkv_prep(p2, t_len // ROW_TM, n_ctx // ROW_TM, mu, w0, w_up, a0,
                                                         a_up, g_up, k_k, k_a, r_k)
    seq = lambda z: z.reshape(z.shape[:-2] + (bsz, t_len, RW_W))
    y_f, y_b = _rwkv_pallas(seq(r), seq(v), seq(kk), seq(k_dir), seq(a), seq(log_decay), n_ctx)
    return y_f.reshape(-1, RW_W), y_b.reshape(-1, RW_W), bonus, g


def _token_mixer(u, n_ctx, w_in,
                 ml_conv_w, ml_conv_b, ml_ig_b, ml_fg_b, ml_norm_g, ml_norm_b, ml_proj,
                 rw_mu, rw_w0, rw_w_up, rw_a0, rw_a_up, rw_g_up, rw_k_k, rw_k_a, rw_r_k,
                 rw_norm_g, rw_norm_b, rw_proj,
                 s5_lam_re, s5_lam_im, s5_log_dt, s5_b_re, s5_b_im, s5_c_re, s5_c_im, s5_d,
                 s5_w_val, s5_w_gate):
    bsz, t_len, _ = u.shape
    n = bsz * t_len
    p = _mm_any(u, _permute_w_in(w_in), tm=PROJ_TM, keep_col_pad=True)

    def col(start, width):
        return p[..., start:start + width]

    n_gate = N_DIR * ML_HEADS
    p2 = p.reshape(n, -1)
    qk = _conv_silu(p2, t_len // ROW_TM, n_ctx // ROW_TM, ml_conv_w, ml_conv_b)
    h_f, h_b = _mlstm_pallas(qk.reshape(bsz, t_len, 2 * ML_W), p, col(P_MLG, 2 * n_gate), ml_ig_b, ml_fg_b, n_ctx)
    y_f, y_b, bonus, g = _rwkv7_branch(p2, bsz, n_ctx, rw_mu, rw_w0, rw_w_up, rw_a0, rw_a_up, rw_g_up,
                                       rw_k_k, rw_k_a, rw_r_k)
    s5 = _s5_pallas(p, n_ctx, s5_lam_re, s5_lam_im, s5_log_dt, s5_b_re, s5_b_im, s5_c_re, s5_c_im, s5_d)
    ml, rw = _post_scan(p2, h_f.reshape(n, ML_W), h_b.reshape(n, ML_W), y_f, y_b, bonus, g,
                        ml_norm_g, ml_norm_b, rw_norm_g, rw_norm_b)
    return _merge_pallas(p2, ml, rw, s5.reshape(S5_W // LANES, n, LANES),
                         ml_proj.astype(_BF), rw_proj.astype(_BF), s5_w_val.astype(_BF), s5_w_gate.astype(_BF))


def _permute_w_in(w_in):
    offs = np.cumsum((0,) + IN_WIDTHS)
    seg = lambda a, b: w_in[:, offs[a]:offs[b]]
    return jnp.concatenate([seg(0, 4), seg(6, 9), seg(12, 13), seg(13, 14), seg(9, 12), seg(4, 6)], axis=1)


def _moe_ffn(u, router_w, router_b, layer, w_gate, w_up, w_down):
    gates_t = _router_pallas(u, router_w, router_b)
    return _moe_routed(u, gates_t, layer, w_gate, w_up, w_down)


def kernel(x, c, ctx, c_ctx, ada_w, ada_b, w_in, ml_conv_w, ml_conv_b, ml_ig_b, ml_fg_b, ml_norm_g,
           ml_norm_b, ml_proj, rw_mu, rw_w0, rw_w_up, rw_a0, rw_a_up, rw_g_up, rw_k_k, rw_k_a, rw_r_k,
           rw_norm_g, rw_norm_b, rw_proj, s5_lam_re, s5_lam_im, s5_log_dt, s5_b_re, s5_b_im, s5_c_re,
           s5_c_im, s5_d, s5_w_val, s5_w_gate, w_out, ln1_g, ln1_b, ln2_g, ln2_b, router_w, router_b,
           exp_w_gate, exp_w_up, exp_w_down):
    bsz, n_ctx = ctx.shape[0], ctx.shape[1]
    t_len = n_ctx + x.shape[1]
    assert n_ctx % ROW_TM == 0 and t_len % ROW_TM == 0
    blocks_per_seq, ctx_blocks = t_len // ROW_TM, n_ctx // ROW_TM
    silu_c = jax.nn.silu(c)
    silu_cc = jax.nn.silu(c_ctx)[None, :]
    mods = []
    for i in range(DEPTH):
        mx = _mm_any(silu_c, ada_w[i]) + ada_b[i]
        mc = jnp.broadcast_to(_mm_any(silu_cc, ada_w[i]) + ada_b[i], mx.shape)
        mods.append(jnp.stack([mc, mx], axis=1).reshape(bsz, 2, N_MOD, 1, D_MODEL))
    xa = jnp.concatenate([ctx, x], axis=1).reshape(bsz * t_len, D_MODEL)
    m0 = mods[0]
    u = jnp.concatenate([_modulate(ctx, m0[:, 0, 0], m0[:, 0, 1]), _modulate(x, m0[:, 1, 0], m0[:, 1, 1])],
                        axis=1).astype(_BF)
    for i in range(DEPTH):
        z = _token_mixer(
            u, n_ctx, w_in[i],
            ml_conv_w[i], ml_conv_b[i], ml_ig_b[i], ml_fg_b[i], ml_norm_g[i], ml_norm_b[i], ml_proj[i],
            rw_mu[i], rw_w0[i], rw_w_up[i], rw_a0[i], rw_a_up[i], rw_g_up[i], rw_k_k[i], rw_k_a[i], rw_r_k[i],
            rw_norm_g[i], rw_norm_b[i], rw_proj[i],
            s5_lam_re[i], s5_lam_im[i], s5_log_dt[i], s5_b_re[i], s5_b_im[i], s5_c_re[i], s5_c_im[i], s5_d[i],
            s5_w_val[i], s5_w_gate[i])
        xa, u_ffn = _resid_norm_mod(z, w_out[i].astype(_BF), xa, mods[i], 2, ln1_g[i], ln1_b[i], mods[i], 3, 4,
                                    blocks_per_seq, ctx_blocks)
        ffn = _moe_ffn(u_ffn, router_w, router_b, i, exp_w_gate, exp_w_up, exp_w_down)
        xa, u = _resid_norm_mod(ffn, None, xa, mods[i], 5, ln2_g[i], ln2_b[i], mods[min(i + 1, DEPTH - 1)], 0, 1,
                                blocks_per_seq, ctx_blocks, u_dtype=_BF)
        u = u.reshape(bsz, t_len, D_MODEL)
    return xa.reshape(bsz, t_len, D_MODEL)[:, n_ctx:]
```

```python
import functools
import math

import jax
import jax.numpy as jnp
import numpy as np
from jax import lax
from jax.experimental import pallas as pl
from jax.experimental.pallas import tpu as pltpu

D_MODEL = 2048
DEPTH = 2
GRID_W = 64
N_DIR = 2
ML_HEADS = 4
ML_DH = 256
ML_W = ML_HEADS * ML_DH
ML_CHUNK = 64
ML_NORM_EPS = 1e-6
RW_HEADS = 16
RW_DH = 64
RW_W = RW_HEADS * RW_DH
RW_DECAY_RANK = 64
RW_A_RANK = 64
RW_G_RANK = 128
RW_DECAY_OFFSET = 0.5
RW_NORM_EPS = 64e-5
S5_W = 1024
S5_GROUP = 16
S5_GROUPS = S5_W // S5_GROUP
S5_STATE = 64
N_BRANCH = 3
N_GROUPS = 4
EXPERTS_PER_GROUP = 4
N_EXPERTS = N_GROUPS * EXPERTS_PER_GROUP
TOP_K = 2
D_EXPERT = 1024
DEEPNORM_ALPHA = (2.0 * DEPTH) ** 0.25
LN_EPS = 1e-5
N_MOD = 6
IN_WIDTHS = (ML_W, ML_W, ML_W, ML_W, N_DIR * ML_HEADS, N_DIR * ML_HEADS,
             RW_W, RW_W, RW_W, N_DIR * RW_DECAY_RANK, N_DIR * RW_A_RANK, RW_G_RANK,
             S5_W, N_BRANCH * D_MODEL)
D_IN = sum(IN_WIDTHS)
RW_IN_WIDTHS = (RW_W, RW_W, RW_W, N_DIR * RW_DECAY_RANK, N_DIR * RW_A_RANK, RW_G_RANK)

VMEM_LIMIT_BYTES = 56 * 1024 * 1024
LANES = 128

S5_CHUNK = 16
S5_UNROLL = 8
RL_TBLK = 256
PROJ_TM = 1536
ML_TBLK = 256
RW_CHUNK = 16
RW_PAIRS = RW_HEADS // 2
RW_ROWS = RW_PAIRS * RW_CHUNK
RW_TBLK = 256
RW_GROUP = 8
RW_HALO = 8
MOE_TM = 256
ROW_TM = 256

P_QK, P_V, P_O = 0, 2 * ML_W, 3 * ML_W
P_RW = 4 * ML_W
P_S5 = P_RW + 3 * RW_W
P_GATE = P_S5 + S5_W
P_RWLR = P_GATE + N_BRANCH * D_MODEL
P_MLG = P_RWLR + 2 * N_DIR * RW_DECAY_RANK + RW_G_RANK
assert P_GATE % D_MODEL == 0 and P_MLG + 2 * N_DIR * ML_HEADS == D_IN and RW_DECAY_RANK == RW_A_RANK

_BF = jnp.bfloat16
_HI = lax.Precision.HIGHEST
_NT = (((1,), (1,)), ((), ()))
_TN = (((0,), (0,)), ((), ()))


def _scan_block(d, i, n_blocks, ctx_blocks):
    bwd = jnp.where(i < ctx_blocks, ctx_blocks - 1 - i, n_blocks - 1 + ctx_blocks - i)
    return jnp.where(d == 0, i, bwd)


def _mm_kernel(a_ref, w_ref, o_ref, abf_ref):
    @pl.when(pl.program_id(1) == 0)
    def _():
        abf_ref[...] = a_ref[...].astype(_BF)

    o_ref[...] = jnp.dot(abf_ref[...], w_ref[...], preferred_element_type=jnp.float32)


def _mm_bf16_kernel(a_ref, w_ref, o_ref):
    o_ref[...] = jnp.dot(a_ref[...], w_ref[...], preferred_element_type=jnp.float32)


def _mm(a, w, tm, tn):
    m, k = a.shape
    n = w.shape[1]
    assert m % tm == 0 and n % tn == 0, (m, n, tm, tn)
    if a.dtype == _BF:
        return pl.pallas_call(
            _mm_bf16_kernel,
            grid=(m // tm, n // tn),
            in_specs=[pl.BlockSpec((tm, k), lambda i, j: (i, 0)),
                      pl.BlockSpec((k, tn), lambda i, j: (0, j))],
            out_specs=pl.BlockSpec((tm, tn), lambda i, j: (i, j)),
            out_shape=jax.ShapeDtypeStruct((m, n), jnp.float32),
            compiler_params=pltpu.CompilerParams(
                dimension_semantics=("parallel", "arbitrary"),
                vmem_limit_bytes=VMEM_LIMIT_BYTES),
            name="mm_bf16",
        )(a, w)
    return pl.pallas_call(
        _mm_kernel,
        grid=(m // tm, n // tn),
        in_specs=[pl.BlockSpec((tm, k), lambda i, j: (i, 0)),
                  pl.BlockSpec((k, tn), lambda i, j: (0, j))],
        out_specs=pl.BlockSpec((tm, tn), lambda i, j: (i, j)),
        out_shape=jax.ShapeDtypeStruct((m, n), jnp.float32),
        scratch_shapes=[pltpu.VMEM((tm, k), _BF)],
        compiler_params=pltpu.CompilerParams(
            dimension_semantics=("parallel", "arbitrary"),
            vmem_limit_bytes=VMEM_LIMIT_BYTES),
        name="mm",
    )(a, w)


def _mm_any(a, w, tm=1024, tn=512, keep_col_pad=False):
    lead = a.shape[:-1]
    a2 = a.reshape(-1, a.shape[-1])
    m, n = a2.shape[0], w.shape[1]
    mp = -(-m // 8) * 8
    if mp > tm:
        mp = -(-m // tm) * tm
    tm = min(tm, mp)
    np_ = -(-n // LANES) * LANES
    if np_ > tn:
        np_ = -(-n // tn) * tn
    tn = min(tn, np_)
    if mp != m:
        a2 = jnp.pad(a2, ((0, mp - m), (0, 0)))
    wb = w.astype(_BF)
    if np_ != n:
        wb = jnp.pad(wb, ((0, 0), (0, np_ - n)))
    out = _mm(a2, wb, tm, tn)
    if keep_col_pad:
        n = np_
    if mp != m or np_ != n:
        out = out[:m, :n]
    return out.reshape(lead + (n,))


def _const_spec(shape):
    nd = len(shape)
    return pl.BlockSpec(shape, lambda i: (0,) * nd, pipeline_mode=pl.Buffered(1))


def _merge_kernel(ml_ref, rw_ref, s5_ref, g0_ref, g1_ref, g2_ref, wml_ref, wrw_ref, wval_ref, wgate_ref, z_ref):
    dot = lambda a, w_ref: jnp.dot(a.astype(_BF), w_ref[...], preferred_element_type=jnp.float32)
    s5 = jnp.concatenate([s5_ref[m] for m in range(S5_W // LANES)], axis=1)
    sval = dot(s5, wval_ref) * jax.nn.sigmoid(dot(s5, wgate_ref))
    z = (jax.nn.sigmoid(g0_ref[...]) * dot(ml_ref[...], wml_ref)
         + jax.nn.sigmoid(g1_ref[...]) * dot(rw_ref[...], wrw_ref)
         + jax.nn.sigmoid(g2_ref[...]) * sval)
    z_ref[...] = z.astype(_BF)


def _merge_pallas(p2, ml, rw, s5, wml, wrw, wval, wgate):
    n, w = ml.shape
    tm = ROW_TM
    gb = P_GATE // D_MODEL
    row = lambda i: (i, 0)
    return pl.pallas_call(
        _merge_kernel,
        grid=(n // tm,),
        in_specs=[pl.BlockSpec((tm, w), row)] * 2 + [pl.BlockSpec((S5_W // LANES, tm, LANES), lambda i: (0, i, 0))]
                 + [pl.BlockSpec((tm, D_MODEL), functools.partial(lambda j, i: (i, gb + j), j))
                    for j in range(N_BRANCH)]
                 + [_const_spec((w, D_MODEL))] * 4,
        out_specs=pl.BlockSpec((tm, D_MODEL), row),
        out_shape=jax.ShapeDtypeStruct((n, D_MODEL), _BF),
        compiler_params=pltpu.CompilerParams(dimension_semantics=("parallel",), vmem_limit_bytes=VMEM_LIMIT_BYTES),
        name="merge_gate_proj",
    )(ml, rw, s5, p2, p2, p2, wml, wrw, wval, wgate)


def _post_scan_kernel(hf_ref, hb_ref, o_ref, yf_ref, yb_ref, bonus_ref, g_ref, mlg_ref, mlb_ref, rwg_ref, rwb_ref,
                      ones_ref, ml_ref, rw_ref):
    h = hf_ref[...] + hb_ref[...]
    parts = []
    for hd in range(ML_HEADS):
        x = h[:, hd * ML_DH:(hd + 1) * ML_DH]
        xc = x - jnp.mean(x, axis=-1, keepdims=True)
        parts.append(xc * lax.rsqrt(jnp.mean(xc * xc, axis=-1, keepdims=True) + ML_NORM_EPS))
    hn = jnp.concatenate(parts, axis=1) * mlg_ref[...] + mlb_ref[...]
    ml_ref[...] = (jax.nn.sigmoid(o_ref[...]) * hn).astype(_BF)

    def head_mean(x):
        hi = x.astype(_BF)
        lo = (x - hi.astype(jnp.float32)).astype(_BF)
        s = (jnp.dot(hi, ones_ref[...], preferred_element_type=jnp.float32)
             + jnp.dot(lo, ones_ref[...], preferred_element_type=jnp.float32))
        return s * (1.0 / RW_DH)

    y = yf_ref[...] + yb_ref[...]
    yc = y - head_mean(y)
    yn = yc * lax.rsqrt(head_mean(yc * yc) + RW_NORM_EPS) * rwg_ref[...] + rwb_ref[...]
    rw_ref[...] = ((yn + bonus_ref[...]) * g_ref[...]).astype(_BF)


def _post_scan(p2, h_f, h_b, y_f, y_b, bonus, g, ml_norm_g, ml_norm_b, rw_norm_g, rw_norm_b):
    n = y_f.shape[0]
    tm = ROW_TM
    head = jnp.arange(RW_W) // RW_DH
    ones_bd = (head[:, None] == head[None, :]).astype(_BF)
    vec = pl.BlockSpec((1, RW_W), lambda i: (0, 0))
    blk = pl.BlockSpec((tm, RW_W), lambda i: (i, 0))
    return pl.pallas_call(
        _post_scan_kernel,
        grid=(n // tm,),
        in_specs=[blk, blk,
                  pl.BlockSpec((tm, ML_W), lambda i: (i, P_O // ML_W)), blk, blk, blk, blk, vec, vec, vec, vec,
                  _const_spec((RW_W, RW_W))],
        out_specs=[blk, blk],
        out_shape=[jax.ShapeDtypeStruct((n, ML_W), _BF), jax.ShapeDtypeStruct((n, RW_W), _BF)],
        compiler_params=pltpu.CompilerParams(dimension_semantics=("parallel",), vmem_limit_bytes=VMEM_LIMIT_BYTES),
        name="post_scan_norm_gate",
    )(h_f, h_b, p2, y_f, y_b, bonus, g, ml_norm_g.reshape(1, -1), ml_norm_b.reshape(1, -1),
      rw_norm_g.reshape(1, -1), rw_norm_b.reshape(1, -1), ones_bd)


def _ln_rows(x, eps=LN_EPS):
    mu = jnp.mean(x, axis=-1, keepdims=True)
    xc = x - mu
    var = jnp.mean(xc * xc, axis=-1, keepdims=True)
    return xc * lax.rsqrt(var + eps)


def _resid_kernel(*refs, with_w):
    if with_w:
        d_ref, w_ref, x_ref, gate_ref, g_ref, b_ref, sh_ref, sc_ref, xo_ref, uo_ref = refs
        delta = jnp.dot(d_ref[...], w_ref[...], preferred_element_type=jnp.float32)
    else:
        d_ref, x_ref, gate_ref, g_ref, b_ref, sh_ref, sc_ref, xo_ref, uo_ref = refs
        delta = d_ref[...]
    xn = _ln_rows(DEEPNORM_ALPHA * x_ref[...] + gate_ref[0, 0, 0] * delta) * g_ref[...] + b_ref[...]
    xo_ref[...] = xn
    uo_ref[...] = (_ln_rows(xn) * (1.0 + sc_ref[0, 0, 0]) + sh_ref[0, 0, 0]).astype(uo_ref.dtype)


def _final_resid_kernel(d_ref, x_ref, gate_ref, g_ref, b_ref, xo_ref):
    xo_ref[...] = (_ln_rows(DEEPNORM_ALPHA * x_ref[...] + gate_ref[0, 0, 0] * d_ref[...]) * g_ref[...]
                   + b_ref[...])


def _final_resid_norm(delta, x, mod_a, ia, ln_g, ln_b, bsz, blocks_per_seq, ctx_blocks):
    n, d = x.shape
    tm = ROW_TM
    lat_blocks = blocks_per_seq - ctx_blocks
    src = lambda i: (i // lat_blocks * blocks_per_seq + ctx_blocks + i % lat_blocks, 0)
    vec = pl.BlockSpec((1, d), lambda i: (0, 0))
    return pl.pallas_call(
        _final_resid_kernel,
        grid=(bsz * lat_blocks,),
        in_specs=[pl.BlockSpec((tm, d), src), pl.BlockSpec((tm, d), src),
                  pl.BlockSpec((1, 1, 1, 1, d), lambda i: (i // lat_blocks, 1, ia, 0, 0)), vec, vec],
        out_specs=pl.BlockSpec((tm, d), lambda i: (i, 0)),
        out_shape=jax.ShapeDtypeStruct((bsz * lat_blocks * tm, d), jnp.float32),
        compiler_params=pltpu.CompilerParams(dimension_semantics=("parallel",), vmem_limit_bytes=VMEM_LIMIT_BYTES),
        name="final_resid_norm",
    )(delta, x, mod_a, ln_g.reshape(1, d), ln_b.reshape(1, d))


def _resid_norm_mod(delta, w, x, mod_a, ia, ln_g, ln_b, mod_b, ish, isc, blocks_per_seq, ctx_blocks,
                    u_dtype=jnp.float32):
    n, d = x.shape
    tm = ROW_TM
    row = lambda i: (i, 0)

    def mod_spec(m):
        return pl.BlockSpec((1, 1, 1, 1, d), lambda i: (i // blocks_per_seq,
                                                       (i % blocks_per_seq >= ctx_blocks).astype(jnp.int32), m, 0, 0))

    vec = pl.BlockSpec((1, d), lambda i: (0, 0))
    if w is not None:
        in_specs = [pl.BlockSpec((tm, delta.shape[1]), row), _const_spec(w.shape), pl.BlockSpec((tm, d), row)]
        args = (delta, w, x)
    else:
        in_specs = [pl.BlockSpec((tm, d), row), pl.BlockSpec((tm, d), row)]
        args = (delta, x)
    return pl.pallas_call(
        functools.partial(_resid_kernel, with_w=w is not None),
        grid=(n // tm,),
        in_specs=in_specs + [mod_spec(ia), vec, vec, mod_spec(ish), mod_spec(isc)],
        out_specs=[pl.BlockSpec((tm, d), row)] * 2,
        out_shape=[jax.ShapeDtypeStruct((n, d), jnp.float32), jax.ShapeDtypeStruct((n, d), u_dtype)],
        compiler_params=pltpu.CompilerParams(dimension_semantics=("parallel",), vmem_limit_bytes=VMEM_LIMIT_BYTES),
        name="resid_norm_mod",
    )(*args, mod_a, ln_g.reshape(1, d), ln_b.reshape(1, d), mod_b, mod_b)


def _moe_ffn_kernel(te_ref, tv_ref, x_ref, wg_ref, wu_ref, wd_ref, y_ref, wgb_ref, wub_ref, wdb_ref):
    t = pl.program_id(0)

    @pl.when((t == 0) | (te_ref[t] != te_ref[jnp.maximum(t - 1, 0)]))
    def _():
        wgb_ref[...] = wg_ref[0, 0].astype(_BF)
        wub_ref[...] = wu_ref[0, 0].astype(_BF)
        wdb_ref[...] = wd_ref[0, 0].astype(_BF)

    @pl.when(tv_ref[t] == 1)
    def _():
        x = x_ref[...].astype(_BF)
        hg = jnp.dot(x, wgb_ref[...], preferred_element_type=jnp.float32)
        hu = jnp.dot(x, wub_ref[...], preferred_element_type=jnp.float32)
        h = (hg * jax.nn.sigmoid(hg)) * hu
        y_ref[...] = jnp.dot(h.astype(_BF), wdb_ref[...], preferred_element_type=jnp.float32)

    @pl.when(tv_ref[t] == 0)
    def _():
        y_ref[...] = jnp.zeros_like(y_ref)


def _moe_routed(u, gates_t, layer, wg, wu, wd):
    n_tok, d = u.shape
    _, n_e, _, d_e = wg.shape
    tm = MOE_TM
    n_tiles = (TOP_K * n_tok) // tm + n_e
    n_slots = n_tiles * tm
    sel = gates_t > 0.0
    seli = sel.astype(jnp.int32)
    rank = jnp.cumsum(seli, axis=1) - 1
    cnt = jnp.sum(seli, axis=1)
    tiles_e = (cnt + tm - 1) // tm
    tile_end = jnp.cumsum(tiles_e)
    off = (tile_end - tiles_e) * tm
    slot = off[:, None] + rank
    order = jnp.cumsum(seli, axis=0)
    slots, owned, gsel = [], [], []
    for j in range(TOP_K):
        pick = sel & (order == j + 1)
        slots.append(jnp.sum(jnp.where(pick, slot, 0), axis=0))
        owned.append(jnp.where(jnp.any(pick, axis=0), slots[j], n_slots))
        gsel.append(jnp.sum(jnp.where(pick, gates_t, 0.0), axis=0))
    tok = jnp.arange(n_tok, dtype=jnp.int32)
    tok_of_slot = jnp.zeros((n_slots,), jnp.int32).at[jnp.concatenate(owned)].set(jnp.tile(tok, TOP_K), mode='drop')
    tile_ids = jnp.arange(n_tiles, dtype=jnp.int32)
    tile_valid = (tile_ids < tile_end[-1]).astype(jnp.int32)
    tile_expert = jnp.minimum(jnp.searchsorted(tile_end, tile_ids, side='right'), n_e - 1).astype(jnp.int32)
    last_e = jnp.max(jnp.where(cnt > 0, jnp.arange(n_e), 0)).astype(jnp.int32)
    tile_expert = jnp.where(tile_valid == 1, tile_expert, last_e)
    xs = u.astype(_BF).at[tok_of_slot].get(mode='promise_in_bounds')
    w_index = lambda t, te, tv: (layer, te[t], 0, 0)
    grid_spec = pltpu.PrefetchScalarGridSpec(
        num_scalar_prefetch=2,
        grid=(n_tiles,),
        in_specs=[pl.BlockSpec((tm, d), lambda t, te, tv: (t, 0)),
                  pl.BlockSpec((1, 1, d, d_e), w_index, pipeline_mode=pl.Buffered(1)),
                  pl.BlockSpec((1, 1, d, d_e), w_index, pipeline_mode=pl.Buffered(1)),
                  pl.BlockSpec((1, 1, d_e, d), w_index, pipeline_mode=pl.Buffered(1))],
        out_specs=pl.BlockSpec((tm, d), lambda t, te, tv: (t, 0)),
        scratch_shapes=[pltpu.VMEM((d, d_e), _BF), pltpu.VMEM((d, d_e), _BF), pltpu.VMEM((d_e, d), _BF)])
    ys = pl.pallas_call(
        _moe_ffn_kernel, grid_spec=grid_spec,
        out_shape=jax.ShapeDtypeStruct((n_slots, d), jnp.float32),
        compiler_params=pltpu.CompilerParams(dimension_semantics=("arbitrary",),
                                             vmem_limit_bytes=VMEM_LIMIT_BYTES),
        name="moe_routed_ffn",
    )(tile_expert, tile_valid, xs, wg, wu, wd)
    return sum(gsel[j][:, None] * ys.at[slots[j]].get(mode='promise_in_bounds') for j in range(TOP_K))


def _router_kernel(u_ref, wt_ref, b_ref, g_ref):
    logits = lax.dot_general(wt_ref[...], u_ref[...], _NT, precision=_HI, preferred_element_type=jnp.float32)
    aff = jax.nn.sigmoid(logits)
    score = aff + b_ref[...]
    s = [score[e:e + 1] for e in range(N_EXPERTS)]
    a = [aff[e:e + 1] for e in range(N_EXPERTS)]
    gs = []
    for g in range(N_GROUPS):
        m = s[g * EXPERTS_PER_GROUP:(g + 1) * EXPERTS_PER_GROUP]
        best = None
        for i in range(EXPERTS_PER_GROUP):
            for j in range(i + 1, EXPERTS_PER_GROUP):
                best = m[i] + m[j] if best is None else jnp.maximum(best, m[i] + m[j])
        gs.append(best)
    best_val = gs[0]
    best_grp = jnp.zeros_like(gs[0], dtype=jnp.int32)
    for g in range(1, N_GROUPS):
        better = gs[g] > best_val
        best_grp = jnp.where(better, g, best_grp)
        best_val = jnp.where(better, gs[g], best_val)
    sel = []
    for e in range(N_EXPERTS):
        g = e // EXPERTS_PER_GROUP
        rank = jnp.zeros_like(best_grp)
        for j in range(g * EXPERTS_PER_GROUP, (g + 1) * EXPERTS_PER_GROUP):
            if j != e:
                ahead = (s[j] > s[e]) | ((s[j] == s[e]) & (j < e))
                rank = rank + ahead.astype(jnp.int32)
        sel.append((best_grp == g) & (rank < TOP_K))
    wsum = sum(jnp.where(sel[e], a[e], 0.0) for e in range(N_EXPERTS))
    g_ref[...] = jnp.concatenate([jnp.where(sel[e], a[e] / wsum, 0.0) for e in range(N_EXPERTS)], axis=0)


def _router_pallas(u, router_w, router_b, tm=512):
    n_tok, d = u.shape
    assert n_tok % tm == 0
    return pl.pallas_call(
        _router_kernel,
        grid=(n_tok // tm,),
        in_specs=[pl.BlockSpec((tm, d), lambda i: (i, 0)),
                  pl.BlockSpec((N_EXPERTS, d), lambda i: (0, 0)),
                  pl.BlockSpec((N_EXPERTS, 1), lambda i: (0, 0))],
        out_specs=pl.BlockSpec((N_EXPERTS, tm), lambda i: (0, i)),
        out_shape=jax.ShapeDtypeStruct((N_EXPERTS, n_tok), jnp.float32),
        compiler_params=pltpu.CompilerParams(dimension_semantics=("parallel",),
                                             vmem_limit_bytes=VMEM_LIMIT_BYTES),
        name="moe_router",
    )(u, router_w.T, router_b.reshape(N_EXPERTS, 1))


def _s5_mats(lam_re, lam_im, log_dt, b_re, b_im, c_re, c_im):
    L = S5_CHUNK
    lam = lax.complex(lam_re, lam_im)
    ldt = lam * jnp.exp(log_dt)[..., None]
    lam_bar = jnp.exp(ldt)
    b_bar = ((lam_bar - 1.0) / lam)[..., None] * lax.complex(b_re, b_im)
    c_mat = lax.complex(c_re, c_im)
    tau = jnp.arange(L + 1, dtype=jnp.float32)
    pw = jnp.exp(ldt[:, :, None, :] * tau[None, None, :, None])
    kern = jnp.real(jnp.einsum('dgon,dgtn,dgni->dgtoi', c_mat, pw[:, :, :L], b_bar))
    s_idx = jnp.arange(L)[:, None]
    t_idx = jnp.arange(L)[None, :]

    def toeplitz(k, lag, valid):
        m = k[:, jnp.clip(lag, 0, L - 1)] * valid[None, :, :, None, None]
        return jnp.transpose(m, (0, 1, 4, 2, 3)).reshape(-1, L * S5_GROUP, L * S5_GROUP)

    tsum = (toeplitz(kern[0], t_idx - s_idx, (t_idx >= s_idx).astype(jnp.float32))
            + toeplitz(kern[1], s_idx - t_idx, (s_idx >= t_idx).astype(jnp.float32)))
    pin_f = pw[0][:, L - 1 - jnp.arange(L)]
    pin_b = pw[1][:, jnp.arange(L)]
    in_f = jnp.einsum('gsn,gni->gsin', pin_f, b_bar[0]).reshape(-1, L * S5_GROUP, S5_STATE)
    in_b = jnp.einsum('gsn,gni->gsin', pin_b, b_bar[1]).reshape(-1, L * S5_GROUP, S5_STATE)
    icat = jnp.concatenate([jnp.real(in_f), jnp.imag(in_f), jnp.real(in_b), jnp.imag(in_b)], axis=-1)
    pout_f = pw[0][:, 1 + jnp.arange(L)]
    pout_b = pw[1][:, L - jnp.arange(L)]
    out_f = jnp.einsum('gon,gtn->gnto', c_mat[0], pout_f).reshape(-1, S5_STATE, L * S5_GROUP)
    out_b = jnp.einsum('gon,gtn->gnto', c_mat[1], pout_b).reshape(-1, S5_STATE, L * S5_GROUP)
    ocat = jnp.concatenate([jnp.real(out_f), -jnp.imag(out_f), jnp.real(out_b), -jnp.imag(out_b)], axis=1)
    lam_l = pw[:, :, L]
    lam_chunk = jnp.stack([jnp.real(lam_l[0]), jnp.imag(lam_l[0]), jnp.real(lam_l[1]), jnp.imag(lam_l[1])],
                          axis=1)
    return tsum, icat, ocat, lam_chunk


def _s5_kernel(u0_ref, u1_ref, t_ref, i_ref, o_ref, lam_ref, d_ref, y0_ref, y1_ref,
               v_ref, xfr_ref, xfi_ref, xbr_ref, xbi_ref, *, n_chunks, ctx_chunks, bsz):
    n = S5_STATE
    u = jnp.concatenate([u0_ref[0], u1_ref[0]], axis=1)
    ub = u.astype(_BF)
    v_ref[...] = jnp.dot(ub, i_ref[0], preferred_element_type=jnp.float32)
    lam = lam_ref[0]
    lfr = jnp.broadcast_to(lam[0:1], (bsz, n))
    lfi = jnp.broadcast_to(lam[1:2], (bsz, n))
    lbr = jnp.broadcast_to(lam[2:3], (bsz, n))
    lbi = jnp.broadcast_to(lam[3:4], (bsz, n))

    def cmul_add(lr, li, xr, xi, vr, vi):
        return lr * xr - li * xi + vr, lr * xi + li * xr + vi

    def step(j, carry):
        fr, fi, br, bi = carry
        rf = pl.multiple_of(j * (2 * bsz), 2 * bsz)
        pb = jnp.where(j < ctx_chunks // 2, ctx_chunks // 2 - 1 - j, (n_chunks + ctx_chunks) // 2 - 1 - j)
        rb = pl.multiple_of(pb * (2 * bsz), 2 * bsz)
        vf = v_ref[pl.ds(rf, 2 * bsz), :]
        vb = v_ref[pl.ds(rb, 2 * bsz), :]
        fr1, fi1 = cmul_add(lfr, lfi, fr, fi, vf[:bsz, 0:n], vf[:bsz, n:2 * n])
        fr2, fi2 = cmul_add(lfr, lfi, fr1, fi1, vf[bsz:, 0:n], vf[bsz:, n:2 * n])
        br1, bi1 = cmul_add(lbr, lbi, br, bi, vb[bsz:, 2 * n:3 * n], vb[bsz:, 3 * n:4 * n])
        br2, bi2 = cmul_add(lbr, lbi, br1, bi1, vb[:bsz, 2 * n:3 * n], vb[:bsz, 3 * n:4 * n])
        xfr_ref[pl.ds(rf, 2 * bsz), :] = jnp.concatenate([fr, fr1], axis=0)
        xfi_ref[pl.ds(rf, 2 * bsz), :] = jnp.concatenate([fi, fi1], axis=0)
        xbr_ref[pl.ds(rb, 2 * bsz), :] = jnp.concatenate([br1, br], axis=0)
        xbi_ref[pl.ds(rb, 2 * bsz), :] = jnp.concatenate([bi1, bi], axis=0)
        return fr2, fi2, br2, bi2

    z = jnp.zeros((bsz, n), jnp.float32)
    lax.fori_loop(0, n_chunks // 2, step, (z, z, z, z), unroll=S5_UNROLL)
    o = o_ref[0]
    y = jnp.dot(ub, t_ref[0], preferred_element_type=jnp.float32)
    y += jnp.dot(xfr_ref[...].astype(_BF), o[0:n], preferred_element_type=jnp.float32)
    y += jnp.dot(xfi_ref[...].astype(_BF), o[n:2 * n], preferred_element_type=jnp.float32)
    y += jnp.dot(xbr_ref[...].astype(_BF), o[2 * n:3 * n], preferred_element_type=jnp.float32)
    y += jnp.dot(xbi_ref[...].astype(_BF), o[3 * n:4 * n], preferred_element_type=jnp.float32)
    y += d_ref[0] * u
    y = 0.5 * y * (1.0 + jnp.tanh(math.sqrt(2.0 / math.pi) * (y + 0.044715 * (y * y * y))))
    y0_ref[0] = y[:, :LANES]
    y1_ref[0] = y[:, LANES:]


def _to_groups_kernel(*refs, bsz):
    L, C = S5_CHUNK, S5_GROUP
    nch = RL_TBLK // L
    gpt = LANES // C
    lane = lax.broadcasted_iota(jnp.int32, (nch, LANES), 1) // C
    x_refs, o_refs = refs[:S5_W // LANES], refs[S5_W // LANES:]
    for b in range(bsz):
        for m in range(S5_W // LANES):
            src = [x_refs[m][b, pl.ds(j, nch, stride=L), :] for j in range(L)]
            for gl in range(gpt):
                g = m * gpt + gl
                for q in range(L * C // LANES):
                    acc = jnp.zeros((nch, LANES), jnp.float32)
                    for jl in range(gpt):
                        j = q * gpt + jl
                        moved = src[j] if jl == gl else pltpu.roll(src[j], ((jl - gl) * C) % LANES, axis=1)
                        acc = jnp.where(lane == jl, moved, acc)
                    o_refs[q][g, pl.ds(b, nch, stride=bsz), :] = acc


def _from_groups_kernel(y0_ref, y1_ref, o_ref, *, bsz):
    L, C = S5_CHUNK, S5_GROUP
    nch = RL_TBLK // L
    gpt = LANES // C
    lane = lax.broadcasted_iota(jnp.int32, (nch, LANES), 1) // C
    y_refs = (y0_ref, y1_ref)
    for b in range(bsz):
        for m in range(S5_W // LANES):
            for q in range(L * C // LANES):
                src = [y_refs[q][m * gpt + gl, pl.ds(b, nch, stride=bsz), :] for gl in range(gpt)]
                for jl in range(gpt):
                    acc = jnp.zeros((nch, LANES), jnp.float32)
                    for gl in range(gpt):
                        moved = src[gl] if jl == gl else pltpu.roll(src[gl], ((gl - jl) * C) % LANES, axis=1)
                        acc = jnp.where(lane == gl, moved, acc)
                    o_ref[m, b, pl.ds(q * gpt + jl, nch, stride=L), :] = acc


def _s5_pallas(p3, n_ctx, lam_re, lam_im, log_dt, b_re, b_im, c_re, c_im, d_skip):
    bsz, t_len, _ = p3.shape
    L, G, C = S5_CHUNK, S5_GROUPS, S5_GROUP
    nc = t_len // L
    assert (2 * bsz) % 8 == 0 and nc % 2 == 0 and (n_ctx // L) % 2 == 0 and t_len % RL_TBLK == 0
    assert L * C == 2 * LANES and P_S5 % LANES == 0
    tsum, icat, ocat, lam_chunk = _s5_mats(lam_re, lam_im, log_dt, b_re, b_im, c_re, c_im)
    dvec = jnp.tile(d_skip.reshape(G, 1, C), (1, L, 1)).reshape(G, 1, L * C)
    rows = nc * bsz
    n_lt = S5_W // LANES
    nch = RL_TBLK // L
    relayout_params = pltpu.CompilerParams(dimension_semantics=("parallel",), vmem_limit_bytes=VMEM_LIMIT_BYTES)
    half_shape = jax.ShapeDtypeStruct((G, rows, LANES), jnp.float32)
    half_blk = pl.BlockSpec((G, nch * bsz, LANES), lambda i: (0, i, 0))
    u0, u1 = pl.pallas_call(
        functools.partial(_to_groups_kernel, bsz=bsz),
        grid=(t_len // RL_TBLK,),
        in_specs=[pl.BlockSpec((bsz, RL_TBLK, LANES), functools.partial(lambda m, i: (0, i, P_S5 // LANES + m), m))
                  for m in range(n_lt)],
        out_specs=[half_blk, half_blk],
        out_shape=[half_shape, half_shape],
        compiler_params=relayout_params, name="s5_to_groups",
    )(*([p3] * n_lt))
    wspec = pl.BlockSpec((1, L * C, L * C), lambda g: (g, 0, 0))
    half = pl.BlockSpec((1, rows, LANES), lambda g: (g, 0, 0))
    y0, y1 = pl.pallas_call(
        functools.partial(_s5_kernel, n_chunks=nc, ctx_chunks=n_ctx // L, bsz=bsz),
        grid=(G,),
        in_specs=[half, half, wspec, wspec, wspec,
                  pl.BlockSpec((1, 4, S5_STATE), lambda g: (g, 0, 0)),
                  pl.BlockSpec((1, 1, L * C), lambda g: (g, 0, 0))],
        out_specs=[half, half],
        out_shape=[half_shape, half_shape],
        scratch_shapes=[pltpu.VMEM((rows, 4 * S5_STATE), jnp.float32)]
                       + [pltpu.VMEM((rows, S5_STATE), jnp.float32)] * 4,
        compiler_params=pltpu.CompilerParams(dimension_semantics=("parallel",),
                                             vmem_limit_bytes=VMEM_LIMIT_BYTES),
        name="s5_scan",
    )(u0, u1, tsum.astype(_BF), icat.astype(_BF), ocat.astype(_BF), lam_chunk, dvec)
    return pl.pallas_call(
        functools.partial(_from_groups_kernel, bsz=bsz),
        grid=(t_len // RL_TBLK,),
        in_specs=[half_blk, half_blk],
        out_specs=pl.BlockSpec((n_lt, bsz, RL_TBLK, LANES), lambda i: (0, 0, i, 0)),
        out_shape=jax.ShapeDtypeStruct((n_lt, bsz, t_len, LANES), jnp.float32),
        compiler_params=relayout_params, name="s5_from_groups",
    )(y0, y1)


def _log_sigmoid(x):
    return jnp.minimum(x, 0.0) - jnp.log(1.0 + jnp.exp(-jnp.abs(x)))


def _mlstm_kernel(igb_ref, fgb_ref, *refs):
    ins, (hf_ref, hb_ref, cmat_ref, nvec_ref, m_ref) = refs[:5 * N_DIR], refs[5 * N_DIR:]
    h_refs = (hf_ref, hb_ref)
    L = ML_CHUNK
    n_sub = ML_TBLK // L
    H = range(N_DIR * ML_HEADS)
    dof = lambda c: c // ML_HEADS
    hof = lambda c: c % ML_HEADS

    @pl.when(pl.program_id(1) == 0)
    def _():
        cmat_ref[...] = jnp.zeros_like(cmat_ref)
        nvec_ref[...] = jnp.zeros_like(nvec_ref)
        m_ref[...] = jnp.zeros_like(m_ref)

    igb = [igb_ref[dof(h), hof(h)] for h in H]
    fgb = [fgb_ref[dof(h), hof(h)] for h in H]
    row = lax.broadcasted_iota(jnp.int32, (L, L), 0)
    col = lax.broadcasted_iota(jnp.int32, (L, L), 1)
    seen_d = [row >= col, row <= col]
    seen_fd = [s.astype(jnp.float32) for s in seen_d]
    seen_td = [seen_fd[1], seen_fd[0]]
    scale = ML_DH ** -0.5

    def chunk(jj, carry):
        cjs = [jj, n_sub - 1 - jj]
        r0s = [pl.multiple_of(cj * L, L) for cj in cjs]
        hs = lambda h: slice(hof(h) * ML_DH, (hof(h) + 1) * ML_DH)
        q_ref, k_ref, v_ref, gc_ref, gr_ref = [[ins[5 * dof(h) + a] for h in H] for a in range(5)]
        q = [q_ref[h][0, pl.ds(r0s[dof(h)], L), hs(h)] for h in H]
        k = [k_ref[h][0, pl.ds(r0s[dof(h)], L), hs(h)] * scale for h in H]
        vb = [v_ref[h][0, pl.ds(r0s[dof(h)], L), hs(h)].astype(_BF) for h in H]
        gc = [gc_ref[h][0, 0, hof(h), pl.ds(r0s[dof(h)], L), :] for h in H]
        gr = [gr_ref[h][0, 0, hof(h), cjs[dof(h)]] for h in H]
        seen, seen_f, seen_t = seen_d, seen_fd, seen_td
        li_col = [gc[h][:, 0:1] + igb[h] for h in H]
        lf_col = [_log_sigmoid(gc[h][:, 1:2] + fgb[h]) for h in H]
        li_row = [gr[h][0:1, :] + igb[h] for h in H]
        lf_row = [_log_sigmoid(gr[h][1:2, :] + fgb[h]) for h in H]
        m_prev = [m_ref[h] for h in H]
        bcum_col = [jnp.dot(seen_f[dof(h)], jnp.broadcast_to(lf_col[h], (L, L)), precision=_HI,
                            preferred_element_type=jnp.float32) for h in H]
        bcum_row = [jnp.dot(jnp.broadcast_to(lf_row[h], (8, L)), seen_t[dof(h)], precision=_HI,
                            preferred_element_type=jnp.float32)[0:1] for h in H]
        qb = [q[h].astype(_BF) for h in H]
        qk = [lax.dot_general(qb[h], k[h].astype(_BF), _NT, preferred_element_type=jnp.float32) for h in H]
        qc = [jnp.dot(qb[h], cmat_ref[h].astype(_BF), preferred_element_type=jnp.float32) for h in H]
        log_d = [jnp.where(seen[dof(h)], bcum_col[h] - bcum_row[h] + li_row[h], -jnp.inf) for h in H]
        inter = [bcum_col[h][:, 0:1] + m_prev[h] for h in H]
        m_j = [jnp.maximum(jnp.max(log_d[h], axis=1, keepdims=True), inter[h]) for h in H]
        scores = [qk[h] * jnp.exp(log_d[h] - m_j[h]) for h in H]
        s_inter = [jnp.exp(inter[h] - m_j[h]) for h in H]
        sv = [jnp.dot(scores[h].astype(_BF), vb[h], preferred_element_type=jnp.float32) for h in H]
        b_last = [jnp.sum(lf_col[h], axis=0, keepdims=True) for h in H]
        log_w = [b_last[h] - bcum_col[h][:, 0:1] + li_col[h] for h in H]
        m_new = [jnp.maximum(b_last[h] + m_prev[h], jnp.max(log_w[h], axis=0, keepdims=True)) for h in H]
        kw = [k[h] * jnp.exp(log_w[h] - m_new[h]) for h in H]
        decay = [jnp.exp(b_last[h] + m_prev[h] - m_new[h]) for h in H]
        kv = [lax.dot_general(kw[h].astype(_BF), vb[h], _TN, preferred_element_type=jnp.float32) for h in H]
        for h in H:
            num = sv[h] + s_inter[h] * qc[h]
            den = (jnp.sum(scores[h], axis=1, keepdims=True)
                   + s_inter[h] * jnp.sum(q[h] * nvec_ref[h], axis=1, keepdims=True))
            h_refs[dof(h)][0, pl.ds(r0s[dof(h)], L), hs(h)] = num / jnp.maximum(jnp.abs(den), jnp.exp(-m_j[h]))
        for h in H:
            cmat_ref[h] = decay[h] * cmat_ref[h] + kv[h]
            nvec_ref[h] = decay[h] * nvec_ref[h] + jnp.sum(kw[h], axis=0, keepdims=True)
            m_ref[h] = m_new[h]
        return carry

    lax.fori_loop(0, n_sub, chunk, 0)


def _mlstm_pallas(qk, p3, gates, ig_b, fg_b, n_ctx):
    bsz, t_len, _ = qk.shape
    nb = t_len // ML_TBLK
    cb = n_ctx // ML_TBLK
    assert t_len % ML_TBLK == 0 and n_ctx % ML_TBLK == 0
    g = gates.reshape(bsz, t_len, 2, N_DIR, ML_HEADS)
    gcol = jnp.transpose(g, (3, 0, 4, 1, 2))
    grow = jnp.transpose(g.reshape(bsz, t_len // ML_CHUNK, ML_CHUNK, 2, N_DIR, ML_HEADS),
                         (4, 0, 5, 1, 3, 2))
    in_specs, out_specs = [], []
    for d in range(N_DIR):
        blk = functools.partial(lambda d, i: _scan_block(d, i, nb, cb), d)
        tok = lambda c, blk=blk: pl.BlockSpec((1, ML_TBLK, ML_W), lambda b, i, *_: (b, blk(i), c))
        in_specs += [tok(0), tok(1), tok(P_V // ML_W),
                     pl.BlockSpec((1, 1, ML_HEADS, ML_TBLK, 2),
                                  functools.partial(lambda d, blk, b, i, *_: (d, b, 0, blk(i), 0), d, blk)),
                     pl.BlockSpec((1, 1, ML_HEADS, ML_TBLK // ML_CHUNK, 2, ML_CHUNK),
                                  functools.partial(lambda d, blk, b, i, *_: (d, b, 0, blk(i), 0, 0), d, blk))]
        out_specs.append(tok(0))
    n_chain = N_DIR * ML_HEADS
    grid_spec = pltpu.PrefetchScalarGridSpec(
        num_scalar_prefetch=2,
        grid=(bsz, nb),
        in_specs=in_specs,
        out_specs=out_specs,
        scratch_shapes=[pltpu.VMEM((n_chain, ML_DH, ML_DH), jnp.float32),
                        pltpu.VMEM((n_chain, 1, ML_DH), jnp.float32),
                        pltpu.VMEM((n_chain, 1, 1), jnp.float32)])
    return pl.pallas_call(
        _mlstm_kernel, grid_spec=grid_spec,
        out_shape=[jax.ShapeDtypeStruct((bsz, t_len, ML_W), jnp.float32)] * N_DIR,
        compiler_params=pltpu.CompilerParams(
            dimension_semantics=("parallel", "arbitrary"),
            vmem_limit_bytes=VMEM_LIMIT_BYTES),
        name="mlstm_scan",
    )(ig_b, fg_b, *([qk, qk, p3, gcol, grow] * N_DIR))


def _to_pairs(x):
    return jnp.concatenate([x[:, p * LANES:(p + 1) * LANES] for p in range(RW_PAIRS)], axis=0)


def _dotf(a, b, dims=None):
    a = a.astype(_BF)
    b = b.astype(_BF)
    if dims is None:
        return jnp.dot(a, b, preferred_element_type=jnp.float32)
    return lax.dot_general(a, b, dims, preferred_element_type=jnp.float32)


def _rwkv_a_kernel(r_ref, k_ref, v_ref, kk_ref, a_ref, lw_ref,
                   att_ref, rt_ref, bw_ref, kw_ref, vt_ref, u0t_ref, y0_ref, wc_ref, *, n_sub):
    d = pl.program_id(0)
    C, R = RW_CHUNK, RW_ROWS
    row = lax.broadcasted_iota(jnp.int32, (R, R), 0)
    col = lax.broadcasted_iota(jnp.int32, (R, R), 1)
    same = (row // C) == (col // C)
    sign = 1 - 2 * d
    before = same & ((row - col) * sign > 0)
    upto = same & ((row - col) * sign >= 0)
    upto2 = jnp.concatenate([upto, upto], axis=1)
    eye = (row == col).astype(jnp.float32)
    first = col < RW_DH
    tpos = row % C

    def chunk_group(jg, carry):
        js = [jg * RW_GROUP + i for i in range(RW_GROUP)]
        G2 = [(i, h2) for i in range(RW_GROUP) for h2 in range(2)]
        pick = lambda lst, off: [jnp.where(first, lst[2 * i][:, off:off + R], lst[2 * i + 1][:, off:off + R])
                                 for i in range(RW_GROUP)]
        r0s = [pl.multiple_of(j * C, C) for j in js]
        ld = lambda ref: [_to_pairs(ref[0, pl.ds(r0, C), :]) for r0 in r0s]
        ldd = lambda ref: [_to_pairs(ref[0, 0, pl.ds(r0, C), :]) for r0 in r0s]
        r, v, kk = ld(r_ref), ld(v_ref), ld(kk_ref)
        k, a, lw = ldd(k_ref), ldd(a_ref), ldd(lw_ref)
        fwd = d == 0
        cum, aft = [], []
        for i in range(RW_GROUP):
            pre = lw[i]
            suf = lw[i]
            for s in (1, 2, 4, 8):
                pre = pre + jnp.where(tpos >= s, pltpu.roll(pre, s, axis=0), 0.0)
                suf = suf + jnp.where(tpos < C - s, pltpu.roll(suf, R - s, axis=0), 0.0)
            cum.append(jnp.where(fwd, pre, suf))
            aft.append(jnp.where(fwd, suf, pre) - lw[i])
        a_hat = [-kk[i] * jnp.exp(cum[i] - lw[i]) for i in range(RW_GROUP)]
        r_hat = [r[i] * jnp.exp(cum[i]) for i in range(RW_GROUP)]
        vb = [v[i].astype(_BF) for i in range(RW_GROUP)]
        m = []
        for i in range(RW_GROUP):
            inv_w = jnp.exp(-cum[i])
            lhs = jnp.concatenate([jnp.where(first, a_hat[i], 0.0), jnp.where(first, 0.0, a_hat[i]),
                                   jnp.where(first, r_hat[i], 0.0), jnp.where(first, 0.0, r_hat[i])], axis=0)
            rhs = jnp.concatenate([kk[i] * a[i] * inv_w, k[i] * inv_w], axis=0)
            m.append(_dotf(lhs, rhs, _NT))
        x = [jnp.where(before, m[i][h2 * R:(h2 + 1) * R, 0:R], 0.0).astype(_BF) for i, h2 in G2]
        ak = [jnp.where(before, m[i][h2 * R:(h2 + 1) * R, R:2 * R], 0.0) for i, h2 in G2]
        rbk = [jnp.where(upto2, m[i][(2 + h2) * R:(3 + h2) * R, :], 0.0).astype(_BF) for i, h2 in G2]
        akv = [_dotf(ak[g], vb[g // 2]) for g in range(len(G2))]
        x2 = [_dotf(xx, xx).astype(_BF) for xx in x]
        x4 = [_dotf(xx, xx).astype(_BF) for xx in x2]
        x8 = [_dotf(xx, xx).astype(_BF) for xx in x4]
        t = [eye + xx.astype(jnp.float32) for xx in x]
        t = [t[g] + _dotf(t[g], x2[g]) for g in range(len(G2))]
        t = [t[g] + _dotf(t[g], x4[g]) for g in range(len(G2))]
        t = [t[g] + _dotf(t[g], x8[g]) for g in range(len(G2))]
        akv = pick(akv, 0)
        rhs2 = [jnp.concatenate([a_hat[i], akv[i]], axis=1).astype(_BF) for i in range(RW_GROUP)]
        ta = [_dotf(t[g], rhs2[g // 2]) for g in range(len(G2))]
        at = pick(ta, 0)
        u0 = pick(ta, R)
        rhs3 = [jnp.concatenate([jnp.concatenate([at[i], u0[i]], axis=1).astype(_BF),
                                 jnp.concatenate([jnp.zeros_like(vb[i]), vb[i]], axis=1)], axis=0)
                for i in range(RW_GROUP)]
        ry = [_dotf(rbk[g], rhs3[g // 2]) for g in range(len(G2))]
        rt = pick(ry, 0)
        y0 = pick(ry, R)
        for i, j in enumerate(js):
            w_aft = jnp.exp(aft[i])
            att_ref[0, 0, j] = at[i].T.astype(_BF)
            u0t_ref[0, 0, j] = u0[i].T
            vt_ref[0, 0, j] = v[i].T.astype(_BF)
            rt_ref[0, 0, j] = (r_hat[i] + rt[i]).astype(_BF)
            bw_ref[0, 0, j] = (kk[i] * a[i] * w_aft).astype(_BF)
            kw_ref[0, 0, j] = (k[i] * w_aft).astype(_BF)
            y0_ref[0, 0, j] = y0[i]
            tot = cum[i] + aft[i]
            wc_ref[0, 0, j] = jnp.exp(jnp.concatenate([tot[p * C:p * C + 1] for p in range(RW_PAIRS)], axis=0))
        return carry

    lax.fori_loop(0, n_sub // RW_GROUP, chunk_group, 0)


def _rwkv_b_kernel(*refs, n_sub):
    ins, (yf_ref, yb_ref, s_ref) = refs[:16], refs[16:]
    C = RW_CHUNK

    @pl.when(pl.program_id(1) == 0)
    def _():
        s_ref[...] = jnp.zeros_like(s_ref)

    row = lax.broadcasted_iota(jnp.int32, (LANES, LANES), 0)
    col = lax.broadcasted_iota(jnp.int32, (LANES, LANES), 1)
    diag = (row // RW_DH) == (col // RW_DH)
    pair_of_row = lax.broadcasted_iota(jnp.int32, (2 * RW_ROWS, LANES), 0) % RW_ROWS // C
    DP = [(d, p) for d in range(N_DIR) for p in range(RW_PAIRS)]

    def chunk(jj, carry):
        cjs = [jj, n_sub - 1 - jj]
        att, rt, bw, kw, vt, u0t, y0, wc = [[ins[8 * d + a][0, 0, cjs[d]] for d in range(N_DIR)] for a in range(8)]
        bk = [jnp.concatenate([bw[d], kw[d]], axis=0) for d in range(N_DIR)]
        sps = [s_ref[d, p] for d, p in DP]
        spb = [sp.astype(_BF) for sp in sps]
        uts = [jnp.dot(spb[i], att[d], preferred_element_type=jnp.float32) + u0t[d] for i, (d, p) in enumerate(DP)]
        for i, (d, p) in enumerate(DP):
            rows = slice(p * C, (p + 1) * C)
            y = lax.dot_general(rt[d][rows], spb[i], _NT, preferred_element_type=jnp.float32) + y0[d][rows]
            y_ref = yf_ref if d == 0 else yb_ref
            y_ref[0, pl.ds(pl.multiple_of(cjs[d] * C, C), C), p * LANES:(p + 1) * LANES] = y
        for i, (d, p) in enumerate(DP):
            lhs = jnp.concatenate([uts[i].astype(_BF), vt[d]], axis=1)
            rhs = jnp.where(pair_of_row == p, bk[d], jnp.zeros_like(bk[d]))
            upd = jnp.dot(lhs, rhs, preferred_element_type=jnp.float32)
            s_ref[d, p] = jnp.where(diag, wc[d][p:p + 1, :] * sps[i] + upd, 0.0)
        return carry

    lax.fori_loop(0, n_sub, chunk, 0)


def _rwkv_pallas(r, v, kk, k_dir, a_dir, lw_dir, n_ctx):
    bsz, t_len, _ = r.shape
    nb, cb = t_len // RW_TBLK, n_ctx // RW_TBLK
    assert t_len % RW_TBLK == 0 and n_ctx % RW_TBLK == 0
    n_sub = RW_TBLK // RW_CHUNK
    nc = t_len // RW_CHUNK
    sh_spec = pl.BlockSpec((1, RW_TBLK, RW_W), lambda d, b, i: (b, _scan_block(d, i, nb, cb), 0))
    dr_spec = pl.BlockSpec((1, 1, RW_TBLK, RW_W), lambda d, b, i: (d, b, _scan_block(d, i, nb, cb), 0))
    ch_spec = pl.BlockSpec((1, 1, n_sub, RW_ROWS, LANES), lambda d, b, i: (d, b, _scan_block(d, i, nb, cb), 0, 0))
    wc_spec = pl.BlockSpec((1, 1, n_sub, RW_PAIRS, LANES), lambda d, b, i: (d, b, _scan_block(d, i, nb, cb), 0, 0))
    ch_shape = lambda dt: jax.ShapeDtypeStruct((N_DIR, bsz, nc, RW_ROWS, LANES), dt)
    params = pltpu.CompilerParams(dimension_semantics=("parallel", "parallel", "arbitrary"),
                                  vmem_limit_bytes=VMEM_LIMIT_BYTES)
    chunk_local = pl.pallas_call(
        functools.partial(_rwkv_a_kernel, n_sub=n_sub),
        grid=(N_DIR, bsz, nb),
        in_specs=[sh_spec, dr_spec, sh_spec, sh_spec, dr_spec, dr_spec],
        out_specs=[ch_spec] * 7 + [wc_spec],
        out_shape=[ch_shape(_BF)] * 5 + [ch_shape(jnp.float32)] * 2
                  + [jax.ShapeDtypeStruct((N_DIR, bsz, nc, RW_PAIRS, LANES), jnp.float32)],
        compiler_params=params, name="rwkv_chunk_local",
    )(r, k_dir, v, kk, a_dir, lw_dir)
    dir_specs = []
    for d in range(N_DIR):
        blk = functools.partial(lambda d, b, i: (d, b, _scan_block(d, i, nb, cb), 0, 0), d)
        dir_specs += [pl.BlockSpec((1, 1, n_sub, RW_ROWS, LANES), blk)] * 7
        dir_specs += [pl.BlockSpec((1, 1, n_sub, RW_PAIRS, LANES), blk)]
    y_specs = [pl.BlockSpec((1, RW_TBLK, RW_W),
                            functools.partial(lambda d, b, i: (b, _scan_block(d, i, nb, cb), 0), d))
               for d in range(N_DIR)]
    return pl.pallas_call(
        functools.partial(_rwkv_b_kernel, n_sub=n_sub),
        grid=(bsz, nb),
        in_specs=dir_specs,
        out_specs=y_specs,
        out_shape=[jax.ShapeDtypeStruct((bsz, t_len, RW_W), jnp.float32)] * N_DIR,
        scratch_shapes=[pltpu.VMEM((N_DIR, RW_PAIRS, LANES, LANES), jnp.float32)],
        compiler_params=pltpu.CompilerParams(dimension_semantics=("parallel", "arbitrary"),
                                             vmem_limit_bytes=VMEM_LIMIT_BYTES),
        name="rwkv_state_scan",
    )(*chunk_local, *chunk_local)


def _softplus(x):
    return jnp.maximum(x, 0.0) + jnp.log(1.0 + jnp.exp(-jnp.abs(x)))


def _rwkv_prep_kernel(*refs, blocks_per_seq, ctx_blocks):
    (zr, zk, zv, zw, za, zg, pr, pk, pv, pw, pa, pg, nr, nk, nv, nw, na, ng,
     mu_ref, w0_ref, a0_ref, wup_ref, aup_ref, gup_ref, kkw_ref, ka_ref, rk_ref, ones_ref,
     r_out, v_out, kk_out, kdir_out, a_out, lw_out, bonus_out, g_out) = refs
    blk = pl.program_id(0) % blocks_per_seq
    has_prev = (blk != 0) & (blk != ctx_blocks)
    has_next = (blk != ctx_blocks - 1) & (blk != blocks_per_seq - 1)
    tm = zr.shape[0]

    def lerp(z_ref, p_ref, n_ref, mu):
        z = z_ref[...]
        rows = lax.broadcasted_iota(jnp.int32, z.shape, 0)
        prev_row = jnp.where(has_prev, p_ref[RW_HALO - 1:RW_HALO, :], 0.0)
        next_row = jnp.where(has_next, n_ref[0:1, :], 0.0)
        before = jnp.where(rows == 0, prev_row, pltpu.roll(z, 1, axis=0))
        after = jnp.where(rows == tm - 1, next_row, pltpu.roll(z, tm - 1, axis=0))
        return z + mu * (0.5 * (before + after) - z)

    mu = mu_ref[...]
    r = lerp(zr, pr, nr, mu[:, 0:RW_W])
    k = lerp(zk, pk, nk, mu[:, RW_W:2 * RW_W])
    v = lerp(zv, pv, nv, mu[:, 2 * RW_W:3 * RW_W])
    wd = lerp(zw, pw, nw, mu[:, 3 * RW_W:3 * RW_W + LANES])
    ad = lerp(za, pa, na, mu[:, 3 * RW_W + LANES:3 * RW_W + 2 * LANES])
    gd = lerp(zg, pg, ng, mu[:, 3 * RW_W + 2 * LANES:3 * RW_W + 3 * LANES])
    dot = lambda a, w_ref: jnp.dot(a.astype(_BF), w_ref[...], preferred_element_type=jnp.float32)
    w_pre = dot(jnp.tanh(wd), wup_ref)
    a_pre = dot(ad, aup_ref)
    g_out[...] = dot(jax.nn.sigmoid(gd), gup_ref)

    def head_sum(x):
        hi = x.astype(_BF)
        lo = (x - hi.astype(jnp.float32)).astype(_BF)
        return (jnp.dot(hi, ones_ref[...], preferred_element_type=jnp.float32)
                + jnp.dot(lo, ones_ref[...], preferred_element_type=jnp.float32))

    kk = k * kkw_ref[...]
    kk_out[...] = kk / jnp.maximum(jnp.sqrt(head_sum(kk * kk)), 1e-12)
    r_out[...] = r
    v_out[...] = v
    rk = r * rk_ref[...]
    bonus = jnp.zeros_like(r)
    for d in range(N_DIR):
        cols = slice(d * RW_W, (d + 1) * RW_W)
        lw_out[d] = -jnp.exp(-_softplus(-(w0_ref[d:d + 1, :] + w_pre[:, cols])) - RW_DECAY_OFFSET)
        a = jax.nn.sigmoid(a0_ref[d:d + 1, :] + a_pre[:, cols])
        a_out[d] = a
        k_dir = k * (1.0 + (a - 1.0) * ka_ref[...])
        kdir_out[d] = k_dir
        bonus = bonus + rk * k_dir
    bonus_out[...] = head_sum(bonus) * v


def _rwkv_prep(p2, blocks_per_seq, ctx_blocks, mu, w0, w_up, a0, a_up, g_up, k_k, k_a, r_k):
    n = p2.shape[0]
    tm = ROW_TM
    hb = tm // RW_HALO
    n_hb = n // RW_HALO
    cw, cl = P_RW // RW_W, P_RWLR // LANES
    assert P_RW % RW_W == 0 and P_RWLR % LANES == 0 and N_DIR * RW_DECAY_RANK == LANES and RW_G_RANK == LANES

    def specs(rows, row_index):
        wide = [pl.BlockSpec((rows, RW_W), functools.partial(lambda j, i: (row_index(i), cw + j), j))
                for j in range(3)]
        return wide + [pl.BlockSpec((rows, LANES), functools.partial(lambda j, i: (row_index(i), cl + j), j))
                       for j in range(3)]

    main = specs(tm, lambda i: i)
    prev = specs(RW_HALO, lambda i: jnp.maximum(i * hb - 1, 0))
    nxt = specs(RW_HALO, lambda i: jnp.minimum((i + 1) * hb, n_hb - 1))
    zero = jnp.zeros((RW_DECAY_RANK, RW_W), jnp.float32)
    both_dirs = lambda up: jnp.concatenate([jnp.concatenate([up[0], zero], axis=1),
                                            jnp.concatenate([zero, up[1]], axis=1)], axis=0).astype(_BF)
    head = jnp.arange(RW_W) // RW_DH
    ones_bd = (head[:, None] == head[None, :]).astype(_BF)
    vec = lambda w: pl.BlockSpec((1, w), lambda i: (0, 0))
    dvec = pl.BlockSpec((N_DIR, RW_W), lambda i: (0, 0))
    consts = [vec(mu.shape[0]), dvec, dvec, _const_spec((LANES, N_DIR * RW_W)), _const_spec((LANES, N_DIR * RW_W)),
              _const_spec((RW_G_RANK, RW_W)), vec(RW_W), vec(RW_W), vec(RW_W), _const_spec((RW_W, RW_W))]
    row = pl.BlockSpec((tm, RW_W), lambda i: (i, 0))
    drow = pl.BlockSpec((N_DIR, tm, RW_W), lambda i: (0, i, 0))
    sh = jax.ShapeDtypeStruct((n, RW_W), jnp.float32)
    dsh = jax.ShapeDtypeStruct((N_DIR, n, RW_W), jnp.float32)
    return pl.pallas_call(
        functools.partial(_rwkv_prep_kernel, blocks_per_seq=blocks_per_seq, ctx_blocks=ctx_blocks),
        grid=(n // tm,),
        in_specs=main + prev + nxt + consts,
        out_specs=[row, row, row, drow, drow, drow, row, row],
        out_shape=[sh, sh, sh, dsh, dsh, dsh, sh, sh],
        compiler_params=pltpu.CompilerParams(dimension_semantics=("parallel",), vmem_limit_bytes=VMEM_LIMIT_BYTES),
        name="rwkv_prep",
    )(*([p2] * 18), mu.reshape(1, -1), w0, a0, both_dirs(w_up), both_dirs(a_up), g_up.astype(_BF),
      k_k.reshape(1, -1), k_a.reshape(1, -1), r_k.reshape(1, -1), ones_bd)


def _layer_norm(x, eps=LN_EPS):
    mu = jnp.mean(x, axis=-1, keepdims=True)
    var = jnp.mean(jnp.square(x - mu), axis=-1, keepdims=True)
    return (x - mu) * lax.rsqrt(var + eps)


def _modulate(x, shift, scale):
    return _layer_norm(x) * (1.0 + scale) + shift


def _conv_silu_kernel(x_ref, prev_ref, next_ref, w_ref, b_ref, o_ref, *, blocks_per_seq, ctx_blocks):
    blk = pl.program_id(0) % blocks_per_seq
    is_lat = blk >= ctx_blocks
    has_prev = is_lat & (blk != ctx_blocks)
    has_next = is_lat & (blk != blocks_per_seq - 1)
    tm, width = x_ref.shape
    ext = jnp.concatenate([jnp.where(has_prev, prev_ref[...], 0.0), x_ref[...],
                           jnp.where(has_next, next_ref[...], 0.0)], axis=0)
    n_ext = ext.shape[0]
    t = lax.broadcasted_iota(jnp.int32, (tm, 1), 0)
    col = jnp.where(is_lat, t % GRID_W, t)
    last_col = jnp.where(is_lat, GRID_W - 1, tm - 1)
    w = w_ref[...]
    acc = jnp.zeros((tm, width), jnp.float32) + b_ref[...]
    for dc in (-1, 0, 1):
        shifted = ext if dc == 0 else pltpu.roll(ext, (-dc) % n_ext, axis=0)
        col_ok = (col != 0) if dc == -1 else ((col != last_col) if dc == 1 else None)
        for dr in (-1, 0, 1):
            src = shifted[GRID_W + dr * GRID_W:GRID_W + dr * GRID_W + tm]
            ok = col_ok if dr == 0 else (is_lat if col_ok is None else (col_ok & is_lat))
            if ok is not None:
                src = jnp.where(ok, src, 0.0)
            tap = (dr + 1) * 3 + (dc + 1)
            acc = acc + src * w[tap:tap + 1, :]
    o_ref[...] = acc * jax.nn.sigmoid(acc)


def _conv_silu(p2, blocks_per_seq, ctx_blocks, conv_w, conv_b):
    n = p2.shape[0]
    tm = ROW_TM
    width = 2 * ML_W
    assert ctx_blocks == 1 and tm % GRID_W == 0 and P_QK % width == 0
    hb = tm // GRID_W
    n_hb = n // GRID_W
    cb = P_QK // width
    return pl.pallas_call(
        functools.partial(_conv_silu_kernel, blocks_per_seq=blocks_per_seq, ctx_blocks=ctx_blocks),
        grid=(n // tm,),
        in_specs=[pl.BlockSpec((tm, width), lambda i: (i, cb)),
                  pl.BlockSpec((GRID_W, width), lambda i: (jnp.maximum(i * hb - 1, 0), cb)),
                  pl.BlockSpec((GRID_W, width), lambda i: (jnp.minimum((i + 1) * hb, n_hb - 1), cb)),
                  pl.BlockSpec((9, width), lambda i: (0, 0)), pl.BlockSpec((1, width), lambda i: (0, 0))],
        out_specs=pl.BlockSpec((tm, width), lambda i: (i, 0)),
        out_shape=jax.ShapeDtypeStruct((n, width), jnp.float32),
        compiler_params=pltpu.CompilerParams(dimension_semantics=("parallel",), vmem_limit_bytes=VMEM_LIMIT_BYTES),
        name="mlstm_conv_silu",
    )(p2, p2, p2, conv_w.reshape(9, width), conv_b.reshape(1, width))


def _rwkv7_branch(p2, bsz, n_ctx, mu, w0, w_up, a0, a_up, g_up, k_k, k_a, r_k):
    t_len = p2.shape[0] // bsz
    r, v, kk, k_dir, a, log_decay, bonus, g = _rwkv_prep(p2, t_len // ROW_TM, n_ctx // ROW_TM, mu, w0, w_up, a0,
                                                         a_up, g_up, k_k, k_a, r_k)
    seq = lambda z: z.reshape(z.shape[:-2] + (bsz, t_len, RW_W))
    y_f, y_b = _rwkv_pallas(seq(r), seq(v), seq(kk), seq(k_dir), seq(a), seq(log_decay), n_ctx)
    return y_f.reshape(-1, RW_W), y_b.reshape(-1, RW_W), bonus, g


def _token_mixer(u, n_ctx, w_in,
                 ml_conv_w, ml_conv_b, ml_ig_b, ml_fg_b, ml_norm_g, ml_norm_b, ml_proj,
                 rw_mu, rw_w0, rw_w_up, rw_a0, rw_a_up, rw_g_up, rw_k_k, rw_k_a, rw_r_k,
                 rw_norm_g, rw_norm_b, rw_proj,
                 s5_lam_re, s5_lam_im, s5_log_dt, s5_b_re, s5_b_im, s5_c_re, s5_c_im, s5_d,
                 s5_w_val, s5_w_gate):
    bsz, t_len, _ = u.shape
    n = bsz * t_len
    p = _mm_any(u, _permute_w_in(w_in), tm=PROJ_TM, keep_col_pad=True)

    def col(start, width):
        return p[..., start:start + width]

    n_gate = N_DIR * ML_HEADS
    p2 = p.reshape(n, -1)
    qk = _conv_silu(p2, t_len // ROW_TM, n_ctx // ROW_TM, ml_conv_w, ml_conv_b)
    h_f, h_b = _mlstm_pallas(qk.reshape(bsz, t_len, 2 * ML_W), p, col(P_MLG, 2 * n_gate), ml_ig_b, ml_fg_b, n_ctx)
    y_f, y_b, bonus, g = _rwkv7_branch(p2, bsz, n_ctx, rw_mu, rw_w0, rw_w_up, rw_a0, rw_a_up, rw_g_up,
                                       rw_k_k, rw_k_a, rw_r_k)
    s5 = _s5_pallas(p, n_ctx, s5_lam_re, s5_lam_im, s5_log_dt, s5_b_re, s5_b_im, s5_c_re, s5_c_im, s5_d)
    ml, rw = _post_scan(p2, h_f.reshape(n, ML_W), h_b.reshape(n, ML_W), y_f, y_b, bonus, g,
                        ml_norm_g, ml_norm_b, rw_norm_g, rw_norm_b)
    return _merge_pallas(p2, ml, rw, s5.reshape(S5_W // LANES, n, LANES),
                         ml_proj.astype(_BF), rw_proj.astype(_BF), s5_w_val.astype(_BF), s5_w_gate.astype(_BF))


def _permute_w_in(w_in):
    offs = np.cumsum((0,) + IN_WIDTHS)
    seg = lambda a, b: w_in[:, offs[a]:offs[b]]
    return jnp.concatenate([seg(0, 4), seg(6, 9), seg(12, 13), seg(13, 14), seg(9, 12), seg(4, 6)], axis=1)


def _moe_ffn(u, router_w, router_b, layer, w_gate, w_up, w_down):
    gates_t = _router_pallas(u, router_w, router_b)
    return _moe_routed(u, gates_t, layer, w_gate, w_up, w_down)


def kernel(x, c, ctx, c_ctx, ada_w, ada_b, w_in, ml_conv_w, ml_conv_b, ml_ig_b, ml_fg_b, ml_norm_g,
           ml_norm_b, ml_proj, rw_mu, rw_w0, rw_w_up, rw_a0, rw_a_up, rw_g_up, rw_k_k, rw_k_a, rw_r_k,
           rw_norm_g, rw_norm_b, rw_proj, s5_lam_re, s5_lam_im, s5_log_dt, s5_b_re, s5_b_im, s5_c_re,
           s5_c_im, s5_d, s5_w_val, s5_w_gate, w_out, ln1_g, ln1_b, ln2_g, ln2_b, router_w, router_b,
           exp_w_gate, exp_w_up, exp_w_down):
    bsz, n_ctx = ctx.shape[0], ctx.shape[1]
    t_len = n_ctx + x.shape[1]
    assert n_ctx % ROW_TM == 0 and t_len % ROW_TM == 0
    blocks_per_seq, ctx_blocks = t_len // ROW_TM, n_ctx // ROW_TM
    silu_c = jax.nn.silu(c)
    silu_cc = jax.nn.silu(c_ctx)[None, :]
    mods = []
    for i in range(DEPTH):
        mx = _mm_any(silu_c, ada_w[i]) + ada_b[i]
        mc = jnp.broadcast_to(_mm_any(silu_cc, ada_w[i]) + ada_b[i], mx.shape)
        mods.append(jnp.stack([mc, mx], axis=1).reshape(bsz, 2, N_MOD, 1, D_MODEL))
    xa = jnp.concatenate([ctx, x], axis=1).reshape(bsz * t_len, D_MODEL)
    m0 = mods[0]
    u = jnp.concatenate([_modulate(ctx, m0[:, 0, 0], m0[:, 0, 1]), _modulate(x, m0[:, 1, 0], m0[:, 1, 1])],
                        axis=1).astype(_BF)
    for i in range(DEPTH):
        z = _token_mixer(
            u, n_ctx, w_in[i],
            ml_conv_w[i], ml_conv_b[i], ml_ig_b[i], ml_fg_b[i], ml_norm_g[i], ml_norm_b[i], ml_proj[i],
            rw_mu[i], rw_w0[i], rw_w_up[i], rw_a0[i], rw_a_up[i], rw_g_up[i], rw_k_k[i], rw_k_a[i], rw_r_k[i],
            rw_norm_g[i], rw_norm_b[i], rw_proj[i],
            s5_lam_re[i], s5_lam_im[i], s5_log_dt[i], s5_b_re[i], s5_b_im[i], s5_c_re[i], s5_c_im[i], s5_d[i],
            s5_w_val[i], s5_w_gate[i])
        xa, u_ffn = _resid_norm_mod(z, w_out[i].astype(_BF), xa, mods[i], 2, ln1_g[i], ln1_b[i], mods[i], 3, 4,
                                    blocks_per_seq, ctx_blocks)
        ffn = _moe_ffn(u_ffn, router_w, router_b, i, exp_w_gate, exp_w_up, exp_w_down)
        if i == DEPTH - 1:
            out = _final_resid_norm(ffn, xa, mods[i], 5, ln2_g[i], ln2_b[i], bsz, blocks_per_seq, ctx_blocks)
            return out.reshape(bsz, t_len - n_ctx, D_MODEL)
        xa, u = _resid_norm_mod(ffn, None, xa, mods[i], 5, ln2_g[i], ln2_b[i], mods[i + 1], 0, 1,
                                blocks_per_seq, ctx_blocks, u_dtype=_BF)
        u = u.reshape(bsz, t_len, D_MODEL)
```
